```python
import math
import jax, jax.numpy as jnp
from jax import lax
import numpy as np

D_MODEL = 1024
BATCH = 4
SEQ = 8192
DEPTH = 2

HEAD_DIM = 64
Q_BLOCK = 128
NEG_INF = -1e30
LN_EPS = 1e-5
RMS_EPS = 1e-6
DN_ALPHA = (2.0 * DEPTH) ** 0.25
DN_BETA = (8.0 * DEPTH) ** -0.25
D_FF = 4 * D_MODEL

MLA_HEADS = (D_MODEL // 2) // HEAD_DIM
MLA_Q_RANK = D_MODEL // 4
MLA_KV_RANK = D_MODEL // 4
MLA_NOPE = 64
MLA_ROPE = 32
MLA_V = HEAD_DIM
ROPE_THETA = 10000.0

NSA_HEADS = (D_MODEL // 2) // HEAD_DIM
NSA_KV_GROUPS = 2
NSA_CMP_LEN = 32
NSA_CMP_STRIDE = 16
NSA_SEL_LEN = 64
NSA_SEL_TOPK = 16
NSA_WINDOW = 512
NSA_FORCE_BONUS = 1e3

DIFF_HEADS = D_MODEL // (2 * HEAD_DIM)
DIFF_D = HEAD_DIM

L0_SPLITS = (MLA_Q_RANK, MLA_KV_RANK, MLA_ROPE, NSA_HEADS * HEAD_DIM) + (NSA_KV_GROUPS * HEAD_DIM,) * 6 + (3 * NSA_HEADS,)
L0_IN_WIDTH = sum(L0_SPLITS)
L0_V_SEGMENTS = (5, 7, 9)
L0_MIX_WIDTH = MLA_HEADS * MLA_V + NSA_HEADS * HEAD_DIM

kernel_name = 'hybrid_mla_nsa_diffattn_deepnorm'


def layer_norm(x, g, b):
    xf = x.astype(jnp.float32)
    mu = jnp.mean(xf, axis=-1, keepdims=True)
    var = jnp.mean(jnp.square(xf - mu), axis=-1, keepdims=True)
    return ((xf - mu) * lax.rsqrt(var + LN_EPS) * g + b).astype(x.dtype)


def rms_norm(x, g):
    xf = x.astype(jnp.float32)
    return (xf * lax.rsqrt(jnp.mean(jnp.square(xf), axis=-1, keepdims=True) + RMS_EPS) * g).astype(x.dtype)


def rope_tables(seq):
    inv = 1.0 / (ROPE_THETA ** (jnp.arange(0, MLA_ROPE, 2, dtype=jnp.float32) / MLA_ROPE))
    ang = jnp.arange(seq, dtype=jnp.float32)[:, None] * inv[None, :]
    return jnp.cos(ang), jnp.sin(ang)


def apply_rope(x, cos, sin):
    x1, x2 = jnp.split(x.astype(jnp.float32), 2, axis=-1)
    return jnp.concatenate([x1 * cos - x2 * sin, x1 * sin + x2 * cos], axis=-1).astype(x.dtype)


def alibi_slopes(n):
    return jnp.asarray(2.0 ** (-8.0 * np.arange(1, n + 1) / n), dtype=jnp.float32)


def to_blocks(a):
    b, s = a.shape[:2]
    return jnp.moveaxis(a.reshape(b, s // Q_BLOCK, Q_BLOCK, *a.shape[2:]), 1, 0)


def from_blocks(a):
    a = jnp.moveaxis(a, 0, 1)
    return a.reshape(a.shape[0], a.shape[1] * a.shape[2], *a.shape[3:])


def block_starts(s):
    return jnp.arange(s // Q_BLOCK, dtype=jnp.int32) * Q_BLOCK


def causal_probs(q_blk, k, start, scale, slopes):
    sc = jnp.einsum('bqhd,bkhd->bhqk', q_blk, k, preferred_element_type=jnp.float32) * scale
    qpos = start + jnp.arange(Q_BLOCK)
    dist = (qpos[:, None] - jnp.arange(k.shape[1])[None, :]).astype(jnp.float32)
    if slopes is not None:
        sc = sc - slopes[None, :, None, None] * dist
    return jax.nn.softmax(jnp.where(dist >= 0, sc, NEG_INF), axis=-1)


def mla_core(q, k, v):
    scale = q.shape[-1] ** -0.5

    def block(args):
        qb, start = args
        p = causal_probs(qb, k, start, scale, None)
        return jnp.einsum('bhqk,bkhd->bqhd', p.astype(v.dtype), v)

    return from_blocks(lax.map(block, (to_blocks(q), block_starts(q.shape[1]))))


def compress_blocks(t, pos_emb, w1, w2):
    b, s, g, d = t.shape
    n_c = (s - NSA_CMP_LEN) // NSA_CMP_STRIDE + 1
    idx = jnp.arange(n_c)[:, None] * NSA_CMP_STRIDE + jnp.arange(NSA_CMP_LEN)[None, :]
    blk = t[:, idx] + pos_emb[:, None, :]
    flat = jnp.swapaxes(blk, 2, 3).reshape(b, n_c, g, NSA_CMP_LEN * d)
    return jax.nn.gelu(flat @ w1) @ w2


def cmp_to_sel_overlap(n_c, n_s):
    c0 = jnp.arange(n_c)[:, None] * NSA_CMP_STRIDE
    s0 = jnp.arange(n_s)[None, :] * NSA_SEL_LEN
    ov = jnp.minimum(c0 + NSA_CMP_LEN, s0 + NSA_SEL_LEN) - jnp.maximum(c0, s0)
    return jnp.maximum(ov, 0).astype(jnp.float32) / NSA_CMP_LEN


def nsa_core(q, k_c, v_c, k_s, v_s, k_w, v_w, gates):
    b, s, h, d = q.shape
    g, hg = NSA_KV_GROUPS, NSA_HEADS // NSA_KV_GROUPS
    n_c = k_c.shape[1]
    n_s = s // NSA_SEL_LEN
    top = min(NSA_SEL_TOPK, n_s)
    scale = d ** -0.5
    slopes = alibi_slopes(h).reshape(g, hg)[None, :, :, None, None]
    cmp_end = jnp.arange(n_c) * NSA_CMP_STRIDE + (NSA_CMP_LEN - 1)
    overlap = cmp_to_sel_overlap(n_c, n_s)
    k_blocks = jnp.transpose(k_s.reshape(b, n_s, NSA_SEL_LEN, g, d), (0, 3, 1, 2, 4))
    v_blocks = jnp.transpose(v_s.reshape(b, n_s, NSA_SEL_LEN, g, d), (0, 3, 1, 2, 4))
    pad = ((0, 0), (NSA_WINDOW, 0), (0, 0), (0, 0))
    k_wp = jnp.pad(k_w, pad)
    v_wp = jnp.pad(v_w, pad)
    bi = jnp.arange(b)[:, None, None, None]
    gi = jnp.arange(g)[None, :, None, None]
    sel_off = jnp.arange(NSA_SEL_LEN)
    blk_ids = jnp.arange(n_s)
    win_off = jnp.arange(NSA_WINDOW + Q_BLOCK)
    n_sel_keys = top * NSA_SEL_LEN

    def block(args):
        qb, gb, start = args
        qg = qb.reshape(b, Q_BLOCK, g, hg, d)
        qpos = start + jnp.arange(Q_BLOCK)
        dist = (qpos[:, None] - cmp_end[None, :]).astype(jnp.float32)
        valid = dist >= 0
        sc = jnp.einsum('bqghd,bcgd->bghqc', qg, k_c, preferred_element_type=jnp.float32) * scale
        sc = jnp.where(valid, sc - slopes * dist, NEG_INF)
        p_c = jnp.where(valid, jax.nn.softmax(sc, axis=-1), 0.0)
        o_c = jnp.einsum('bghqc,bcgd->bqghd', p_c.astype(v_c.dtype), v_c)
        imp = jnp.einsum('bghqc,cn->bgqn', p_c, overlap)
        cur = (qpos // NSA_SEL_LEN)[:, None]
        forced = (blk_ids == 0) | (blk_ids == cur) | (blk_ids == cur - 1)
        imp = jnp.where(blk_ids <= cur, imp + NSA_FORCE_BONUS * forced.astype(jnp.float32), NEG_INF)
        _, sel = lax.top_k(imp, top)
        ks = k_blocks[bi, gi, sel].reshape(b, g, Q_BLOCK, n_sel_keys, d)
        vs = v_blocks[bi, gi, sel].reshape(b, g, Q_BLOCK, n_sel_keys, d)
        kpos = (sel[..., None] * NSA_SEL_LEN + sel_off).reshape(b, g, Q_BLOCK, n_sel_keys)
        dist = (qpos[:, None] - kpos).astype(jnp.float32)[:, :, None]
        sc = jnp.einsum('bqghd,bgqkd->bghqk', qg, ks, preferred_element_type=jnp.float32) * scale
        sc = jnp.where(dist >= 0, sc - slopes * dist, NEG_INF)
        o_s = jnp.einsum('bghqk,bgqkd->bqghd', jax.nn.softmax(sc, axis=-1).astype(vs.dtype), vs)
        kw = lax.dynamic_slice_in_dim(k_wp, start, NSA_WINDOW + Q_BLOCK, axis=1)
        vw = lax.dynamic_slice_in_dim(v_wp, start, NSA_WINDOW + Q_BLOCK, axis=1)
        kpos_w = start - NSA_WINDOW + win_off
        dist_i = qpos[:, None] - kpos_w[None, :]
        valid = (dist_i >= 0) & (dist_i < NSA_WINDOW) & (kpos_w[None, :] >= 0)
        sc = jnp.einsum('bqghd,bkgd->bghqk', qg, kw, preferred_element_type=jnp.float32) * scale
        sc = jnp.where(valid, sc - slopes * dist_i.astype(jnp.float32), NEG_INF)
        o_w = jnp.einsum('bghqk,bkgd->bqghd', jax.nn.softmax(sc, axis=-1).astype(vw.dtype), vw)
        gg = gb.reshape(b, Q_BLOCK, g, hg, 3)
        o = gg[..., 0:1] * o_c + gg[..., 1:2] * o_s + gg[..., 2:3] * o_w
        return o.reshape(b, Q_BLOCK, h, d)

    return from_blocks(lax.map(block, (to_blocks(q), to_blocks(gates), block_starts(s))))


def mla_nsa_mixer(x, w_in, q_norm, w_uq, kv_norm, w_ukv, pos_k, w1_k, w2_k, pos_v, w1_v, w2_v, w_out, cos, sin):
    b, s, _ = x.shape
    h = x @ w_in
    parts = jnp.split(h, np.cumsum(L0_SPLITS)[:-1].tolist(), axis=-1)
    q_lat, kv_lat, k_rope, nq, nkc, nvc, nks, nvs, nkw, nvw, ng = parts
    q = (rms_norm(q_lat, q_norm) @ w_uq).reshape(b, s, MLA_HEADS, MLA_NOPE + MLA_ROPE)
    q_pe = apply_rope(q[..., MLA_NOPE:], cos[:, None, :], sin[:, None, :])
    kv = (rms_norm(kv_lat, kv_norm) @ w_ukv).reshape(b, s, MLA_HEADS, MLA_NOPE + MLA_V)
    k_pe = apply_rope(k_rope, cos, sin)
    q_full = jnp.concatenate([q[..., :MLA_NOPE], q_pe], axis=-1)
    k_full = jnp.concatenate([kv[..., :MLA_NOPE], jnp.broadcast_to(k_pe[:, :, None, :], (b, s, MLA_HEADS, MLA_ROPE))], axis=-1)
    o_mla = mla_core(q_full, k_full, kv[..., MLA_NOPE:])
    kv_shape = (b, s, NSA_KV_GROUPS, HEAD_DIM)
    k_c = compress_blocks(nkc.reshape(kv_shape), pos_k, w1_k, w2_k)
    v_c = compress_blocks(nvc.reshape(kv_shape), pos_v, w1_v, w2_v)
    gates = jax.nn.sigmoid(ng.reshape(b, s, NSA_HEADS, 3))
    o_nsa = nsa_core(nq.reshape(b, s, NSA_HEADS, HEAD_DIM), k_c, v_c, nks.reshape(kv_shape), nvs.reshape(kv_shape), nkw.reshape(kv_shape), nvw.reshape(kv_shape), gates)
    o = jnp.concatenate([o_mla.reshape(b, s, -1), o_nsa.reshape(b, s, -1)], axis=-1)
    return o @ w_out


def diff_mixer(x, w_qkv, lam_q1, lam_k1, lam_q2, lam_k2, subln_g, w_o, layer_idx):
    b, s, _ = x.shape
    q, k, v = jnp.split(x @ w_qkv, 3, axis=-1)
    q = q.reshape(b, s, DIFF_HEADS, 2, DIFF_D)
    k = k.reshape(b, s, DIFF_HEADS, 2, DIFF_D)
    v = v.reshape(b, s, DIFF_HEADS, 2 * DIFF_D)
    k1, k2 = k[..., 0, :], k[..., 1, :]
    lam_init = 0.8 - 0.6 * math.exp(-0.3 * layer_idx)
    lam = (jnp.exp(jnp.sum(lam_q1.astype(jnp.float32) * lam_k1.astype(jnp.float32)))
           - jnp.exp(jnp.sum(lam_q2.astype(jnp.float32) * lam_k2.astype(jnp.float32))) + lam_init)
    slopes = alibi_slopes(DIFF_HEADS)
    scale = DIFF_D ** -0.5

    def block(args):
        qb, start = args
        p1 = causal_probs(qb[..., 0, :], k1, start, scale, slopes)
        p2 = causal_probs(qb[..., 1, :], k2, start, scale, slopes)
        return jnp.einsum('bhqk,bkhe->bqhe', (p1 - lam * p2).astype(v.dtype), v)

    o = from_blocks(lax.map(block, (to_blocks(q), block_starts(s))))
    o = rms_norm(o, subln_g) * (1.0 - lam_init)
    return o.reshape(b, s, -1) @ w_o


def channel_mixer(x, w_up, w_down, ln_g, ln_b):
    y = jnp.square(jax.nn.relu(x @ w_up)) @ w_down
    return layer_norm(DN_ALPHA * x + y, ln_g, ln_b)


def _w(k, shape, fan_in, gain=1.0):
    return jax.random.normal(k, shape, jnp.float32) * (gain * fan_in ** -0.5)


def _gain(k, n):
    return 1.0 + 0.02 * jax.random.normal(k, (n,), jnp.float32)


def _bias(k, n):
    return 0.02 * jax.random.normal(k, (n,), jnp.float32)


def setup_inputs(seed: int = 0) -> dict:
    key = jax.random.key(seed)
    ks = iter(jax.random.split(key, 40))
    d = D_MODEL
    cmp_in = NSA_CMP_LEN * HEAD_DIM
    in_scale = jnp.concatenate([jnp.full((n,), DN_BETA if i in L0_V_SEGMENTS else 1.0, jnp.float32) for i, n in enumerate(L0_SPLITS)])
    ukv_scale = jnp.tile(jnp.concatenate([jnp.ones((MLA_NOPE,), jnp.float32), jnp.full((MLA_V,), DN_BETA, jnp.float32)]), MLA_HEADS)
    qkv_scale = jnp.concatenate([jnp.ones((2 * d,), jnp.float32), jnp.full((d,), DN_BETA, jnp.float32)])
    return {
        'x': jax.random.normal(next(ks), (BATCH, SEQ, d), jnp.float32),
        'l0_w_in': _w(next(ks), (d, L0_IN_WIDTH), d) * in_scale,
        'l0_mla_q_norm': _gain(next(ks), MLA_Q_RANK),
        'l0_mla_w_uq': _w(next(ks), (MLA_Q_RANK, MLA_HEADS * (MLA_NOPE + MLA_ROPE)), MLA_Q_RANK),
        'l0_mla_kv_norm': _gain(next(ks), MLA_KV_RANK),
        'l0_mla_w_ukv': _w(next(ks), (MLA_KV_RANK, MLA_HEADS * (MLA_NOPE + MLA_V)), MLA_KV_RANK) * ukv_scale,
        'l0_nsa_cmp_pos_k': 0.1 * jax.random.normal(next(ks), (NSA_CMP_LEN, HEAD_DIM), jnp.float32),
        'l0_nsa_cmp_w1_k': _w(next(ks), (cmp_in, HEAD_DIM), cmp_in),
        'l0_nsa_cmp_w2_k': _w(next(ks), (HEAD_DIM, HEAD_DIM), HEAD_DIM),
        'l0_nsa_cmp_pos_v': 0.1 * jax.random.normal(next(ks), (NSA_CMP_LEN, HEAD_DIM), jnp.float32),
        'l0_nsa_cmp_w1_v': _w(next(ks), (cmp_in, HEAD_DIM), cmp_in),
        'l0_nsa_cmp_w2_v': _w(next(ks), (HEAD_DIM, HEAD_DIM), HEAD_DIM),
        'l0_w_out': _w(next(ks), (L0_MIX_WIDTH, d), L0_MIX_WIDTH, DN_BETA),
        'l0_ln_mix_g': _gain(next(ks), d),
        'l0_ln_mix_b': _bias(next(ks), d),
        'l0_w_up': _w(next(ks), (d, D_FF), d, DN_BETA),
        'l0_w_down': _w(next(ks), (D_FF, d), D_FF, DN_BETA),
        'l0_ln_ffn_g': _gain(next(ks), d),
        'l0_ln_ffn_b': _bias(next(ks), d),
        'l1_w_qkv': _w(next(ks), (d, 3 * d), d) * qkv_scale,
        'l1_lam_q1': 0.1 * jax.random.normal(next(ks), (DIFF_D,), jnp.float32),
        'l1_lam_k1': 0.1 * jax.random.normal(next(ks), (DIFF_D,), jnp.float32),
        'l1_lam_q2': 0.1 * jax.random.normal(next(ks), (DIFF_D,), jnp.float32),
        'l1_lam_k2': 0.1 * jax.random.normal(next(ks), (DIFF_D,), jnp.float32),
        'l1_subln_g': _gain(next(ks), 2 * DIFF_D),
        'l1_w_o': _w(next(ks), (d, d), d, DN_BETA),
        'l1_ln_mix_g': _gain(next(ks), d),
        'l1_ln_mix_b': _bias(next(ks), d),
        'l1_w_up': _w(next(ks), (d, D_FF), d, DN_BETA),
        'l1_w_down': _w(next(ks), (D_FF, d), D_FF, DN_BETA),
        'l1_ln_ffn_g': _gain(next(ks), d),
        'l1_ln_ffn_b': _bias(next(ks), d),
    }


def reference(x, l0_w_in, l0_mla_q_norm, l0_mla_w_uq, l0_mla_kv_norm, l0_mla_w_ukv,
              l0_nsa_cmp_pos_k, l0_nsa_cmp_w1_k, l0_nsa_cmp_w2_k, l0_nsa_cmp_pos_v, l0_nsa_cmp_w1_v, l0_nsa_cmp_w2_v,
              l0_w_out, l0_ln_mix_g, l0_ln_mix_b, l0_w_up, l0_w_down, l0_ln_ffn_g, l0_ln_ffn_b,
              l1_w_qkv, l1_lam_q1, l1_lam_k1, l1_lam_q2, l1_lam_k2, l1_subln_g, l1_w_o,
              l1_ln_mix_g, l1_ln_mix_b, l1_w_up, l1_w_down, l1_ln_ffn_g, l1_ln_ffn_b):
    cos, sin = rope_tables(x.shape[1])
    mixer_params = [
        (l0_w_in, l0_mla_q_norm, l0_mla_w_uq, l0_mla_kv_norm, l0_mla_w_ukv, l0_nsa_cmp_pos_k, l0_nsa_cmp_w1_k,
         l0_nsa_cmp_w2_k, l0_nsa_cmp_pos_v, l0_nsa_cmp_w1_v, l0_nsa_cmp_w2_v, l0_w_out),
        (l1_w_qkv, l1_lam_q1, l1_lam_k1, l1_lam_q2, l1_lam_k2, l1_subln_g, l1_w_o),
    ]
    mix_norms = [(l0_ln_mix_g, l0_ln_mix_b), (l1_ln_mix_g, l1_ln_mix_b)]
    ffn_params = [(l0_w_up, l0_w_down, l0_ln_ffn_g, l0_ln_ffn_b), (l1_w_up, l1_w_down, l1_ln_ffn_g, l1_ln_ffn_b)]
    for i in range(DEPTH):
        if i % 2 == 0:
            y = mla_nsa_mixer(x, *mixer_params[i], cos, sin)
        else:
            y = diff_mixer(x, *mixer_params[i], i)
        x = layer_norm(DN_ALPHA * x + y, *mix_norms[i])
        x = channel_mixer(x, *ffn_params[i])
    return x
```

```python
import functools
import math

import jax
import jax.numpy as jnp
import numpy as np
from jax import lax
from jax.experimental import pallas as pl
from jax.experimental.pallas import tpu as pltpu

F32 = jnp.float32
BF16 = jnp.bfloat16

LANES = 128
HEAD_DIM = 64
Q_TILE = 256
KV_TILE = 512
ROW_TILE = 512
FF_TILE = 1024
VMEM_LIMIT = 56 * 1024 * 1024

NEG_INF = -1e30
LN_EPS = 1e-5
RMS_EPS = 1e-6
DEPTH = 2
DN_ALPHA = (2.0 * DEPTH) ** 0.25

MLA_HEADS = 8
MLA_NOPE = 64
MLA_ROPE = 32
ROPE_THETA = 10000.0
NSA_HEADS = 8
NSA_GROUPS = 2
NSA_HG = NSA_HEADS // NSA_GROUPS
CMP_LEN = 32
CMP_STRIDE = 16
SEL_LEN = 64
SEL_TOPK = 16
WINDOW = 512
FORCE_BONUS = 1e3
DIFF_HEADS = 8


def _params(*sem):
    return pltpu.CompilerParams(dimension_semantics=sem, vmem_limit_bytes=VMEM_LIMIT)


def _dot(a, b):
    return jnp.dot(a, b, preferred_element_type=F32)


def _dot_nt(a, b):
    return lax.dot_general(a, b, (((1,), (1,)), ((), ())), preferred_element_type=F32)


def _split_bf16(x):
    hi = x.astype(BF16)
    lo = (x - hi.astype(F32)).astype(BF16)
    return hi, lo


def _layer_norm(z, g, b):
    mu = jnp.mean(z, axis=-1, keepdims=True)
    zc = z - mu
    var = jnp.mean(zc * zc, axis=-1, keepdims=True)
    return zc * lax.rsqrt(var + LN_EPS) * g + b


def _rms_norm(z, g, eps):
    return z * lax.rsqrt(jnp.mean(z * z, axis=-1, keepdims=True) + eps) * g


def _proj_kernel(x_ref, *refs, n_out):
    w_refs, o_refs = refs[:n_out], refs[n_out:]
    xb = x_ref[...].astype(BF16)
    for w_ref, o_ref in zip(w_refs, o_refs):
        o_ref[...] = _dot(xb, w_ref[...]).astype(o_ref.dtype)


def _project(x, weights, out_dtypes):
    m, k = x.shape
    tm = min(ROW_TILE, m)
    n_out = len(weights)
    return pl.pallas_call(
        functools.partial(_proj_kernel, n_out=n_out),
        grid=(m // tm,),
        in_specs=[pl.BlockSpec((tm, k), lambda i: (i, 0))]
        + [pl.BlockSpec(w.shape, lambda i: (0, 0)) for w in weights],
        out_specs=[pl.BlockSpec((tm, w.shape[1]), lambda i: (i, 0)) for w in weights],
        out_shape=[jax.ShapeDtypeStruct((m, w.shape[1]), dt) for w, dt in zip(weights, out_dtypes)],
        compiler_params=_params("parallel"),
        name="project",
    )(x, *weights)


def _rope_slab(slab, c, s1, s2):
    half = MLA_ROPE // 2
    up = pltpu.roll(slab, half, 1)
    down = pltpu.roll(slab, LANES - half, 1)
    return slab * c + down * s1 + up * s2


def _mla_prep_kernel(ql_ref, kvl_ref, kpe_ref, qg_ref, kvg_ref, wq_ref, wk_ref, wv_ref,
                     c_ref, s1_ref, s2_ref, q_ref, k_ref, v_ref, *, q_scale):
    c, s1, s2 = c_ref[...], s1_ref[...], s2_ref[...]
    qn = _rms_norm(ql_ref[...], qg_ref[...], RMS_EPS).astype(BF16)
    kvn = _rms_norm(kvl_ref[...], kvg_ref[...], RMS_EPS).astype(BF16)
    q = _dot(qn, wq_ref[...])
    k = _dot(kvn, wk_ref[...])
    v_ref[...] = _dot(kvn, wv_ref[...]).astype(v_ref.dtype)
    kpe = _rope_slab(kpe_ref[...], c, s1, s2)
    for h in range(MLA_HEADS):
        sl = slice(h * LANES, (h + 1) * LANES)
        q_ref[:, sl] = (_rope_slab(q[:, sl], c, s1, s2) * q_scale).astype(q_ref.dtype)
        k_ref[:, sl] = (k[:, sl] + kpe).astype(k_ref.dtype)


def _mla_prep(slab_a, q_gain, kv_gain, wq, wk, wv, rope_c, rope_s1, rope_s2, seq):
    m = slab_a.shape[0]
    tm = min(ROW_TILE, seq)
    per_seq = seq // tm
    rank = q_gain.shape[1]
    row = lambda j: (lambda i: (i, j))
    tab = lambda i: (i % per_seq, 0)
    const = lambda i: (0, 0)
    hw = MLA_HEADS * LANES
    vw = wv.shape[1]
    return pl.pallas_call(
        functools.partial(_mla_prep_kernel, q_scale=float((MLA_NOPE + MLA_ROPE) ** -0.5)),
        grid=(m // tm,),
        in_specs=[pl.BlockSpec((tm, rank), row(0)), pl.BlockSpec((tm, rank), row(1)),
                  pl.BlockSpec((tm, LANES), row(2 * rank // LANES)),
                  pl.BlockSpec((1, rank), const), pl.BlockSpec((1, rank), const),
                  pl.BlockSpec(wq.shape, const), pl.BlockSpec(wk.shape, const), pl.BlockSpec(wv.shape, const),
                  pl.BlockSpec((tm, LANES), tab), pl.BlockSpec((tm, LANES), tab), pl.BlockSpec((tm, LANES), tab)],
        out_specs=[pl.BlockSpec((tm, hw), row(0)), pl.BlockSpec((tm, hw), row(0)), pl.BlockSpec((tm, vw), row(0))],
        out_shape=[jax.ShapeDtypeStruct((m, hw), BF16), jax.ShapeDtypeStruct((m, hw), BF16),
                   jax.ShapeDtypeStruct((m, vw), BF16)],
        compiler_params=_params("parallel"),
        name="mla_prep",
    )(slab_a, slab_a, slab_a, q_gain, kv_gain, wq, wk, wv, rope_c, rope_s1, rope_s2)


def _flash_init(m_ref, l_ref, acc_ref, i):
    m_ref[i] = jnp.full(m_ref.shape[1:], NEG_INF, F32)
    l_ref[i] = jnp.zeros(l_ref.shape[1:], F32)
    acc_ref[i] = jnp.zeros(acc_ref.shape[1:], F32)


def _flash_update(s, v, m_ref, l_ref, acc_ref, i):
    reps = s.shape[1] // LANES
    m_prev = m_ref[i]
    m_next = jnp.maximum(m_prev, jnp.max(s, axis=1, keepdims=True))
    p = jnp.exp(s - jnp.tile(m_next, (1, reps)))
    alpha = jnp.exp(m_prev - m_next)
    l_ref[i] = alpha * l_ref[i] + jnp.sum(p, axis=1, keepdims=True)
    m_ref[i] = m_next
    acc_ref[i] = alpha * acc_ref[i] + _dot(p.astype(BF16), v)


def _flash_result(l_ref, acc_ref, i):
    return acc_ref[i] / l_ref[i]


def _lane_iota(shape):
    return lax.broadcasted_iota(jnp.int32, shape, 1)


def _rel_pos(tq, tk):
    return (lax.broadcasted_iota(jnp.int32, (tq, tk), 1) - lax.broadcasted_iota(jnp.int32, (tq, tk), 0)).astype(F32)


def _kv_rows(j, tk):
    return pl.ds(pl.multiple_of(j * tk, tk), tk)


def _half_select(lo_half, hi_half):
    return jnp.where(_lane_iota(lo_half.shape) < HEAD_DIM, lo_half, hi_half)


def _keep_half(x, half):
    lane = _lane_iota(x.shape)
    keep = (lane < HEAD_DIM) if half == 0 else (lane >= HEAD_DIM)
    return jnp.where(keep, x, jnp.zeros_like(x))


def _move_head(slab, src_half, dst_half):
    if src_half != dst_half:
        slab = pltpu.roll(slab, HEAD_DIM, 1)
    return _keep_half(slab, dst_half)


def _mla_attn_kernel(q_ref, k_ref, v_ref, o_ref, m_ref, l_ref, acc_ref, *, tq, tk):
    qi = pl.program_id(2)
    q_start = qi * tq
    n_full = q_start // tk
    diag_ok = _rel_pos(tq, tk) <= (q_start - n_full * tk).astype(F32)
    outs = []
    for hh in range(2):
        cols = slice(hh * LANES, (hh + 1) * LANES)
        q = q_ref[0, :, cols]
        _flash_init(m_ref, l_ref, acc_ref, hh)

        def step(j, masked, q=q, cols=cols, hh=hh):
            rows = _kv_rows(j, tk)
            s = _dot_nt(q, k_ref[0, rows, cols])
            if masked:
                s = jnp.where(diag_ok, s, NEG_INF)
            _flash_update(s, v_ref[0, rows, :], m_ref, l_ref, acc_ref, hh)

        lax.fori_loop(0, n_full, lambda j, c: (step(j, False), c)[1], 0)
        step(n_full, True)
        outs.append(_flash_result(l_ref, acc_ref, hh))
    o_ref[0] = _half_select(outs[0], outs[1]).astype(o_ref.dtype)


def _mla_attn(q, k, v):
    b, s, _ = q.shape
    tq, tk = min(Q_TILE, s), min(KV_TILE, s)
    pairs = MLA_HEADS // 2
    return pl.pallas_call(
        functools.partial(_mla_attn_kernel, tq=tq, tk=tk),
        grid=(b, pairs, s // tq),
        in_specs=[pl.BlockSpec((1, tq, 2 * LANES), lambda bi, p, i: (bi, i, p)),
                  pl.BlockSpec((1, s, 2 * LANES), lambda bi, p, i: (bi, 0, p)),
                  pl.BlockSpec((1, s, LANES), lambda bi, p, i: (bi, 0, p))],
        out_specs=pl.BlockSpec((1, tq, LANES), lambda bi, p, i: (bi, i, p)),
        out_shape=jax.ShapeDtypeStruct((b, s, pairs * LANES), BF16),
        scratch_shapes=[pltpu.VMEM((2, tq, LANES), F32)] * 3,
        compiler_params=_params("parallel", "parallel", "arbitrary"),
        name="mla_attn",
    )(q, k, v)


def _gelu_tanh(x):
    return 0.5 * x * (1.0 + jnp.tanh(math.sqrt(2.0 / math.pi) * (x + 0.044715 * (x * x * x))))


def _compress_kernel(x_ref, pos_ref, w1a_ref, w1b_ref, w2_ref, o_ref, *, n_real):
    x = x_ref[0]
    n = x.shape[0]
    first = _dot(x, w1a_ref[...])
    second = _dot(x, w1b_ref[...])
    pos_hi, pos_lo = _split_bf16(pos_ref[...])
    bias = (_dot(pos_hi[:8], w1a_ref[...]) + _dot(pos_lo[:8], w1a_ref[...])
            + _dot(pos_hi[8:], w1b_ref[...]) + _dot(pos_lo[8:], w1b_ref[...]))[:1]
    pre = first + pltpu.roll(second, n - 1, 0) + bias
    out = _dot(_gelu_tanh(pre).astype(BF16), w2_ref[...])
    real = lax.broadcasted_iota(jnp.int32, out.shape, 0) < n_real
    o_ref[0] = jnp.where(real, out, 0.0).astype(o_ref.dtype)


def _compress(x_chunks, pos_exp, w1a, w1b, w2, n_real):
    b, n, width = x_chunks.shape
    const = lambda bi: (0, 0)
    return pl.pallas_call(
        functools.partial(_compress_kernel, n_real=n_real),
        grid=(b,),
        in_specs=[pl.BlockSpec((1, n, width), lambda bi: (bi, 0, 0)),
                  pl.BlockSpec(pos_exp.shape, const), pl.BlockSpec(w1a.shape, const),
                  pl.BlockSpec(w1b.shape, const), pl.BlockSpec(w2.shape, const)],
        out_specs=pl.BlockSpec((1, n, w2.shape[1]), lambda bi: (bi, 0, 0)),
        out_shape=jax.ShapeDtypeStruct((b, n, w2.shape[1]), BF16),
        compiler_params=_params("parallel"),
        name="nsa_compress",
    )(x_chunks, pos_exp, w1a, w1b, w2)


def _nsa_head_slope(h):
    return float(2.0 ** (-8.0 * (h + 1) / NSA_HEADS))


def _nsa_queries(q_ref, g):
    out = []
    for hg in range(NSA_HG):
        h = g * NSA_HG + hg
        slab = q_ref[0, :, (h // 2) * LANES:(h // 2 + 1) * LANES].astype(F32) * (HEAD_DIM ** -0.5)
        out.append(_move_head(slab, h % 2, g).astype(BF16))
    return out


def _place_heads(results, g):
    slabs = []
    for pair in range(NSA_HG // 2):
        even = _move_head(results[2 * pair], g, 0)
        odd = _move_head(results[2 * pair + 1], g, 1)
        slabs.append(even + odd)
    return slabs


def _select_blocks(imp, q_pos, n_lanes):
    tq = imp.shape[0]
    blk = _lane_iota((tq, n_lanes))
    cur = q_pos // SEL_LEN
    forced = jnp.where(blk == 0, 1.0, 0.0) + jnp.where(blk == cur, 1.0, 0.0) + jnp.where(blk == cur - 1, 1.0, 0.0)
    forced = jnp.minimum(forced, 1.0)
    val = jnp.where(blk <= cur, imp + FORCE_BONUS * forced, NEG_INF)
    val_t = val.T
    blk_t = lax.broadcasted_iota(jnp.int32, val_t.shape, 0)
    chosen = jnp.zeros(val_t.shape, F32)
    for _ in range(SEL_TOPK):
        top = jnp.max(val_t, axis=0, keepdims=True)
        first = jnp.min(jnp.where(val_t == top, blk_t, n_lanes), axis=0, keepdims=True)
        hit = blk_t == first
        chosen = jnp.where(hit, 1.0, chosen)
        val_t = jnp.where(hit, -jnp.inf, val_t)
    return jnp.where(chosen > 0.5, 0.0, NEG_INF).T


def _nsa_cmp_win_kernel(q_ref, kc_ref, vc_ref, kw_ref, vw_ref, ov_ref, oc_ref, ow_ref, sel_ref, *, tq, seq):
    qi = pl.program_id(1)
    q_start = qi * tq
    n_cmp = kc_ref.shape[1]
    row = lax.broadcasted_iota(jnp.int32, (tq, 1), 0)
    q_pos = q_start + row
    cmp_end = lax.broadcasted_iota(jnp.int32, (tq, n_cmp), 1) * CMP_STRIDE + (CMP_LEN - 1)
    cmp_dist = (q_pos - cmp_end).astype(F32)
    cmp_ok = cmp_dist >= 0.0
    span = WINDOW + tq
    w_start = jnp.maximum(q_start - WINDOW, 0)
    w_rows = pl.ds(pl.multiple_of(w_start, tq), span)
    win_dist = (q_pos - (w_start + lax.broadcasted_iota(jnp.int32, (tq, span), 1))).astype(F32)
    win_ok = jnp.abs(win_dist - (WINDOW - 1) * 0.5) <= (WINDOW - 1) * 0.5
    kc, vc = kc_ref[0], vc_ref[0]
    kw, vw = kw_ref[0, w_rows, :], vw_ref[0, w_rows, :]
    for g in range(NSA_GROUPS):
        queries = _nsa_queries(q_ref, g)
        p_sum = jnp.zeros((tq, n_cmp), F32)
        res_c, res_w = [], []
        for hg in range(NSA_HG):
            slope = _nsa_head_slope(g * NSA_HG + hg)
            s = jnp.where(cmp_ok, _dot_nt(queries[hg], kc) - slope * cmp_dist, NEG_INF)
            p = jnp.where(cmp_ok, jnp.exp(s - jnp.max(s, axis=1, keepdims=True)), 0.0)
            p = p / jnp.maximum(jnp.sum(p, axis=1, keepdims=True), 1e-30)
            p_sum = p_sum + p
            res_c.append(_dot(p.astype(BF16), vc))
            s = jnp.where(win_ok, _dot_nt(queries[hg], kw) - slope * win_dist, NEG_INF)
            p = jnp.exp(s - jnp.max(s, axis=1, keepdims=True))
            p = p / jnp.sum(p, axis=1, keepdims=True)
            res_w.append(_dot(p.astype(BF16), vw))
        for pair, (slab_c, slab_w) in enumerate(zip(_place_heads(res_c, g), _place_heads(res_w, g))):
            cols = slice((g * 2 + pair) * LANES, (g * 2 + pair + 1) * LANES)
            oc_ref[0, :, cols] = slab_c
            ow_ref[0, :, cols] = slab_w
        p_hi, p_lo = _split_bf16(p_sum)
        imp = _dot(p_hi, ov_ref[...]) + _dot(p_lo, ov_ref[...])
        sel_ref[0, :, g * LANES:(g + 1) * LANES] = _select_blocks(imp, q_pos, LANES).astype(sel_ref.dtype)


def _nsa_cmp_win(slab_b, cmp_kv, overlap, seq):
    b = slab_b.shape[0]
    tq = min(Q_TILE, seq)
    n_cmp = cmp_kv.shape[1]
    qw = NSA_HEADS * HEAD_DIM
    base = qw // LANES
    return pl.pallas_call(
        functools.partial(_nsa_cmp_win_kernel, tq=tq, seq=seq),
        grid=(b, seq // tq),
        in_specs=[pl.BlockSpec((1, tq, qw), lambda bi, i: (bi, i, 0)),
                  pl.BlockSpec((1, n_cmp, LANES), lambda bi, i: (bi, 0, 0)),
                  pl.BlockSpec((1, n_cmp, LANES), lambda bi, i: (bi, 0, 1)),
                  pl.BlockSpec((1, seq, LANES), lambda bi, i: (bi, 0, base + 2)),
                  pl.BlockSpec((1, seq, LANES), lambda bi, i: (bi, 0, base + 3)),
                  pl.BlockSpec(overlap.shape, lambda bi, i: (0, 0))],
        out_specs=[pl.BlockSpec((1, tq, qw), lambda bi, i: (bi, i, 0)),
                   pl.BlockSpec((1, tq, qw), lambda bi, i: (bi, i, 0)),
                   pl.BlockSpec((1, tq, NSA_GROUPS * LANES), lambda bi, i: (bi, i, 0))],
        out_shape=[jax.ShapeDtypeStruct((b, seq, qw), F32), jax.ShapeDtypeStruct((b, seq, qw), F32),
                   jax.ShapeDtypeStruct((b, seq, NSA_GROUPS * LANES), BF16)],
        compiler_params=_params("parallel", "arbitrary"),
        name="nsa_cmp_win",
    )(slab_b, cmp_kv, cmp_kv, slab_b, slab_b, overlap)


def _nsa_sel_kernel(q_ref, k_ref, v_ref, sel_ref, expand_ref, oc_ref, ow_ref, gate_ref, gx_ref, o_ref,
                    m_ref, l_ref, acc_ref, *, tq, tk):
    qi = pl.program_id(1)
    q_start = qi * tq
    n_full = q_start // tk
    rel = _rel_pos(tq, tk)
    diag_ok = rel <= (q_start - n_full * tk).astype(F32)
    out_slabs = []
    for g in range(NSA_GROUPS):
        queries = _nsa_queries(q_ref, g)
        sel_bias = sel_ref[0, :, g * LANES:(g + 1) * LANES]
        for hg in range(NSA_HG):
            _flash_init(m_ref, l_ref, acc_ref, hg)

        def step(j, masked, queries=queries, sel_bias=sel_bias, g=g):
            rows = _kv_rows(j, tk)
            k, v = k_ref[0, rows, :], v_ref[0, rows, :]
            key_minus_query = rel + (j * tk - q_start).astype(F32)
            block_bias = _dot(sel_bias, expand_ref[j])
            for hg in range(NSA_HG):
                slope = _nsa_head_slope(g * NSA_HG + hg)
                s = _dot_nt(queries[hg], k) + (block_bias + slope * key_minus_query)
                if masked:
                    s = jnp.where(diag_ok, s, NEG_INF)
                _flash_update(s, v, m_ref, l_ref, acc_ref, hg)

        lax.fori_loop(0, n_full, lambda j, c: (step(j, False), c)[1], 0)
        step(n_full, True)
        out_slabs += _place_heads([_flash_result(l_ref, acc_ref, hg) for hg in range(NSA_HG)], g)
    gates = jax.nn.sigmoid(gate_ref[...])
    g_hi, g_lo = _split_bf16(gates)
    width = NSA_HEADS * HEAD_DIM
    for i, o_sel in enumerate(out_slabs):
        mixed = None
        for branch, o_branch in enumerate((oc_ref[0, :, i * LANES:(i + 1) * LANES], o_sel,
                                           ow_ref[0, :, i * LANES:(i + 1) * LANES])):
            gx = gx_ref[:, branch * width + i * LANES:branch * width + (i + 1) * LANES]
            term = (_dot(g_hi, gx) + _dot(g_lo, gx)) * o_branch
            mixed = term if mixed is None else mixed + term
        o_ref[0, :, i * LANES:(i + 1) * LANES] = mixed.astype(o_ref.dtype)


def _nsa_sel(slab_b, sel_bias, expand, o_cmp, o_win, slab_a, gate_expand, seq, gate_col_block):
    b = slab_b.shape[0]
    tq, tk = min(Q_TILE, seq), min(KV_TILE, seq)
    qw = NSA_HEADS * HEAD_DIM
    base = qw // LANES
    per_seq = seq // tq
    tile = lambda bi, i: (bi, i, 0)
    return pl.pallas_call(
        functools.partial(_nsa_sel_kernel, tq=tq, tk=tk),
        grid=(b, seq // tq),
        in_specs=[pl.BlockSpec((1, tq, qw), tile),
                  pl.BlockSpec((1, seq, LANES), lambda bi, i: (bi, 0, base)),
                  pl.BlockSpec((1, seq, LANES), lambda bi, i: (bi, 0, base + 1)),
                  pl.BlockSpec((1, tq, NSA_GROUPS * LANES), tile),
                  pl.BlockSpec(expand.shape, lambda bi, i: (0, 0, 0)),
                  pl.BlockSpec((1, tq, qw), tile), pl.BlockSpec((1, tq, qw), tile),
                  pl.BlockSpec((tq, LANES), lambda bi, i: (bi * per_seq + i, gate_col_block)),
                  pl.BlockSpec(gate_expand.shape, lambda bi, i: (0, 0))],
        out_specs=pl.BlockSpec((1, tq, qw), tile),
        out_shape=jax.ShapeDtypeStruct((b, seq, qw), BF16),
        scratch_shapes=[pltpu.VMEM((NSA_HG, tq, LANES), F32)] * 3,
        compiler_params=_params("parallel", "arbitrary"),
        name="nsa_sel",
    )(slab_b, slab_b, slab_b, sel_bias, expand, o_cmp, o_win, slab_a, gate_expand)


def _diff_attn_kernel(slope_ref, lam_ref, q_ref, k_ref, v_ref, g_ref, o_ref, m_ref, l_ref, acc_ref,
                      *, tq, tk, lam_init):
    h, qi = pl.program_id(1), pl.program_id(2)
    q_start = qi * tq
    n_full = q_start // tk
    slope = slope_ref[h]
    slope_rel = slope * _rel_pos(tq, tk)
    diag_ok = _rel_pos(tq, tk) <= (q_start - n_full * tk).astype(F32)
    q = q_ref[0].astype(F32) * (HEAD_DIM ** -0.5)
    queries = [_keep_half(q, half).astype(BF16) for half in range(2)]
    for i in range(2):
        _flash_init(m_ref, l_ref, acc_ref, i)

    def step(j, masked):
        rows = _kv_rows(j, tk)
        k, v = k_ref[0, rows, :], v_ref[0, rows, :]
        bias = slope_rel + slope * (j * tk - q_start).astype(F32)
        for i in range(2):
            s = _dot_nt(queries[i], k) + bias
            if masked:
                s = jnp.where(diag_ok, s, NEG_INF)
            _flash_update(s, v, m_ref, l_ref, acc_ref, i)

    lax.fori_loop(0, n_full, lambda j, c: (step(j, False), c)[1], 0)
    step(n_full, True)
    lam_vec = lam_ref[...]
    lam = (jnp.exp(jnp.sum(lam_vec[0:1] * lam_vec[1:2], axis=1, keepdims=True))
           - jnp.exp(jnp.sum(lam_vec[2:3] * lam_vec[3:4], axis=1, keepdims=True)) + lam_init)
    o = _flash_result(l_ref, acc_ref, 0) - lam * _flash_result(l_ref, acc_ref, 1)
    o_ref[0] = (_rms_norm(o, g_ref[...], RMS_EPS) * (1.0 - lam_init)).astype(o_ref.dtype)


def _diff_attn(qkv, slopes, lam_vecs, subln_g, lam_init):
    b, s, _ = qkv.shape
    tq, tk = min(Q_TILE, s), min(KV_TILE, s)
    hn = DIFF_HEADS
    smem = pl.BlockSpec(memory_space=pltpu.SMEM)
    return pl.pallas_call(
        functools.partial(_diff_attn_kernel, tq=tq, tk=tk, lam_init=lam_init),
        grid=(b, hn, s // tq),
        in_specs=[smem, pl.BlockSpec(lam_vecs.shape, lambda bi, h, i: (0, 0)),
                  pl.BlockSpec((1, tq, LANES), lambda bi, h, i: (bi, i, h)),
                  pl.BlockSpec((1, s, LANES), lambda bi, h, i: (bi, 0, hn + h)),
                  pl.BlockSpec((1, s, LANES), lambda bi, h, i: (bi, 0, 2 * hn + h)),
                  pl.BlockSpec((1, LANES), lambda bi, h, i: (0, 0))],
        out_specs=pl.BlockSpec((1, tq, LANES), lambda bi, h, i: (bi, i, h)),
        out_shape=jax.ShapeDtypeStruct((b, s, hn * LANES), BF16),
        scratch_shapes=[pltpu.VMEM((2, tq, LANES), F32)] * 3,
        compiler_params=_params("parallel", "parallel", "arbitrary"),
        name="diff_attn",
    )(slopes, lam_vecs, qkv, qkv, qkv, subln_g)


def _out_ln_kernel(*refs, n_in):
    a_refs, w_refs = refs[:n_in], refs[n_in:2 * n_in]
    x_ref, g_ref, b_ref, o_ref = refs[2 * n_in:]
    y = None
    for a_ref, w_ref in zip(a_refs, w_refs):
        t = _dot(a_ref[...], w_ref[...])
        y = t if y is None else y + t
    o_ref[...] = _layer_norm(DN_ALPHA * x_ref[...] + y, g_ref[...], b_ref[...])


def _out_ln(acts, weights, x, g, b):
    m, d = x.shape
    tm = min(ROW_TILE, m)
    row = lambda i: (i, 0)
    const = lambda i: (0, 0)
    return pl.pallas_call(
        functools.partial(_out_ln_kernel, n_in=len(acts)),
        grid=(m // tm,),
        in_specs=[pl.BlockSpec((tm, a.shape[1]), row) for a in acts]
        + [pl.BlockSpec(w.shape, const) for w in weights]
        + [pl.BlockSpec((tm, d), row), pl.BlockSpec((1, d), const), pl.BlockSpec((1, d), const)],
        out_specs=pl.BlockSpec((tm, d), row),
        out_shape=jax.ShapeDtypeStruct((m, d), F32),
        compiler_params=_params("parallel"),
        name="out_proj_ln",
    )(*acts, *weights, x, g, b)


def _mlp_kernel(x_ref, wu_ref, wd_ref, g_ref, b_ref, o_ref, xb_ref, acc_ref):
    f = pl.program_id(1)

    @pl.when(f == 0)
    def _():
        xb_ref[...] = x_ref[...].astype(BF16)
        acc_ref[...] = jnp.zeros(acc_ref.shape, F32)

    hidden = jnp.maximum(_dot(xb_ref[...], wu_ref[...]), 0.0)
    acc_ref[...] += _dot((hidden * hidden).astype(BF16), wd_ref[...])

    @pl.when(f == pl.num_programs(1) - 1)
    def _():
        o_ref[...] = _layer_norm(DN_ALPHA * x_ref[...] + acc_ref[...], g_ref[...], b_ref[...])


def _mlp(x, w_up, w_down, g, b):
    m, d = x.shape
    ff = w_up.shape[1]
    tm, tf = min(ROW_TILE, m), min(FF_TILE, ff)
    return pl.pallas_call(
        _mlp_kernel,
        grid=(m // tm, ff // tf),
        in_specs=[pl.BlockSpec((tm, d), lambda i, f: (i, 0)),
                  pl.BlockSpec((d, tf), lambda i, f: (0, f)),
                  pl.BlockSpec((tf, d), lambda i, f: (f, 0)),
                  pl.BlockSpec((1, d), lambda i, f: (0, 0)), pl.BlockSpec((1, d), lambda i, f: (0, 0))],
        out_specs=pl.BlockSpec((tm, d), lambda i, f: (i, 0)),
        out_shape=jax.ShapeDtypeStruct((m, d), F32),
        scratch_shapes=[pltpu.VMEM((tm, d), BF16), pltpu.VMEM((tm, d), F32)],
        compiler_params=_params("parallel", "arbitrary"),
        name="mlp_ln",
    )(x, w_up, w_down, g, b)


def _pad_cols(w, width):
    return jnp.pad(w, ((0, 0), (0, width - w.shape[1])))


def _layer0_weights(w_in, w_uq, w_ukv, d_model):
    rank = d_model // 4
    kvw = NSA_GROUPS * HEAD_DIM
    o = np.cumsum([0, rank, rank, MLA_ROPE, NSA_HEADS * HEAD_DIM] + [kvw] * 6 + [3 * NSA_HEADS])
    seg = lambda i: w_in[:, o[i]:o[i + 1]]
    zeros = lambda n: jnp.zeros((w_in.shape[0], n), w_in.dtype)
    rope_slab = jnp.concatenate([zeros(MLA_NOPE), seg(2), zeros(LANES - MLA_NOPE - MLA_ROPE)], axis=1)
    w_a = jnp.concatenate([seg(0), seg(1), rope_slab, _pad_cols(seg(10), LANES)], axis=1)
    w_b = jnp.concatenate([seg(3), seg(6), seg(7), seg(8), seg(9)], axis=1)
    w_c = jnp.concatenate([seg(4), seg(5)], axis=1)
    wq = jnp.pad(w_uq.reshape(rank, MLA_HEADS, MLA_NOPE + MLA_ROPE),
                 ((0, 0), (0, 0), (0, LANES - MLA_NOPE - MLA_ROPE))).reshape(rank, MLA_HEADS * LANES)
    ukv = w_ukv.reshape(rank, MLA_HEADS, MLA_NOPE + HEAD_DIM)
    wk = jnp.pad(ukv[:, :, :MLA_NOPE], ((0, 0), (0, 0), (0, LANES - MLA_NOPE))).reshape(rank, MLA_HEADS * LANES)
    wv = ukv[:, :, MLA_NOPE:].reshape(rank, MLA_HEADS * HEAD_DIM)
    return [w.astype(BF16) for w in (w_a, w_b, w_c, wq, wk, wv)]


def _rope_tables(seq):
    inv = 1.0 / (ROPE_THETA ** (jnp.arange(0, MLA_ROPE, 2, dtype=F32) / MLA_ROPE))
    ang = jnp.arange(seq, dtype=F32)[:, None] * inv[None, :]
    cos, sin = jnp.cos(ang), jnp.sin(ang)
    half = MLA_ROPE // 2
    z = lambda n: jnp.zeros((seq, n), F32)
    tail = LANES - MLA_NOPE - MLA_ROPE
    c = jnp.concatenate([jnp.ones((seq, MLA_NOPE), F32), cos, cos, z(tail)], axis=1)
    s1 = jnp.concatenate([z(MLA_NOPE), -sin, z(half), z(tail)], axis=1)
    s2 = jnp.concatenate([z(MLA_NOPE), z(half), sin, z(tail)], axis=1)
    return c, s1, s2


def _compress_weights(pos_k, w1_k, w2_k, pos_v, w1_v, w2_v):
    eye = jnp.eye(2 * NSA_GROUPS, dtype=F32)
    halves = []
    for a in range(CMP_LEN // CMP_STRIDE):
        rows = slice(a * CMP_STRIDE * HEAD_DIM, (a + 1) * CMP_STRIDE * HEAD_DIM)
        wk = w1_k[rows].reshape(CMP_STRIDE, HEAD_DIM, HEAD_DIM)
        wv = w1_v[rows].reshape(CMP_STRIDE, HEAD_DIM, HEAD_DIM)
        per_slot = jnp.stack([wk, wk, wv, wv], axis=0)
        full = jnp.einsum('st,srdj->rsdtj', eye, per_slot)
        halves.append(full.reshape(CMP_STRIDE * 4 * HEAD_DIM, 4 * HEAD_DIM).astype(BF16))
    w2 = jnp.einsum('st,sdj->sdtj', eye, jnp.stack([w2_k, w2_k, w2_v, w2_v])).reshape(4 * HEAD_DIM, 4 * HEAD_DIM)
    pos = jnp.concatenate([pos_k, pos_k, pos_v, pos_v], axis=1)
    pos = pos.reshape(CMP_LEN // CMP_STRIDE, 1, CMP_STRIDE * 4 * HEAD_DIM)
    pos = jnp.broadcast_to(pos, (pos.shape[0], 8, pos.shape[2])).reshape(-1, pos.shape[2])
    return pos, halves[0], halves[1], w2.astype(BF16)


def _overlap_table(n_cmp_pad, n_cmp):
    c0 = np.arange(n_cmp_pad)[:, None] * CMP_STRIDE
    s0 = np.arange(LANES)[None, :] * SEL_LEN
    ov = np.maximum(np.minimum(c0 + CMP_LEN, s0 + SEL_LEN) - np.maximum(c0, s0), 0) / CMP_LEN
    ov = ov * (np.arange(n_cmp_pad)[:, None] < n_cmp)
    return jnp.asarray(ov, BF16)


def _block_expand_table(seq, tk):
    key_block = np.arange(seq) // SEL_LEN
    table = (np.arange(LANES)[:, None] == key_block[None, :]).astype(np.float32)
    return jnp.asarray(table.reshape(LANES, seq // tk, tk).transpose(1, 0, 2), BF16)


def _gate_expand_table():
    width = NSA_HEADS * HEAD_DIM
    table = np.zeros((LANES, 3 * width), np.float32)
    for h in range(NSA_HEADS):
        for branch in range(3):
            table[h * 3 + branch, branch * width + h * HEAD_DIM:branch * width + (h + 1) * HEAD_DIM] = 1.0
    return jnp.asarray(table, BF16)


def _alibi_slopes(n):
    return jnp.asarray(2.0 ** (-8.0 * np.arange(1, n + 1) / n), dtype=F32)


def _layer0_mixer(x2, b, s, w_in, q_norm, w_uq, kv_norm, w_ukv, pos_k, w1_k, w2_k, pos_v, w1_v, w2_v, w_out):
    d = x2.shape[1]
    rank = d // 4
    w_a, w_b, w_c, wq, wk, wv = _layer0_weights(w_in, w_uq, w_ukv, d)
    slab_a, slab_b, slab_c = _project(x2, [w_a, w_b, w_c], [F32, BF16, BF16])
    rope_c, rope_s1, rope_s2 = _rope_tables(s)
    q, k, v = _mla_prep(slab_a, q_norm.reshape(1, rank), kv_norm.reshape(1, rank), wq, wk, wv,
                        rope_c, rope_s1, rope_s2, s)
    o_mla = _mla_attn(q.reshape(b, s, -1), k.reshape(b, s, -1), v.reshape(b, s, -1))
    n_chunks = s // CMP_STRIDE
    n_cmp = (s - CMP_LEN) // CMP_STRIDE + 1
    pos, w1a, w1b, w2 = _compress_weights(pos_k, w1_k, w2_k, pos_v, w1_v, w2_v)
    cmp_kv = _compress(slab_c.reshape(b, n_chunks, CMP_STRIDE * slab_c.shape[1]), pos, w1a, w1b, w2, n_cmp)
    slab_b3 = slab_b.reshape(b, s, -1)
    o_cmp, o_win, sel_bias = _nsa_cmp_win(slab_b3, cmp_kv, _overlap_table(n_chunks, n_cmp), s)
    o_nsa = _nsa_sel(slab_b3, sel_bias, _block_expand_table(s, min(KV_TILE, s)), o_cmp, o_win, slab_a,
                     _gate_expand_table(), s, (2 * rank + LANES) // LANES)
    half = o_mla.shape[-1]
    w_out_b = w_out.astype(BF16)
    return [o_mla.reshape(b * s, half), o_nsa.reshape(b * s, -1)], [w_out_b[:half], w_out_b[half:]]


def _layer1_mixer(x2, b, s, w_qkv, lam_q1, lam_k1, lam_q2, lam_k2, subln_g, w_o, layer_idx):
    (qkv,) = _project(x2, [w_qkv.astype(BF16)], [BF16])
    lam_init = 0.8 - 0.6 * math.exp(-0.3 * layer_idx)
    lam_vecs = jnp.stack([lam_q1, lam_k1, lam_q2, lam_k2]).astype(F32)
    o = _diff_attn(qkv.reshape(b, s, -1), _alibi_slopes(DIFF_HEADS), lam_vecs, subln_g.reshape(1, -1), lam_init)
    return [o.reshape(b * s, -1)], [w_o.astype(BF16)]


def kernel(x, l0_w_in, l0_mla_q_norm, l0_mla_w_uq, l0_mla_kv_norm, l0_mla_w_ukv, l0_nsa_cmp_pos_k, l0_nsa_cmp_w1_k, l0_nsa_cmp_w2_k, l0_nsa_cmp_pos_v, l0_nsa_cmp_w1_v, l0_nsa_cmp_w2_v, l0_w_out, l0_ln_mix_g, l0_ln_mix_b, l0_w_up, l0_w_down, l0_ln_ffn_g, l0_ln_ffn_b, l1_w_qkv, l1_lam_q1, l1_lam_k1, l1_lam_q2, l1_lam_k2, l1_subln_g, l1_w_o, l1_ln_mix_g, l1_ln_mix_b, l1_w_up, l1_w_down, l1_ln_ffn_g, l1_ln_ffn_b):
    b, s, d = x.shape
    x2 = x.reshape(b * s, d)
    vec = lambda p: p.reshape(1, d)
    acts, weights = _layer0_mixer(x2, b, s, l0_w_in, l0_mla_q_norm, l0_mla_w_uq, l0_mla_kv_norm, l0_mla_w_ukv,
                                  l0_nsa_cmp_pos_k, l0_nsa_cmp_w1_k, l0_nsa_cmp_w2_k,
                                  l0_nsa_cmp_pos_v, l0_nsa_cmp_w1_v, l0_nsa_cmp_w2_v, l0_w_out)
    x2 = _out_ln(acts, weights, x2, vec(l0_ln_mix_g), vec(l0_ln_mix_b))
    x2 = _mlp(x2, l0_w_up.astype(BF16), l0_w_down.astype(BF16), vec(l0_ln_ffn_g), vec(l0_ln_ffn_b))
    acts, weights = _layer1_mixer(x2, b, s, l1_w_qkv, l1_lam_q1, l1_lam_k1, l1_lam_q2, l1_lam_k2,
                                  l1_subln_g, l1_w_o, 1)
    x2 = _out_ln(acts, weights, x2, vec(l1_ln_mix_g), vec(l1_ln_mix_b))
    x2 = _mlp(x2, l1_w_up.astype(BF16), l1_w_down.astype(BF16), vec(l1_ln_ffn_g), vec(l1_ln_ffn_b))
    return x2.reshape(b, s, d)
```

```python
import functools
import math

import jax
import jax.numpy as jnp
import numpy as np
from jax import lax
from jax.experimental import pallas as pl
from jax.experimental.pallas import tpu as pltpu

F32 = jnp.float32
BF16 = jnp.bfloat16

LANES = 128
SUBLANES = 8
BF16_ROWS = 16
MXU_DEPTH = 256
HEAD_DIM = 64
Q_TILE = 256
KV_TILE = 512
KEY_CHUNK = 64
ROW_TILE = 512
FF_TILE = 1024
VMEM_LIMIT = 56 * 1024 * 1024

NEG_INF = -1e30
LOG2E = math.log2(math.e)
LN_EPS = 1e-5
RMS_EPS = 1e-6
DEPTH = 2
DN_ALPHA = (2.0 * DEPTH) ** 0.25

MLA_HEADS = 8
MLA_NOPE = 64
MLA_ROPE = 32
ROPE_THETA = 10000.0
NSA_HEADS = 8
NSA_GROUPS = 2
NSA_HG = NSA_HEADS // NSA_GROUPS
CMP_LEN = 32
CMP_STRIDE = 16
SEL_LEN = 64
SEL_TOPK = 16
WINDOW = 512
FORCE_BONUS = 1e3
DIFF_HEADS = 8

POS_SPLIT = 16
FEATURE_ROWS = 16
BLOCK_LANE0 = 8
BLOCKS_PER_TILE = KV_TILE // SEL_LEN


def _params(*sem):
    return pltpu.CompilerParams(dimension_semantics=sem, vmem_limit_bytes=VMEM_LIMIT)


def _dot(a, b):
    return jnp.dot(a, b, preferred_element_type=F32)


def _dot_nt(a, b):
    return lax.dot_general(a, b, (((1,), (1,)), ((), ())), preferred_element_type=F32)


def _split_bf16(x):
    hi = x.astype(BF16)
    lo = (x - hi.astype(F32)).astype(BF16)
    return hi, lo


def _layer_norm(z, g, b):
    mu = jnp.mean(z, axis=-1, keepdims=True)
    zc = z - mu
    var = jnp.mean(zc * zc, axis=-1, keepdims=True)
    return zc * lax.rsqrt(var + LN_EPS) * g + b


def _rms_norm(z, g, eps):
    return z * lax.rsqrt(jnp.mean(z * z, axis=-1, keepdims=True) + eps) * g


def _lane_iota(shape):
    return lax.broadcasted_iota(jnp.int32, shape, 1)


def _keep_half(x, half):
    lane = _lane_iota(x.shape)
    keep = (lane < HEAD_DIM) if half == 0 else (lane >= HEAD_DIM)
    return jnp.where(keep, x, jnp.zeros_like(x))


def _move_head(slab, src_half, dst_half):
    if src_half != dst_half:
        slab = pltpu.roll(slab, HEAD_DIM, 1)
    return _keep_half(slab, dst_half)


def _store_transposed(o_ref, res):
    for c in range(res.shape[1] // LANES):
        cols = slice(c * LANES, (c + 1) * LANES)
        o_ref[0, 0, cols, :] = res[:, cols].T.astype(o_ref.dtype)


def _proj_kernel(x_ref, *refs, transposed):
    n_out = len(transposed)
    w_refs, o_refs = refs[:n_out], refs[n_out:]
    xb = x_ref[...].astype(BF16)
    for w_ref, o_ref, t in zip(w_refs, o_refs, transposed):
        res = _dot(xb, w_ref[...])
        if t:
            _store_transposed(o_ref, res)
        else:
            o_ref[...] = res.astype(o_ref.dtype)


def _transposed_out(b, seq, width, tm):
    per_seq = seq // tm
    spec = pl.BlockSpec((1, 1, width, tm), lambda i: (i // per_seq, i % per_seq, 0, 0))
    return spec, jax.ShapeDtypeStruct((b, per_seq, width, tm), BF16)


def _project(x, weights, out_dtypes, transposed, b, seq):
    m, k = x.shape
    tm = min(KV_TILE, seq)
    specs, shapes = [], []
    for w, dt, t in zip(weights, out_dtypes, transposed):
        if t:
            spec, shape = _transposed_out(b, seq, w.shape[1], tm)
        else:
            spec, shape = pl.BlockSpec((tm, w.shape[1]), lambda i: (i, 0)), jax.ShapeDtypeStruct((m, w.shape[1]), dt)
        specs.append(spec)
        shapes.append(shape)
    return pl.pallas_call(
        functools.partial(_proj_kernel, transposed=tuple(transposed)),
        grid=(m // tm,),
        in_specs=[pl.BlockSpec((tm, k), lambda i: (i, 0))]
        + [pl.BlockSpec(w.shape, lambda i: (0, 0)) for w in weights],
        out_specs=specs,
        out_shape=shapes,
        compiler_params=_params("parallel"),
        name="project",
    )(x, *weights)


def _rope_slab(slab, c, s1, s2):
    half = MLA_ROPE // 2
    up = pltpu.roll(slab, half, 1)
    down = pltpu.roll(slab, LANES - half, 1)
    return slab * c + down * s1 + up * s2


def _mla_prep_kernel(ql_ref, kvl_ref, kpe_ref, qg_ref, kvg_ref, wq_ref, wk_ref, wv_ref,
                     c_ref, s1_ref, s2_ref, q_ref, k_ref, vt_ref, *, q_scale):
    c, s1, s2 = c_ref[...], s1_ref[...], s2_ref[...]
    qn = _rms_norm(ql_ref[...], qg_ref[...], RMS_EPS).astype(BF16)
    kvn = _rms_norm(kvl_ref[...], kvg_ref[...], RMS_EPS).astype(BF16)
    q = _dot(qn, wq_ref[...])
    k = _dot(kvn, wk_ref[...])
    _store_transposed(vt_ref, _dot(kvn, wv_ref[...]))
    kpe = _rope_slab(kpe_ref[...], c, s1, s2)
    for h in range(MLA_HEADS):
        sl = slice(h * LANES, (h + 1) * LANES)
        q_ref[:, sl] = (_rope_slab(q[:, sl], c, s1, s2) * q_scale).astype(q_ref.dtype)
        k_ref[:, sl] = (k[:, sl] + kpe).astype(k_ref.dtype)


def _mla_prep(slab_a, q_gain, kv_gain, wq, wk, wv, rope_c, rope_s1, rope_s2, b, seq):
    m = slab_a.shape[0]
    tm = min(KV_TILE, seq)
    per_seq = seq // tm
    rank = q_gain.shape[1]
    row = lambda j: (lambda i: (i, j))
    tab = lambda i: (i % per_seq, 0)
    const = lambda i: (0, 0)
    hw = MLA_HEADS * LANES
    vt_spec, vt_shape = _transposed_out(b, seq, wv.shape[1], tm)
    return pl.pallas_call(
        functools.partial(_mla_prep_kernel, q_scale=float((MLA_NOPE + MLA_ROPE) ** -0.5 * LOG2E)),
        grid=(m // tm,),
        in_specs=[pl.BlockSpec((tm, rank), row(0)), pl.BlockSpec((tm, rank), row(1)),
                  pl.BlockSpec((tm, LANES), row(2 * rank // LANES)),
                  pl.BlockSpec((1, rank), const), pl.BlockSpec((1, rank), const),
                  pl.BlockSpec(wq.shape, const), pl.BlockSpec(wk.shape, const), pl.BlockSpec(wv.shape, const),
                  pl.BlockSpec((tm, LANES), tab), pl.BlockSpec((tm, LANES), tab), pl.BlockSpec((tm, LANES), tab)],
        out_specs=[pl.BlockSpec((tm, hw), row(0)), pl.BlockSpec((tm, hw), row(0)), vt_spec],
        out_shape=[jax.ShapeDtypeStruct((m, hw), BF16), jax.ShapeDtypeStruct((m, hw), BF16), vt_shape],
        compiler_params=_params("parallel"),
        name="mla_prep",
    )(slab_a, slab_a, slab_a, q_gain, kv_gain, wq, wk, wv, rope_c, rope_s1, rope_s2)


def _flash_scratch(n_streams, v_rows, tq, tk):
    scores = pltpu.VMEM((n_streams, tk, tq), F32)
    stat = pltpu.VMEM((n_streams, 1, tq), F32)
    probs = pltpu.VMEM((n_streams, tk, tq), BF16)
    return [scores, scores, stat, stat, probs, probs, stat, stat, pltpu.VMEM((n_streams, v_rows, tq), F32)]


def _chunk_rows(c):
    return slice(c * KEY_CHUNK, (c + 1) * KEY_CHUNK)


def _fold_rows(x):
    return x.reshape(x.shape[0] // SUBLANES, SUBLANES, x.shape[1])


def _flash_transposed(n_full, n_streams, q_start, tq, tk, key_operand, query_operand, values, offset, scratch):
    sa_ref, sb_ref, mxa_ref, mxb_ref, pa_ref, pb_ref, m_ref, l_ref, acc_ref = scratch
    n_chunks = tk // KEY_CHUNK
    for i in range(n_streams):
        m_ref[i] = jnp.full((1, tq), NEG_INF, F32)
        l_ref[i] = jnp.zeros((1, tq), F32)
        acc_ref[i] = jnp.zeros(acc_ref.shape[1:], F32)

    def column_max(s_ref, i):
        part = jnp.full((SUBLANES, tq), NEG_INF, F32)
        for c in range(n_chunks):
            part = jnp.maximum(part, jnp.max(_fold_rows(s_ref[i, _chunk_rows(c), :]), axis=0))
        return jnp.max(part, axis=0, keepdims=True)

    def stage1(j, s_ref, mx_ref):
        for i in range(n_streams):
            s_ref[i] = _dot(key_operand(j, i), query_operand(j, i))
            mx_ref[i] = column_max(s_ref, i)

    def stage2(j, s_ref, mx_ref, p_ref, diagonal):
        for i in range(n_streams):
            if diagonal:
                key_minus_query = (lax.broadcasted_iota(jnp.int32, (tk, tq), 0)
                                   - lax.broadcasted_iota(jnp.int32, (tk, tq), 1))
                s_ref[i] = jnp.where(key_minus_query <= q_start - j * tk, s_ref[i], NEG_INF)
                mx = column_max(s_ref, i)
            else:
                mx = mx_ref[i]
            off = offset(j, i)
            m_prev = m_ref[i]
            if off is None:
                m_next = jnp.maximum(m_prev, mx)
                shift = m_next
            else:
                m_next = jnp.maximum(m_prev, mx + off)
                shift = m_next - off
            alpha = jnp.exp2(m_prev - m_next)
            part = jnp.zeros((SUBLANES, tq), F32)
            for c in range(n_chunks):
                p = jnp.exp2(s_ref[i, _chunk_rows(c), :] - shift)
                part = part + jnp.sum(_fold_rows(p), axis=0)
                p_ref[i, _chunk_rows(c), :] = p.astype(BF16)
            l_ref[i] = alpha * l_ref[i] + jnp.sum(part, axis=0, keepdims=True)
            m_ref[i] = m_next
            acc_ref[i] = alpha * acc_ref[i] + _dot(values(j, i), p_ref[i])

    stage1(0, sa_ref, mxa_ref)

    def pair(t, carry):
        j0 = 2 * t
        stage1(j0 + 1, sb_ref, mxb_ref)
        stage2(j0, sa_ref, mxa_ref, pa_ref, False)
        stage1(j0 + 2, sa_ref, mxa_ref)
        stage2(j0 + 1, sb_ref, mxb_ref, pb_ref, False)
        return carry

    lax.fori_loop(0, n_full // 2, pair, 0)

    @pl.when(n_full % 2 == 1)
    def _():
        stage1(n_full, sb_ref, mxb_ref)
        stage2(n_full - 1, sa_ref, mxa_ref, pa_ref, False)
        stage2(n_full, sb_ref, mxb_ref, pb_ref, True)

    @pl.when(n_full % 2 == 0)
    def _():
        stage2(n_full, sa_ref, mxa_ref, pa_ref, True)


def _flash_result(scratch, i):
    l_ref, acc_ref = scratch[-2], scratch[-1]
    return acc_ref[i] / l_ref[i]


def _kv_rows(j, tk):
    return pl.ds(pl.multiple_of(j * tk, tk), tk)


def _transposed_bf16(x):
    return x.astype(F32).T.astype(BF16)


def _alibi_rows(coef, tq):
    c = jnp.zeros((1, tq), F32) + coef
    hi = c.astype(BF16).astype(F32)
    rest = c - hi
    mid = rest.astype(BF16).astype(F32)
    lo = rest - mid
    zero = jnp.zeros((1, tq), F32)
    return jnp.concatenate([POS_SPLIT * hi, POS_SPLIT * mid, POS_SPLIT * lo, hi, mid, lo, zero, zero], axis=0)


def _augmented_query(q_t, feature_rows):
    tq = q_t.shape[1]
    pad = jnp.zeros((MXU_DEPTH - LANES - FEATURE_ROWS, tq), BF16)
    return jnp.concatenate([q_t, feature_rows.astype(BF16), pad], axis=0)


def _mla_attn_kernel(q_ref, k_ref, vt_ref, o_ref, *scratch, tq, tk):
    q_start = pl.program_id(2) * tq
    queries = [_transposed_bf16(q_ref[0, :, hh * LANES:(hh + 1) * LANES]) for hh in range(2)]
    _flash_transposed(
        q_start // tk, 2, q_start, tq, tk,
        lambda j, i: k_ref[0, _kv_rows(j, tk), i * LANES:(i + 1) * LANES],
        lambda j, i: queries[i],
        lambda j, i: vt_ref[0, j, i * HEAD_DIM:(i + 1) * HEAD_DIM, :],
        lambda j, i: None, scratch)
    o_t = jnp.concatenate([_flash_result(scratch, 0), _flash_result(scratch, 1)], axis=0)
    o_ref[0] = o_t.T.astype(o_ref.dtype)


def _mla_attn(q, k, vt):
    b, s, _ = q.shape
    tq, tk = min(Q_TILE, s), min(KV_TILE, s)
    pairs = MLA_HEADS // 2
    return pl.pallas_call(
        functools.partial(_mla_attn_kernel, tq=tq, tk=tk),
        grid=(b, pairs, s // tq),
        in_specs=[pl.BlockSpec((1, tq, 2 * LANES), lambda bi, p, i: (bi, i, p)),
                  pl.BlockSpec((1, s, 2 * LANES), lambda bi, p, i: (bi, 0, p)),
                  pl.BlockSpec((1, s // tk, LANES, tk), lambda bi, p, i: (bi, 0, p, 0))],
        out_specs=pl.BlockSpec((1, tq, LANES), lambda bi, p, i: (bi, i, p)),
        out_shape=jax.ShapeDtypeStruct((b, s, pairs * LANES), BF16),
        scratch_shapes=_flash_scratch(2, HEAD_DIM, tq, tk),
        compiler_params=_params("parallel", "parallel", "arbitrary"),
        name="mla_attn",
    )(q, k, vt)


def _gelu_tanh(x):
    return 0.5 * x * (1.0 + jnp.tanh(math.sqrt(2.0 / math.pi) * (x + 0.044715 * (x * x * x))))


def _compress_kernel(x_ref, pos_ref, w1a_ref, w1b_ref, w2_ref, o_ref, *, n_real):
    x = x_ref[0]
    n = x.shape[0]
    first = _dot(x, w1a_ref[...])
    second = _dot(x, w1b_ref[...])
    pos_hi, pos_lo = _split_bf16(pos_ref[...])
    bias = (_dot(pos_hi[:8], w1a_ref[...]) + _dot(pos_lo[:8], w1a_ref[...])
            + _dot(pos_hi[8:], w1b_ref[...]) + _dot(pos_lo[8:], w1b_ref[...]))[:1]
    pre = first + pltpu.roll(second, n - 1, 0) + bias
    out = _dot(_gelu_tanh(pre).astype(BF16), w2_ref[...])
    real = lax.broadcasted_iota(jnp.int32, out.shape, 0) < n_real
    o_ref[0] = jnp.where(real, out, 0.0).astype(o_ref.dtype)


def _compress(x_chunks, pos_exp, w1a, w1b, w2, n_real):
    b, n, width = x_chunks.shape
    const = lambda bi: (0, 0)
    return pl.pallas_call(
        functools.partial(_compress_kernel, n_real=n_real),
        grid=(b,),
        in_specs=[pl.BlockSpec((1, n, width), lambda bi: (bi, 0, 0)),
                  pl.BlockSpec(pos_exp.shape, const), pl.BlockSpec(w1a.shape, const),
                  pl.BlockSpec(w1b.shape, const), pl.BlockSpec(w2.shape, const)],
        out_specs=pl.BlockSpec((1, n, w2.shape[1]), lambda bi: (bi, 0, 0)),
        out_shape=jax.ShapeDtypeStruct((b, n, w2.shape[1]), BF16),
        compiler_params=_params("parallel"),
        name="nsa_compress",
    )(x_chunks, pos_exp, w1a, w1b, w2)


def _nsa_head_slope(h):
    return float(2.0 ** (-8.0 * (h + 1) / NSA_HEADS))


def _nsa_queries(q_ref, g, scale):
    out = []
    for hg in range(NSA_HG):
        h = g * NSA_HG + hg
        slab = q_ref[0, :, (h // 2) * LANES:(h // 2 + 1) * LANES].astype(F32) * scale
        out.append(_move_head(slab, h % 2, g))
    return out


def _place_heads(results, g):
    slabs = []
    for pair in range(NSA_HG // 2):
        even = _move_head(results[2 * pair], g, 0)
        odd = _move_head(results[2 * pair + 1], g, 1)
        slabs.append(even + odd)
    return slabs


def _select_blocks(imp, q_pos, n_lanes):
    tq = imp.shape[0]
    blk = _lane_iota((tq, n_lanes))
    cur = q_pos // SEL_LEN
    forced = jnp.where(blk == 0, 1.0, 0.0) + jnp.where(blk == cur, 1.0, 0.0) + jnp.where(blk == cur - 1, 1.0, 0.0)
    forced = jnp.minimum(forced, 1.0)
    val = jnp.where(blk <= cur, imp + FORCE_BONUS * forced, NEG_INF)
    val_t = val.T
    blk_t = lax.broadcasted_iota(jnp.int32, val_t.shape, 0)
    chosen = jnp.zeros(val_t.shape, F32)
    for _ in range(SEL_TOPK):
        top = jnp.max(val_t, axis=0, keepdims=True)
        first = jnp.min(jnp.where(val_t == top, blk_t, n_lanes), axis=0, keepdims=True)
        hit = blk_t == first
        chosen = jnp.where(hit, 1.0, chosen)
        val_t = jnp.where(hit, -jnp.inf, val_t)
    return jnp.where(chosen > 0.5, 0.0, NEG_INF)


def _nsa_cmp_win_kernel(q_ref, kc_ref, vc_ref, kw_ref, vw_ref, ov_ref, oc_ref, ow_ref, sel_ref, *, tq, seq):
    qi = pl.program_id(1)
    q_start = qi * tq
    n_cmp = kc_ref.shape[1]
    row = lax.broadcasted_iota(jnp.int32, (tq, 1), 0)
    q_pos = q_start + row
    cmp_end = lax.broadcasted_iota(jnp.int32, (tq, n_cmp), 1) * CMP_STRIDE + (CMP_LEN - 1)
    cmp_dist = (q_pos - cmp_end).astype(F32)
    cmp_ok = cmp_dist >= 0.0
    span = WINDOW + tq
    w_start = jnp.maximum(q_start - WINDOW, 0)
    w_rows = pl.ds(pl.multiple_of(w_start, tq), span)
    win_dist = (q_pos - (w_start + lax.broadcasted_iota(jnp.int32, (tq, span), 1))).astype(F32)
    win_ok = jnp.abs(win_dist - (WINDOW - 1) * 0.5) <= (WINDOW - 1) * 0.5
    kc, vc = kc_ref[0], vc_ref[0]
    kw, vw = kw_ref[0, w_rows, :], vw_ref[0, w_rows, :]
    for g in range(NSA_GROUPS):
        queries = [q.astype(BF16) for q in _nsa_queries(q_ref, g, HEAD_DIM ** -0.5)]
        p_sum = jnp.zeros((tq, n_cmp), F32)
        res_c, res_w = [], []
        for hg in range(NSA_HG):
            slope = _nsa_head_slope(g * NSA_HG + hg)
            s = jnp.where(cmp_ok, _dot_nt(queries[hg], kc) - slope * cmp_dist, NEG_INF)
            p = jnp.where(cmp_ok, jnp.exp(s - jnp.max(s, axis=1, keepdims=True)), 0.0)
            p = p / jnp.maximum(jnp.sum(p, axis=1, keepdims=True), 1e-30)
            p_sum = p_sum + p
            res_c.append(_dot(p.astype(BF16), vc))
            s = jnp.where(win_ok, _dot_nt(queries[hg], kw) - slope * win_dist, NEG_INF)
            p = jnp.exp(s - jnp.max(s, axis=1, keepdims=True))
            p = p / jnp.sum(p, axis=1, keepdims=True)
            res_w.append(_dot(p.astype(BF16), vw))
        for pair, (slab_c, slab_w) in enumerate(zip(_place_heads(res_c, g), _place_heads(res_w, g))):
            cols = slice((g * 2 + pair) * LANES, (g * 2 + pair + 1) * LANES)
            oc_ref[0, :, cols] = slab_c
            ow_ref[0, :, cols] = slab_w
        p_hi, p_lo = _split_bf16(p_sum)
        imp = _dot(p_hi, ov_ref[...]) + _dot(p_lo, ov_ref[...])
        sel_ref[0, g * LANES:(g + 1) * LANES, :] = _select_blocks(imp, q_pos, LANES)


def _nsa_cmp_win(slab_b, cmp_kv, overlap, seq):
    b = slab_b.shape[0]
    tq = min(Q_TILE, seq)
    n_cmp = cmp_kv.shape[1]
    qw = NSA_HEADS * HEAD_DIM
    base = qw // LANES
    return pl.pallas_call(
        functools.partial(_nsa_cmp_win_kernel, tq=tq, seq=seq),
        grid=(b, seq // tq),
        in_specs=[pl.BlockSpec((1, tq, qw), lambda bi, i: (bi, i, 0)),
                  pl.BlockSpec((1, n_cmp, LANES), lambda bi, i: (bi, 0, 0)),
                  pl.BlockSpec((1, n_cmp, LANES), lambda bi, i: (bi, 0, 1)),
                  pl.BlockSpec((1, seq, LANES), lambda bi, i: (bi, 0, base + 1)),
                  pl.BlockSpec((1, seq, LANES), lambda bi, i: (bi, 0, base + 2)),
                  pl.BlockSpec(overlap.shape, lambda bi, i: (0, 0))],
        out_specs=[pl.BlockSpec((1, tq, qw), lambda bi, i: (bi, i, 0)),
                   pl.BlockSpec((1, tq, qw), lambda bi, i: (bi, i, 0)),
                   pl.BlockSpec((1, NSA_GROUPS * LANES, tq), lambda bi, i: (bi, 0, i))],
        out_shape=[jax.ShapeDtypeStruct((b, seq, qw), F32), jax.ShapeDtypeStruct((b, seq, qw), F32),
                   jax.ShapeDtypeStruct((b, NSA_GROUPS * LANES, seq), F32)],
        compiler_params=_params("parallel", "arbitrary"),
        name="nsa_cmp_win",
    )(slab_b, cmp_kv, cmp_kv, slab_b, slab_b, overlap)


def _nsa_sel_kernel(q_ref, k_ref, vt_ref, feat_ref, sel_ref, oc_ref, ow_ref, gate_ref, gx_ref, o_ref,
                    *scratch, tq, tk):
    q_start = pl.program_id(1) * tq
    n_full = q_start // tk
    out_slabs = []
    for g in range(NSA_GROUPS):
        queries = [_transposed_bf16(q) for q in _nsa_queries(q_ref, g, HEAD_DIM ** -0.5 * LOG2E)]
        coefs = [_nsa_head_slope(g * NSA_HG + hg) * LOG2E for hg in range(NSA_HG)]
        alibi = [_alibi_rows(c, tq) for c in coefs]

        def key_operand(j, i):
            return jnp.concatenate([k_ref[0, _kv_rows(j, tk), :], feat_ref[...]], axis=1)

        def query_operand(j, i, g=g, queries=queries, alibi=alibi):
            first_block = pl.multiple_of(g * LANES + j * BLOCKS_PER_TILE, BLOCKS_PER_TILE)
            blocks = sel_ref[0, pl.ds(first_block, BLOCKS_PER_TILE), :]
            return _augmented_query(queries[i], jnp.concatenate([alibi[i], blocks], axis=0))

        _flash_transposed(
            n_full, NSA_HG, q_start, tq, tk, key_operand, query_operand,
            lambda j, i, g=g: vt_ref[0, j, g * HEAD_DIM:(g + 1) * HEAD_DIM, :],
            lambda j, i, coefs=coefs: coefs[i] * (j * tk - q_start).astype(F32), scratch)
        heads = [_flash_result(scratch, hg) for hg in range(NSA_HG)]
        for pair in range(NSA_HG // 2):
            out_slabs.append(jnp.concatenate(heads[2 * pair:2 * pair + 2], axis=0).T)
    gates = jax.nn.sigmoid(gate_ref[...])
    g_hi, g_lo = _split_bf16(gates)
    width = NSA_HEADS * HEAD_DIM
    for i, o_sel in enumerate(out_slabs):
        mixed = None
        for branch, o_branch in enumerate((oc_ref[0, :, i * LANES:(i + 1) * LANES], o_sel,
                                           ow_ref[0, :, i * LANES:(i + 1) * LANES])):
            gx = gx_ref[:, branch * width + i * LANES:branch * width + (i + 1) * LANES]
            term = (_dot(g_hi, gx) + _dot(g_lo, gx)) * o_branch
            mixed = term if mixed is None else mixed + term
        o_ref[0, :, i * LANES:(i + 1) * LANES] = mixed.astype(o_ref.dtype)


def _nsa_sel(slab_b, vt, key_features, sel_bias_t, o_cmp, o_win, slab_a, gate_expand, seq, gate_col_block):
    b = slab_b.shape[0]
    tq, tk = min(Q_TILE, seq), min(KV_TILE, seq)
    qw = NSA_HEADS * HEAD_DIM
    base = qw // LANES
    per_seq = seq // tq
    tile = lambda bi, i: (bi, i, 0)
    return pl.pallas_call(
        functools.partial(_nsa_sel_kernel, tq=tq, tk=tk),
        grid=(b, seq // tq),
        in_specs=[pl.BlockSpec((1, tq, qw), tile),
                  pl.BlockSpec((1, seq, LANES), lambda bi, i: (bi, 0, base)),
                  pl.BlockSpec((1, seq // tk, LANES, tk), lambda bi, i: (bi, 0, 0, 0)),
                  pl.BlockSpec(key_features.shape, lambda bi, i: (0, 0)),
                  pl.BlockSpec((1, NSA_GROUPS * LANES, tq), lambda bi, i: (bi, 0, i)),
                  pl.BlockSpec((1, tq, qw), tile), pl.BlockSpec((1, tq, qw), tile),
                  pl.BlockSpec((tq, LANES), lambda bi, i: (bi * per_seq + i, gate_col_block)),
                  pl.BlockSpec(gate_expand.shape, lambda bi, i: (0, 0))],
        out_specs=pl.BlockSpec((1, tq, qw), tile),
        out_shape=jax.ShapeDtypeStruct((b, seq, qw), BF16),
        scratch_shapes=_flash_scratch(NSA_HG, HEAD_DIM, tq, tk),
        compiler_params=_params("parallel", "arbitrary"),
        name="nsa_sel",
    )(slab_b, slab_b, vt, key_features, sel_bias_t, o_cmp, o_win, slab_a, gate_expand)


def _diff_attn_kernel(slope_ref, lam_ref, q_ref, k_ref, vt_ref, feat_ref, g_ref, o_ref, *scratch,
                      tq, tk, lam_init):
    h = pl.program_id(1)
    q_start = pl.program_id(2) * tq
    coef = slope_ref[h] * LOG2E
    q = q_ref[0].astype(F32) * (HEAD_DIM ** -0.5 * LOG2E)
    features = jnp.concatenate([_alibi_rows(coef, tq), jnp.zeros((FEATURE_ROWS - SUBLANES, tq), F32)], axis=0)
    queries = [_augmented_query(_transposed_bf16(_keep_half(q, half)), features) for half in range(2)]
    _flash_transposed(
        q_start // tk, 2, q_start, tq, tk,
        lambda j, i: jnp.concatenate([k_ref[0, _kv_rows(j, tk), :], feat_ref[...]], axis=1),
        lambda j, i: queries[i],
        lambda j, i: vt_ref[0, j],
        lambda j, i: coef * (j * tk - q_start).astype(F32), scratch)
    lam_vec = lam_ref[...]
    lam = (jnp.exp(jnp.sum(lam_vec[0:1] * lam_vec[1:2], axis=1, keepdims=True))
           - jnp.exp(jnp.sum(lam_vec[2:3] * lam_vec[3:4], axis=1, keepdims=True)) + lam_init)
    o = (_flash_result(scratch, 0) - lam * _flash_result(scratch, 1)).T
    o_ref[0] = (_rms_norm(o, g_ref[...], RMS_EPS) * (1.0 - lam_init)).astype(o_ref.dtype)


def _diff_attn(qk, vt, key_features, slopes, lam_vecs, subln_g, lam_init):
    b, s, _ = qk.shape
    tq, tk = min(Q_TILE, s), min(KV_TILE, s)
    hn = DIFF_HEADS
    smem = pl.BlockSpec(memory_space=pltpu.SMEM)
    return pl.pallas_call(
        functools.partial(_diff_attn_kernel, tq=tq, tk=tk, lam_init=lam_init),
        grid=(b, hn, s // tq),
        in_specs=[smem, pl.BlockSpec(lam_vecs.shape, lambda bi, h, i: (0, 0)),
                  pl.BlockSpec((1, tq, LANES), lambda bi, h, i: (bi, i, h)),
                  pl.BlockSpec((1, s, LANES), lambda bi, h, i: (bi, 0, hn + h)),
                  pl.BlockSpec((1, s // tk, LANES, tk), lambda bi, h, i: (bi, 0, h, 0)),
                  pl.BlockSpec(key_features.shape, lambda bi, h, i: (0, 0)),
                  pl.BlockSpec((1, LANES), lambda bi, h, i: (0, 0))],
        out_specs=pl.BlockSpec((1, tq, LANES), lambda bi, h, i: (bi, i, h)),
        out_shape=jax.ShapeDtypeStruct((b, s, hn * LANES), BF16),
        scratch_shapes=_flash_scratch(2, LANES, tq, tk),
        compiler_params=_params("parallel", "parallel", "arbitrary"),
        name="diff_attn",
    )(slopes, lam_vecs, qk, qk, vt, key_features, subln_g)


def _out_ln_kernel(*refs, n_in):
    a_refs, w_refs = refs[:n_in], refs[n_in:2 * n_in]
    x_ref, g_ref, b_ref, o_ref = refs[2 * n_in:]
    y = None
    for a_ref, w_ref in zip(a_refs, w_refs):
        t = _dot(a_ref[...], w_ref[...])
        y = t if y is None else y + t
    o_ref[...] = _layer_norm(DN_ALPHA * x_ref[...] + y, g_ref[...], b_ref[...])


def _out_ln(acts, weights, x, g, b):
    m, d = x.shape
    tm = min(ROW_TILE, m)
    row = lambda i: (i, 0)
    const = lambda i: (0, 0)
    return pl.pallas_call(
        functools.partial(_out_ln_kernel, n_in=len(acts)),
        grid=(m // tm,),
        in_specs=[pl.BlockSpec((tm, a.shape[1]), row) for a in acts]
        + [pl.BlockSpec(w.shape, const) for w in weights]
        + [pl.BlockSpec((tm, d), row), pl.BlockSpec((1, d), const), pl.BlockSpec((1, d), const)],
        out_specs=pl.BlockSpec((tm, d), row),
        out_shape=jax.ShapeDtypeStruct((m, d), F32),
        compiler_params=_params("parallel"),
        name="out_proj_ln",
    )(*acts, *weights, x, g, b)


def _mlp_kernel(x_ref, wu_ref, wd_ref, g_ref, b_ref, o_ref, xb_ref, acc_ref):
    f = pl.program_id(1)

    @pl.when(f == 0)
    def _():
        xb_ref[...] = x_ref[...].astype(BF16)
        acc_ref[...] = jnp.zeros(acc_ref.shape, F32)

    hidden = jnp.maximum(_dot(xb_ref[...], wu_ref[...]), 0.0)
    acc_ref[...] += _dot((hidden * hidden).astype(BF16), wd_ref[...])

    @pl.when(f == pl.num_programs(1) - 1)
    def _():
        o_ref[...] = _layer_norm(DN_ALPHA * x_ref[...] + acc_ref[...], g_ref[...], b_ref[...])


def _mlp(x, w_up, w_down, g, b):
    m, d = x.shape
    ff = w_up.shape[1]
    tm, tf = min(ROW_TILE, m), min(FF_TILE, ff)
    return pl.pallas_call(
        _mlp_kernel,
        grid=(m // tm, ff // tf),
        in_specs=[pl.BlockSpec((tm, d), lambda i, f: (i, 0)),
                  pl.BlockSpec((d, tf), lambda i, f: (0, f)),
                  pl.BlockSpec((tf, d), lambda i, f: (f, 0)),
                  pl.BlockSpec((1, d), lambda i, f: (0, 0)), pl.BlockSpec((1, d), lambda i, f: (0, 0))],
        out_specs=pl.BlockSpec((tm, d), lambda i, f: (i, 0)),
        out_shape=jax.ShapeDtypeStruct((m, d), F32),
        scratch_shapes=[pltpu.VMEM((tm, d), BF16), pltpu.VMEM((tm, d), F32)],
        compiler_params=_params("parallel", "arbitrary"),
        name="mlp_ln",
    )(x, w_up, w_down, g, b)


def _pad_cols(w, width):
    return jnp.pad(w, ((0, 0), (0, width - w.shape[1])))


def _layer0_weights(w_in, w_uq, w_ukv, d_model):
    rank = d_model // 4
    kvw = NSA_GROUPS * HEAD_DIM
    o = np.cumsum([0, rank, rank, MLA_ROPE, NSA_HEADS * HEAD_DIM] + [kvw] * 6 + [3 * NSA_HEADS])
    seg = lambda i: w_in[:, o[i]:o[i + 1]]
    zeros = lambda n: jnp.zeros((w_in.shape[0], n), w_in.dtype)
    rope_slab = jnp.concatenate([zeros(MLA_NOPE), seg(2), zeros(LANES - MLA_NOPE - MLA_ROPE)], axis=1)
    w_a = jnp.concatenate([seg(0), seg(1), rope_slab, _pad_cols(seg(10), LANES)], axis=1)
    w_b = jnp.concatenate([seg(3), seg(6), seg(8), seg(9)], axis=1)
    w_c = jnp.concatenate([seg(4), seg(5)], axis=1)
    w_vs = seg(7)
    wq = jnp.pad(w_uq.reshape(rank, MLA_HEADS, MLA_NOPE + MLA_ROPE),
                 ((0, 0), (0, 0), (0, LANES - MLA_NOPE - MLA_ROPE))).reshape(rank, MLA_HEADS * LANES)
    ukv = w_ukv.reshape(rank, MLA_HEADS, MLA_NOPE + HEAD_DIM)
    wk = jnp.pad(ukv[:, :, :MLA_NOPE], ((0, 0), (0, 0), (0, LANES - MLA_NOPE))).reshape(rank, MLA_HEADS * LANES)
    wv = ukv[:, :, MLA_NOPE:].reshape(rank, MLA_HEADS * HEAD_DIM)
    return [w.astype(BF16) for w in (w_a, w_b, w_c, w_vs, wq, wk, wv)]


def _rope_tables(seq):
    inv = 1.0 / (ROPE_THETA ** (jnp.arange(0, MLA_ROPE, 2, dtype=F32) / MLA_ROPE))
    ang = jnp.arange(seq, dtype=F32)[:, None] * inv[None, :]
    cos, sin = jnp.cos(ang), jnp.sin(ang)
    half = MLA_ROPE // 2
    z = lambda n: jnp.zeros((seq, n), F32)
    tail = LANES - MLA_NOPE - MLA_ROPE
    c = jnp.concatenate([jnp.ones((seq, MLA_NOPE), F32), cos, cos, z(tail)], axis=1)
    s1 = jnp.concatenate([z(MLA_NOPE), -sin, z(half), z(tail)], axis=1)
    s2 = jnp.concatenate([z(MLA_NOPE), z(half), sin, z(tail)], axis=1)
    return c, s1, s2


def _compress_weights(pos_k, w1_k, w2_k, pos_v, w1_v, w2_v):
    eye = jnp.eye(2 * NSA_GROUPS, dtype=F32)
    halves = []
    for a in range(CMP_LEN // CMP_STRIDE):
        rows = slice(a * CMP_STRIDE * HEAD_DIM, (a + 1) * CMP_STRIDE * HEAD_DIM)
        wk = w1_k[rows].reshape(CMP_STRIDE, HEAD_DIM, HEAD_DIM)
        wv = w1_v[rows].reshape(CMP_STRIDE, HEAD_DIM, HEAD_DIM)
        per_slot = jnp.stack([wk, wk, wv, wv], axis=0)
        full = jnp.einsum('st,srdj->rsdtj', eye, per_slot)
        halves.append(full.reshape(CMP_STRIDE * 4 * HEAD_DIM, 4 * HEAD_DIM).astype(BF16))
    w2 = jnp.einsum('st,sdj->sdtj', eye, jnp.stack([w2_k, w2_k, w2_v, w2_v])).reshape(4 * HEAD_DIM, 4 * HEAD_DIM)
    pos = jnp.concatenate([pos_k, pos_k, pos_v, pos_v], axis=1)
    pos = pos.reshape(CMP_LEN // CMP_STRIDE, 1, CMP_STRIDE * 4 * HEAD_DIM)
    pos = jnp.broadcast_to(pos, (pos.shape[0], 8, pos.shape[2])).reshape(-1, pos.shape[2])
    return pos, halves[0], halves[1], w2.astype(BF16)


def _overlap_table(n_cmp_pad, n_cmp):
    c0 = np.arange(n_cmp_pad)[:, None] * CMP_STRIDE
    s0 = np.arange(LANES)[None, :] * SEL_LEN
    ov = np.maximum(np.minimum(c0 + CMP_LEN, s0 + SEL_LEN) - np.maximum(c0, s0), 0) / CMP_LEN
    ov = ov * (np.arange(n_cmp_pad)[:, None] < n_cmp)
    return jnp.asarray(ov, BF16)


def _key_feature_table(tk):
    c = np.arange(tk)
    table = np.zeros((tk, LANES), np.float32)
    table[:, 0:3] = (c // POS_SPLIT)[:, None]
    table[:, 3:6] = (c % POS_SPLIT)[:, None]
    table[c, BLOCK_LANE0 + c // SEL_LEN] = 1.0
    return jnp.asarray(table, BF16)


def _gate_expand_table():
    width = NSA_HEADS * HEAD_DIM
    table = np.zeros((LANES, 3 * width), np.float32)
    for h in range(NSA_HEADS):
        for branch in range(3):
            table[h * 3 + branch, branch * width + h * HEAD_DIM:branch * width + (h + 1) * HEAD_DIM] = 1.0
    return jnp.asarray(table, BF16)


def _alibi_slopes(n):
    return jnp.asarray(2.0 ** (-8.0 * np.arange(1, n + 1) / n), dtype=F32)


def _layer0_mixer(x2, b, s, w_in, q_norm, w_uq, kv_norm, w_ukv, pos_k, w1_k, w2_k, pos_v, w1_v, w2_v, w_out):
    d = x2.shape[1]
    rank = d // 4
    w_a, w_b, w_c, w_vs, wq, wk, wv = _layer0_weights(w_in, w_uq, w_ukv, d)
    slab_a, slab_b, slab_c, vt_sel = _project(x2, [w_a, w_b, w_c, w_vs], [F32, BF16, BF16, BF16],
                                              [False, False, False, True], b, s)
    rope_c, rope_s1, rope_s2 = _rope_tables(s)
    q, k, vt = _mla_prep(slab_a, q_norm.reshape(1, rank), kv_norm.reshape(1, rank), wq, wk, wv,
                         rope_c, rope_s1, rope_s2, b, s)
    o_mla = _mla_attn(q.reshape(b, s, -1), k.reshape(b, s, -1), vt)
    n_chunks = s // CMP_STRIDE
    n_cmp = (s - CMP_LEN) // CMP_STRIDE + 1
    pos, w1a, w1b, w2 = _compress_weights(pos_k, w1_k, w2_k, pos_v, w1_v, w2_v)
    cmp_kv = _compress(slab_c.reshape(b, n_chunks, CMP_STRIDE * slab_c.shape[1]), pos, w1a, w1b, w2, n_cmp)
    slab_b3 = slab_b.reshape(b, s, -1)
    o_cmp, o_win, sel_bias_t = _nsa_cmp_win(slab_b3, cmp_kv, _overlap_table(n_chunks, n_cmp), s)
    o_nsa = _nsa_sel(slab_b3, vt_sel, _key_feature_table(min(KV_TILE, s)), sel_bias_t, o_cmp, o_win, slab_a,
                     _gate_expand_table(), s, (2 * rank + LANES) // LANES)
    half = o_mla.shape[-1]
    w_out_b = w_out.astype(BF16)
    return [o_mla.reshape(b * s, half), o_nsa.reshape(b * s, -1)], [w_out_b[:half], w_out_b[half:]]


def _layer1_mixer(x2, b, s, w_qkv, lam_q1, lam_k1, lam_q2, lam_k2, subln_g, w_o, layer_idx):
    d = x2.shape[1]
    w = w_qkv.astype(BF16)
    qk, vt = _project(x2, [w[:, :2 * d], w[:, 2 * d:]], [BF16, BF16], [False, True], b, s)
    lam_init = 0.8 - 0.6 * math.exp(-0.3 * layer_idx)
    lam_vecs = jnp.stack([lam_q1, lam_k1, lam_q2, lam_k2]).astype(F32)
    o = _diff_attn(qk.reshape(b, s, -1), vt, _key_feature_table(min(KV_TILE, s)), _alibi_slopes(DIFF_HEADS),
                   lam_vecs, subln_g.reshape(1, -1), lam_init)
    return [o.reshape(b * s, -1)], [w_o.astype(BF16)]


def kernel(x, l0_w_in, l0_mla_q_norm, l0_mla_w_uq, l0_mla_kv_norm, l0_mla_w_ukv, l0_nsa_cmp_pos_k, l0_nsa_cmp_w1_k, l0_nsa_cmp_w2_k, l0_nsa_cmp_pos_v, l0_nsa_cmp_w1_v, l0_nsa_cmp_w2_v, l0_w_out, l0_ln_mix_g, l0_ln_mix_b, l0_w_up, l0_w_down, l0_ln_ffn_g, l0_ln_ffn_b, l1_w_qkv, l1_lam_q1, l1_lam_k1, l1_lam_q2, l1_lam_k2, l1_subln_g, l1_w_o, l1_ln_mix_g, l1_ln_mix_b, l1_w_up, l1_w_down, l1_ln_ffn_g, l1_ln_ffn_b):
    b, s, d = x.shape
    x2 = x.reshape(b * s, d)
    vec = lambda p: p.reshape(1, d)
    acts, weights = _layer0_mixer(x2, b, s, l0_w_in, l0_mla_q_norm, l0_mla_w_uq, l0_mla_kv_norm, l0_mla_w_ukv,
                                  l0_nsa_cmp_pos_k, l0_nsa_cmp_w1_k, l0_nsa_cmp_w2_k,
                                  l0_nsa_cmp_pos_v, l0_nsa_cmp_w1_v, l0_nsa_cmp_w2_v, l0_w_out)
    x2 = _out_ln(acts, weights, x2, vec(l0_ln_mix_g), vec(l0_ln_mix_b))
    x2 = _mlp(x2, l0_w_up.astype(BF16), l0_w_down.astype(BF16), vec(l0_ln_ffn_g), vec(l0_ln_ffn_b))
    acts, weights = _layer1_mixer(x2, b, s, l1_w_qkv, l1_lam_q1, l1_lam_k1, l1_lam_q2, l1_lam_k2,
                                  l1_subln_g, l1_w_o, 1)
    x2 = _out_ln(acts, weights, x2, vec(l1_ln_mix_g), vec(l1_ln_mix_b))
    x2 = _mlp(x2, l1_w_up.astype(BF16), l1_w_down.astype(BF16), vec(l1_ln_ffn_g), vec(l1_ln_ffn_b))
    return x2.reshape(b, s, d)
```

```python
import functools
import math

import jax
import jax.numpy as jnp
import numpy as np
from jax import lax
from jax.experimental import pallas as pl
from jax.experimental.pallas import tpu as pltpu

F32 = jnp.float32
BF16 = jnp.bfloat16

LANES = 128
SUBLANES = 8
BF16_ROWS = 16
MXU_DEPTH = 256
HEAD_DIM = 64
Q_TILE = 256
FLASH_Q_TILE = 512
KV_TILE = 512
KEY_CHUNK = 32
ROW_TILE = 512
FF_TILE = 1024
VMEM_LIMIT = 56 * 1024 * 1024

NEG_INF = -1e30
LOG2E = math.log2(math.e)
LN_EPS = 1e-5
RMS_EPS = 1e-6
DEPTH = 2
DN_ALPHA = (2.0 * DEPTH) ** 0.25

MLA_HEADS = 8
MLA_NOPE = 64
MLA_ROPE = 32
ROPE_THETA = 10000.0
NSA_HEADS = 8
NSA_GROUPS = 2
NSA_HG = NSA_HEADS // NSA_GROUPS
CMP_LEN = 32
CMP_STRIDE = 16
SEL_LEN = 64
SEL_TOPK = 16
WINDOW = 512
FORCE_BONUS = 1e3
DIFF_HEADS = 8

POS_SPLIT = 16
FEATURE_ROWS = 16
BLOCK_LANE0 = 8
BLOCKS_PER_TILE = KV_TILE // SEL_LEN


def _params(*sem):
    return pltpu.CompilerParams(dimension_semantics=sem, vmem_limit_bytes=VMEM_LIMIT)


def _dot(a, b):
    return jnp.dot(a, b, preferred_element_type=F32)


def _dot_nt(a, b):
    return lax.dot_general(a, b, (((1,), (1,)), ((), ())), preferred_element_type=F32)


def _split_bf16(x):
    hi = x.astype(BF16)
    lo = (x - hi.astype(F32)).astype(BF16)
    return hi, lo


def _layer_norm(z, g, b):
    mu = jnp.mean(z, axis=-1, keepdims=True)
    zc = z - mu
    var = jnp.mean(zc * zc, axis=-1, keepdims=True)
    return zc * lax.rsqrt(var + LN_EPS) * g + b


def _rms_norm(z, g, eps):
    return z * lax.rsqrt(jnp.mean(z * z, axis=-1, keepdims=True) + eps) * g


def _lane_iota(shape):
    return lax.broadcasted_iota(jnp.int32, shape, 1)


def _keep_half(x, half):
    lane = _lane_iota(x.shape)
    keep = (lane < HEAD_DIM) if half == 0 else (lane >= HEAD_DIM)
    return jnp.where(keep, x, jnp.zeros_like(x))


def _move_head(slab, src_half, dst_half):
    if src_half != dst_half:
        slab = pltpu.roll(slab, HEAD_DIM, 1)
    return _keep_half(slab, dst_half)


def _store_transposed(o_ref, res):
    for c in range(res.shape[1] // LANES):
        cols = slice(c * LANES, (c + 1) * LANES)
        o_ref[0, 0, cols, :] = res[:, cols].T.astype(o_ref.dtype)


def _proj_kernel(x_ref, *refs, transposed):
    n_out = len(transposed)
    w_refs, o_refs = refs[:n_out], refs[n_out:]
    xb = x_ref[...].astype(BF16)
    for w_ref, o_ref, t in zip(w_refs, o_refs, transposed):
        res = _dot(xb, w_ref[...])
        if t:
            _store_transposed(o_ref, res)
        else:
            o_ref[...] = res.astype(o_ref.dtype)


def _transposed_out(b, seq, width, tm):
    per_seq = seq // tm
    spec = pl.BlockSpec((1, 1, width, tm), lambda i: (i // per_seq, i % per_seq, 0, 0))
    return spec, jax.ShapeDtypeStruct((b, per_seq, width, tm), BF16)


def _project(x, weights, out_dtypes, transposed, b, seq):
    m, k = x.shape
    tm = min(KV_TILE, seq)
    specs, shapes = [], []
    for w, dt, t in zip(weights, out_dtypes, transposed):
        if t:
            spec, shape = _transposed_out(b, seq, w.shape[1], tm)
        else:
            spec, shape = pl.BlockSpec((tm, w.shape[1]), lambda i: (i, 0)), jax.ShapeDtypeStruct((m, w.shape[1]), dt)
        specs.append(spec)
        shapes.append(shape)
    return pl.pallas_call(
        functools.partial(_proj_kernel, transposed=tuple(transposed)),
        grid=(m // tm,),
        in_specs=[pl.BlockSpec((tm, k), lambda i: (i, 0))]
        + [pl.BlockSpec(w.shape, lambda i: (0, 0)) for w in weights],
        out_specs=specs,
        out_shape=shapes,
        compiler_params=_params("parallel"),
        name="project",
    )(x, *weights)


def _rope_slab(slab, c, s1, s2):
    half = MLA_ROPE // 2
    up = pltpu.roll(slab, half, 1)
    down = pltpu.roll(slab, LANES - half, 1)
    return slab * c + down * s1 + up * s2


def _mla_prep_kernel(ql_ref, kvl_ref, kpe_ref, qg_ref, kvg_ref, wq_ref, wk_ref, wv_ref,
                     c_ref, s1_ref, s2_ref, q_ref, k_ref, vt_ref, *, q_scale):
    c, s1, s2 = c_ref[...], s1_ref[...], s2_ref[...]
    qn = _rms_norm(ql_ref[...], qg_ref[...], RMS_EPS).astype(BF16)
    kvn = _rms_norm(kvl_ref[...], kvg_ref[...], RMS_EPS).astype(BF16)
    q = _dot(qn, wq_ref[...])
    k = _dot(kvn, wk_ref[...])
    _store_transposed(vt_ref, _dot(kvn, wv_ref[...]))
    kpe = _rope_slab(kpe_ref[...], c, s1, s2)
    for h in range(MLA_HEADS):
        sl = slice(h * LANES, (h + 1) * LANES)
        q_ref[:, sl] = (_rope_slab(q[:, sl], c, s1, s2) * q_scale).astype(q_ref.dtype)
        k_ref[:, sl] = (k[:, sl] + kpe).astype(k_ref.dtype)


def _mla_prep(slab_a, q_gain, kv_gain, wq, wk, wv, rope_c, rope_s1, rope_s2, b, seq):
    m = slab_a.shape[0]
    tm = min(KV_TILE, seq)
    per_seq = seq // tm
    rank = q_gain.shape[1]
    row = lambda j: (lambda i: (i, j))
    tab = lambda i: (i % per_seq, 0)
    const = lambda i: (0, 0)
    hw = MLA_HEADS * LANES
    vt_spec, vt_shape = _transposed_out(b, seq, wv.shape[1], tm)
    return pl.pallas_call(
        functools.partial(_mla_prep_kernel, q_scale=float((MLA_NOPE + MLA_ROPE) ** -0.5 * LOG2E)),
        grid=(m // tm,),
        in_specs=[pl.BlockSpec((tm, rank), row(0)), pl.BlockSpec((tm, rank), row(1)),
                  pl.BlockSpec((tm, LANES), row(2 * rank // LANES)),
                  pl.BlockSpec((1, rank), const), pl.BlockSpec((1, rank), const),
                  pl.BlockSpec(wq.shape, const), pl.BlockSpec(wk.shape, const), pl.BlockSpec(wv.shape, const),
                  pl.BlockSpec((tm, LANES), tab), pl.BlockSpec((tm, LANES), tab), pl.BlockSpec((tm, LANES), tab)],
        out_specs=[pl.BlockSpec((tm, hw), row(0)), pl.BlockSpec((tm, hw), row(0)), vt_spec],
        out_shape=[jax.ShapeDtypeStruct((m, hw), BF16), jax.ShapeDtypeStruct((m, hw), BF16), vt_shape],
        compiler_params=_params("parallel"),
        name="mla_prep",
    )(slab_a, slab_a, slab_a, q_gain, kv_gain, wq, wk, wv, rope_c, rope_s1, rope_s2)


def _flash_scratch(n_streams, v_rows, tq, tk):
    scores = pltpu.VMEM((n_streams, tk, tq), F32)
    stat = pltpu.VMEM((n_streams, 1, tq), F32)
    probs = pltpu.VMEM((n_streams, tk, tq), BF16)
    slot = [scores, probs, stat]
    return slot + slot + [stat, stat, pltpu.VMEM((n_streams, v_rows, tq), F32)]


def _chunk_rows(c):
    return slice(c * KEY_CHUNK, (c + 1) * KEY_CHUNK)


def _fold_rows(x):
    return x.reshape(x.shape[0] // SUBLANES, SUBLANES, x.shape[1])


def _flash_transposed(n_full, n_streams, q_start, tq, tk, key_operand, query_operand, values, offset, scratch):
    slot_a, slot_b, (m_ref, l_ref, acc_ref) = scratch[0:3], scratch[3:6], scratch[6:]
    n_chunks = tk // KEY_CHUNK
    for i in range(n_streams):
        m_ref[i] = jnp.full((1, tq), NEG_INF, F32)
        l_ref[i] = jnp.zeros((1, tq), F32)
        acc_ref[i] = jnp.zeros(acc_ref.shape[1:], F32)
        slot_b[1][i] = jnp.zeros((tk, tq), BF16)
        slot_b[2][i] = jnp.ones((1, tq), F32)

    def column_max(s_ref, i):
        part = jnp.full((SUBLANES, tq), NEG_INF, F32)
        for c in range(n_chunks):
            part = jnp.maximum(part, jnp.max(_fold_rows(s_ref[i, _chunk_rows(c), :]), axis=0))
        return jnp.max(part, axis=0, keepdims=True)

    def stage1(j, slot, i):
        slot[0][i] = _dot(key_operand(j, i), query_operand(j, i))

    def stage2(j, slot, i, causal):
        s_ref, p_ref, alpha_ref = slot
        if causal is not None:
            s_ref[i] = jnp.where(causal, s_ref[i], NEG_INF)
        mx = column_max(s_ref, i)
        off = offset(j, i)
        m_prev = m_ref[i]
        if off is None:
            m_next = jnp.maximum(m_prev, mx)
            shift = m_next
        else:
            m_next = jnp.maximum(m_prev, mx + off)
            shift = m_next - off
        alpha = jnp.exp2(m_prev - m_next)
        part = jnp.zeros((SUBLANES, tq), F32)
        for c in range(n_chunks):
            p = jnp.exp2(s_ref[i, _chunk_rows(c), :] - shift)
            part = part + jnp.sum(_fold_rows(p), axis=0)
            p_ref[i, _chunk_rows(c), :] = p.astype(BF16)
        l_ref[i] = alpha * l_ref[i] + jnp.sum(part, axis=0, keepdims=True)
        m_ref[i] = m_next
        alpha_ref[i] = alpha

    def step(accumulate=None, produce=None, exponentiate=None, diagonal=False):
        causal = None
        if diagonal:
            key_minus_query = (lax.broadcasted_iota(jnp.int32, (tk, tq), 0)
                               - lax.broadcasted_iota(jnp.int32, (tk, tq), 1))
            causal = key_minus_query <= q_start - exponentiate[0] * tk
        products = []
        for i in range(n_streams + 1):
            if i < n_streams:
                if accumulate is not None:
                    products.append(_dot(values(accumulate[0], i), accumulate[1][1][i]))
                if produce is not None:
                    stage1(*produce, i)
                if exponentiate is not None:
                    stage2(*exponentiate, i, causal)
            if accumulate is not None and i > 0:
                acc_ref[i - 1] = accumulate[1][2][i - 1] * acc_ref[i - 1] + products[i - 1]

    step(produce=(0, slot_a))

    def pair(t, carry):
        j0 = 2 * t
        step((jnp.maximum(j0 - 1, 0), slot_b), (j0 + 1, slot_b), (j0, slot_a))
        step((j0, slot_a), (j0 + 2, slot_a), (j0 + 1, slot_b))
        return carry

    lax.fori_loop(0, n_full // 2, pair, 0)
    pending = jnp.maximum(n_full - 1 - n_full % 2, 0)

    @pl.when(n_full % 2 == 1)
    def _():
        step((pending, slot_b), (n_full, slot_b), (n_full - 1, slot_a))
        step(accumulate=(n_full - 1, slot_a), exponentiate=(n_full, slot_b), diagonal=True)
        step(accumulate=(n_full, slot_b))

    @pl.when(n_full % 2 == 0)
    def _():
        step(accumulate=(pending, slot_b), exponentiate=(n_full, slot_a), diagonal=True)
        step(accumulate=(n_full, slot_a))


def _flash_result(scratch, i):
    l_ref, acc_ref = scratch[-2], scratch[-1]
    return acc_ref[i] / l_ref[i]


def _kv_rows(j, tk):
    return pl.ds(pl.multiple_of(j * tk, tk), tk)


def _transposed_bf16(x):
    return x.astype(F32).T.astype(BF16)


def _alibi_rows(coef, tq):
    c = jnp.zeros((1, tq), F32) + coef
    hi = c.astype(BF16).astype(F32)
    rest = c - hi
    mid = rest.astype(BF16).astype(F32)
    lo = rest - mid
    zero = jnp.zeros((1, tq), F32)
    return jnp.concatenate([POS_SPLIT * hi, POS_SPLIT * mid, POS_SPLIT * lo, hi, mid, lo, zero, zero], axis=0)


def _augmented_query(q_t, feature_rows):
    tq = q_t.shape[1]
    pad = jnp.zeros((MXU_DEPTH - LANES - FEATURE_ROWS, tq), BF16)
    return jnp.concatenate([q_t, feature_rows.astype(BF16), pad], axis=0)


MLA_STEP_HEADS = 4


def _mla_attn_kernel(q_ref, k_ref, vt_ref, o_ref, *scratch, tq, tk):
    q_start = pl.program_id(2) * tq
    n = MLA_STEP_HEADS
    queries = [_transposed_bf16(q_ref[0, :, hh * LANES:(hh + 1) * LANES]) for hh in range(n)]
    _flash_transposed(
        q_start // tk, n, q_start, tq, tk,
        lambda j, i: k_ref[0, _kv_rows(j, tk), i * LANES:(i + 1) * LANES],
        lambda j, i: queries[i],
        lambda j, i: vt_ref[0, j, i * HEAD_DIM:(i + 1) * HEAD_DIM, :],
        lambda j, i: None, scratch)
    for pair in range(n // 2):
        o_t = jnp.concatenate([_flash_result(scratch, 2 * pair), _flash_result(scratch, 2 * pair + 1)], axis=0)
        o_ref[0, :, pair * LANES:(pair + 1) * LANES] = o_t.T.astype(o_ref.dtype)


def _mla_attn(q, k, vt):
    b, s, _ = q.shape
    tq, tk = min(FLASH_Q_TILE, s), min(KV_TILE, s)
    n = MLA_STEP_HEADS
    groups = MLA_HEADS // n
    return pl.pallas_call(
        functools.partial(_mla_attn_kernel, tq=tq, tk=tk),
        grid=(b, groups, s // tq),
        in_specs=[pl.BlockSpec((1, tq, n * LANES), lambda bi, p, i: (bi, i, p)),
                  pl.BlockSpec((1, s, n * LANES), lambda bi, p, i: (bi, 0, p)),
                  pl.BlockSpec((1, s // tk, n * HEAD_DIM, tk), lambda bi, p, i: (bi, 0, p, 0))],
        out_specs=pl.BlockSpec((1, tq, n * HEAD_DIM), lambda bi, p, i: (bi, i, p)),
        out_shape=jax.ShapeDtypeStruct((b, s, MLA_HEADS * HEAD_DIM), BF16),
        scratch_shapes=_flash_scratch(n, HEAD_DIM, tq, tk),
        compiler_params=_params("parallel", "parallel", "arbitrary"),
        name="mla_attn",
    )(q, k, vt)


def _gelu_tanh(x):
    return 0.5 * x * (1.0 + jnp.tanh(math.sqrt(2.0 / math.pi) * (x + 0.044715 * (x * x * x))))


def _compress_kernel(x_ref, pos_ref, w1a_ref, w1b_ref, w2_ref, o_ref, *, n_real):
    x = x_ref[0]
    n = x.shape[0]
    first = _dot(x, w1a_ref[...])
    second = _dot(x, w1b_ref[...])
    pos_hi, pos_lo = _split_bf16(pos_ref[...])
    bias = (_dot(pos_hi[:8], w1a_ref[...]) + _dot(pos_lo[:8], w1a_ref[...])
            + _dot(pos_hi[8:], w1b_ref[...]) + _dot(pos_lo[8:], w1b_ref[...]))[:1]
    pre = first + pltpu.roll(second, n - 1, 0) + bias
    out = _dot(_gelu_tanh(pre).astype(BF16), w2_ref[...])
    real = lax.broadcasted_iota(jnp.int32, out.shape, 0) < n_real
    o_ref[0] = jnp.where(real, out, 0.0).astype(o_ref.dtype)


def _compress(x_chunks, pos_exp, w1a, w1b, w2, n_real):
    b, n, width = x_chunks.shape
    const = lambda bi: (0, 0)
    return pl.pallas_call(
        functools.partial(_compress_kernel, n_real=n_real),
        grid=(b,),
        in_specs=[pl.BlockSpec((1, n, width), lambda bi: (bi, 0, 0)),
                  pl.BlockSpec(pos_exp.shape, const), pl.BlockSpec(w1a.shape, const),
                  pl.BlockSpec(w1b.shape, const), pl.BlockSpec(w2.shape, const)],
        out_specs=pl.BlockSpec((1, n, w2.shape[1]), lambda bi: (bi, 0, 0)),
        out_shape=jax.ShapeDtypeStruct((b, n, w2.shape[1]), BF16),
        compiler_params=_params("parallel"),
        name="nsa_compress",
    )(x_chunks, pos_exp, w1a, w1b, w2)


def _nsa_head_slope(h):
    return float(2.0 ** (-8.0 * (h + 1) / NSA_HEADS))


def _nsa_queries(q_ref, g, scale):
    out = []
    for hg in range(NSA_HG):
        h = g * NSA_HG + hg
        slab = q_ref[0, :, (h // 2) * LANES:(h // 2 + 1) * LANES].astype(F32) * scale
        out.append(_move_head(slab, h % 2, g))
    return out


def _place_heads(results, g):
    slabs = []
    for pair in range(NSA_HG // 2):
        even = _move_head(results[2 * pair], g, 0)
        odd = _move_head(results[2 * pair + 1], g, 1)
        slabs.append(even + odd)
    return slabs


def _select_blocks(imp, q_pos, n_lanes):
    tq = imp.shape[0]
    blk = _lane_iota((tq, n_lanes))
    cur = q_pos // SEL_LEN
    forced = jnp.where(blk == 0, 1.0, 0.0) + jnp.where(blk == cur, 1.0, 0.0) + jnp.where(blk == cur - 1, 1.0, 0.0)
    forced = jnp.minimum(forced, 1.0)
    val = jnp.where(blk <= cur, imp + FORCE_BONUS * forced, NEG_INF)
    val_t = val.T
    blk_t = lax.broadcasted_iota(jnp.int32, val_t.shape, 0)
    chosen = jnp.zeros(val_t.shape, F32)
    for _ in range(SEL_TOPK):
        top = jnp.max(val_t, axis=0, keepdims=True)
        first = jnp.min(jnp.where(val_t == top, blk_t, n_lanes), axis=0, keepdims=True)
        hit = blk_t == first
        chosen = jnp.where(hit, 1.0, chosen)
        val_t = jnp.where(hit, -jnp.inf, val_t)
    return jnp.where(chosen > 0.5, 0.0, NEG_INF)


def _nsa_cmp_win_kernel(q_ref, kc_ref, vc_ref, kw_ref, vw_ref, ov_ref, oc_ref, ow_ref, sel_ref, *, tq, seq):
    qi = pl.program_id(1)
    q_start = qi * tq
    n_cmp = kc_ref.shape[1]
    row = lax.broadcasted_iota(jnp.int32, (tq, 1), 0)
    q_pos = q_start + row
    cmp_end = lax.broadcasted_iota(jnp.int32, (tq, n_cmp), 1) * CMP_STRIDE + (CMP_LEN - 1)
    cmp_dist = (q_pos - cmp_end).astype(F32)
    cmp_ok = cmp_dist >= 0.0
    span = WINDOW + tq
    w_start = jnp.maximum(q_start - WINDOW, 0)
    w_rows = pl.ds(pl.multiple_of(w_start, tq), span)
    win_dist = (q_pos - (w_start + lax.broadcasted_iota(jnp.int32, (tq, span), 1))).astype(F32)
    win_ok = jnp.abs(win_dist - (WINDOW - 1) * 0.5) <= (WINDOW - 1) * 0.5
    kc, vc = kc_ref[0], vc_ref[0]
    kw, vw = kw_ref[0, w_rows, :], vw_ref[0, w_rows, :]
    for g in range(NSA_GROUPS):
        queries = [q.astype(BF16) for q in _nsa_queries(q_ref, g, HEAD_DIM ** -0.5)]
        p_sum = jnp.zeros((tq, n_cmp), F32)
        res_c, res_w = [], []
        for hg in range(NSA_HG):
            slope = _nsa_head_slope(g * NSA_HG + hg)
            s = jnp.where(cmp_ok, _dot_nt(queries[hg], kc) - slope * cmp_dist, NEG_INF)
            p = jnp.where(cmp_ok, jnp.exp(s - jnp.max(s, axis=1, keepdims=True)), 0.0)
            p = p / jnp.maximum(jnp.sum(p, axis=1, keepdims=True), 1e-30)
            p_sum = p_sum + p
            res_c.append(_dot(p.astype(BF16), vc))
            s = jnp.where(win_ok, _dot_nt(queries[hg], kw) - slope * win_dist, NEG_INF)
            p = jnp.exp(s - jnp.max(s, axis=1, keepdims=True))
            p = p / jnp.sum(p, axis=1, keepdims=True)
            res_w.append(_dot(p.astype(BF16), vw))
        for pair, (slab_c, slab_w) in enumerate(zip(_place_heads(res_c, g), _place_heads(res_w, g))):
            cols = slice((g * 2 + pair) * LANES, (g * 2 + pair + 1) * LANES)
            oc_ref[0, :, cols] = slab_c
            ow_ref[0, :, cols] = slab_w
        p_hi, p_lo = _split_bf16(p_sum)
        imp = _dot(p_hi, ov_ref[...]) + _dot(p_lo, ov_ref[...])
        sel_ref[0, g * LANES:(g + 1) * LANES, :] = _select_blocks(imp, q_pos, LANES)


def _nsa_cmp_win(slab_b, cmp_kv, overlap, seq):
    b = slab_b.shape[0]
    tq = min(Q_TILE, seq)
    n_cmp = cmp_kv.shape[1]
    qw = NSA_HEADS * HEAD_DIM
    base = qw // LANES
    return pl.pallas_call(
        functools.partial(_nsa_cmp_win_kernel, tq=tq, seq=seq),
        grid=(b, seq // tq),
        in_specs=[pl.BlockSpec((1, tq, qw), lambda bi, i: (bi, i, 0)),
                  pl.BlockSpec((1, n_cmp, LANES), lambda bi, i: (bi, 0, 0)),
                  pl.BlockSpec((1, n_cmp, LANES), lambda bi, i: (bi, 0, 1)),
                  pl.BlockSpec((1, seq, LANES), lambda bi, i: (bi, 0, base + 1)),
                  pl.BlockSpec((1, seq, LANES), lambda bi, i: (bi, 0, base + 2)),
                  pl.BlockSpec(overlap.shape, lambda bi, i: (0, 0))],
        out_specs=[pl.BlockSpec((1, tq, qw), lambda bi, i: (bi, i, 0)),
                   pl.BlockSpec((1, tq, qw), lambda bi, i: (bi, i, 0)),
                   pl.BlockSpec((1, NSA_GROUPS * LANES, tq), lambda bi, i: (bi, 0, i))],
        out_shape=[jax.ShapeDtypeStruct((b, seq, qw), F32), jax.ShapeDtypeStruct((b, seq, qw), F32),
                   jax.ShapeDtypeStruct((b, NSA_GROUPS * LANES, seq), F32)],
        compiler_params=_params("parallel", "arbitrary"),
        name="nsa_cmp_win",
    )(slab_b, cmp_kv, cmp_kv, slab_b, slab_b, overlap)


def _nsa_sel_kernel(q_ref, k_ref, vt_ref, feat_ref, sel_ref, oc_ref, ow_ref, gate_ref, gx_ref, o_ref,
                    *scratch, tq, tk):
    q_start = pl.program_id(1) * tq
    n_full = q_start // tk
    out_slabs = []
    for g in range(NSA_GROUPS):
        queries = [_transposed_bf16(q) for q in _nsa_queries(q_ref, g, HEAD_DIM ** -0.5 * LOG2E)]
        coefs = [_nsa_head_slope(g * NSA_HG + hg) * LOG2E for hg in range(NSA_HG)]
        alibi = [_alibi_rows(c, tq) for c in coefs]

        def key_operand(j, i):
            return jnp.concatenate([k_ref[0, _kv_rows(j, tk), :], feat_ref[...]], axis=1)

        def query_operand(j, i, g=g, queries=queries, alibi=alibi):
            first_block = pl.multiple_of(g * LANES + j * BLOCKS_PER_TILE, BLOCKS_PER_TILE)
            blocks = sel_ref[0, pl.ds(first_block, BLOCKS_PER_TILE), :]
            return _augmented_query(queries[i], jnp.concatenate([alibi[i], blocks], axis=0))

        _flash_transposed(
            n_full, NSA_HG, q_start, tq, tk, key_operand, query_operand,
            lambda j, i, g=g: vt_ref[0, j, g * HEAD_DIM:(g + 1) * HEAD_DIM, :],
            lambda j, i, coefs=coefs: coefs[i] * (j * tk - q_start).astype(F32), scratch)
        heads = [_flash_result(scratch, hg) for hg in range(NSA_HG)]
        for pair in range(NSA_HG // 2):
            out_slabs.append(jnp.concatenate(heads[2 * pair:2 * pair + 2], axis=0).T)
    gates = jax.nn.sigmoid(gate_ref[...])
    g_hi, g_lo = _split_bf16(gates)
    width = NSA_HEADS * HEAD_DIM
    for i, o_sel in enumerate(out_slabs):
        mixed = None
        for branch, o_branch in enumerate((oc_ref[0, :, i * LANES:(i + 1) * LANES], o_sel,
                                           ow_ref[0, :, i * LANES:(i + 1) * LANES])):
            gx = gx_ref[:, branch * width + i * LANES:branch * width + (i + 1) * LANES]
            term = (_dot(g_hi, gx) + _dot(g_lo, gx)) * o_branch
            mixed = term if mixed is None else mixed + term
        o_ref[0, :, i * LANES:(i + 1) * LANES] = mixed.astype(o_ref.dtype)


def _nsa_sel(slab_b, vt, key_features, sel_bias_t, o_cmp, o_win, slab_a, gate_expand, seq, gate_col_block):
    b = slab_b.shape[0]
    tq, tk = min(FLASH_Q_TILE, seq), min(KV_TILE, seq)
    qw = NSA_HEADS * HEAD_DIM
    base = qw // LANES
    per_seq = seq // tq
    tile = lambda bi, i: (bi, i, 0)
    return pl.pallas_call(
        functools.partial(_nsa_sel_kernel, tq=tq, tk=tk),
        grid=(b, seq // tq),
        in_specs=[pl.BlockSpec((1, tq, qw), tile),
                  pl.BlockSpec((1, seq, LANES), lambda bi, i: (bi, 0, base)),
                  pl.BlockSpec((1, seq // tk, LANES, tk), lambda bi, i: (bi, 0, 0, 0)),
                  pl.BlockSpec(key_features.shape, lambda bi, i: (0, 0)),
                  pl.BlockSpec((1, NSA_GROUPS * LANES, tq), lambda bi, i: (bi, 0, i)),
                  pl.BlockSpec((1, tq, qw), tile), pl.BlockSpec((1, tq, qw), tile),
                  pl.BlockSpec((tq, LANES), lambda bi, i: (bi * per_seq + i, gate_col_block)),
                  pl.BlockSpec(gate_expand.shape, lambda bi, i: (0, 0))],
        out_specs=pl.BlockSpec((1, tq, qw), tile),
        out_shape=jax.ShapeDtypeStruct((b, seq, qw), BF16),
        scratch_shapes=_flash_scratch(NSA_HG, HEAD_DIM, tq, tk),
        compiler_params=_params("parallel", "arbitrary"),
        name="nsa_sel",
    )(slab_b, slab_b, vt, key_features, sel_bias_t, o_cmp, o_win, slab_a, gate_expand)


DIFF_STEP_HEADS = 2


def _diff_attn_kernel(slope_ref, lam_ref, q_ref, k_ref, vt_ref, feat_ref, g_ref, o_ref, *scratch,
                      tq, tk, lam_init):
    first_head = pl.program_id(1) * DIFF_STEP_HEADS
    q_start = pl.program_id(2) * tq
    zeros = jnp.zeros((FEATURE_ROWS - SUBLANES, tq), F32)
    coefs, queries = [], []
    for hh in range(DIFF_STEP_HEADS):
        coef = slope_ref[first_head + hh] * LOG2E
        q = q_ref[0, :, hh * LANES:(hh + 1) * LANES].astype(F32) * (HEAD_DIM ** -0.5 * LOG2E)
        features = jnp.concatenate([_alibi_rows(coef, tq), zeros], axis=0)
        coefs.append(coef)
        queries += [_augmented_query(_transposed_bf16(_keep_half(q, half)), features) for half in range(2)]

    def head_lanes(i):
        return slice((i // 2) * LANES, (i // 2 + 1) * LANES)

    _flash_transposed(
        q_start // tk, 2 * DIFF_STEP_HEADS, q_start, tq, tk,
        lambda j, i: jnp.concatenate([k_ref[0, _kv_rows(j, tk), head_lanes(i)], feat_ref[...]], axis=1),
        lambda j, i: queries[i],
        lambda j, i: vt_ref[0, j, head_lanes(i), :],
        lambda j, i: coefs[i // 2] * (j * tk - q_start).astype(F32), scratch)
    lam_vec = lam_ref[...]
    lam = (jnp.exp(jnp.sum(lam_vec[0:1] * lam_vec[1:2], axis=1, keepdims=True))
           - jnp.exp(jnp.sum(lam_vec[2:3] * lam_vec[3:4], axis=1, keepdims=True)) + lam_init)
    for hh in range(DIFF_STEP_HEADS):
        o = (_flash_result(scratch, 2 * hh) - lam * _flash_result(scratch, 2 * hh + 1)).T
        o_ref[0, :, hh * LANES:(hh + 1) * LANES] = (
            _rms_norm(o, g_ref[...], RMS_EPS) * (1.0 - lam_init)).astype(o_ref.dtype)


def _diff_attn(qk, vt, key_features, slopes, lam_vecs, subln_g, lam_init):
    b, s, _ = qk.shape
    tq, tk = min(FLASH_Q_TILE, s), min(KV_TILE, s)
    n = DIFF_STEP_HEADS
    groups = DIFF_HEADS // n
    smem = pl.BlockSpec(memory_space=pltpu.SMEM)
    return pl.pallas_call(
        functools.partial(_diff_attn_kernel, tq=tq, tk=tk, lam_init=lam_init),
        grid=(b, groups, s // tq),
        in_specs=[smem, pl.BlockSpec(lam_vecs.shape, lambda bi, h, i: (0, 0)),
                  pl.BlockSpec((1, tq, n * LANES), lambda bi, h, i: (bi, i, h)),
                  pl.BlockSpec((1, s, n * LANES), lambda bi, h, i: (bi, 0, groups + h)),
                  pl.BlockSpec((1, s // tk, n * LANES, tk), lambda bi, h, i: (bi, 0, h, 0)),
                  pl.BlockSpec(key_features.shape, lambda bi, h, i: (0, 0)),
                  pl.BlockSpec((1, LANES), lambda bi, h, i: (0, 0))],
        out_specs=pl.BlockSpec((1, tq, n * LANES), lambda bi, h, i: (bi, i, h)),
        out_shape=jax.ShapeDtypeStruct((b, s, DIFF_HEADS * LANES), BF16),
        scratch_shapes=_flash_scratch(2 * n, LANES, tq, tk),
        compiler_params=_params("parallel", "parallel", "arbitrary"),
        name="diff_attn",
    )(slopes, lam_vecs, qk, qk, vt, key_features, subln_g)


def _out_ln_kernel(*refs, n_in):
    a_refs, w_refs = refs[:n_in], refs[n_in:2 * n_in]
    x_ref, g_ref, b_ref, o_ref = refs[2 * n_in:]
    y = None
    for a_ref, w_ref in zip(a_refs, w_refs):
        t = _dot(a_ref[...], w_ref[...])
        y = t if y is None else y + t
    o_ref[...] = _layer_norm(DN_ALPHA * x_ref[...] + y, g_ref[...], b_ref[...])


def _out_ln(acts, weights, x, g, b):
    m, d = x.shape
    tm = min(ROW_TILE, m)
    row = lambda i: (i, 0)
    const = lambda i: (0, 0)
    return pl.pallas_call(
        functools.partial(_out_ln_kernel, n_in=len(acts)),
        grid=(m // tm,),
        in_specs=[pl.BlockSpec((tm, a.shape[1]), row) for a in acts]
        + [pl.BlockSpec(w.shape, const) for w in weights]
        + [pl.BlockSpec((tm, d), row), pl.BlockSpec((1, d), const), pl.BlockSpec((1, d), const)],
        out_specs=pl.BlockSpec((tm, d), row),
        out_shape=jax.ShapeDtypeStruct((m, d), F32),
        compiler_params=_params("parallel"),
        name="out_proj_ln",
    )(*acts, *weights, x, g, b)


def _mlp_kernel(x_ref, wu_ref, wd_ref, g_ref, b_ref, o_ref, xb_ref, acc_ref):
    f = pl.program_id(1)

    @pl.when(f == 0)
    def _():
        xb_ref[...] = x_ref[...].astype(BF16)
        acc_ref[...] = jnp.zeros(acc_ref.shape, F32)

    hidden = jnp.maximum(_dot(xb_ref[...], wu_ref[...]), 0.0)
    acc_ref[...] += _dot((hidden * hidden).astype(BF16), wd_ref[...])

    @pl.when(f == pl.num_programs(1) - 1)
    def _():
        o_ref[...] = _layer_norm(DN_ALPHA * x_ref[...] + acc_ref[...], g_ref[...], b_ref[...])


def _mlp(x, w_up, w_down, g, b):
    m, d = x.shape
    ff = w_up.shape[1]
    tm, tf = min(ROW_TILE, m), min(FF_TILE, ff)
    return pl.pallas_call(
        _mlp_kernel,
        grid=(m // tm, ff // tf),
        in_specs=[pl.BlockSpec((tm, d), lambda i, f: (i, 0)),
                  pl.BlockSpec((d, tf), lambda i, f: (0, f)),
                  pl.BlockSpec((tf, d), lambda i, f: (f, 0)),
                  pl.BlockSpec((1, d), lambda i, f: (0, 0)), pl.BlockSpec((1, d), lambda i, f: (0, 0))],
        out_specs=pl.BlockSpec((tm, d), lambda i, f: (i, 0)),
        out_shape=jax.ShapeDtypeStruct((m, d), F32),
        scratch_shapes=[pltpu.VMEM((tm, d), BF16), pltpu.VMEM((tm, d), F32)],
        compiler_params=_params("parallel", "arbitrary"),
        name="mlp_ln",
    )(x, w_up, w_down, g, b)


def _pad_cols(w, width):
    return jnp.pad(w, ((0, 0), (0, width - w.shape[1])))


def _layer0_weights(w_in, w_uq, w_ukv, d_model):
    rank = d_model // 4
    kvw = NSA_GROUPS * HEAD_DIM
    o = np.cumsum([0, rank, rank, MLA_ROPE, NSA_HEADS * HEAD_DIM] + [kvw] * 6 + [3 * NSA_HEADS])
    seg = lambda i: w_in[:, o[i]:o[i + 1]]
    zeros = lambda n: jnp.zeros((w_in.shape[0], n), w_in.dtype)
    rope_slab = jnp.concatenate([zeros(MLA_NOPE), seg(2), zeros(LANES - MLA_NOPE - MLA_ROPE)], axis=1)
    w_a = jnp.concatenate([seg(0), seg(1), rope_slab, _pad_cols(seg(10), LANES)], axis=1)
    w_b = jnp.concatenate([seg(3), seg(6), seg(8), seg(9)], axis=1)
    w_c = jnp.concatenate([seg(4), seg(5)], axis=1)
    w_vs = seg(7)
    wq = jnp.pad(w_uq.reshape(rank, MLA_HEADS, MLA_NOPE + MLA_ROPE),
                 ((0, 0), (0, 0), (0, LANES - MLA_NOPE - MLA_ROPE))).reshape(rank, MLA_HEADS * LANES)
    ukv = w_ukv.reshape(rank, MLA_HEADS, MLA_NOPE + HEAD_DIM)
    wk = jnp.pad(ukv[:, :, :MLA_NOPE], ((0, 0), (0, 0), (0, LANES - MLA_NOPE))).reshape(rank, MLA_HEADS * LANES)
    wv = ukv[:, :, MLA_NOPE:].reshape(rank, MLA_HEADS * HEAD_DIM)
    return [w.astype(BF16) for w in (w_a, w_b, w_c, w_vs, wq, wk, wv)]


def _rope_tables(seq):
    inv = 1.0 / (ROPE_THETA ** (jnp.arange(0, MLA_ROPE, 2, dtype=F32) / MLA_ROPE))
    ang = jnp.arange(seq, dtype=F32)[:, None] * inv[None, :]
    cos, sin = jnp.cos(ang), jnp.sin(ang)
    half = MLA_ROPE // 2
    z = lambda n: jnp.zeros((seq, n), F32)
    tail = LANES - MLA_NOPE - MLA_ROPE
    c = jnp.concatenate([jnp.ones((seq, MLA_NOPE), F32), cos, cos, z(tail)], axis=1)
    s1 = jnp.concatenate([z(MLA_NOPE), -sin, z(half), z(tail)], axis=1)
    s2 = jnp.concatenate([z(MLA_NOPE), z(half), sin, z(tail)], axis=1)
    return c, s1, s2


def _compress_weights(pos_k, w1_k, w2_k, pos_v, w1_v, w2_v):
    eye = jnp.eye(2 * NSA_GROUPS, dtype=F32)
    halves = []
    for a in range(CMP_LEN // CMP_STRIDE):
        rows = slice(a * CMP_STRIDE * HEAD_DIM, (a + 1) * CMP_STRIDE * HEAD_DIM)
        wk = w1_k[rows].reshape(CMP_STRIDE, HEAD_DIM, HEAD_DIM)
        wv = w1_v[rows].reshape(CMP_STRIDE, HEAD_DIM, HEAD_DIM)
        per_slot = jnp.stack([wk, wk, wv, wv], axis=0)
        full = jnp.einsum('st,srdj->rsdtj', eye, per_slot)
        halves.append(full.reshape(CMP_STRIDE * 4 * HEAD_DIM, 4 * HEAD_DIM).astype(BF16))
    w2 = jnp.einsum('st,sdj->sdtj', eye, jnp.stack([w2_k, w2_k, w2_v, w2_v])).reshape(4 * HEAD_DIM, 4 * HEAD_DIM)
    pos = jnp.concatenate([pos_k, pos_k, pos_v, pos_v], axis=1)
    pos = pos.reshape(CMP_LEN // CMP_STRIDE, 1, CMP_STRIDE * 4 * HEAD_DIM)
    pos = jnp.broadcast_to(pos, (pos.shape[0], 8, pos.shape[2])).reshape(-1, pos.shape[2])
    return pos, halves[0], halves[1], w2.astype(BF16)


def _overlap_table(n_cmp_pad, n_cmp):
    c0 = np.arange(n_cmp_pad)[:, None] * CMP_STRIDE
    s0 = np.arange(LANES)[None, :] * SEL_LEN
    ov = np.maximum(np.minimum(c0 + CMP_LEN, s0 + SEL_LEN) - np.maximum(c0, s0), 0) / CMP_LEN
    ov = ov * (np.arange(n_cmp_pad)[:, None] < n_cmp)
    return jnp.asarray(ov, BF16)


def _key_feature_table(tk):
    c = np.arange(tk)
    table = np.zeros((tk, LANES), np.float32)
    table[:, 0:3] = (c // POS_SPLIT)[:, None]
    table[:, 3:6] = (c % POS_SPLIT)[:, None]
    table[c, BLOCK_LANE0 + c // SEL_LEN] = 1.0
    return jnp.asarray(table, BF16)


def _gate_expand_table():
    width = NSA_HEADS * HEAD_DIM
    table = np.zeros((LANES, 3 * width), np.float32)
    for h in range(NSA_HEADS):
        for branch in range(3):
            table[h * 3 + branch, branch * width + h * HEAD_DIM:branch * width + (h + 1) * HEAD_DIM] = 1.0
    return jnp.asarray(table, BF16)


def _alibi_slopes(n):
    return jnp.asarray(2.0 ** (-8.0 * np.arange(1, n + 1) / n), dtype=F32)


def _layer0_mixer(x2, b, s, w_in, q_norm, w_uq, kv_norm, w_ukv, pos_k, w1_k, w2_k, pos_v, w1_v, w2_v, w_out):
    d = x2.shape[1]
    rank = d // 4
    w_a, w_b, w_c, w_vs, wq, wk, wv = _layer0_weights(w_in, w_uq, w_ukv, d)
    slab_a, slab_b, slab_c, vt_sel = _project(x2, [w_a, w_b, w_c, w_vs], [F32, BF16, BF16, BF16],
                                              [False, False, False, True], b, s)
    rope_c, rope_s1, rope_s2 = _rope_tables(s)
    q, k, vt = _mla_prep(slab_a, q_norm.reshape(1, rank), kv_norm.reshape(1, rank), wq, wk, wv,
                         rope_c, rope_s1, rope_s2, b, s)
    o_mla = _mla_attn(q.reshape(b, s, -1), k.reshape(b, s, -1), vt)
    n_chunks = s // CMP_STRIDE
    n_cmp = (s - CMP_LEN) // CMP_STRIDE + 1
    pos, w1a, w1b, w2 = _compress_weights(pos_k, w1_k, w2_k, pos_v, w1_v, w2_v)
    cmp_kv = _compress(slab_c.reshape(b, n_chunks, CMP_STRIDE * slab_c.shape[1]), pos, w1a, w1b, w2, n_cmp)
    slab_b3 = slab_b.reshape(b, s, -1)
    o_cmp, o_win, sel_bias_t = _nsa_cmp_win(slab_b3, cmp_kv, _overlap_table(n_chunks, n_cmp), s)
    o_nsa = _nsa_sel(slab_b3, vt_sel, _key_feature_table(min(KV_TILE, s)), sel_bias_t, o_cmp, o_win, slab_a,
                     _gate_expand_table(), s, (2 * rank + LANES) // LANES)
    half = o_mla.shape[-1]
    w_out_b = w_out.astype(BF16)
    return [o_mla.reshape(b * s, half), o_nsa.reshape(b * s, -1)], [w_out_b[:half], w_out_b[half:]]


def _layer1_mixer(x2, b, s, w_qkv, lam_q1, lam_k1, lam_q2, lam_k2, subln_g, w_o, layer_idx):
    d = x2.shape[1]
    w = w_qkv.astype(BF16)
    qk, vt = _project(x2, [w[:, :2 * d], w[:, 2 * d:]], [BF16, BF16], [False, True], b, s)
    lam_init = 0.8 - 0.6 * math.exp(-0.3 * layer_idx)
    lam_vecs = jnp.stack([lam_q1, lam_k1, lam_q2, lam_k2]).astype(F32)
    o = _diff_attn(qk.reshape(b, s, -1), vt, _key_feature_table(min(KV_TILE, s)), _alibi_slopes(DIFF_HEADS),
                   lam_vecs, subln_g.reshape(1, -1), lam_init)
    return [o.reshape(b * s, -1)], [w_o.astype(BF16)]


def kernel(x, l0_w_in, l0_mla_q_norm, l0_mla_w_uq, l0_mla_kv_norm, l0_mla_w_ukv, l0_nsa_cmp_pos_k, l0_nsa_cmp_w1_k, l0_nsa_cmp_w2_k, l0_nsa_cmp_pos_v, l0_nsa_cmp_w1_v, l0_nsa_cmp_w2_v, l0_w_out, l0_ln_mix_g, l0_ln_mix_b, l0_w_up, l0_w_down, l0_ln_ffn_g, l0_ln_ffn_b, l1_w_qkv, l1_lam_q1, l1_lam_k1, l1_lam_q2, l1_lam_k2, l1_subln_g, l1_w_o, l1_ln_mix_g, l1_ln_mix_b, l1_w_up, l1_w_down, l1_ln_ffn_g, l1_ln_ffn_b):
    b, s, d = x.shape
    x2 = x.reshape(b * s, d)
    vec = lambda p: p.reshape(1, d)
    acts, weights = _layer0_mixer(x2, b, s, l0_w_in, l0_mla_q_norm, l0_mla_w_uq, l0_mla_kv_norm, l0_mla_w_ukv,
                                  l0_nsa_cmp_pos_k, l0_nsa_cmp_w1_k, l0_nsa_cmp_w2_k,
                                  l0_nsa_cmp_pos_v, l0_nsa_cmp_w1_v, l0_nsa_cmp_w2_v, l0_w_out)
    x2 = _out_ln(acts, weights, x2, vec(l0_ln_mix_g), vec(l0_ln_mix_b))
    x2 = _mlp(x2, l0_w_up.astype(BF16), l0_w_down.astype(BF16), vec(l0_ln_ffn_g), vec(l0_ln_ffn_b))
    acts, weights = _layer1_mixer(x2, b, s, l1_w_qkv, l1_lam_q1, l1_lam_k1, l1_lam_q2, l1_lam_k2,
                                  l1_subln_g, l1_w_o, 1)
    x2 = _out_ln(acts, weights, x2, vec(l1_ln_mix_g), vec(l1_ln_mix_b))
    x2 = _mlp(x2, l1_w_up.astype(BF16), l1_w_down.astype(BF16), vec(l1_ln_ffn_g), vec(l1_ln_ffn_b))
    return x2.reshape(b, s, d)
```

```python
import functools
import math

import jax
import jax.numpy as jnp
import numpy as np
from jax import lax
from jax.experimental import pallas as pl
from jax.experimental.pallas import tpu as pltpu

F32 = jnp.float32
BF16 = jnp.bfloat16

LANES = 128
SUBLANES = 8
BF16_ROWS = 16
MXU_DEPTH = 256
HEAD_DIM = 64
FLASH_Q_TILE = 512
KV_TILE = 512
KEY_CHUNK = 32
ROW_TILE = 512
FF_TILE = 1024
VMEM_LIMIT = 56 * 1024 * 1024

NEG_INF = -1e30
LOG2E = math.log2(math.e)
LN_EPS = 1e-5
RMS_EPS = 1e-6
DEPTH = 2
DN_ALPHA = (2.0 * DEPTH) ** 0.25

MLA_HEADS = 8
MLA_NOPE = 64
MLA_ROPE = 32
ROPE_THETA = 10000.0
NSA_HEADS = 8
NSA_GROUPS = 2
NSA_HG = NSA_HEADS // NSA_GROUPS
CMP_LEN = 32
CMP_STRIDE = 16
SEL_LEN = 64
SEL_TOPK = 16
WINDOW = 512
FORCE_BONUS = 1e3
DIFF_HEADS = 8

POS_SPLIT = 16
FEATURE_ROWS = 16
BLOCK_LANE0 = 8
BLOCKS_PER_TILE = KV_TILE // SEL_LEN


def _params(*sem):
    return pltpu.CompilerParams(dimension_semantics=sem, vmem_limit_bytes=VMEM_LIMIT)


def _dot(a, b):
    return jnp.dot(a, b, preferred_element_type=F32)


def _split_bf16(x):
    hi = x.astype(BF16)
    lo = (x - hi.astype(F32)).astype(BF16)
    return hi, lo


def _layer_norm(z, g, b):
    mu = jnp.mean(z, axis=-1, keepdims=True)
    zc = z - mu
    var = jnp.mean(zc * zc, axis=-1, keepdims=True)
    return zc * lax.rsqrt(var + LN_EPS) * g + b


def _rms_norm(z, g, eps):
    return z * lax.rsqrt(jnp.mean(z * z, axis=-1, keepdims=True) + eps) * g


def _lane_iota(shape):
    return lax.broadcasted_iota(jnp.int32, shape, 1)


def _keep_half(x, half):
    lane = _lane_iota(x.shape)
    keep = (lane < HEAD_DIM) if half == 0 else (lane >= HEAD_DIM)
    return jnp.where(keep, x, jnp.zeros_like(x))


def _move_head(slab, src_half, dst_half):
    if src_half != dst_half:
        slab = pltpu.roll(slab, HEAD_DIM, 1)
    return _keep_half(slab, dst_half)


def _store_transposed(o_ref, res):
    for c in range(res.shape[1] // LANES):
        cols = slice(c * LANES, (c + 1) * LANES)
        o_ref[0, 0, cols, :] = res[:, cols].T.astype(o_ref.dtype)


def _proj_kernel(x_ref, *refs, transposed):
    n_out = len(transposed)
    w_refs, o_refs = refs[:n_out], refs[n_out:]
    xb = x_ref[...].astype(BF16)
    for w_ref, o_ref, t in zip(w_refs, o_refs, transposed):
        res = _dot(xb, w_ref[...])
        if t:
            _store_transposed(o_ref, res)
        else:
            o_ref[...] = res.astype(o_ref.dtype)


def _transposed_out(b, seq, width, tm):
    per_seq = seq // tm
    spec = pl.BlockSpec((1, 1, width, tm), lambda i: (i // per_seq, i % per_seq, 0, 0))
    return spec, jax.ShapeDtypeStruct((b, per_seq, width, tm), BF16)


def _project(x, weights, out_dtypes, transposed, b, seq):
    m, k = x.shape
    tm = min(KV_TILE, seq)
    specs, shapes = [], []
    for w, dt, t in zip(weights, out_dtypes, transposed):
        if t:
            spec, shape = _transposed_out(b, seq, w.shape[1], tm)
        else:
            spec, shape = pl.BlockSpec((tm, w.shape[1]), lambda i: (i, 0)), jax.ShapeDtypeStruct((m, w.shape[1]), dt)
        specs.append(spec)
        shapes.append(shape)
    return pl.pallas_call(
        functools.partial(_proj_kernel, transposed=tuple(transposed)),
        grid=(m // tm,),
        in_specs=[pl.BlockSpec((tm, k), lambda i: (i, 0))]
        + [pl.BlockSpec(w.shape, lambda i: (0, 0)) for w in weights],
        out_specs=specs,
        out_shape=shapes,
        compiler_params=_params("parallel"),
        name="project",
    )(x, *weights)


def _rope_slab(slab, c, s1, s2):
    half = MLA_ROPE // 2
    up = pltpu.roll(slab, half, 1)
    down = pltpu.roll(slab, LANES - half, 1)
    return slab * c + down * s1 + up * s2


def _mla_prep_kernel(ql_ref, kvl_ref, kpe_ref, qg_ref, kvg_ref, wq_ref, wk_ref, wv_ref,
                     c_ref, s1_ref, s2_ref, q_ref, k_ref, vt_ref, *, q_scale):
    c, s1, s2 = c_ref[...], s1_ref[...], s2_ref[...]
    qn = _rms_norm(ql_ref[...], qg_ref[...], RMS_EPS).astype(BF16)
    kvn = _rms_norm(kvl_ref[...], kvg_ref[...], RMS_EPS).astype(BF16)
    q = _dot(qn, wq_ref[...])
    k = _dot(kvn, wk_ref[...])
    _store_transposed(vt_ref, _dot(kvn, wv_ref[...]))
    kpe = _rope_slab(kpe_ref[...], c, s1, s2)
    for h in range(MLA_HEADS):
        sl = slice(h * LANES, (h + 1) * LANES)
        q_ref[:, sl] = (_rope_slab(q[:, sl], c, s1, s2) * q_scale).astype(q_ref.dtype)
        k_ref[:, sl] = (k[:, sl] + kpe).astype(k_ref.dtype)


def _mla_prep(slab_a, q_gain, kv_gain, wq, wk, wv, rope_c, rope_s1, rope_s2, b, seq):
    m = slab_a.shape[0]
    tm = min(KV_TILE, seq)
    per_seq = seq // tm
    rank = q_gain.shape[1]
    row = lambda j: (lambda i: (i, j))
    tab = lambda i: (i % per_seq, 0)
    const = lambda i: (0, 0)
    hw = MLA_HEADS * LANES
    vt_spec, vt_shape = _transposed_out(b, seq, wv.shape[1], tm)
    return pl.pallas_call(
        functools.partial(_mla_prep_kernel, q_scale=float((MLA_NOPE + MLA_ROPE) ** -0.5 * LOG2E)),
        grid=(m // tm,),
        in_specs=[pl.BlockSpec((tm, rank), row(0)), pl.BlockSpec((tm, rank), row(1)),
                  pl.BlockSpec((tm, LANES), row(2 * rank // LANES)),
                  pl.BlockSpec((1, rank), const), pl.BlockSpec((1, rank), const),
                  pl.BlockSpec(wq.shape, const), pl.BlockSpec(wk.shape, const), pl.BlockSpec(wv.shape, const),
                  pl.BlockSpec((tm, LANES), tab), pl.BlockSpec((tm, LANES), tab), pl.BlockSpec((tm, LANES), tab)],
        out_specs=[pl.BlockSpec((tm, hw), row(0)), pl.BlockSpec((tm, hw), row(0)), vt_spec],
        out_shape=[jax.ShapeDtypeStruct((m, hw), BF16), jax.ShapeDtypeStruct((m, hw), BF16), vt_shape],
        compiler_params=_params("parallel"),
        name="mla_prep",
    )(slab_a, slab_a, slab_a, q_gain, kv_gain, wq, wk, wv, rope_c, rope_s1, rope_s2)


def _flash_scratch(n_streams, v_rows, tq, tk):
    scores = pltpu.VMEM((n_streams, tk, tq), F32)
    stat = pltpu.VMEM((n_streams, 1, tq), F32)
    probs = pltpu.VMEM((n_streams, tk, tq), BF16)
    slot = [scores, probs, stat]
    return slot + slot + [stat, stat, pltpu.VMEM((n_streams, v_rows, tq), F32)]


def _chunk_rows(c):
    return slice(c * KEY_CHUNK, (c + 1) * KEY_CHUNK)


def _fold_rows(x):
    return x.reshape(x.shape[0] // SUBLANES, SUBLANES, x.shape[1])


def _flash_transposed(n_full, n_streams, q_start, tq, tk, key_operand, query_operand, values, offset, scratch):
    slot_a, slot_b, (m_ref, l_ref, acc_ref) = scratch[0:3], scratch[3:6], scratch[6:]
    n_chunks = tk // KEY_CHUNK
    for i in range(n_streams):
        m_ref[i] = jnp.full((1, tq), NEG_INF, F32)
        l_ref[i] = jnp.zeros((1, tq), F32)
        acc_ref[i] = jnp.zeros(acc_ref.shape[1:], F32)
        slot_b[1][i] = jnp.zeros((tk, tq), BF16)
        slot_b[2][i] = jnp.ones((1, tq), F32)

    def column_max(s_ref, i):
        part = jnp.full((SUBLANES, tq), NEG_INF, F32)
        for c in range(n_chunks):
            part = jnp.maximum(part, jnp.max(_fold_rows(s_ref[i, _chunk_rows(c), :]), axis=0))
        return jnp.max(part, axis=0, keepdims=True)

    def stage1(j, slot, i):
        slot[0][i] = _dot(key_operand(j, i), query_operand(j, i))

    def stage2(j, slot, i, causal):
        s_ref, p_ref, alpha_ref = slot
        if causal is not None:
            s_ref[i] = jnp.where(causal, s_ref[i], NEG_INF)
        mx = column_max(s_ref, i)
        off = offset(j, i)
        m_prev = m_ref[i]
        if off is None:
            m_next = jnp.maximum(m_prev, mx)
            shift = m_next
        else:
            m_next = jnp.maximum(m_prev, mx + off)
            shift = m_next - off
        alpha = jnp.exp2(m_prev - m_next)
        part = jnp.zeros((SUBLANES, tq), F32)
        for c in range(n_chunks):
            p = jnp.exp2(s_ref[i, _chunk_rows(c), :] - shift)
            part = part + jnp.sum(_fold_rows(p), axis=0)
            p_ref[i, _chunk_rows(c), :] = p.astype(BF16)
        l_ref[i] = alpha * l_ref[i] + jnp.sum(part, axis=0, keepdims=True)
        m_ref[i] = m_next
        alpha_ref[i] = alpha

    def step(accumulate=None, produce=None, exponentiate=None, diagonal=False):
        causal = None
        if diagonal:
            key_minus_query = (lax.broadcasted_iota(jnp.int32, (tk, tq), 0)
                               - lax.broadcasted_iota(jnp.int32, (tk, tq), 1))
            causal = key_minus_query <= q_start - exponentiate[0] * tk
        products = []
        for i in range(n_streams + 1):
            if i < n_streams:
                if accumulate is not None:
                    products.append(_dot(values(accumulate[0], i), accumulate[1][1][i]))
                if produce is not None:
                    stage1(*produce, i)
                if exponentiate is not None:
                    stage2(*exponentiate, i, causal)
            if accumulate is not None and i > 0:
                acc_ref[i - 1] = accumulate[1][2][i - 1] * acc_ref[i - 1] + products[i - 1]

    step(produce=(0, slot_a))

    def pair(t, carry):
        j0 = 2 * t
        step((jnp.maximum(j0 - 1, 0), slot_b), (j0 + 1, slot_b), (j0, slot_a))
        step((j0, slot_a), (j0 + 2, slot_a), (j0 + 1, slot_b))
        return carry

    lax.fori_loop(0, n_full // 2, pair, 0)
    pending = jnp.maximum(n_full - 1 - n_full % 2, 0)

    @pl.when(n_full % 2 == 1)
    def _():
        step((pending, slot_b), (n_full, slot_b), (n_full - 1, slot_a))
        step(accumulate=(n_full - 1, slot_a), exponentiate=(n_full, slot_b), diagonal=True)
        step(accumulate=(n_full, slot_b))

    @pl.when(n_full % 2 == 0)
    def _():
        step(accumulate=(pending, slot_b), exponentiate=(n_full, slot_a), diagonal=True)
        step(accumulate=(n_full, slot_a))


def _flash_result(scratch, i):
    l_ref, acc_ref = scratch[-2], scratch[-1]
    return acc_ref[i] / l_ref[i]


def _kv_rows(j, tk):
    return pl.ds(pl.multiple_of(j * tk, tk), tk)


def _transposed_bf16(x):
    return x.astype(F32).T.astype(BF16)


def _alibi_rows(coef, tq):
    c = jnp.zeros((1, tq), F32) + coef
    hi = c.astype(BF16).astype(F32)
    rest = c - hi
    mid = rest.astype(BF16).astype(F32)
    lo = rest - mid
    zero = jnp.zeros((1, tq), F32)
    return jnp.concatenate([POS_SPLIT * hi, POS_SPLIT * mid, POS_SPLIT * lo, hi, mid, lo, zero, zero], axis=0)


def _augmented_query(q_t, feature_rows):
    tq = q_t.shape[1]
    pad = jnp.zeros((MXU_DEPTH - LANES - FEATURE_ROWS, tq), BF16)
    return jnp.concatenate([q_t, feature_rows.astype(BF16), pad], axis=0)


MLA_STEP_HEADS = 4


def _mla_attn_kernel(q_ref, k_ref, vt_ref, o_ref, *scratch, tq, tk):
    q_start = pl.program_id(2) * tq
    n = MLA_STEP_HEADS
    queries = [_transposed_bf16(q_ref[0, :, hh * LANES:(hh + 1) * LANES]) for hh in range(n)]
    _flash_transposed(
        q_start // tk, n, q_start, tq, tk,
        lambda j, i: k_ref[0, _kv_rows(j, tk), i * LANES:(i + 1) * LANES],
        lambda j, i: queries[i],
        lambda j, i: vt_ref[0, j, i * HEAD_DIM:(i + 1) * HEAD_DIM, :],
        lambda j, i: None, scratch)
    for pair in range(n // 2):
        o_t = jnp.concatenate([_flash_result(scratch, 2 * pair), _flash_result(scratch, 2 * pair + 1)], axis=0)
        o_ref[0, :, pair * LANES:(pair + 1) * LANES] = o_t.T.astype(o_ref.dtype)


def _mla_attn(q, k, vt):
    b, s, _ = q.shape
    tq, tk = min(FLASH_Q_TILE, s), min(KV_TILE, s)
    n = MLA_STEP_HEADS
    groups = MLA_HEADS // n
    return pl.pallas_call(
        functools.partial(_mla_attn_kernel, tq=tq, tk=tk),
        grid=(b, groups, s // tq),
        in_specs=[pl.BlockSpec((1, tq, n * LANES), lambda bi, p, i: (bi, i, p)),
                  pl.BlockSpec((1, s, n * LANES), lambda bi, p, i: (bi, 0, p)),
                  pl.BlockSpec((1, s // tk, n * HEAD_DIM, tk), lambda bi, p, i: (bi, 0, p, 0))],
        out_specs=pl.BlockSpec((1, tq, n * HEAD_DIM), lambda bi, p, i: (bi, i, p)),
        out_shape=jax.ShapeDtypeStruct((b, s, MLA_HEADS * HEAD_DIM), BF16),
        scratch_shapes=_flash_scratch(n, HEAD_DIM, tq, tk),
        compiler_params=_params("parallel", "parallel", "arbitrary"),
        name="mla_attn",
    )(q, k, vt)


def _gelu_tanh(x):
    return 0.5 * x * (1.0 + jnp.tanh(math.sqrt(2.0 / math.pi) * (x + 0.044715 * (x * x * x))))


def _compress_kernel(x_ref, pos_ref, w1a_ref, w1b_ref, w2_ref, k_ref, vt_ref, *, n_real):
    x = x_ref[0]
    n = x.shape[0]
    first = _dot(x, w1a_ref[...])
    second = _dot(x, w1b_ref[...])
    pos_hi, pos_lo = _split_bf16(pos_ref[...])
    bias = (_dot(pos_hi[:8], w1a_ref[...]) + _dot(pos_lo[:8], w1a_ref[...])
            + _dot(pos_hi[8:], w1b_ref[...]) + _dot(pos_lo[8:], w1b_ref[...]))[:1]
    pre = first + pltpu.roll(second, n - 1, 0) + bias
    out = _dot(_gelu_tanh(pre).astype(BF16), w2_ref[...])
    real = lax.broadcasted_iota(jnp.int32, out.shape, 0) < n_real
    out = jnp.where(real, out, 0.0)
    half = out.shape[1] // 2
    k_ref[0] = out[:, :half].astype(k_ref.dtype)
    vt_ref[0] = out[:, half:].T.astype(vt_ref.dtype)


def _compress(x_chunks, pos_exp, w1a, w1b, w2, n_real):
    b, n, width = x_chunks.shape
    half = w2.shape[1] // 2
    const = lambda bi: (0, 0)
    return pl.pallas_call(
        functools.partial(_compress_kernel, n_real=n_real),
        grid=(b,),
        in_specs=[pl.BlockSpec((1, n, width), lambda bi: (bi, 0, 0)),
                  pl.BlockSpec(pos_exp.shape, const), pl.BlockSpec(w1a.shape, const),
                  pl.BlockSpec(w1b.shape, const), pl.BlockSpec(w2.shape, const)],
        out_specs=[pl.BlockSpec((1, n, half), lambda bi: (bi, 0, 0)), pl.BlockSpec((1, half, n), lambda bi: (bi, 0, 0))],
        out_shape=[jax.ShapeDtypeStruct((b, n, half), BF16), jax.ShapeDtypeStruct((b, half, n), BF16)],
        compiler_params=_params("parallel"),
        name="nsa_compress",
    )(x_chunks, pos_exp, w1a, w1b, w2)


def _nsa_head_slope(h):
    return float(2.0 ** (-8.0 * (h + 1) / NSA_HEADS))


def _nsa_queries(q_ref, g, scale):
    out = []
    for hg in range(NSA_HG):
        h = g * NSA_HG + hg
        slab = q_ref[0, :, (h // 2) * LANES:(h // 2 + 1) * LANES].astype(F32) * scale
        out.append(_move_head(slab, h % 2, g))
    return out


def _select_blocks(imp_t, q_pos):
    n_blocks = imp_t.shape[0]
    blk = lax.broadcasted_iota(jnp.int32, imp_t.shape, 0)
    cur = q_pos // SEL_LEN
    forced = jnp.where(blk == 0, 1.0, 0.0) + jnp.where(blk == cur, 1.0, 0.0) + jnp.where(blk == cur - 1, 1.0, 0.0)
    forced = jnp.minimum(forced, 1.0)
    val = jnp.where(blk <= cur, imp_t + FORCE_BONUS * forced, NEG_INF)
    chosen = jnp.zeros(imp_t.shape, F32)
    for _ in range(SEL_TOPK):
        top = jnp.max(val, axis=0, keepdims=True)
        first = jnp.min(jnp.where(val == top, blk, n_blocks), axis=0, keepdims=True)
        hit = blk == first
        chosen = jnp.where(hit, 1.0, chosen)
        val = jnp.where(hit, -jnp.inf, val)
    return jnp.where(chosen > 0.5, 0.0, NEG_INF)


def _masked_softmax_pass(s_ref, p_ref, tiles, tq):
    part = jnp.full((SUBLANES, tq), NEG_INF, F32)
    for t, rows, keep, off in tiles:
        for c in range(rows // KEY_CHUNK):
            s = jnp.where(keep(_chunk_rows(c)), s_ref[t, _chunk_rows(c), :], NEG_INF)
            s_ref[t, _chunk_rows(c), :] = s
            part = jnp.maximum(part, jnp.max(_fold_rows(s), axis=0) + off)
    m = jnp.max(part, axis=0, keepdims=True)
    total = jnp.zeros((SUBLANES, tq), F32)
    for t, rows, keep, off in tiles:
        shift = m - off
        for c in range(rows // KEY_CHUNK):
            p = jnp.exp2(s_ref[t, _chunk_rows(c), :] - shift)
            total = total + jnp.sum(_fold_rows(p), axis=0)
            p_ref[t, _chunk_rows(c), :] = p.astype(BF16)
    return jnp.sum(total, axis=0, keepdims=True), m > 0.5 * NEG_INF


def _nsa_cmp_win_kernel(q_ref, kc_ref, vct_ref, kw_ref, vwt_ref, feat_ref, ovt_ref, oc_ref, ow_ref, sel_ref,
                        end_ref, rel_ref, sc_ref, sw_ref, pc_ref, pw_ref, *, tq, tk):
    qi = pl.program_id(1)
    q_start = qi * tq
    n_cmp = kc_ref.shape[1]
    end_ref[...] = (lax.broadcasted_iota(jnp.int32, (n_cmp, tq), 0) * CMP_STRIDE + (CMP_LEN - 1)
                    - lax.broadcasted_iota(jnp.int32, (n_cmp, tq), 1))
    rel_ref[...] = (lax.broadcasted_iota(jnp.int32, (tk, tq), 0) - lax.broadcasted_iota(jnp.int32, (tk, tq), 1))
    prev_tile = jnp.maximum(qi - 1, 0)
    prev_bound = jnp.where(qi >= 1, 0, tk)
    zeros = jnp.zeros((FEATURE_ROWS - SUBLANES, tq), F32)
    k_cmp = jnp.concatenate([kc_ref[0], feat_ref[0:n_cmp, :]], axis=1)
    k_win = [jnp.concatenate([kw_ref[0, _kv_rows(j, tk), :], feat_ref[...]], axis=1) for j in (prev_tile, qi)]
    q_pos = q_start + lax.broadcasted_iota(jnp.int32, (1, tq), 1)
    for g in range(NSA_GROUPS):
        queries = [_transposed_bf16(q) for q in _nsa_queries(q_ref, g, HEAD_DIM ** -0.5 * LOG2E)]
        group_rows = slice(g * HEAD_DIM, (g + 1) * HEAD_DIM)
        imp_t = jnp.zeros((LANES, tq), F32)
        out_c, out_w = [], []
        for hg in range(NSA_HG):
            coef = _nsa_head_slope(g * NSA_HG + hg) * LOG2E
            buf = hg % 2
            sc, sw, pc, pw = sc_ref.at[buf], sw_ref.at[buf], pc_ref.at[buf], pw_ref.at[buf]
            cmp_query = _augmented_query(queries[hg], jnp.concatenate([_alibi_rows(CMP_STRIDE * coef, tq), zeros], 0))
            win_query = _augmented_query(queries[hg], jnp.concatenate([_alibi_rows(coef, tq), zeros], 0))
            sc[0] = _dot(k_cmp, cmp_query)
            sw[0] = _dot(k_win[0], win_query)
            sw[1] = _dot(k_win[1], win_query)
            norm, has_any = _masked_softmax_pass(
                sc, pc, [(0, n_cmp, lambda r: end_ref[r, :] <= q_start, 0.0)], tq)
            inv = jnp.where(has_any, 1.0 / norm, 0.0)
            out_c.append(_dot(vct_ref[0, group_rows, :], pc[0]) * inv)
            imp_t = imp_t + _dot(ovt_ref[...], pc[0]) * inv
            norm, _ = _masked_softmax_pass(
                sw, pw, [(0, tk, lambda r: rel_ref[r, :] > prev_bound, -coef * tk),
                         (1, tk, lambda r: rel_ref[r, :] <= 0, 0.0)], tq)
            acc = (_dot(vwt_ref[0, prev_tile, group_rows, :], pw[0]) + _dot(vwt_ref[0, qi, group_rows, :], pw[1]))
            out_w.append(acc / norm)
        for pair in range(NSA_HG // 2):
            cols = slice((g * 2 + pair) * LANES, (g * 2 + pair + 1) * LANES)
            oc_ref[0, :, cols] = jnp.concatenate(out_c[2 * pair:2 * pair + 2], axis=0).T
            ow_ref[0, :, cols] = jnp.concatenate(out_w[2 * pair:2 * pair + 2], axis=0).T
        sel_ref[0, g * LANES:(g + 1) * LANES, :] = _select_blocks(imp_t, q_pos)


def _nsa_cmp_win(slab_b, cmp_k, cmp_vt, vt_nsa, key_features, overlap_t, seq):
    b = slab_b.shape[0]
    tq, tk = min(FLASH_Q_TILE, seq), min(KV_TILE, seq)
    assert tq == tk == WINDOW, "the window branch is written for one previous and one diagonal key tile"
    n_cmp = cmp_k.shape[1]
    qw = NSA_HEADS * HEAD_DIM
    base = qw // LANES
    tile = lambda bi, i: (bi, i, 0)
    return pl.pallas_call(
        functools.partial(_nsa_cmp_win_kernel, tq=tq, tk=tk),
        grid=(b, seq // tq),
        in_specs=[pl.BlockSpec((1, tq, qw), tile),
                  pl.BlockSpec((1, n_cmp, LANES), lambda bi, i: (bi, 0, 0)),
                  pl.BlockSpec((1, LANES, n_cmp), lambda bi, i: (bi, 0, 0)),
                  pl.BlockSpec((1, seq, LANES), lambda bi, i: (bi, 0, base + 1)),
                  pl.BlockSpec((1, seq // tk, LANES, tk), lambda bi, i: (bi, 0, 1, 0)),
                  pl.BlockSpec(key_features.shape, lambda bi, i: (0, 0)),
                  pl.BlockSpec(overlap_t.shape, lambda bi, i: (0, 0))],
        out_specs=[pl.BlockSpec((1, tq, qw), tile), pl.BlockSpec((1, tq, qw), tile),
                   pl.BlockSpec((1, NSA_GROUPS * LANES, tq), lambda bi, i: (bi, 0, i))],
        out_shape=[jax.ShapeDtypeStruct((b, seq, qw), F32), jax.ShapeDtypeStruct((b, seq, qw), F32),
                   jax.ShapeDtypeStruct((b, NSA_GROUPS * LANES, seq), F32)],
        scratch_shapes=[pltpu.VMEM((n_cmp, tq), jnp.int32), pltpu.VMEM((tk, tq), jnp.int32),
                        pltpu.VMEM((2, 1, n_cmp, tq), F32), pltpu.VMEM((2, 2, tk, tq), F32),
                        pltpu.VMEM((2, 1, n_cmp, tq), BF16), pltpu.VMEM((2, 2, tk, tq), BF16)],
        compiler_params=_params("parallel", "arbitrary"),
        name="nsa_cmp_win",
    )(slab_b, cmp_k, cmp_vt, slab_b, vt_nsa, key_features, overlap_t)


def _nsa_sel_kernel(q_ref, k_ref, vt_ref, feat_ref, sel_ref, oc_ref, ow_ref, gate_ref, gx_ref, o_ref,
                    *scratch, tq, tk):
    q_start = pl.program_id(1) * tq
    n_full = q_start // tk
    out_slabs = []
    for g in range(NSA_GROUPS):
        queries = [_transposed_bf16(q) for q in _nsa_queries(q_ref, g, HEAD_DIM ** -0.5 * LOG2E)]
        coefs = [_nsa_head_slope(g * NSA_HG + hg) * LOG2E for hg in range(NSA_HG)]
        alibi = [_alibi_rows(c, tq) for c in coefs]

        def key_operand(j, i):
            return jnp.concatenate([k_ref[0, _kv_rows(j, tk), :], feat_ref[...]], axis=1)

        def query_operand(j, i, g=g, queries=queries, alibi=alibi):
            first_block = pl.multiple_of(g * LANES + j * BLOCKS_PER_TILE, BLOCKS_PER_TILE)
            blocks = sel_ref[0, pl.ds(first_block, BLOCKS_PER_TILE), :]
            return _augmented_query(queries[i], jnp.concatenate([alibi[i], blocks], axis=0))

        _flash_transposed(
            n_full, NSA_HG, q_start, tq, tk, key_operand, query_operand,
            lambda j, i, g=g: vt_ref[0, j, g * HEAD_DIM:(g + 1) * HEAD_DIM, :],
            lambda j, i, coefs=coefs: coefs[i] * (j * tk - q_start).astype(F32), scratch)
        heads = [_flash_result(scratch, hg) for hg in range(NSA_HG)]
        for pair in range(NSA_HG // 2):
            out_slabs.append(jnp.concatenate(heads[2 * pair:2 * pair + 2], axis=0).T)
    gates = jax.nn.sigmoid(gate_ref[...])
    g_hi, g_lo = _split_bf16(gates)
    width = NSA_HEADS * HEAD_DIM
    for i, o_sel in enumerate(out_slabs):
        mixed = None
        for branch, o_branch in enumerate((oc_ref[0, :, i * LANES:(i + 1) * LANES], o_sel,
                                           ow_ref[0, :, i * LANES:(i + 1) * LANES])):
            gx = gx_ref[:, branch * width + i * LANES:branch * width + (i + 1) * LANES]
            term = (_dot(g_hi, gx) + _dot(g_lo, gx)) * o_branch
            mixed = term if mixed is None else mixed + term
        o_ref[0, :, i * LANES:(i + 1) * LANES] = mixed.astype(o_ref.dtype)


def _nsa_sel(slab_b, vt, key_features, sel_bias_t, o_cmp, o_win, slab_a, gate_expand, seq, gate_col_block):
    b = slab_b.shape[0]
    tq, tk = min(FLASH_Q_TILE, seq), min(KV_TILE, seq)
    qw = NSA_HEADS * HEAD_DIM
    base = qw // LANES
    per_seq = seq // tq
    tile = lambda bi, i: (bi, i, 0)
    return pl.pallas_call(
        functools.partial(_nsa_sel_kernel, tq=tq, tk=tk),
        grid=(b, seq // tq),
        in_specs=[pl.BlockSpec((1, tq, qw), tile),
                  pl.BlockSpec((1, seq, LANES), lambda bi, i: (bi, 0, base)),
                  pl.BlockSpec((1, seq // tk, LANES, tk), lambda bi, i: (bi, 0, 0, 0)),
                  pl.BlockSpec(key_features.shape, lambda bi, i: (0, 0)),
                  pl.BlockSpec((1, NSA_GROUPS * LANES, tq), lambda bi, i: (bi, 0, i)),
                  pl.BlockSpec((1, tq, qw), tile), pl.BlockSpec((1, tq, qw), tile),
                  pl.BlockSpec((tq, LANES), lambda bi, i: (bi * per_seq + i, gate_col_block)),
                  pl.BlockSpec(gate_expand.shape, lambda bi, i: (0, 0))],
        out_specs=pl.BlockSpec((1, tq, qw), tile),
        out_shape=jax.ShapeDtypeStruct((b, seq, qw), BF16),
        scratch_shapes=_flash_scratch(NSA_HG, HEAD_DIM, tq, tk),
        compiler_params=_params("parallel", "arbitrary"),
        name="nsa_sel",
    )(slab_b, slab_b, vt, key_features, sel_bias_t, o_cmp, o_win, slab_a, gate_expand)


DIFF_STEP_HEADS = 2


def _diff_attn_kernel(slope_ref, lam_ref, q_ref, k_ref, vt_ref, feat_ref, g_ref, o_ref, *scratch,
                      tq, tk, lam_init):
    first_head = pl.program_id(1) * DIFF_STEP_HEADS
    q_start = pl.program_id(2) * tq
    zeros = jnp.zeros((FEATURE_ROWS - SUBLANES, tq), F32)
    coefs, queries = [], []
    for hh in range(DIFF_STEP_HEADS):
        coef = slope_ref[first_head + hh] * LOG2E
        q = q_ref[0, :, hh * LANES:(hh + 1) * LANES].astype(F32) * (HEAD_DIM ** -0.5 * LOG2E)
        features = jnp.concatenate([_alibi_rows(coef, tq), zeros], axis=0)
        coefs.append(coef)
        queries += [_augmented_query(_transposed_bf16(_keep_half(q, half)), features) for half in range(2)]

    def head_lanes(i):
        return slice((i // 2) * LANES, (i // 2 + 1) * LANES)

    _flash_transposed(
        q_start // tk, 2 * DIFF_STEP_HEADS, q_start, tq, tk,
        lambda j, i: jnp.concatenate([k_ref[0, _kv_rows(j, tk), head_lanes(i)], feat_ref[...]], axis=1),
        lambda j, i: queries[i],
        lambda j, i: vt_ref[0, j, head_lanes(i), :],
        lambda j, i: coefs[i // 2] * (j * tk - q_start).astype(F32), scratch)
    lam_vec = lam_ref[...]
    lam = (jnp.exp(jnp.sum(lam_vec[0:1] * lam_vec[1:2], axis=1, keepdims=True))
           - jnp.exp(jnp.sum(lam_vec[2:3] * lam_vec[3:4], axis=1, keepdims=True)) + lam_init)
    for hh in range(DIFF_STEP_HEADS):
        o = (_flash_result(scratch, 2 * hh) - lam * _flash_result(scratch, 2 * hh + 1)).T
        o_ref[0, :, hh * LANES:(hh + 1) * LANES] = (
            _rms_norm(o, g_ref[...], RMS_EPS) * (1.0 - lam_init)).astype(o_ref.dtype)


def _diff_attn(qk, vt, key_features, slopes, lam_vecs, subln_g, lam_init):
    b, s, _ = qk.shape
    tq, tk = min(FLASH_Q_TILE, s), min(KV_TILE, s)
    n = DIFF_STEP_HEADS
    groups = DIFF_HEADS // n
    smem = pl.BlockSpec(memory_space=pltpu.SMEM)
    return pl.pallas_call(
        functools.partial(_diff_attn_kernel, tq=tq, tk=tk, lam_init=lam_init),
        grid=(b, groups, s // tq),
        in_specs=[smem, pl.BlockSpec(lam_vecs.shape, lambda bi, h, i: (0, 0)),
                  pl.BlockSpec((1, tq, n * LANES), lambda bi, h, i: (bi, i, h)),
                  pl.BlockSpec((1, s, n * LANES), lambda bi, h, i: (bi, 0, groups + h)),
                  pl.BlockSpec((1, s // tk, n * LANES, tk), lambda bi, h, i: (bi, 0, h, 0)),
                  pl.BlockSpec(key_features.shape, lambda bi, h, i: (0, 0)),
                  pl.BlockSpec((1, LANES), lambda bi, h, i: (0, 0))],
        out_specs=pl.BlockSpec((1, tq, n * LANES), lambda bi, h, i: (bi, i, h)),
        out_shape=jax.ShapeDtypeStruct((b, s, DIFF_HEADS * LANES), BF16),
        scratch_shapes=_flash_scratch(2 * n, LANES, tq, tk),
        compiler_params=_params("parallel", "parallel", "arbitrary"),
        name="diff_attn",
    )(slopes, lam_vecs, qk, qk, vt, key_features, subln_g)


def _out_ln_kernel(*refs, n_in):
    a_refs, w_refs = refs[:n_in], refs[n_in:2 * n_in]
    x_ref, g_ref, b_ref, o_ref = refs[2 * n_in:]
    y = None
    for a_ref, w_ref in zip(a_refs, w_refs):
        t = _dot(a_ref[...], w_ref[...])
        y = t if y is None else y + t
    o_ref[...] = _layer_norm(DN_ALPHA * x_ref[...] + y, g_ref[...], b_ref[...])


def _out_ln(acts, weights, x, g, b):
    m, d = x.shape
    tm = min(ROW_TILE, m)
    row = lambda i: (i, 0)
    const = lambda i: (0, 0)
    return pl.pallas_call(
        functools.partial(_out_ln_kernel, n_in=len(acts)),
        grid=(m // tm,),
        in_specs=[pl.BlockSpec((tm, a.shape[1]), row) for a in acts]
        + [pl.BlockSpec(w.shape, const) for w in weights]
        + [pl.BlockSpec((tm, d), row), pl.BlockSpec((1, d), const), pl.BlockSpec((1, d), const)],
        out_specs=pl.BlockSpec((tm, d), row),
        out_shape=jax.ShapeDtypeStruct((m, d), F32),
        compiler_params=_params("parallel"),
        name="out_proj_ln",
    )(*acts, *weights, x, g, b)


def _mlp_kernel(x_ref, wu_ref, wd_ref, g_ref, b_ref, o_ref, xb_ref, acc_ref):
    f = pl.program_id(1)

    @pl.when(f == 0)
    def _():
        xb_ref[...] = x_ref[...].astype(BF16)
        acc_ref[...] = jnp.zeros(acc_ref.shape, F32)

    hidden = jnp.maximum(_dot(xb_ref[...], wu_ref[...]), 0.0)
    acc_ref[...] += _dot((hidden * hidden).astype(BF16), wd_ref[...])

    @pl.when(f == pl.num_programs(1) - 1)
    def _():
        o_ref[...] = _layer_norm(DN_ALPHA * x_ref[...] + acc_ref[...], g_ref[...], b_ref[...])


def _mlp(x, w_up, w_down, g, b):
    m, d = x.shape
    ff = w_up.shape[1]
    tm, tf = min(ROW_TILE, m), min(FF_TILE, ff)
    return pl.pallas_call(
        _mlp_kernel,
        grid=(m // tm, ff // tf),
        in_specs=[pl.BlockSpec((tm, d), lambda i, f: (i, 0)),
                  pl.BlockSpec((d, tf), lambda i, f: (0, f)),
                  pl.BlockSpec((tf, d), lambda i, f: (f, 0)),
                  pl.BlockSpec((1, d), lambda i, f: (0, 0)), pl.BlockSpec((1, d), lambda i, f: (0, 0))],
        out_specs=pl.BlockSpec((tm, d), lambda i, f: (i, 0)),
        out_shape=jax.ShapeDtypeStruct((m, d), F32),
        scratch_shapes=[pltpu.VMEM((tm, d), BF16), pltpu.VMEM((tm, d), F32)],
        compiler_params=_params("parallel", "arbitrary"),
        name="mlp_ln",
    )(x, w_up, w_down, g, b)


def _pad_cols(w, width):
    return jnp.pad(w, ((0, 0), (0, width - w.shape[1])))


def _layer0_weights(w_in, w_uq, w_ukv, d_model):
    rank = d_model // 4
    kvw = NSA_GROUPS * HEAD_DIM
    o = np.cumsum([0, rank, rank, MLA_ROPE, NSA_HEADS * HEAD_DIM] + [kvw] * 6 + [3 * NSA_HEADS])
    seg = lambda i: w_in[:, o[i]:o[i + 1]]
    zeros = lambda n: jnp.zeros((w_in.shape[0], n), w_in.dtype)
    rope_slab = jnp.concatenate([zeros(MLA_NOPE), seg(2), zeros(LANES - MLA_NOPE - MLA_ROPE)], axis=1)
    w_a = jnp.concatenate([seg(0), seg(1), rope_slab, _pad_cols(seg(10), LANES)], axis=1)
    w_b = jnp.concatenate([seg(3), seg(6), seg(8)], axis=1)
    w_c = jnp.concatenate([seg(4), seg(5)], axis=1)
    w_vs = jnp.concatenate([seg(7), seg(9)], axis=1)
    wq = jnp.pad(w_uq.reshape(rank, MLA_HEADS, MLA_NOPE + MLA_ROPE),
                 ((0, 0), (0, 0), (0, LANES - MLA_NOPE - MLA_ROPE))).reshape(rank, MLA_HEADS * LANES)
    ukv = w_ukv.reshape(rank, MLA_HEADS, MLA_NOPE + HEAD_DIM)
    wk = jnp.pad(ukv[:, :, :MLA_NOPE], ((0, 0), (0, 0), (0, LANES - MLA_NOPE))).reshape(rank, MLA_HEADS * LANES)
    wv = ukv[:, :, MLA_NOPE:].reshape(rank, MLA_HEADS * HEAD_DIM)
    return [w.astype(BF16) for w in (w_a, w_b, w_c, w_vs, wq, wk, wv)]


def _rope_tables(seq):
    inv = 1.0 / (ROPE_THETA ** (jnp.arange(0, MLA_ROPE, 2, dtype=F32) / MLA_ROPE))
    ang = jnp.arange(seq, dtype=F32)[:, None] * inv[None, :]
    cos, sin = jnp.cos(ang), jnp.sin(ang)
    half = MLA_ROPE // 2
    z = lambda n: jnp.zeros((seq, n), F32)
    tail = LANES - MLA_NOPE - MLA_ROPE
    c = jnp.concatenate([jnp.ones((seq, MLA_NOPE), F32), cos, cos, z(tail)], axis=1)
    s1 = jnp.concatenate([z(MLA_NOPE), -sin, z(half), z(tail)], axis=1)
    s2 = jnp.concatenate([z(MLA_NOPE), z(half), sin, z(tail)], axis=1)
    return c, s1, s2


def _compress_weights(pos_k, w1_k, w2_k, pos_v, w1_v, w2_v):
    eye = jnp.eye(2 * NSA_GROUPS, dtype=F32)
    halves = []
    for a in range(CMP_LEN // CMP_STRIDE):
        rows = slice(a * CMP_STRIDE * HEAD_DIM, (a + 1) * CMP_STRIDE * HEAD_DIM)
        wk = w1_k[rows].reshape(CMP_STRIDE, HEAD_DIM, HEAD_DIM)
        wv = w1_v[rows].reshape(CMP_STRIDE, HEAD_DIM, HEAD_DIM)
        per_slot = jnp.stack([wk, wk, wv, wv], axis=0)
        full = jnp.einsum('st,srdj->rsdtj', eye, per_slot)
        halves.append(full.reshape(CMP_STRIDE * 4 * HEAD_DIM, 4 * HEAD_DIM).astype(BF16))
    w2 = jnp.einsum('st,sdj->sdtj', eye, jnp.stack([w2_k, w2_k, w2_v, w2_v])).reshape(4 * HEAD_DIM, 4 * HEAD_DIM)
    pos = jnp.concatenate([pos_k, pos_k, pos_v, pos_v], axis=1)
    pos = pos.reshape(CMP_LEN // CMP_STRIDE, 1, CMP_STRIDE * 4 * HEAD_DIM)
    pos = jnp.broadcast_to(pos, (pos.shape[0], 8, pos.shape[2])).reshape(-1, pos.shape[2])
    return pos, halves[0], halves[1], w2.astype(BF16)


def _overlap_table(n_cmp_pad, n_cmp):
    c0 = np.arange(n_cmp_pad)[None, :] * CMP_STRIDE
    s0 = np.arange(LANES)[:, None] * SEL_LEN
    ov = np.maximum(np.minimum(c0 + CMP_LEN, s0 + SEL_LEN) - np.maximum(c0, s0), 0) / CMP_LEN
    ov = ov * (np.arange(n_cmp_pad)[None, :] < n_cmp)
    return jnp.asarray(ov, BF16)


def _key_feature_table(tk):
    c = np.arange(tk)
    table = np.zeros((tk, LANES), np.float32)
    table[:, 0:3] = (c // POS_SPLIT)[:, None]
    table[:, 3:6] = (c % POS_SPLIT)[:, None]
    table[c, BLOCK_LANE0 + c // SEL_LEN] = 1.0
    return jnp.asarray(table, BF16)


def _gate_expand_table():
    width = NSA_HEADS * HEAD_DIM
    table = np.zeros((LANES, 3 * width), np.float32)
    for h in range(NSA_HEADS):
        for branch in range(3):
            table[h * 3 + branch, branch * width + h * HEAD_DIM:branch * width + (h + 1) * HEAD_DIM] = 1.0
    return jnp.asarray(table, BF16)


def _alibi_slopes(n):
    return jnp.asarray(2.0 ** (-8.0 * np.arange(1, n + 1) / n), dtype=F32)


def _layer0_mixer(x2, b, s, w_in, q_norm, w_uq, kv_norm, w_ukv, pos_k, w1_k, w2_k, pos_v, w1_v, w2_v, w_out):
    d = x2.shape[1]
    rank = d // 4
    w_a, w_b, w_c, w_vs, wq, wk, wv = _layer0_weights(w_in, w_uq, w_ukv, d)
    slab_a, slab_b, slab_c, vt_nsa = _project(x2, [w_a, w_b, w_c, w_vs], [F32, BF16, BF16, BF16],
                                              [False, False, False, True], b, s)
    rope_c, rope_s1, rope_s2 = _rope_tables(s)
    q, k, vt = _mla_prep(slab_a, q_norm.reshape(1, rank), kv_norm.reshape(1, rank), wq, wk, wv,
                         rope_c, rope_s1, rope_s2, b, s)
    o_mla = _mla_attn(q.reshape(b, s, -1), k.reshape(b, s, -1), vt)
    n_chunks = s // CMP_STRIDE
    n_cmp = (s - CMP_LEN) // CMP_STRIDE + 1
    pos, w1a, w1b, w2 = _compress_weights(pos_k, w1_k, w2_k, pos_v, w1_v, w2_v)
    cmp_k, cmp_vt = _compress(slab_c.reshape(b, n_chunks, CMP_STRIDE * slab_c.shape[1]), pos, w1a, w1b, w2, n_cmp)
    slab_b3 = slab_b.reshape(b, s, -1)
    key_features = _key_feature_table(min(KV_TILE, s))
    o_cmp, o_win, sel_bias_t = _nsa_cmp_win(slab_b3, cmp_k, cmp_vt, vt_nsa, key_features,
                                            _overlap_table(n_chunks, n_cmp), s)
    o_nsa = _nsa_sel(slab_b3, vt_nsa, key_features, sel_bias_t, o_cmp, o_win, slab_a,
                     _gate_expand_table(), s, (2 * rank + LANES) // LANES)
    half = o_mla.shape[-1]
    w_out_b = w_out.astype(BF16)
    return [o_mla.reshape(b * s, half), o_nsa.reshape(b * s, -1)], [w_out_b[:half], w_out_b[half:]]


def _layer1_mixer(x2, b, s, w_qkv, lam_q1, lam_k1, lam_q2, lam_k2, subln_g, w_o, layer_idx):
    d = x2.shape[1]
    w = w_qkv.astype(BF16)
    qk, vt = _project(x2, [w[:, :2 * d], w[:, 2 * d:]], [BF16, BF16], [False, True], b, s)
    lam_init = 0.8 - 0.6 * math.exp(-0.3 * layer_idx)
    lam_vecs = jnp.stack([lam_q1, lam_k1, lam_q2, lam_k2]).astype(F32)
    o = _diff_attn(qk.reshape(b, s, -1), vt, _key_feature_table(min(KV_TILE, s)), _alibi_slopes(DIFF_HEADS),
                   lam_vecs, subln_g.reshape(1, -1), lam_init)
    return [o.reshape(b * s, -1)], [w_o.astype(BF16)]


def kernel(x, l0_w_in, l0_mla_q_norm, l0_mla_w_uq, l0_mla_kv_norm, l0_mla_w_ukv, l0_nsa_cmp_pos_k, l0_nsa_cmp_w1_k, l0_nsa_cmp_w2_k, l0_nsa_cmp_pos_v, l0_nsa_cmp_w1_v, l0_nsa_cmp_w2_v, l0_w_out, l0_ln_mix_g, l0_ln_mix_b, l0_w_up, l0_w_down, l0_ln_ffn_g, l0_ln_ffn_b, l1_w_qkv, l1_lam_q1, l1_lam_k1, l1_lam_q2, l1_lam_k2, l1_subln_g, l1_w_o, l1_ln_mix_g, l1_ln_mix_b, l1_w_up, l1_w_down, l1_ln_ffn_g, l1_ln_ffn_b):
    b, s, d = x.shape
    x2 = x.reshape(b * s, d)
    vec = lambda p: p.reshape(1, d)
    acts, weights = _layer0_mixer(x2, b, s, l0_w_in, l0_mla_q_norm, l0_mla_w_uq, l0_mla_kv_norm, l0_mla_w_ukv,
                                  l0_nsa_cmp_pos_k, l0_nsa_cmp_w1_k, l0_nsa_cmp_w2_k,
                                  l0_nsa_cmp_pos_v, l0_nsa_cmp_w1_v, l0_nsa_cmp_w2_v, l0_w_out)
    x2 = _out_ln(acts, weights, x2, vec(l0_ln_mix_g), vec(l0_ln_mix_b))
    x2 = _mlp(x2, l0_w_up.astype(BF16), l0_w_down.astype(BF16), vec(l0_ln_ffn_g), vec(l0_ln_ffn_b))
    acts, weights = _layer1_mixer(x2, b, s, l1_w_qkv, l1_lam_q1, l1_lam_k1, l1_lam_q2, l1_lam_k2,
                                  l1_subln_g, l1_w_o, 1)
    x2 = _out_ln(acts, weights, x2, vec(l1_ln_mix_g), vec(l1_ln_mix_b))
    x2 = _mlp(x2, l1_w_up.astype(BF16), l1_w_down.astype(BF16), vec(l1_ln_ffn_g), vec(l1_ln_ffn_b))
    return x2.reshape(b, s, d)
```

```python
import functools
import math

import jax
import jax.numpy as jnp
import numpy as np
from jax import lax
from jax.experimental import pallas as pl
from jax.experimental.pallas import tpu as pltpu

F32 = jnp.float32
BF16 = jnp.bfloat16

LANES = 128
SUBLANES = 8
BF16_ROWS = 16
MXU_DEPTH = 256
HEAD_DIM = 64
FLASH_Q_TILE = 512
KV_TILE = 512
KEY_CHUNK = 32
ROW_TILE = 512
FF_TILE = 1024
VMEM_LIMIT = 56 * 1024 * 1024

NEG_INF = -1e30
LOG2E = math.log2(math.e)
LN_EPS = 1e-5
RMS_EPS = 1e-6
DEPTH = 2
DN_ALPHA = (2.0 * DEPTH) ** 0.25

MLA_HEADS = 8
MLA_NOPE = 64
MLA_ROPE = 32
ROPE_THETA = 10000.0
NSA_HEADS = 8
NSA_GROUPS = 2
NSA_HG = NSA_HEADS // NSA_GROUPS
CMP_LEN = 32
CMP_STRIDE = 16
SEL_LEN = 64
SEL_TOPK = 16
WINDOW = 512
FORCE_BONUS = 1e3
DIFF_HEADS = 8

POS_SPLIT = 16
FEATURE_ROWS = 16
BLOCK_LANE0 = 8
BLOCKS_PER_TILE = KV_TILE // SEL_LEN


def _params(*sem):
    return pltpu.CompilerParams(dimension_semantics=sem, vmem_limit_bytes=VMEM_LIMIT)


def _dot(a, b):
    return jnp.dot(a, b, preferred_element_type=F32)


def _split_bf16(x):
    hi = x.astype(BF16)
    lo = (x - hi.astype(F32)).astype(BF16)
    return hi, lo


def _layer_norm(z, g, b):
    mu = jnp.mean(z, axis=-1, keepdims=True)
    zc = z - mu
    var = jnp.mean(zc * zc, axis=-1, keepdims=True)
    return zc * lax.rsqrt(var + LN_EPS) * g + b


def _rms_norm(z, g, eps):
    return z * lax.rsqrt(jnp.mean(z * z, axis=-1, keepdims=True) + eps) * g


def _lane_iota(shape):
    return lax.broadcasted_iota(jnp.int32, shape, 1)


def _keep_half(x, half):
    lane = _lane_iota(x.shape)
    keep = (lane < HEAD_DIM) if half == 0 else (lane >= HEAD_DIM)
    return jnp.where(keep, x, jnp.zeros_like(x))


def _move_head(slab, src_half, dst_half):
    if src_half != dst_half:
        slab = pltpu.roll(slab, HEAD_DIM, 1)
    return _keep_half(slab, dst_half)


def _store_transposed(o_ref, res):
    for c in range(res.shape[1] // LANES):
        cols = slice(c * LANES, (c + 1) * LANES)
        o_ref[0, 0, cols, :] = res[:, cols].T.astype(o_ref.dtype)


def _proj_kernel(x_ref, *refs, transposed):
    n_out = len(transposed)
    w_refs, o_refs = refs[:n_out], refs[n_out:]
    xb = x_ref[...].astype(BF16)
    for w_ref, o_ref, t in zip(w_refs, o_refs, transposed):
        res = _dot(xb, w_ref[...])
        if t:
            _store_transposed(o_ref, res)
        else:
            o_ref[...] = res.astype(o_ref.dtype)


def _transposed_out(b, seq, width, tm):
    per_seq = seq // tm
    spec = pl.BlockSpec((1, 1, width, tm), lambda i: (i // per_seq, i % per_seq, 0, 0))
    return spec, jax.ShapeDtypeStruct((b, per_seq, width, tm), BF16)


def _project(x, weights, out_dtypes, transposed, b, seq):
    m, k = x.shape
    tm = min(KV_TILE, seq)
    specs, shapes = [], []
    for w, dt, t in zip(weights, out_dtypes, transposed):
        if t:
            spec, shape = _transposed_out(b, seq, w.shape[1], tm)
        else:
            spec, shape = pl.BlockSpec((tm, w.shape[1]), lambda i: (i, 0)), jax.ShapeDtypeStruct((m, w.shape[1]), dt)
        specs.append(spec)
        shapes.append(shape)
    return pl.pallas_call(
        functools.partial(_proj_kernel, transposed=tuple(transposed)),
        grid=(m // tm,),
        in_specs=[pl.BlockSpec((tm, k), lambda i: (i, 0))]
        + [pl.BlockSpec(w.shape, lambda i: (0, 0)) for w in weights],
        out_specs=specs,
        out_shape=shapes,
        compiler_params=_params("parallel"),
        name="project",
    )(x, *weights)


def _rope_slab(slab, c, s1, s2):
    half = MLA_ROPE // 2
    up = pltpu.roll(slab, half, 1)
    down = pltpu.roll(slab, LANES - half, 1)
    return slab * c + down * s1 + up * s2


def _mla_prep_kernel(ql_ref, kvl_ref, kpe_ref, qg_ref, kvg_ref, wq_ref, wk_ref, wv_ref,
                     c_ref, s1_ref, s2_ref, q_ref, k_ref, vt_ref, *, q_scale):
    c, s1, s2 = c_ref[...], s1_ref[...], s2_ref[...]
    qn = _rms_norm(ql_ref[...], qg_ref[...], RMS_EPS).astype(BF16)
    kvn = _rms_norm(kvl_ref[...], kvg_ref[...], RMS_EPS).astype(BF16)
    q = _dot(qn, wq_ref[...])
    k = _dot(kvn, wk_ref[...])
    _store_transposed(vt_ref, _dot(kvn, wv_ref[...]))
    kpe = _rope_slab(kpe_ref[...], c, s1, s2)
    for h in range(MLA_HEADS):
        sl = slice(h * LANES, (h + 1) * LANES)
        q_ref[:, sl] = (_rope_slab(q[:, sl], c, s1, s2) * q_scale).astype(q_ref.dtype)
        k_ref[:, sl] = (k[:, sl] + kpe).astype(k_ref.dtype)


def _mla_prep(slab_a, q_gain, kv_gain, wq, wk, wv, rope_c, rope_s1, rope_s2, b, seq):
    m = slab_a.shape[0]
    tm = min(KV_TILE, seq)
    per_seq = seq // tm
    rank = q_gain.shape[1]
    row = lambda j: (lambda i: (i, j))
    tab = lambda i: (i % per_seq, 0)
    const = lambda i: (0, 0)
    hw = MLA_HEADS * LANES
    vt_spec, vt_shape = _transposed_out(b, seq, wv.shape[1], tm)
    return pl.pallas_call(
        functools.partial(_mla_prep_kernel, q_scale=float((MLA_NOPE + MLA_ROPE) ** -0.5 * LOG2E)),
        grid=(m // tm,),
        in_specs=[pl.BlockSpec((tm, rank), row(0)), pl.BlockSpec((tm, rank), row(1)),
                  pl.BlockSpec((tm, LANES), row(2 * rank // LANES)),
                  pl.BlockSpec((1, rank), const), pl.BlockSpec((1, rank), const),
                  pl.BlockSpec(wq.shape, const), pl.BlockSpec(wk.shape, const), pl.BlockSpec(wv.shape, const),
                  pl.BlockSpec((tm, LANES), tab), pl.BlockSpec((tm, LANES), tab), pl.BlockSpec((tm, LANES), tab)],
        out_specs=[pl.BlockSpec((tm, hw), row(0)), pl.BlockSpec((tm, hw), row(0)), vt_spec],
        out_shape=[jax.ShapeDtypeStruct((m, hw), BF16), jax.ShapeDtypeStruct((m, hw), BF16), vt_shape],
        compiler_params=_params("parallel"),
        name="mla_prep",
    )(slab_a, slab_a, slab_a, q_gain, kv_gain, wq, wk, wv, rope_c, rope_s1, rope_s2)


def _flash_scratch(n_streams, v_rows, tq, tk):
    scores = pltpu.VMEM((n_streams, tk, tq), F32)
    stat = pltpu.VMEM((n_streams, 1, tq), F32)
    probs = pltpu.VMEM((n_streams, tk, tq), BF16)
    slot = [scores, probs, stat]
    return slot + slot + [stat, stat, pltpu.VMEM((n_streams, v_rows, tq), F32)]


def _chunk_rows(c):
    return slice(c * KEY_CHUNK, (c + 1) * KEY_CHUNK)


def _fold_rows(x):
    return x.reshape(x.shape[0] // SUBLANES, SUBLANES, x.shape[1])


def _flash_transposed(n_full, n_streams, q_start, tq, tk, key_operand, query_operand, values, offset, scratch,
                      tile_id=lambda position: position):
    slot_a, slot_b, (m_ref, l_ref, acc_ref) = scratch[0:3], scratch[3:6], scratch[6:]
    n_chunks = tk // KEY_CHUNK
    for i in range(n_streams):
        m_ref[i] = jnp.full((1, tq), NEG_INF, F32)
        l_ref[i] = jnp.zeros((1, tq), F32)
        acc_ref[i] = jnp.zeros(acc_ref.shape[1:], F32)
        slot_b[1][i] = jnp.zeros((tk, tq), BF16)
        slot_b[2][i] = jnp.ones((1, tq), F32)

    def column_max(s_ref, i):
        part = jnp.full((SUBLANES, tq), NEG_INF, F32)
        for c in range(n_chunks):
            part = jnp.maximum(part, jnp.max(_fold_rows(s_ref[i, _chunk_rows(c), :]), axis=0))
        return jnp.max(part, axis=0, keepdims=True)

    def stage1(j, slot, i):
        slot[0][i] = _dot(key_operand(j, i), query_operand(j, i))

    def stage2(j, slot, i, causal):
        s_ref, p_ref, alpha_ref = slot
        if causal is not None:
            s_ref[i] = jnp.where(causal, s_ref[i], NEG_INF)
        mx = column_max(s_ref, i)
        off = offset(j, i)
        m_prev = m_ref[i]
        if off is None:
            m_next = jnp.maximum(m_prev, mx)
            shift = m_next
        else:
            m_next = jnp.maximum(m_prev, mx + off)
            shift = m_next - off
        alpha = jnp.exp2(m_prev - m_next)
        part = jnp.zeros((SUBLANES, tq), F32)
        for c in range(n_chunks):
            p = jnp.exp2(s_ref[i, _chunk_rows(c), :] - shift)
            part = part + jnp.sum(_fold_rows(p), axis=0)
            p_ref[i, _chunk_rows(c), :] = p.astype(BF16)
        l_ref[i] = alpha * l_ref[i] + jnp.sum(part, axis=0, keepdims=True)
        m_ref[i] = m_next
        alpha_ref[i] = alpha

    def step(accumulate=None, produce=None, exponentiate=None, diagonal=False):
        accumulate, produce, exponentiate = [
            None if stage is None else (tile_id(stage[0]), stage[1]) for stage in (accumulate, produce, exponentiate)]
        causal = None
        if diagonal:
            key_minus_query = (lax.broadcasted_iota(jnp.int32, (tk, tq), 0)
                               - lax.broadcasted_iota(jnp.int32, (tk, tq), 1))
            causal = key_minus_query <= q_start - exponentiate[0] * tk
        products = []
        for i in range(n_streams + 1):
            if i < n_streams:
                if accumulate is not None:
                    products.append(_dot(values(accumulate[0], i), accumulate[1][1][i]))
                if produce is not None:
                    stage1(*produce, i)
                if exponentiate is not None:
                    stage2(*exponentiate, i, causal)
            if accumulate is not None and i > 0:
                acc_ref[i - 1] = accumulate[1][2][i - 1] * acc_ref[i - 1] + products[i - 1]

    step(produce=(0, slot_a))

    def pair(t, carry):
        j0 = 2 * t
        step((jnp.maximum(j0 - 1, 0), slot_b), (j0 + 1, slot_b), (j0, slot_a))
        step((j0, slot_a), (j0 + 2, slot_a), (j0 + 1, slot_b))
        return carry

    lax.fori_loop(0, n_full // 2, pair, 0)
    pending = jnp.maximum(n_full - 1 - n_full % 2, 0)

    @pl.when(n_full % 2 == 1)
    def _():
        step((pending, slot_b), (n_full, slot_b), (n_full - 1, slot_a))
        step(accumulate=(n_full - 1, slot_a), exponentiate=(n_full, slot_b), diagonal=True)
        step(accumulate=(n_full, slot_b))

    @pl.when(n_full % 2 == 0)
    def _():
        step(accumulate=(pending, slot_b), exponentiate=(n_full, slot_a), diagonal=True)
        step(accumulate=(n_full, slot_a))


def _flash_result(scratch, i):
    l_ref, acc_ref = scratch[-2], scratch[-1]
    return acc_ref[i] / l_ref[i]


def _kv_rows(j, tk):
    return pl.ds(pl.multiple_of(j * tk, tk), tk)


def _transposed_bf16(x):
    return x.astype(F32).T.astype(BF16)


def _alibi_rows(coef, tq):
    c = jnp.zeros((1, tq), F32) + coef
    hi = c.astype(BF16).astype(F32)
    rest = c - hi
    mid = rest.astype(BF16).astype(F32)
    lo = rest - mid
    zero = jnp.zeros((1, tq), F32)
    return jnp.concatenate([POS_SPLIT * hi, POS_SPLIT * mid, POS_SPLIT * lo, hi, mid, lo, zero, zero], axis=0)


def _augmented_query(q_t, feature_rows):
    tq = q_t.shape[1]
    pad = jnp.zeros((MXU_DEPTH - LANES - FEATURE_ROWS, tq), BF16)
    return jnp.concatenate([q_t, feature_rows.astype(BF16), pad], axis=0)


MLA_STEP_HEADS = 4


def _mla_attn_kernel(q_ref, k_ref, vt_ref, o_ref, *scratch, tq, tk):
    q_start = pl.program_id(2) * tq
    n = MLA_STEP_HEADS
    queries = [_transposed_bf16(q_ref[0, :, hh * LANES:(hh + 1) * LANES]) for hh in range(n)]
    _flash_transposed(
        q_start // tk, n, q_start, tq, tk,
        lambda j, i: k_ref[0, _kv_rows(j, tk), i * LANES:(i + 1) * LANES],
        lambda j, i: queries[i],
        lambda j, i: vt_ref[0, j, i * HEAD_DIM:(i + 1) * HEAD_DIM, :],
        lambda j, i: None, scratch)
    for pair in range(n // 2):
        o_t = jnp.concatenate([_flash_result(scratch, 2 * pair), _flash_result(scratch, 2 * pair + 1)], axis=0)
        o_ref[0, :, pair * LANES:(pair + 1) * LANES] = o_t.T.astype(o_ref.dtype)


def _mla_attn(q, k, vt):
    b, s, _ = q.shape
    tq, tk = min(FLASH_Q_TILE, s), min(KV_TILE, s)
    n = MLA_STEP_HEADS
    groups = MLA_HEADS // n
    return pl.pallas_call(
        functools.partial(_mla_attn_kernel, tq=tq, tk=tk),
        grid=(b, groups, s // tq),
        in_specs=[pl.BlockSpec((1, tq, n * LANES), lambda bi, p, i: (bi, i, p)),
                  pl.BlockSpec((1, s, n * LANES), lambda bi, p, i: (bi, 0, p)),
                  pl.BlockSpec((1, s // tk, n * HEAD_DIM, tk), lambda bi, p, i: (bi, 0, p, 0))],
        out_specs=pl.BlockSpec((1, tq, n * HEAD_DIM), lambda bi, p, i: (bi, i, p)),
        out_shape=jax.ShapeDtypeStruct((b, s, MLA_HEADS * HEAD_DIM), BF16),
        scratch_shapes=_flash_scratch(n, HEAD_DIM, tq, tk),
        compiler_params=_params("parallel", "parallel", "arbitrary"),
        name="mla_attn",
    )(q, k, vt)


def _gelu_tanh(x):
    return 0.5 * x * (1.0 + jnp.tanh(math.sqrt(2.0 / math.pi) * (x + 0.044715 * (x * x * x))))


def _compress_kernel(x_ref, pos_ref, w1a_ref, w1b_ref, w2_ref, k_ref, vt_ref, *, n_real):
    x = x_ref[0]
    n = x.shape[0]
    first = _dot(x, w1a_ref[...])
    second = _dot(x, w1b_ref[...])
    pos_hi, pos_lo = _split_bf16(pos_ref[...])
    bias = (_dot(pos_hi[:8], w1a_ref[...]) + _dot(pos_lo[:8], w1a_ref[...])
            + _dot(pos_hi[8:], w1b_ref[...]) + _dot(pos_lo[8:], w1b_ref[...]))[:1]
    pre = first + pltpu.roll(second, n - 1, 0) + bias
    out = _dot(_gelu_tanh(pre).astype(BF16), w2_ref[...])
    real = lax.broadcasted_iota(jnp.int32, out.shape, 0) < n_real
    out = jnp.where(real, out, 0.0)
    half = out.shape[1] // 2
    k_ref[0] = out[:, :half].astype(k_ref.dtype)
    vt_ref[0] = out[:, half:].T.astype(vt_ref.dtype)


def _compress(x_chunks, pos_exp, w1a, w1b, w2, n_real):
    b, n, width = x_chunks.shape
    half = w2.shape[1] // 2
    const = lambda bi: (0, 0)
    return pl.pallas_call(
        functools.partial(_compress_kernel, n_real=n_real),
        grid=(b,),
        in_specs=[pl.BlockSpec((1, n, width), lambda bi: (bi, 0, 0)),
                  pl.BlockSpec(pos_exp.shape, const), pl.BlockSpec(w1a.shape, const),
                  pl.BlockSpec(w1b.shape, const), pl.BlockSpec(w2.shape, const)],
        out_specs=[pl.BlockSpec((1, n, half), lambda bi: (bi, 0, 0)), pl.BlockSpec((1, half, n), lambda bi: (bi, 0, 0))],
        out_shape=[jax.ShapeDtypeStruct((b, n, half), BF16), jax.ShapeDtypeStruct((b, half, n), BF16)],
        compiler_params=_params("parallel"),
        name="nsa_compress",
    )(x_chunks, pos_exp, w1a, w1b, w2)


def _nsa_head_slope(h):
    return float(2.0 ** (-8.0 * (h + 1) / NSA_HEADS))


def _nsa_queries(q_ref, g, scale):
    out = []
    for hg in range(NSA_HG):
        h = g * NSA_HG + hg
        slab = q_ref[0, :, (h // 2) * LANES:(h // 2 + 1) * LANES].astype(F32) * scale
        out.append(_move_head(slab, h % 2, g))
    return out


def _select_blocks(imp_t, q_pos):
    n_blocks = imp_t.shape[0]
    blk = lax.broadcasted_iota(jnp.int32, imp_t.shape, 0)
    cur = q_pos // SEL_LEN
    forced = jnp.where(blk == 0, 1.0, 0.0) + jnp.where(blk == cur, 1.0, 0.0) + jnp.where(blk == cur - 1, 1.0, 0.0)
    forced = jnp.minimum(forced, 1.0)
    val = jnp.where(blk <= cur, imp_t + FORCE_BONUS * forced, NEG_INF)
    chosen = jnp.zeros(imp_t.shape, F32)
    for _ in range(SEL_TOPK):
        top = jnp.max(val, axis=0, keepdims=True)
        first = jnp.min(jnp.where(val == top, blk, n_blocks), axis=0, keepdims=True)
        hit = blk == first
        chosen = jnp.where(hit, 1.0, chosen)
        val = jnp.where(hit, -jnp.inf, val)
    return jnp.where(chosen > 0.5, 0.0, NEG_INF)


def _masked_softmax_pass(s_ref, p_ref, tiles, tq):
    part = jnp.full((SUBLANES, tq), NEG_INF, F32)
    for t, rows, keep, off in tiles:
        for c in range(rows // KEY_CHUNK):
            s = jnp.where(keep(_chunk_rows(c)), s_ref[t, _chunk_rows(c), :], NEG_INF)
            s_ref[t, _chunk_rows(c), :] = s
            part = jnp.maximum(part, jnp.max(_fold_rows(s), axis=0) + off)
    m = jnp.max(part, axis=0, keepdims=True)
    total = jnp.zeros((SUBLANES, tq), F32)
    for t, rows, keep, off in tiles:
        shift = m - off
        for c in range(rows // KEY_CHUNK):
            p = jnp.exp2(s_ref[t, _chunk_rows(c), :] - shift)
            total = total + jnp.sum(_fold_rows(p), axis=0)
            p_ref[t, _chunk_rows(c), :] = p.astype(BF16)
    return jnp.sum(total, axis=0, keepdims=True), m > 0.5 * NEG_INF


def _nsa_cmp_win_kernel(q_ref, kc_ref, vct_ref, kw_ref, vwt_ref, feat_ref, ovt_ref, oc_ref, ow_ref, sel_ref, any_ref,
                        end_ref, rel_ref, sc_ref, sw_ref, pc_ref, pw_ref, *, tq, tk):
    qi = pl.program_id(1)
    q_start = qi * tq
    n_cmp = kc_ref.shape[1]
    end_ref[...] = (lax.broadcasted_iota(jnp.int32, (n_cmp, tq), 0) * CMP_STRIDE + (CMP_LEN - 1)
                    - lax.broadcasted_iota(jnp.int32, (n_cmp, tq), 1))
    rel_ref[...] = (lax.broadcasted_iota(jnp.int32, (tk, tq), 0) - lax.broadcasted_iota(jnp.int32, (tk, tq), 1))
    prev_tile = jnp.maximum(qi - 1, 0)
    prev_bound = jnp.where(qi >= 1, 0, tk)
    zeros = jnp.zeros((FEATURE_ROWS - SUBLANES, tq), F32)
    k_cmp = jnp.concatenate([kc_ref[0], feat_ref[0:n_cmp, :]], axis=1)
    k_win = [jnp.concatenate([kw_ref[0, _kv_rows(j, tk), :], feat_ref[...]], axis=1) for j in (prev_tile, qi)]
    q_pos = q_start + lax.broadcasted_iota(jnp.int32, (1, tq), 1)
    for g in range(NSA_GROUPS):
        queries = [_transposed_bf16(q) for q in _nsa_queries(q_ref, g, HEAD_DIM ** -0.5 * LOG2E)]
        group_rows = slice(g * HEAD_DIM, (g + 1) * HEAD_DIM)
        imp_t = jnp.zeros((LANES, tq), F32)
        out_c, out_w = [], []
        for hg in range(NSA_HG):
            coef = _nsa_head_slope(g * NSA_HG + hg) * LOG2E
            buf = hg % 2
            sc, sw, pc, pw = sc_ref.at[buf], sw_ref.at[buf], pc_ref.at[buf], pw_ref.at[buf]
            cmp_query = _augmented_query(queries[hg], jnp.concatenate([_alibi_rows(CMP_STRIDE * coef, tq), zeros], 0))
            win_query = _augmented_query(queries[hg], jnp.concatenate([_alibi_rows(coef, tq), zeros], 0))
            sc[0] = _dot(k_cmp, cmp_query)
            sw[0] = _dot(k_win[0], win_query)
            sw[1] = _dot(k_win[1], win_query)
            norm, has_any = _masked_softmax_pass(
                sc, pc, [(0, n_cmp, lambda r: end_ref[r, :] <= q_start, 0.0)], tq)
            inv = jnp.where(has_any, 1.0 / norm, 0.0)
            out_c.append(_dot(vct_ref[0, group_rows, :], pc[0]) * inv)
            imp_t = imp_t + _dot(ovt_ref[...], pc[0]) * inv
            norm, _ = _masked_softmax_pass(
                sw, pw, [(0, tk, lambda r: rel_ref[r, :] > prev_bound, -coef * tk),
                         (1, tk, lambda r: rel_ref[r, :] <= 0, 0.0)], tq)
            acc = (_dot(vwt_ref[0, prev_tile, group_rows, :], pw[0]) + _dot(vwt_ref[0, qi, group_rows, :], pw[1]))
            out_w.append(acc / norm)
        for pair in range(NSA_HG // 2):
            cols = slice((g * 2 + pair) * LANES, (g * 2 + pair + 1) * LANES)
            oc_ref[0, :, cols] = jnp.concatenate(out_c[2 * pair:2 * pair + 2], axis=0).T
            ow_ref[0, :, cols] = jnp.concatenate(out_w[2 * pair:2 * pair + 2], axis=0).T
        bias_t = _select_blocks(imp_t, q_pos)
        sel_ref[0, g * LANES:(g + 1) * LANES, :] = bias_t
        any_ref[0, 0, g * LANES:(g + 1) * LANES, :] = jnp.broadcast_to(
            jnp.max(bias_t, axis=1, keepdims=True), (LANES, LANES))


def _nsa_cmp_win(slab_b, cmp_k, cmp_vt, vt_nsa, key_features, overlap_t, seq):
    b = slab_b.shape[0]
    tq, tk = min(FLASH_Q_TILE, seq), min(KV_TILE, seq)
    assert tq == tk == WINDOW, "the window branch is written for one previous and one diagonal key tile"
    n_cmp = cmp_k.shape[1]
    qw = NSA_HEADS * HEAD_DIM
    base = qw // LANES
    tile = lambda bi, i: (bi, i, 0)
    return pl.pallas_call(
        functools.partial(_nsa_cmp_win_kernel, tq=tq, tk=tk),
        grid=(b, seq // tq),
        in_specs=[pl.BlockSpec((1, tq, qw), tile),
                  pl.BlockSpec((1, n_cmp, LANES), lambda bi, i: (bi, 0, 0)),
                  pl.BlockSpec((1, LANES, n_cmp), lambda bi, i: (bi, 0, 0)),
                  pl.BlockSpec((1, seq, LANES), lambda bi, i: (bi, 0, base + 1)),
                  pl.BlockSpec((1, seq // tk, LANES, tk), lambda bi, i: (bi, 0, 1, 0)),
                  pl.BlockSpec(key_features.shape, lambda bi, i: (0, 0)),
                  pl.BlockSpec(overlap_t.shape, lambda bi, i: (0, 0))],
        out_specs=[pl.BlockSpec((1, tq, qw), tile), pl.BlockSpec((1, tq, qw), tile),
                   pl.BlockSpec((1, NSA_GROUPS * LANES, tq), lambda bi, i: (bi, 0, i)),
                   pl.BlockSpec((1, 1, NSA_GROUPS * LANES, LANES), lambda bi, i: (bi, i, 0, 0))],
        out_shape=[jax.ShapeDtypeStruct((b, seq, qw), F32), jax.ShapeDtypeStruct((b, seq, qw), F32),
                   jax.ShapeDtypeStruct((b, NSA_GROUPS * LANES, seq), F32),
                   jax.ShapeDtypeStruct((b, seq // tq, NSA_GROUPS * LANES, LANES), F32)],
        scratch_shapes=[pltpu.VMEM((n_cmp, tq), jnp.int32), pltpu.VMEM((tk, tq), jnp.int32),
                        pltpu.VMEM((2, 1, n_cmp, tq), F32), pltpu.VMEM((2, 2, tk, tq), F32),
                        pltpu.VMEM((2, 1, n_cmp, tq), BF16), pltpu.VMEM((2, 2, tk, tq), BF16)],
        compiler_params=_params("parallel", "arbitrary"),
        name="nsa_cmp_win",
    )(slab_b, cmp_k, cmp_vt, slab_b, vt_nsa, key_features, overlap_t)


def _nsa_sel_kernel(tiles_ref, counts_ref, q_ref, k_ref, vt_ref, feat_ref, sel_ref, oc_ref, ow_ref, gate_ref, gx_ref,
                    o_ref, *scratch, tq, tk, max_tiles):
    qi = pl.program_id(1)
    q_start = qi * tq
    diagonal_tile = q_start // tk
    out_slabs = []
    for g in range(NSA_GROUPS):
        queries = [_transposed_bf16(q) for q in _nsa_queries(q_ref, g, HEAD_DIM ** -0.5 * LOG2E)]
        coefs = [_nsa_head_slope(g * NSA_HG + hg) * LOG2E for hg in range(NSA_HG)]
        alibi = [_alibi_rows(c, tq) for c in coefs]
        entry = (pl.program_id(0) * pl.num_programs(1) + qi) * NSA_GROUPS + g
        n_active = counts_ref[entry]

        def tile_id(position, entry=entry, n_active=n_active):
            listed = tiles_ref[entry * max_tiles + jnp.minimum(position, max_tiles - 1)]
            return jnp.where(position < n_active, listed, diagonal_tile)

        def key_operand(j, i):
            return jnp.concatenate([k_ref[0, _kv_rows(j, tk), :], feat_ref[...]], axis=1)

        def query_operand(j, i, g=g, queries=queries, alibi=alibi):
            first_block = pl.multiple_of(g * LANES + j * BLOCKS_PER_TILE, BLOCKS_PER_TILE)
            blocks = sel_ref[0, pl.ds(first_block, BLOCKS_PER_TILE), :]
            return _augmented_query(queries[i], jnp.concatenate([alibi[i], blocks], axis=0))

        _flash_transposed(
            n_active, NSA_HG, q_start, tq, tk, key_operand, query_operand,
            lambda j, i, g=g: vt_ref[0, j, g * HEAD_DIM:(g + 1) * HEAD_DIM, :],
            lambda j, i, coefs=coefs: coefs[i] * (j * tk - q_start).astype(F32), scratch, tile_id)
        heads = [_flash_result(scratch, hg) for hg in range(NSA_HG)]
        for pair in range(NSA_HG // 2):
            out_slabs.append(jnp.concatenate(heads[2 * pair:2 * pair + 2], axis=0).T)
    gates = jax.nn.sigmoid(gate_ref[...])
    g_hi, g_lo = _split_bf16(gates)
    width = NSA_HEADS * HEAD_DIM
    for i, o_sel in enumerate(out_slabs):
        mixed = None
        for branch, o_branch in enumerate((oc_ref[0, :, i * LANES:(i + 1) * LANES], o_sel,
                                           ow_ref[0, :, i * LANES:(i + 1) * LANES])):
            gx = gx_ref[:, branch * width + i * LANES:branch * width + (i + 1) * LANES]
            term = (_dot(g_hi, gx) + _dot(g_lo, gx)) * o_branch
            mixed = term if mixed is None else mixed + term
        o_ref[0, :, i * LANES:(i + 1) * LANES] = mixed.astype(o_ref.dtype)


def _active_key_tiles(block_any, tq, tk):
    b, n_q = block_any.shape[:2]
    max_tiles = LANES // BLOCKS_PER_TILE
    hit = block_any[..., 0].reshape(b, n_q, NSA_GROUPS, max_tiles, BLOCKS_PER_TILE).max(axis=-1) > 0.5 * NEG_INF
    before_diagonal = jnp.arange(max_tiles)[None, :] < (jnp.arange(n_q) * tq // tk)[:, None]
    hit = hit & before_diagonal[None, :, None, :]
    order = jnp.argsort(jnp.logical_not(hit), axis=-1, stable=True)
    return order.astype(jnp.int32).reshape(-1), hit.sum(axis=-1).astype(jnp.int32).reshape(-1), max_tiles


def _nsa_sel(slab_b, vt, key_features, sel_bias_t, block_any, o_cmp, o_win, slab_a, gate_expand, seq, gate_col_block):
    b = slab_b.shape[0]
    tq, tk = min(FLASH_Q_TILE, seq), min(KV_TILE, seq)
    qw = NSA_HEADS * HEAD_DIM
    base = qw // LANES
    per_seq = seq // tq
    tiles, counts, max_tiles = _active_key_tiles(block_any, tq, tk)
    tile = lambda bi, i, *_: (bi, i, 0)
    grid_spec = pltpu.PrefetchScalarGridSpec(
        num_scalar_prefetch=2,
        grid=(b, seq // tq),
        in_specs=[pl.BlockSpec((1, tq, qw), tile),
                  pl.BlockSpec((1, seq, LANES), lambda bi, i, *_: (bi, 0, base)),
                  pl.BlockSpec((1, seq // tk, LANES, tk), lambda bi, i, *_: (bi, 0, 0, 0)),
                  pl.BlockSpec(key_features.shape, lambda bi, i, *_: (0, 0)),
                  pl.BlockSpec((1, NSA_GROUPS * LANES, tq), lambda bi, i, *_: (bi, 0, i)),
                  pl.BlockSpec((1, tq, qw), tile), pl.BlockSpec((1, tq, qw), tile),
                  pl.BlockSpec((tq, LANES), lambda bi, i, *_: (bi * per_seq + i, gate_col_block)),
                  pl.BlockSpec(gate_expand.shape, lambda bi, i, *_: (0, 0))],
        out_specs=pl.BlockSpec((1, tq, qw), tile),
        scratch_shapes=_flash_scratch(NSA_HG, HEAD_DIM, tq, tk))
    return pl.pallas_call(
        functools.partial(_nsa_sel_kernel, tq=tq, tk=tk, max_tiles=max_tiles),
        grid_spec=grid_spec,
        out_shape=jax.ShapeDtypeStruct((b, seq, qw), BF16),
        compiler_params=_params("parallel", "arbitrary"),
        name="nsa_sel",
    )(tiles, counts, slab_b, slab_b, vt, key_features, sel_bias_t, o_cmp, o_win, slab_a, gate_expand)


DIFF_STEP_HEADS = 2


def _diff_attn_kernel(slope_ref, lam_ref, q_ref, k_ref, vt_ref, feat_ref, g_ref, o_ref, *scratch,
                      tq, tk, lam_init):
    first_head = pl.program_id(1) * DIFF_STEP_HEADS
    q_start = pl.program_id(2) * tq
    zeros = jnp.zeros((FEATURE_ROWS - SUBLANES, tq), F32)
    coefs, queries = [], []
    for hh in range(DIFF_STEP_HEADS):
        coef = slope_ref[first_head + hh] * LOG2E
        q = q_ref[0, :, hh * LANES:(hh + 1) * LANES].astype(F32) * (HEAD_DIM ** -0.5 * LOG2E)
        features = jnp.concatenate([_alibi_rows(coef, tq), zeros], axis=0)
        coefs.append(coef)
        queries += [_augmented_query(_transposed_bf16(_keep_half(q, half)), features) for half in range(2)]

    def head_lanes(i):
        return slice((i // 2) * LANES, (i // 2 + 1) * LANES)

    _flash_transposed(
        q_start // tk, 2 * DIFF_STEP_HEADS, q_start, tq, tk,
        lambda j, i: jnp.concatenate([k_ref[0, _kv_rows(j, tk), head_lanes(i)], feat_ref[...]], axis=1),
        lambda j, i: queries[i],
        lambda j, i: vt_ref[0, j, head_lanes(i), :],
        lambda j, i: coefs[i // 2] * (j * tk - q_start).astype(F32), scratch)
    lam_vec = lam_ref[...]
    lam = (jnp.exp(jnp.sum(lam_vec[0:1] * lam_vec[1:2], axis=1, keepdims=True))
           - jnp.exp(jnp.sum(lam_vec[2:3] * lam_vec[3:4], axis=1, keepdims=True)) + lam_init)
    for hh in range(DIFF_STEP_HEADS):
        o = (_flash_result(scratch, 2 * hh) - lam * _flash_result(scratch, 2 * hh + 1)).T
        o_ref[0, :, hh * LANES:(hh + 1) * LANES] = (
            _rms_norm(o, g_ref[...], RMS_EPS) * (1.0 - lam_init)).astype(o_ref.dtype)


def _diff_attn(qk, vt, key_features, slopes, lam_vecs, subln_g, lam_init):
    b, s, _ = qk.shape
    tq, tk = min(FLASH_Q_TILE, s), min(KV_TILE, s)
    n = DIFF_STEP_HEADS
    groups = DIFF_HEADS // n
    smem = pl.BlockSpec(memory_space=pltpu.SMEM)
    return pl.pallas_call(
        functools.partial(_diff_attn_kernel, tq=tq, tk=tk, lam_init=lam_init),
        grid=(b, groups, s // tq),
        in_specs=[smem, pl.BlockSpec(lam_vecs.shape, lambda bi, h, i: (0, 0)),
                  pl.BlockSpec((1, tq, n * LANES), lambda bi, h, i: (bi, i, h)),
                  pl.BlockSpec((1, s, n * LANES), lambda bi, h, i: (bi, 0, groups + h)),
                  pl.BlockSpec((1, s // tk, n * LANES, tk), lambda bi, h, i: (bi, 0, h, 0)),
                  pl.BlockSpec(key_features.shape, lambda bi, h, i: (0, 0)),
                  pl.BlockSpec((1, LANES), lambda bi, h, i: (0, 0))],
        out_specs=pl.BlockSpec((1, tq, n * LANES), lambda bi, h, i: (bi, i, h)),
        out_shape=jax.ShapeDtypeStruct((b, s, DIFF_HEADS * LANES), BF16),
        scratch_shapes=_flash_scratch(2 * n, LANES, tq, tk),
        compiler_params=_params("parallel", "parallel", "arbitrary"),
        name="diff_attn",
    )(slopes, lam_vecs, qk, qk, vt, key_features, subln_g)


def _out_ln_kernel(*refs, n_in):
    a_refs, w_refs = refs[:n_in], refs[n_in:2 * n_in]
    x_ref, g_ref, b_ref, o_ref = refs[2 * n_in:]
    y = None
    for a_ref, w_ref in zip(a_refs, w_refs):
        t = _dot(a_ref[...], w_ref[...])
        y = t if y is None else y + t
    o_ref[...] = _layer_norm(DN_ALPHA * x_ref[...] + y, g_ref[...], b_ref[...])


def _out_ln(acts, weights, x, g, b):
    m, d = x.shape
    tm = min(ROW_TILE, m)
    row = lambda i: (i, 0)
    const = lambda i: (0, 0)
    return pl.pallas_call(
        functools.partial(_out_ln_kernel, n_in=len(acts)),
        grid=(m // tm,),
        in_specs=[pl.BlockSpec((tm, a.shape[1]), row) for a in acts]
        + [pl.BlockSpec(w.shape, const) for w in weights]
        + [pl.BlockSpec((tm, d), row), pl.BlockSpec((1, d), const), pl.BlockSpec((1, d), const)],
        out_specs=pl.BlockSpec((tm, d), row),
        out_shape=jax.ShapeDtypeStruct((m, d), F32),
        compiler_params=_params("parallel"),
        name="out_proj_ln",
    )(*acts, *weights, x, g, b)


def _mlp_kernel(x_ref, wu_ref, wd_ref, g_ref, b_ref, o_ref, xb_ref, acc_ref):
    f = pl.program_id(1)

    @pl.when(f == 0)
    def _():
        xb_ref[...] = x_ref[...].astype(BF16)
        acc_ref[...] = jnp.zeros(acc_ref.shape, F32)

    hidden = jnp.maximum(_dot(xb_ref[...], wu_ref[...]), 0.0)
    acc_ref[...] += _dot((hidden * hidden).astype(BF16), wd_ref[...])

    @pl.when(f == pl.num_programs(1) - 1)
    def _():
        o_ref[...] = _layer_norm(DN_ALPHA * x_ref[...] + acc_ref[...], g_ref[...], b_ref[...])


def _mlp(x, w_up, w_down, g, b):
    m, d = x.shape
    ff = w_up.shape[1]
    tm, tf = min(ROW_TILE, m), min(FF_TILE, ff)
    return pl.pallas_call(
        _mlp_kernel,
        grid=(m // tm, ff // tf),
        in_specs=[pl.BlockSpec((tm, d), lambda i, f: (i, 0)),
                  pl.BlockSpec((d, tf), lambda i, f: (0, f)),
                  pl.BlockSpec((tf, d), lambda i, f: (f, 0)),
                  pl.BlockSpec((1, d), lambda i, f: (0, 0)), pl.BlockSpec((1, d), lambda i, f: (0, 0))],
        out_specs=pl.BlockSpec((tm, d), lambda i, f: (i, 0)),
        out_shape=jax.ShapeDtypeStruct((m, d), F32),
        scratch_shapes=[pltpu.VMEM((tm, d), BF16), pltpu.VMEM((tm, d), F32)],
        compiler_params=_params("parallel", "arbitrary"),
        name="mlp_ln",
    )(x, w_up, w_down, g, b)


def _pad_cols(w, width):
    return jnp.pad(w, ((0, 0), (0, width - w.shape[1])))


def _layer0_weights(w_in, w_uq, w_ukv, d_model):
    rank = d_model // 4
    kvw = NSA_GROUPS * HEAD_DIM
    o = np.cumsum([0, rank, rank, MLA_ROPE, NSA_HEADS * HEAD_DIM] + [kvw] * 6 + [3 * NSA_HEADS])
    seg = lambda i: w_in[:, o[i]:o[i + 1]]
    zeros = lambda n: jnp.zeros((w_in.shape[0], n), w_in.dtype)
    rope_slab = jnp.concatenate([zeros(MLA_NOPE), seg(2), zeros(LANES - MLA_NOPE - MLA_ROPE)], axis=1)
    w_a = jnp.concatenate([seg(0), seg(1), rope_slab, _pad_cols(seg(10), LANES)], axis=1)
    w_b = jnp.concatenate([seg(3), seg(6), seg(8)], axis=1)
    w_c = jnp.concatenate([seg(4), seg(5)], axis=1)
    w_vs = jnp.concatenate([seg(7), seg(9)], axis=1)
    wq = jnp.pad(w_uq.reshape(rank, MLA_HEADS, MLA_NOPE + MLA_ROPE),
                 ((0, 0), (0, 0), (0, LANES - MLA_NOPE - MLA_ROPE))).reshape(rank, MLA_HEADS * LANES)
    ukv = w_ukv.reshape(rank, MLA_HEADS, MLA_NOPE + HEAD_DIM)
    wk = jnp.pad(ukv[:, :, :MLA_NOPE], ((0, 0), (0, 0), (0, LANES - MLA_NOPE))).reshape(rank, MLA_HEADS * LANES)
    wv = ukv[:, :, MLA_NOPE:].reshape(rank, MLA_HEADS * HEAD_DIM)
    return [w.astype(BF16) for w in (w_a, w_b, w_c, w_vs, wq, wk, wv)]


def _rope_tables(seq):
    inv = 1.0 / (ROPE_THETA ** (jnp.arange(0, MLA_ROPE, 2, dtype=F32) / MLA_ROPE))
    ang = jnp.arange(seq, dtype=F32)[:, None] * inv[None, :]
    cos, sin = jnp.cos(ang), jnp.sin(ang)
    half = MLA_ROPE // 2
    z = lambda n: jnp.zeros((seq, n), F32)
    tail = LANES - MLA_NOPE - MLA_ROPE
    c = jnp.concatenate([jnp.ones((seq, MLA_NOPE), F32), cos, cos, z(tail)], axis=1)
    s1 = jnp.concatenate([z(MLA_NOPE), -sin, z(half), z(tail)], axis=1)
    s2 = jnp.concatenate([z(MLA_NOPE), z(half), sin, z(tail)], axis=1)
    return c, s1, s2


def _compress_weights(pos_k, w1_k, w2_k, pos_v, w1_v, w2_v):
    eye = jnp.eye(2 * NSA_GROUPS, dtype=F32)
    halves = []
    for a in range(CMP_LEN // CMP_STRIDE):
        rows = slice(a * CMP_STRIDE * HEAD_DIM, (a + 1) * CMP_STRIDE * HEAD_DIM)
        wk = w1_k[rows].reshape(CMP_STRIDE, HEAD_DIM, HEAD_DIM)
        wv = w1_v[rows].reshape(CMP_STRIDE, HEAD_DIM, HEAD_DIM)
        per_slot = jnp.stack([wk, wk, wv, wv], axis=0)
        full = jnp.einsum('st,srdj->rsdtj', eye, per_slot)
        halves.append(full.reshape(CMP_STRIDE * 4 * HEAD_DIM, 4 * HEAD_DIM).astype(BF16))
    w2 = jnp.einsum('st,sdj->sdtj', eye, jnp.stack([w2_k, w2_k, w2_v, w2_v])).reshape(4 * HEAD_DIM, 4 * HEAD_DIM)
    pos = jnp.concatenate([pos_k, pos_k, pos_v, pos_v], axis=1)
    pos = pos.reshape(CMP_LEN // CMP_STRIDE, 1, CMP_STRIDE * 4 * HEAD_DIM)
    pos = jnp.broadcast_to(pos, (pos.shape[0], 8, pos.shape[2])).reshape(-1, pos.shape[2])
    return pos, halves[0], halves[1], w2.astype(BF16)


def _overlap_table(n_cmp_pad, n_cmp):
    c0 = np.arange(n_cmp_pad)[None, :] * CMP_STRIDE
    s0 = np.arange(LANES)[:, None] * SEL_LEN
    ov = np.maximum(np.minimum(c0 + CMP_LEN, s0 + SEL_LEN) - np.maximum(c0, s0), 0) / CMP_LEN
    ov = ov * (np.arange(n_cmp_pad)[None, :] < n_cmp)
    return jnp.asarray(ov, BF16)


def _key_feature_table(tk):
    c = np.arange(tk)
    table = np.zeros((tk, LANES), np.float32)
    table[:, 0:3] = (c // POS_SPLIT)[:, None]
    table[:, 3:6] = (c % POS_SPLIT)[:, None]
    table[c, BLOCK_LANE0 + c // SEL_LEN] = 1.0
    return jnp.asarray(table, BF16)


def _gate_expand_table():
    width = NSA_HEADS * HEAD_DIM
    table = np.zeros((LANES, 3 * width), np.float32)
    for h in range(NSA_HEADS):
        for branch in range(3):
            table[h * 3 + branch, branch * width + h * HEAD_DIM:branch * width + (h + 1) * HEAD_DIM] = 1.0
    return jnp.asarray(table, BF16)


def _alibi_slopes(n):
    return jnp.asarray(2.0 ** (-8.0 * np.arange(1, n + 1) / n), dtype=F32)


def _layer0_mixer(x2, b, s, w_in, q_norm, w_uq, kv_norm, w_ukv, pos_k, w1_k, w2_k, pos_v, w1_v, w2_v, w_out):
    d = x2.shape[1]
    rank = d // 4
    w_a, w_b, w_c, w_vs, wq, wk, wv = _layer0_weights(w_in, w_uq, w_ukv, d)
    slab_a, slab_b, slab_c, vt_nsa = _project(x2, [w_a, w_b, w_c, w_vs], [F32, BF16, BF16, BF16],
                                              [False, False, False, True], b, s)
    rope_c, rope_s1, rope_s2 = _rope_tables(s)
    q, k, vt = _mla_prep(slab_a, q_norm.reshape(1, rank), kv_norm.reshape(1, rank), wq, wk, wv,
                         rope_c, rope_s1, rope_s2, b, s)
    o_mla = _mla_attn(q.reshape(b, s, -1), k.reshape(b, s, -1), vt)
    n_chunks = s // CMP_STRIDE
    n_cmp = (s - CMP_LEN) // CMP_STRIDE + 1
    pos, w1a, w1b, w2 = _compress_weights(pos_k, w1_k, w2_k, pos_v, w1_v, w2_v)
    cmp_k, cmp_vt = _compress(slab_c.reshape(b, n_chunks, CMP_STRIDE * slab_c.shape[1]), pos, w1a, w1b, w2, n_cmp)
    slab_b3 = slab_b.reshape(b, s, -1)
    key_features = _key_feature_table(min(KV_TILE, s))
    o_cmp, o_win, sel_bias_t, block_any = _nsa_cmp_win(slab_b3, cmp_k, cmp_vt, vt_nsa, key_features,
                                                       _overlap_table(n_chunks, n_cmp), s)
    o_nsa = _nsa_sel(slab_b3, vt_nsa, key_features, sel_bias_t, block_any, o_cmp, o_win, slab_a,
                     _gate_expand_table(), s, (2 * rank + LANES) // LANES)
    half = o_mla.shape[-1]
    w_out_b = w_out.astype(BF16)
    return [o_mla.reshape(b * s, half), o_nsa.reshape(b * s, -1)], [w_out_b[:half], w_out_b[half:]]


def _layer1_mixer(x2, b, s, w_qkv, lam_q1, lam_k1, lam_q2, lam_k2, subln_g, w_o, layer_idx):
    d = x2.shape[1]
    w = w_qkv.astype(BF16)
    qk, vt = _project(x2, [w[:, :2 * d], w[:, 2 * d:]], [BF16, BF16], [False, True], b, s)
    lam_init = 0.8 - 0.6 * math.exp(-0.3 * layer_idx)
    lam_vecs = jnp.stack([lam_q1, lam_k1, lam_q2, lam_k2]).astype(F32)
    o = _diff_attn(qk.reshape(b, s, -1), vt, _key_feature_table(min(KV_TILE, s)), _alibi_slopes(DIFF_HEADS),
                   lam_vecs, subln_g.reshape(1, -1), lam_init)
    return [o.reshape(b * s, -1)], [w_o.astype(BF16)]


def kernel(x, l0_w_in, l0_mla_q_norm, l0_mla_w_uq, l0_mla_kv_norm, l0_mla_w_ukv, l0_nsa_cmp_pos_k, l0_nsa_cmp_w1_k, l0_nsa_cmp_w2_k, l0_nsa_cmp_pos_v, l0_nsa_cmp_w1_v, l0_nsa_cmp_w2_v, l0_w_out, l0_ln_mix_g, l0_ln_mix_b, l0_w_up, l0_w_down, l0_ln_ffn_g, l0_ln_ffn_b, l1_w_qkv, l1_lam_q1, l1_lam_k1, l1_lam_q2, l1_lam_k2, l1_subln_g, l1_w_o, l1_ln_mix_g, l1_ln_mix_b, l1_w_up, l1_w_down, l1_ln_ffn_g, l1_ln_ffn_b):
    b, s, d = x.shape
    x2 = x.reshape(b * s, d)
    vec = lambda p: p.reshape(1, d)
    acts, weights = _layer0_mixer(x2, b, s, l0_w_in, l0_mla_q_norm, l0_mla_w_uq, l0_mla_kv_norm, l0_mla_w_ukv,
                                  l0_nsa_cmp_pos_k, l0_nsa_cmp_w1_k, l0_nsa_cmp_w2_k,
                                  l0_nsa_cmp_pos_v, l0_nsa_cmp_w1_v, l0_nsa_cmp_w2_v, l0_w_out)
    x2 = _out_ln(acts, weights, x2, vec(l0_ln_mix_g), vec(l0_ln_mix_b))
    x2 = _mlp(x2, l0_w_up.astype(BF16), l0_w_down.astype(BF16), vec(l0_ln_ffn_g), vec(l0_ln_ffn_b))
    acts, weights = _layer1_mixer(x2, b, s, l1_w_qkv, l1_lam_q1, l1_lam_k1, l1_lam_q2, l1_lam_k2,
                                  l1_subln_g, l1_w_o, 1)
    x2 = _out_ln(acts, weights, x2, vec(l1_ln_mix_g), vec(l1_ln_mix_b))
    x2 = _mlp(x2, l1_w_up.astype(BF16), l1_w_down.astype(BF16), vec(l1_ln_ffn_g), vec(l1_ln_ffn_b))
    return x2.reshape(b, s, d)
```

```python
import functools
import math

import jax
import jax.numpy as jnp
import numpy as np
from jax import lax
from jax.experimental import pallas as pl
from jax.experimental.pallas import tpu as pltpu

F32 = jnp.float32
BF16 = jnp.bfloat16

LANES = 128
SUBLANES = 8
BF16_ROWS = 16
MXU_DEPTH = 256
HEAD_DIM = 64
FLASH_Q_TILE = 512
KV_TILE = 512
KEY_CHUNK = 32
ROW_TILE = 512
FF_TILE = 1024
VMEM_LIMIT = 56 * 1024 * 1024

NEG_INF = -1e30
LOG2E = math.log2(math.e)
LN_EPS = 1e-5
RMS_EPS = 1e-6
DEPTH = 2
DN_ALPHA = (2.0 * DEPTH) ** 0.25

MLA_HEADS = 8
MLA_NOPE = 64
MLA_ROPE = 32
ROPE_THETA = 10000.0
NSA_HEADS = 8
NSA_GROUPS = 2
NSA_HG = NSA_HEADS // NSA_GROUPS
CMP_LEN = 32
CMP_STRIDE = 16
SEL_LEN = 64
SEL_TOPK = 16
WINDOW = 512
FORCE_BONUS = 1e3
DIFF_HEADS = 8

POS_SPLIT = 16
FEATURE_ROWS = 16
BLOCK_LANE0 = 8
BLOCKS_PER_TILE = KV_TILE // SEL_LEN


def _params(*sem):
    return pltpu.CompilerParams(dimension_semantics=sem, vmem_limit_bytes=VMEM_LIMIT)


def _dot(a, b):
    return jnp.dot(a, b, preferred_element_type=F32)


def _split_bf16(x):
    hi = x.astype(BF16)
    lo = (x - hi.astype(F32)).astype(BF16)
    return hi, lo


def _layer_norm(z, g, b):
    mu = jnp.mean(z, axis=-1, keepdims=True)
    zc = z - mu
    var = jnp.mean(zc * zc, axis=-1, keepdims=True)
    return zc * lax.rsqrt(var + LN_EPS) * g + b


def _rms_norm(z, g, eps):
    return z * lax.rsqrt(jnp.mean(z * z, axis=-1, keepdims=True) + eps) * g


def _lane_iota(shape):
    return lax.broadcasted_iota(jnp.int32, shape, 1)


def _keep_half(x, half):
    lane = _lane_iota(x.shape)
    keep = (lane < HEAD_DIM) if half == 0 else (lane >= HEAD_DIM)
    return jnp.where(keep, x, jnp.zeros_like(x))


def _move_head(slab, src_half, dst_half):
    if src_half != dst_half:
        slab = pltpu.roll(slab, HEAD_DIM, 1)
    return _keep_half(slab, dst_half)


def _store_transposed(o_ref, res):
    for c in range(res.shape[1] // LANES):
        cols = slice(c * LANES, (c + 1) * LANES)
        o_ref[0, 0, cols, :] = res[:, cols].T.astype(o_ref.dtype)


def _proj_kernel(x_ref, *refs, transposed):
    n_out = len(transposed)
    w_refs, o_refs = refs[:n_out], refs[n_out:]
    xb = x_ref[...].astype(BF16)
    for w_ref, o_ref, t in zip(w_refs, o_refs, transposed):
        res = _dot(xb, w_ref[...])
        if t:
            _store_transposed(o_ref, res)
        else:
            o_ref[...] = res.astype(o_ref.dtype)


def _transposed_out(b, seq, width, tm):
    per_seq = seq // tm
    spec = pl.BlockSpec((1, 1, width, tm), lambda i: (i // per_seq, i % per_seq, 0, 0))
    return spec, jax.ShapeDtypeStruct((b, per_seq, width, tm), BF16)


def _project(x, weights, out_dtypes, transposed, b, seq):
    m, k = x.shape
    tm = min(KV_TILE, seq)
    specs, shapes = [], []
    for w, dt, t in zip(weights, out_dtypes, transposed):
        if t:
            spec, shape = _transposed_out(b, seq, w.shape[1], tm)
        else:
            spec, shape = pl.BlockSpec((tm, w.shape[1]), lambda i: (i, 0)), jax.ShapeDtypeStruct((m, w.shape[1]), dt)
        specs.append(spec)
        shapes.append(shape)
    return pl.pallas_call(
        functools.partial(_proj_kernel, transposed=tuple(transposed)),
        grid=(m // tm,),
        in_specs=[pl.BlockSpec((tm, k), lambda i: (i, 0))]
        + [pl.BlockSpec(w.shape, lambda i: (0, 0)) for w in weights],
        out_specs=specs,
        out_shape=shapes,
        compiler_params=_params("parallel"),
        name="project",
    )(x, *weights)


def _rope_slab(slab, c, s1, s2):
    half = MLA_ROPE // 2
    up = pltpu.roll(slab, half, 1)
    down = pltpu.roll(slab, LANES - half, 1)
    return slab * c + down * s1 + up * s2


def _mla_prep_kernel(ql_ref, kvl_ref, kpe_ref, qg_ref, kvg_ref, wq_ref, wk_ref, wv_ref,
                     c_ref, s1_ref, s2_ref, q_ref, k_ref, vt_ref, *, q_scale):
    c, s1, s2 = c_ref[...], s1_ref[...], s2_ref[...]
    qn = _rms_norm(ql_ref[...], qg_ref[...], RMS_EPS).astype(BF16)
    kvn = _rms_norm(kvl_ref[...], kvg_ref[...], RMS_EPS).astype(BF16)
    q = _dot(qn, wq_ref[...])
    k = _dot(kvn, wk_ref[...])
    _store_transposed(vt_ref, _dot(kvn, wv_ref[...]))
    kpe = _rope_slab(kpe_ref[...], c, s1, s2)
    for h in range(MLA_HEADS):
        sl = slice(h * LANES, (h + 1) * LANES)
        q_ref[:, sl] = (_rope_slab(q[:, sl], c, s1, s2) * q_scale).astype(q_ref.dtype)
        k_ref[:, sl] = (k[:, sl] + kpe).astype(k_ref.dtype)


def _mla_prep(slab_a, q_gain, kv_gain, wq, wk, wv, rope_c, rope_s1, rope_s2, b, seq):
    m = slab_a.shape[0]
    tm = min(KV_TILE, seq)
    per_seq = seq // tm
    rank = q_gain.shape[1]
    row = lambda j: (lambda i: (i, j))
    tab = lambda i: (i % per_seq, 0)
    const = lambda i: (0, 0)
    hw = MLA_HEADS * LANES
    vt_spec, vt_shape = _transposed_out(b, seq, wv.shape[1], tm)
    return pl.pallas_call(
        functools.partial(_mla_prep_kernel, q_scale=float((MLA_NOPE + MLA_ROPE) ** -0.5 * LOG2E)),
        grid=(m // tm,),
        in_specs=[pl.BlockSpec((tm, rank), row(0)), pl.BlockSpec((tm, rank), row(1)),
                  pl.BlockSpec((tm, LANES), row(2 * rank // LANES)),
                  pl.BlockSpec((1, rank), const), pl.BlockSpec((1, rank), const),
                  pl.BlockSpec(wq.shape, const), pl.BlockSpec(wk.shape, const), pl.BlockSpec(wv.shape, const),
                  pl.BlockSpec((tm, LANES), tab), pl.BlockSpec((tm, LANES), tab), pl.BlockSpec((tm, LANES), tab)],
        out_specs=[pl.BlockSpec((tm, hw), row(0)), pl.BlockSpec((tm, hw), row(0)), vt_spec],
        out_shape=[jax.ShapeDtypeStruct((m, hw), BF16), jax.ShapeDtypeStruct((m, hw), BF16), vt_shape],
        compiler_params=_params("parallel"),
        name="mla_prep",
    )(slab_a, slab_a, slab_a, q_gain, kv_gain, wq, wk, wv, rope_c, rope_s1, rope_s2)


def _flash_scratch(n_streams, v_rows, tq, tk):
    scores = pltpu.VMEM((n_streams, tk, tq), F32)
    stat = pltpu.VMEM((n_streams, 1, tq), F32)
    probs = pltpu.VMEM((n_streams, tk, tq), BF16)
    slot = [scores, probs, stat]
    return slot + slot + [stat, pltpu.VMEM((n_streams, v_rows + BF16_ROWS, tq), F32)]


def _chunk_rows(c):
    return slice(c * KEY_CHUNK, (c + 1) * KEY_CHUNK)


def _fold_rows(x):
    return x.reshape(x.shape[0] // SUBLANES, SUBLANES, x.shape[1])


def _flash_transposed(n_full, n_streams, q_start, tq, tk, key_operand, query_operand, values, offset, scratch,
                      tile_id=lambda position: position, first_needed=None):
    slot_a, slot_b, (m_ref, acc_ref) = scratch[0:3], scratch[3:6], scratch[6:]
    n_chunks = tk // KEY_CHUNK
    ones_rows = jnp.ones((BF16_ROWS, tk), BF16)
    for i in range(n_streams):
        m_ref[i] = jnp.full((1, tq), NEG_INF, F32)
        acc_ref[i] = jnp.zeros(acc_ref.shape[1:], F32)
        slot_b[1][i] = jnp.zeros((tk, tq), BF16)
        slot_b[2][i] = jnp.ones((1, tq), F32)

    def column_max(s_ref, i):
        part = jnp.full((SUBLANES, tq), NEG_INF, F32)
        for c in range(n_chunks):
            part = jnp.maximum(part, jnp.max(_fold_rows(s_ref[i, _chunk_rows(c), :]), axis=0))
        return jnp.max(part, axis=0, keepdims=True)

    def stage1(j, slot, i):
        slot[0][i] = _dot(key_operand(j, i), query_operand(j, i))

    def stage2(j, slot, i, causal):
        s_ref, p_ref, alpha_ref = slot
        if causal is not None:
            s_ref[i] = jnp.where(causal, s_ref[i], NEG_INF)
        mx = column_max(s_ref, i)
        off = offset(j, i)
        m_prev = m_ref[i]
        if off is None:
            m_next = jnp.maximum(m_prev, mx)
            shift = m_next
        else:
            m_next = jnp.maximum(m_prev, mx + off)
            shift = m_next - off
        alpha = jnp.exp2(m_prev - m_next)
        for c in range(n_chunks):
            p_ref[i, _chunk_rows(c), :] = jnp.exp2(s_ref[i, _chunk_rows(c), :] - shift).astype(BF16)
        m_ref[i] = m_next
        alpha_ref[i] = alpha

    def causal_mask(tile):
        key_minus_query = (lax.broadcasted_iota(jnp.int32, (tk, tq), 0)
                           - lax.broadcasted_iota(jnp.int32, (tk, tq), 1))
        return key_minus_query <= q_start - tile * tk

    def weighted_values(tile, slot, i):
        return _dot(jnp.concatenate([values(tile, i), ones_rows], axis=0), slot[1][i])

    def run(n_last, to_tile, last_is_diagonal):
        def step(accumulate=None, produce=None, exponentiate=None, diagonal=False):
            accumulate, produce, exponentiate = [
                None if stage is None else (to_tile(stage[0]), stage[1])
                for stage in (accumulate, produce, exponentiate)]
            causal = causal_mask(exponentiate[0]) if diagonal else None
            products = []
            for i in range(n_streams + 1):
                if i < n_streams:
                    if accumulate is not None:
                        products.append(weighted_values(*accumulate, i))
                    if produce is not None:
                        stage1(*produce, i)
                    if exponentiate is not None:
                        stage2(*exponentiate, i, causal)
                if accumulate is not None and i > 0:
                    acc_ref[i - 1] = accumulate[1][2][i - 1] * acc_ref[i - 1] + products[i - 1]

        step(produce=(0, slot_a))

        def pair(t, carry):
            j0 = 2 * t
            step((jnp.maximum(j0 - 1, 0), slot_b), (j0 + 1, slot_b), (j0, slot_a))
            step((j0, slot_a), (j0 + 2, slot_a), (j0 + 1, slot_b))
            return carry

        lax.fori_loop(0, n_last // 2, pair, 0)
        pending = jnp.maximum(n_last - 1 - n_last % 2, 0)

        @pl.when(n_last % 2 == 1)
        def _():
            step((pending, slot_b), (n_last, slot_b), (n_last - 1, slot_a))
            step(accumulate=(n_last - 1, slot_a), exponentiate=(n_last, slot_b), diagonal=last_is_diagonal)
            step(accumulate=(n_last, slot_b))

        @pl.when(n_last % 2 == 0)
        def _():
            step(accumulate=(pending, slot_b), exponentiate=(n_last, slot_a), diagonal=last_is_diagonal)
            step(accumulate=(n_last, slot_a))

    if first_needed is None:
        run(n_full, tile_id, True)
        return
    causal = causal_mask(n_full)
    for i in range(n_streams):
        stage1(n_full, slot_a, i)
        stage2(n_full, slot_a, i, causal)
        acc_ref[i] = slot_a[2][i] * acc_ref[i] + weighted_values(n_full, slot_a, i)
    start = first_needed(m_ref)
    count = n_full - start

    @pl.when(count > 0)
    def _():
        run(count - 1, lambda position: start + position, False)


def _flash_result(scratch, i):
    acc = scratch[-1][i]
    v_rows = acc.shape[0] - BF16_ROWS
    return acc[:v_rows] / acc[v_rows:v_rows + 1]


def _kv_rows(j, tk):
    return pl.ds(pl.multiple_of(j * tk, tk), tk)


def _transposed_bf16(x):
    return x.astype(F32).T.astype(BF16)


def _alibi_rows(coef, tq):
    c = jnp.zeros((1, tq), F32) + coef
    hi = c.astype(BF16).astype(F32)
    rest = c - hi
    mid = rest.astype(BF16).astype(F32)
    lo = rest - mid
    zero = jnp.zeros((1, tq), F32)
    return jnp.concatenate([POS_SPLIT * hi, POS_SPLIT * mid, POS_SPLIT * lo, hi, mid, lo, zero, zero], axis=0)


def _augmented_query(q_t, feature_rows):
    tq = q_t.shape[1]
    pad = jnp.zeros((MXU_DEPTH - LANES - FEATURE_ROWS, tq), BF16)
    return jnp.concatenate([q_t, feature_rows.astype(BF16), pad], axis=0)


MLA_STEP_HEADS = 4


def _mla_attn_kernel(q_ref, k_ref, vt_ref, o_ref, *scratch, tq, tk):
    q_start = pl.program_id(2) * tq
    n = MLA_STEP_HEADS
    queries = [_transposed_bf16(q_ref[0, :, hh * LANES:(hh + 1) * LANES]) for hh in range(n)]
    _flash_transposed(
        q_start // tk, n, q_start, tq, tk,
        lambda j, i: k_ref[0, _kv_rows(j, tk), i * LANES:(i + 1) * LANES],
        lambda j, i: queries[i],
        lambda j, i: vt_ref[0, j, i * HEAD_DIM:(i + 1) * HEAD_DIM, :],
        lambda j, i: None, scratch)
    for pair in range(n // 2):
        o_t = jnp.concatenate([_flash_result(scratch, 2 * pair), _flash_result(scratch, 2 * pair + 1)], axis=0)
        o_ref[0, :, pair * LANES:(pair + 1) * LANES] = o_t.T.astype(o_ref.dtype)


def _mla_attn(q, k, vt):
    b, s, _ = q.shape
    tq, tk = min(FLASH_Q_TILE, s), min(KV_TILE, s)
    n = MLA_STEP_HEADS
    groups = MLA_HEADS // n
    return pl.pallas_call(
        functools.partial(_mla_attn_kernel, tq=tq, tk=tk),
        grid=(b, groups, s // tq),
        in_specs=[pl.BlockSpec((1, tq, n * LANES), lambda bi, p, i: (bi, i, p)),
                  pl.BlockSpec((1, s, n * LANES), lambda bi, p, i: (bi, 0, p)),
                  pl.BlockSpec((1, s // tk, n * HEAD_DIM, tk), lambda bi, p, i: (bi, 0, p, 0))],
        out_specs=pl.BlockSpec((1, tq, n * HEAD_DIM), lambda bi, p, i: (bi, i, p)),
        out_shape=jax.ShapeDtypeStruct((b, s, MLA_HEADS * HEAD_DIM), BF16),
        scratch_shapes=_flash_scratch(n, HEAD_DIM, tq, tk),
        compiler_params=_params("parallel", "parallel", "arbitrary"),
        name="mla_attn",
    )(q, k, vt)


def _gelu_tanh(x):
    return 0.5 * x * (1.0 + jnp.tanh(math.sqrt(2.0 / math.pi) * (x + 0.044715 * (x * x * x))))


def _compress_kernel(x_ref, pos_ref, w1a_ref, w1b_ref, w2_ref, k_ref, vt_ref, *, n_real):
    x = x_ref[0]
    n = x.shape[0]
    first = _dot(x, w1a_ref[...])
    second = _dot(x, w1b_ref[...])
    pos_hi, pos_lo = _split_bf16(pos_ref[...])
    bias = (_dot(pos_hi[:8], w1a_ref[...]) + _dot(pos_lo[:8], w1a_ref[...])
            + _dot(pos_hi[8:], w1b_ref[...]) + _dot(pos_lo[8:], w1b_ref[...]))[:1]
    pre = first + pltpu.roll(second, n - 1, 0) + bias
    out = _dot(_gelu_tanh(pre).astype(BF16), w2_ref[...])
    real = lax.broadcasted_iota(jnp.int32, out.shape, 0) < n_real
    out = jnp.where(real, out, 0.0)
    half = out.shape[1] // 2
    k_ref[0] = out[:, :half].astype(k_ref.dtype)
    vt_ref[0] = out[:, half:].T.astype(vt_ref.dtype)


def _compress(x_chunks, pos_exp, w1a, w1b, w2, n_real):
    b, n, width = x_chunks.shape
    half = w2.shape[1] // 2
    const = lambda bi: (0, 0)
    return pl.pallas_call(
        functools.partial(_compress_kernel, n_real=n_real),
        grid=(b,),
        in_specs=[pl.BlockSpec((1, n, width), lambda bi: (bi, 0, 0)),
                  pl.BlockSpec(pos_exp.shape, const), pl.BlockSpec(w1a.shape, const),
                  pl.BlockSpec(w1b.shape, const), pl.BlockSpec(w2.shape, const)],
        out_specs=[pl.BlockSpec((1, n, half), lambda bi: (bi, 0, 0)), pl.BlockSpec((1, half, n), lambda bi: (bi, 0, 0))],
        out_shape=[jax.ShapeDtypeStruct((b, n, half), BF16), jax.ShapeDtypeStruct((b, half, n), BF16)],
        compiler_params=_params("parallel"),
        name="nsa_compress",
    )(x_chunks, pos_exp, w1a, w1b, w2)


def _nsa_head_slope(h):
    return float(2.0 ** (-8.0 * (h + 1) / NSA_HEADS))


def _nsa_queries(q_ref, g, scale):
    out = []
    for hg in range(NSA_HG):
        h = g * NSA_HG + hg
        slab = q_ref[0, :, (h // 2) * LANES:(h // 2 + 1) * LANES].astype(F32) * scale
        out.append(_move_head(slab, h % 2, g))
    return out


def _select_blocks(imp_t, q_pos):
    n_blocks = imp_t.shape[0]
    blk = lax.broadcasted_iota(jnp.int32, imp_t.shape, 0)
    cur = q_pos // SEL_LEN
    forced = jnp.where(blk == 0, 1.0, 0.0) + jnp.where(blk == cur, 1.0, 0.0) + jnp.where(blk == cur - 1, 1.0, 0.0)
    forced = jnp.minimum(forced, 1.0)
    val = jnp.where(blk <= cur, imp_t + FORCE_BONUS * forced, NEG_INF)
    chosen = jnp.zeros(imp_t.shape, F32)
    for _ in range(SEL_TOPK):
        top = jnp.max(val, axis=0, keepdims=True)
        first = jnp.min(jnp.where(val == top, blk, n_blocks), axis=0, keepdims=True)
        hit = blk == first
        chosen = jnp.where(hit, 1.0, chosen)
        val = jnp.where(hit, -jnp.inf, val)
    return jnp.where(chosen > 0.5, 0.0, NEG_INF)


def _masked_softmax_pass(s_ref, p_ref, tiles, tq):
    part = jnp.full((SUBLANES, tq), NEG_INF, F32)
    for t, rows, keep, off in tiles:
        for c in range(rows // KEY_CHUNK):
            s = jnp.where(keep(_chunk_rows(c)), s_ref[t, _chunk_rows(c), :], NEG_INF)
            s_ref[t, _chunk_rows(c), :] = s
            part = jnp.maximum(part, jnp.max(_fold_rows(s), axis=0) + off)
    m = jnp.max(part, axis=0, keepdims=True)
    total = jnp.zeros((SUBLANES, tq), F32)
    for t, rows, keep, off in tiles:
        shift = m - off
        for c in range(rows // KEY_CHUNK):
            p = jnp.exp2(s_ref[t, _chunk_rows(c), :] - shift)
            total = total + jnp.sum(_fold_rows(p), axis=0)
            p_ref[t, _chunk_rows(c), :] = p.astype(BF16)
    return jnp.sum(total, axis=0, keepdims=True), m > 0.5 * NEG_INF


def _nsa_cmp_win_kernel(q_ref, kc_ref, vct_ref, kw_ref, vwt_ref, feat_ref, ovt_ref, oc_ref, ow_ref, sel_ref, any_ref,
                        end_ref, rel_ref, sc_ref, sw_ref, pc_ref, pw_ref, *, tq, tk):
    qi = pl.program_id(1)
    q_start = qi * tq
    n_cmp = kc_ref.shape[1]
    end_ref[...] = (lax.broadcasted_iota(jnp.int32, (n_cmp, tq), 0) * CMP_STRIDE + (CMP_LEN - 1)
                    - lax.broadcasted_iota(jnp.int32, (n_cmp, tq), 1))
    rel_ref[...] = (lax.broadcasted_iota(jnp.int32, (tk, tq), 0) - lax.broadcasted_iota(jnp.int32, (tk, tq), 1))
    prev_tile = jnp.maximum(qi - 1, 0)
    prev_bound = jnp.where(qi >= 1, 0, tk)
    zeros = jnp.zeros((FEATURE_ROWS - SUBLANES, tq), F32)
    k_cmp = jnp.concatenate([kc_ref[0], feat_ref[0:n_cmp, :]], axis=1)
    k_win = [jnp.concatenate([kw_ref[0, _kv_rows(j, tk), :], feat_ref[...]], axis=1) for j in (prev_tile, qi)]
    q_pos = q_start + lax.broadcasted_iota(jnp.int32, (1, tq), 1)
    for g in range(NSA_GROUPS):
        queries = [_transposed_bf16(q) for q in _nsa_queries(q_ref, g, HEAD_DIM ** -0.5 * LOG2E)]
        group_rows = slice(g * HEAD_DIM, (g + 1) * HEAD_DIM)
        imp_t = jnp.zeros((LANES, tq), F32)
        out_c, out_w = [], []
        for hg in range(NSA_HG):
            coef = _nsa_head_slope(g * NSA_HG + hg) * LOG2E
            buf = hg % 2
            sc, sw, pc, pw = sc_ref.at[buf], sw_ref.at[buf], pc_ref.at[buf], pw_ref.at[buf]
            cmp_query = _augmented_query(queries[hg], jnp.concatenate([_alibi_rows(CMP_STRIDE * coef, tq), zeros], 0))
            win_query = _augmented_query(queries[hg], jnp.concatenate([_alibi_rows(coef, tq), zeros], 0))
            sc[0] = _dot(k_cmp, cmp_query)
            sw[0] = _dot(k_win[0], win_query)
            sw[1] = _dot(k_win[1], win_query)
            norm, has_any = _masked_softmax_pass(
                sc, pc, [(0, n_cmp, lambda r: end_ref[r, :] <= q_start, 0.0)], tq)
            inv = jnp.where(has_any, 1.0 / norm, 0.0)
            out_c.append(_dot(vct_ref[0, group_rows, :], pc[0]) * inv)
            imp_t = imp_t + _dot(ovt_ref[...], pc[0]) * inv
            norm, _ = _masked_softmax_pass(
                sw, pw, [(0, tk, lambda r: rel_ref[r, :] > prev_bound, -coef * tk),
                         (1, tk, lambda r: rel_ref[r, :] <= 0, 0.0)], tq)
            acc = (_dot(vwt_ref[0, prev_tile, group_rows, :], pw[0]) + _dot(vwt_ref[0, qi, group_rows, :], pw[1]))
            out_w.append(acc / norm)
        for pair in range(NSA_HG // 2):
            cols = slice((g * 2 + pair) * LANES, (g * 2 + pair + 1) * LANES)
            oc_ref[0, :, cols] = jnp.concatenate(out_c[2 * pair:2 * pair + 2], axis=0).T
            ow_ref[0, :, cols] = jnp.concatenate(out_w[2 * pair:2 * pair + 2], axis=0).T
        bias_t = _select_blocks(imp_t, q_pos)
        sel_ref[0, g * LANES:(g + 1) * LANES, :] = bias_t
        any_ref[0, 0, g * LANES:(g + 1) * LANES, :] = jnp.broadcast_to(
            jnp.max(bias_t, axis=1, keepdims=True), (LANES, LANES))


def _nsa_cmp_win(slab_b, cmp_k, cmp_vt, vt_nsa, key_features, overlap_t, seq):
    b = slab_b.shape[0]
    tq, tk = min(FLASH_Q_TILE, seq), min(KV_TILE, seq)
    assert tq == tk == WINDOW, "the window branch is written for one previous and one diagonal key tile"
    n_cmp = cmp_k.shape[1]
    qw = NSA_HEADS * HEAD_DIM
    base = qw // LANES
    tile = lambda bi, i: (bi, i, 0)
    return pl.pallas_call(
        functools.partial(_nsa_cmp_win_kernel, tq=tq, tk=tk),
        grid=(b, seq // tq),
        in_specs=[pl.BlockSpec((1, tq, qw), tile),
                  pl.BlockSpec((1, n_cmp, LANES), lambda bi, i: (bi, 0, 0)),
                  pl.BlockSpec((1, LANES, n_cmp), lambda bi, i: (bi, 0, 0)),
                  pl.BlockSpec((1, seq, LANES), lambda bi, i: (bi, 0, base + 1)),
                  pl.BlockSpec((1, seq // tk, LANES, tk), lambda bi, i: (bi, 0, 1, 0)),
                  pl.BlockSpec(key_features.shape, lambda bi, i: (0, 0)),
                  pl.BlockSpec(overlap_t.shape, lambda bi, i: (0, 0))],
        out_specs=[pl.BlockSpec((1, tq, qw), tile), pl.BlockSpec((1, tq, qw), tile),
                   pl.BlockSpec((1, NSA_GROUPS * LANES, tq), lambda bi, i: (bi, 0, i)),
                   pl.BlockSpec((1, 1, NSA_GROUPS * LANES, LANES), lambda bi, i: (bi, i, 0, 0))],
        out_shape=[jax.ShapeDtypeStruct((b, seq, qw), F32), jax.ShapeDtypeStruct((b, seq, qw), F32),
                   jax.ShapeDtypeStruct((b, NSA_GROUPS * LANES, seq), F32),
                   jax.ShapeDtypeStruct((b, seq // tq, NSA_GROUPS * LANES, LANES), F32)],
        scratch_shapes=[pltpu.VMEM((n_cmp, tq), jnp.int32), pltpu.VMEM((tk, tq), jnp.int32),
                        pltpu.VMEM((2, 1, n_cmp, tq), F32), pltpu.VMEM((2, 2, tk, tq), F32),
                        pltpu.VMEM((2, 1, n_cmp, tq), BF16), pltpu.VMEM((2, 2, tk, tq), BF16)],
        compiler_params=_params("parallel", "arbitrary"),
        name="nsa_cmp_win",
    )(slab_b, cmp_k, cmp_vt, slab_b, vt_nsa, key_features, overlap_t)


def _nsa_sel_kernel(tiles_ref, counts_ref, q_ref, k_ref, vt_ref, feat_ref, sel_ref, oc_ref, ow_ref, gate_ref, gx_ref,
                    o_ref, *scratch, tq, tk, max_tiles):
    qi = pl.program_id(1)
    q_start = qi * tq
    diagonal_tile = q_start // tk
    out_slabs = []
    for g in range(NSA_GROUPS):
        queries = [_transposed_bf16(q) for q in _nsa_queries(q_ref, g, HEAD_DIM ** -0.5 * LOG2E)]
        coefs = [_nsa_head_slope(g * NSA_HG + hg) * LOG2E for hg in range(NSA_HG)]
        alibi = [_alibi_rows(c, tq) for c in coefs]
        entry = (pl.program_id(0) * pl.num_programs(1) + qi) * NSA_GROUPS + g
        n_active = counts_ref[entry]

        def tile_id(position, entry=entry, n_active=n_active):
            listed = tiles_ref[entry * max_tiles + jnp.minimum(position, max_tiles - 1)]
            return jnp.where(position < n_active, listed, diagonal_tile)

        def key_operand(j, i):
            return jnp.concatenate([k_ref[0, _kv_rows(j, tk), :], feat_ref[...]], axis=1)

        def query_operand(j, i, g=g, queries=queries, alibi=alibi):
            first_block = pl.multiple_of(g * LANES + j * BLOCKS_PER_TILE, BLOCKS_PER_TILE)
            blocks = sel_ref[0, pl.ds(first_block, BLOCKS_PER_TILE), :]
            return _augmented_query(queries[i], jnp.concatenate([alibi[i], blocks], axis=0))

        _flash_transposed(
            n_active, NSA_HG, q_start, tq, tk, key_operand, query_operand,
            lambda j, i, g=g: vt_ref[0, j, g * HEAD_DIM:(g + 1) * HEAD_DIM, :],
            lambda j, i, coefs=coefs: coefs[i] * (j * tk - q_start).astype(F32), scratch, tile_id)
        heads = [_flash_result(scratch, hg) for hg in range(NSA_HG)]
        for pair in range(NSA_HG // 2):
            out_slabs.append(jnp.concatenate(heads[2 * pair:2 * pair + 2], axis=0).T)
    gates = jax.nn.sigmoid(gate_ref[...])
    g_hi, g_lo = _split_bf16(gates)
    width = NSA_HEADS * HEAD_DIM
    for i, o_sel in enumerate(out_slabs):
        mixed = None
        for branch, o_branch in enumerate((oc_ref[0, :, i * LANES:(i + 1) * LANES], o_sel,
                                           ow_ref[0, :, i * LANES:(i + 1) * LANES])):
            gx = gx_ref[:, branch * width + i * LANES:branch * width + (i + 1) * LANES]
            term = (_dot(g_hi, gx) + _dot(g_lo, gx)) * o_branch
            mixed = term if mixed is None else mixed + term
        o_ref[0, :, i * LANES:(i + 1) * LANES] = mixed.astype(o_ref.dtype)


def _active_key_tiles(block_any, tq, tk):
    b, n_q = block_any.shape[:2]
    max_tiles = LANES // BLOCKS_PER_TILE
    hit = block_any[..., 0].reshape(b, n_q, NSA_GROUPS, max_tiles, BLOCKS_PER_TILE).max(axis=-1) > 0.5 * NEG_INF
    before_diagonal = jnp.arange(max_tiles)[None, :] < (jnp.arange(n_q) * tq // tk)[:, None]
    hit = hit & before_diagonal[None, :, None, :]
    order = jnp.argsort(jnp.logical_not(hit), axis=-1, stable=True)
    return order.astype(jnp.int32).reshape(-1), hit.sum(axis=-1).astype(jnp.int32).reshape(-1), max_tiles


def _nsa_sel(slab_b, vt, key_features, sel_bias_t, block_any, o_cmp, o_win, slab_a, gate_expand, seq, gate_col_block):
    b = slab_b.shape[0]
    tq, tk = min(FLASH_Q_TILE, seq), min(KV_TILE, seq)
    qw = NSA_HEADS * HEAD_DIM
    base = qw // LANES
    per_seq = seq // tq
    tiles, counts, max_tiles = _active_key_tiles(block_any, tq, tk)
    tile = lambda bi, i, *_: (bi, i, 0)
    grid_spec = pltpu.PrefetchScalarGridSpec(
        num_scalar_prefetch=2,
        grid=(b, seq // tq),
        in_specs=[pl.BlockSpec((1, tq, qw), tile),
                  pl.BlockSpec((1, seq, LANES), lambda bi, i, *_: (bi, 0, base)),
                  pl.BlockSpec((1, seq // tk, LANES, tk), lambda bi, i, *_: (bi, 0, 0, 0)),
                  pl.BlockSpec(key_features.shape, lambda bi, i, *_: (0, 0)),
                  pl.BlockSpec((1, NSA_GROUPS * LANES, tq), lambda bi, i, *_: (bi, 0, i)),
                  pl.BlockSpec((1, tq, qw), tile), pl.BlockSpec((1, tq, qw), tile),
                  pl.BlockSpec((tq, LANES), lambda bi, i, *_: (bi * per_seq + i, gate_col_block)),
                  pl.BlockSpec(gate_expand.shape, lambda bi, i, *_: (0, 0))],
        out_specs=pl.BlockSpec((1, tq, qw), tile),
        scratch_shapes=_flash_scratch(NSA_HG, HEAD_DIM, tq, tk))
    return pl.pallas_call(
        functools.partial(_nsa_sel_kernel, tq=tq, tk=tk, max_tiles=max_tiles),
        grid_spec=grid_spec,
        out_shape=jax.ShapeDtypeStruct((b, seq, qw), BF16),
        compiler_params=_params("parallel", "arbitrary"),
        name="nsa_sel",
    )(tiles, counts, slab_b, slab_b, vt, key_features, sel_bias_t, o_cmp, o_win, slab_a, gate_expand)


DIFF_STEP_HEADS = 2


SKIP_GAP = 180.0
NORM_SLACK = 1.01


def _diff_attn_kernel(slope_ref, lam_ref, q_ref, k_ref, vt_ref, feat_ref, g_ref, o_ref, knorm_ref, *scratch,
                      tq, tk, lam_init):
    first_head = pl.program_id(1) * DIFF_STEP_HEADS
    q_start = pl.program_id(2) * tq
    n_full = q_start // tk
    n_tiles = k_ref.shape[1] // tk

    @pl.when(pl.program_id(2) == 0)
    def _():
        for hh in range(DIFF_STEP_HEADS):
            for j in range(n_tiles):
                k = k_ref[0, j * tk:(j + 1) * tk, hh * LANES:(hh + 1) * LANES].astype(F32)
                knorm_ref[hh * n_tiles + j] = jnp.sqrt(jnp.max(jnp.sum(k * k, axis=1, keepdims=True)))

    zeros = jnp.zeros((FEATURE_ROWS - SUBLANES, tq), F32)
    coefs, queries, q_norms = [], [], []
    for hh in range(DIFF_STEP_HEADS):
        coef = slope_ref[first_head + hh] * LOG2E
        q = q_ref[0, :, hh * LANES:(hh + 1) * LANES].astype(F32) * (HEAD_DIM ** -0.5 * LOG2E)
        features = jnp.concatenate([_alibi_rows(coef, tq), zeros], axis=0)
        coefs.append(coef)
        for half in range(2):
            q_half = _keep_half(q, half)
            queries.append(_augmented_query(_transposed_bf16(q_half), features))
            q_norms.append(NORM_SLACK * jnp.sqrt(jnp.max(jnp.sum(q_half * q_half, axis=1, keepdims=True))))

    def head_lanes(i):
        return slice((i // 2) * LANES, (i // 2 + 1) * LANES)

    def offset(j, i):
        return coefs[i // 2] * (j * tk - q_start).astype(F32)

    def first_needed(m_ref):
        floors = [jnp.min(m_ref[i]) - SKIP_GAP for i in range(2 * DIFF_STEP_HEADS)]

        def body(j, first):
            needed = jnp.bool_(False)
            for i in range(2 * DIFF_STEP_HEADS):
                bound = q_norms[i] * knorm_ref[(i // 2) * n_tiles + j] + coefs[i // 2] * (tk - 1) + offset(j, i)
                needed = jnp.logical_or(needed, bound >= floors[i])
            return jnp.where(needed, jnp.minimum(first, j), first)

        return lax.fori_loop(0, n_full, body, n_full)

    _flash_transposed(
        n_full, 2 * DIFF_STEP_HEADS, q_start, tq, tk,
        lambda j, i: jnp.concatenate([k_ref[0, _kv_rows(j, tk), head_lanes(i)], feat_ref[...]], axis=1),
        lambda j, i: queries[i],
        lambda j, i: vt_ref[0, j, head_lanes(i), :],
        offset, scratch, first_needed=first_needed)
    lam_vec = lam_ref[...]
    lam = (jnp.exp(jnp.sum(lam_vec[0:1] * lam_vec[1:2], axis=1, keepdims=True))
           - jnp.exp(jnp.sum(lam_vec[2:3] * lam_vec[3:4], axis=1, keepdims=True)) + lam_init)
    for hh in range(DIFF_STEP_HEADS):
        o = (_flash_result(scratch, 2 * hh) - lam * _flash_result(scratch, 2 * hh + 1)).T
        o_ref[0, :, hh * LANES:(hh + 1) * LANES] = (
            _rms_norm(o, g_ref[...], RMS_EPS) * (1.0 - lam_init)).astype(o_ref.dtype)


def _diff_attn(qk, vt, key_features, slopes, lam_vecs, subln_g, lam_init):
    b, s, _ = qk.shape
    tq, tk = min(FLASH_Q_TILE, s), min(KV_TILE, s)
    n = DIFF_STEP_HEADS
    groups = DIFF_HEADS // n
    smem = pl.BlockSpec(memory_space=pltpu.SMEM)
    return pl.pallas_call(
        functools.partial(_diff_attn_kernel, tq=tq, tk=tk, lam_init=lam_init),
        grid=(b, groups, s // tq),
        in_specs=[smem, pl.BlockSpec(lam_vecs.shape, lambda bi, h, i: (0, 0)),
                  pl.BlockSpec((1, tq, n * LANES), lambda bi, h, i: (bi, i, h)),
                  pl.BlockSpec((1, s, n * LANES), lambda bi, h, i: (bi, 0, groups + h)),
                  pl.BlockSpec((1, s // tk, n * LANES, tk), lambda bi, h, i: (bi, 0, h, 0)),
                  pl.BlockSpec(key_features.shape, lambda bi, h, i: (0, 0)),
                  pl.BlockSpec((1, LANES), lambda bi, h, i: (0, 0))],
        out_specs=pl.BlockSpec((1, tq, n * LANES), lambda bi, h, i: (bi, i, h)),
        out_shape=jax.ShapeDtypeStruct((b, s, DIFF_HEADS * LANES), BF16),
        scratch_shapes=[pltpu.SMEM((n * (s // tk),), F32)] + _flash_scratch(2 * n, LANES, tq, tk),
        compiler_params=_params("parallel", "parallel", "arbitrary"),
        name="diff_attn",
    )(slopes, lam_vecs, qk, qk, vt, key_features, subln_g)


def _out_ln_kernel(*refs, n_in):
    a_refs, w_refs = refs[:n_in], refs[n_in:2 * n_in]
    x_ref, g_ref, b_ref, o_ref = refs[2 * n_in:]
    y = None
    for a_ref, w_ref in zip(a_refs, w_refs):
        t = _dot(a_ref[...], w_ref[...])
        y = t if y is None else y + t
    o_ref[...] = _layer_norm(DN_ALPHA * x_ref[...] + y, g_ref[...], b_ref[...])


def _out_ln(acts, weights, x, g, b):
    m, d = x.shape
    tm = min(ROW_TILE, m)
    row = lambda i: (i, 0)
    const = lambda i: (0, 0)
    return pl.pallas_call(
        functools.partial(_out_ln_kernel, n_in=len(acts)),
        grid=(m // tm,),
        in_specs=[pl.BlockSpec((tm, a.shape[1]), row) for a in acts]
        + [pl.BlockSpec(w.shape, const) for w in weights]
        + [pl.BlockSpec((tm, d), row), pl.BlockSpec((1, d), const), pl.BlockSpec((1, d), const)],
        out_specs=pl.BlockSpec((tm, d), row),
        out_shape=jax.ShapeDtypeStruct((m, d), F32),
        compiler_params=_params("parallel"),
        name="out_proj_ln",
    )(*acts, *weights, x, g, b)


def _mlp_kernel(x_ref, wu_ref, wd_ref, g_ref, b_ref, o_ref, xb_ref, acc_ref):
    f = pl.program_id(1)

    @pl.when(f == 0)
    def _():
        xb_ref[...] = x_ref[...].astype(BF16)
        acc_ref[...] = jnp.zeros(acc_ref.shape, F32)

    hidden = jnp.maximum(_dot(xb_ref[...], wu_ref[...]), 0.0)
    acc_ref[...] += _dot((hidden * hidden).astype(BF16), wd_ref[...])

    @pl.when(f == pl.num_programs(1) - 1)
    def _():
        o_ref[...] = _layer_norm(DN_ALPHA * x_ref[...] + acc_ref[...], g_ref[...], b_ref[...])


def _mlp(x, w_up, w_down, g, b):
    m, d = x.shape
    ff = w_up.shape[1]
    tm, tf = min(ROW_TILE, m), min(FF_TILE, ff)
    return pl.pallas_call(
        _mlp_kernel,
        grid=(m // tm, ff // tf),
        in_specs=[pl.BlockSpec((tm, d), lambda i, f: (i, 0)),
                  pl.BlockSpec((d, tf), lambda i, f: (0, f)),
                  pl.BlockSpec((tf, d), lambda i, f: (f, 0)),
                  pl.BlockSpec((1, d), lambda i, f: (0, 0)), pl.BlockSpec((1, d), lambda i, f: (0, 0))],
        out_specs=pl.BlockSpec((tm, d), lambda i, f: (i, 0)),
        out_shape=jax.ShapeDtypeStruct((m, d), F32),
        scratch_shapes=[pltpu.VMEM((tm, d), BF16), pltpu.VMEM((tm, d), F32)],
        compiler_params=_params("parallel", "arbitrary"),
        name="mlp_ln",
    )(x, w_up, w_down, g, b)


def _pad_cols(w, width):
    return jnp.pad(w, ((0, 0), (0, width - w.shape[1])))


def _layer0_weights(w_in, w_uq, w_ukv, d_model):
    rank = d_model // 4
    kvw = NSA_GROUPS * HEAD_DIM
    o = np.cumsum([0, rank, rank, MLA_ROPE, NSA_HEADS * HEAD_DIM] + [kvw] * 6 + [3 * NSA_HEADS])
    seg = lambda i: w_in[:, o[i]:o[i + 1]]
    zeros = lambda n: jnp.zeros((w_in.shape[0], n), w_in.dtype)
    rope_slab = jnp.concatenate([zeros(MLA_NOPE), seg(2), zeros(LANES - MLA_NOPE - MLA_ROPE)], axis=1)
    w_a = jnp.concatenate([seg(0), seg(1), rope_slab, _pad_cols(seg(10), LANES)], axis=1)
    w_b = jnp.concatenate([seg(3), seg(6), seg(8)], axis=1)
    w_c = jnp.concatenate([seg(4), seg(5)], axis=1)
    w_vs = jnp.concatenate([seg(7), seg(9)], axis=1)
    wq = jnp.pad(w_uq.reshape(rank, MLA_HEADS, MLA_NOPE + MLA_ROPE),
                 ((0, 0), (0, 0), (0, LANES - MLA_NOPE - MLA_ROPE))).reshape(rank, MLA_HEADS * LANES)
    ukv = w_ukv.reshape(rank, MLA_HEADS, MLA_NOPE + HEAD_DIM)
    wk = jnp.pad(ukv[:, :, :MLA_NOPE], ((0, 0), (0, 0), (0, LANES - MLA_NOPE))).reshape(rank, MLA_HEADS * LANES)
    wv = ukv[:, :, MLA_NOPE:].reshape(rank, MLA_HEADS * HEAD_DIM)
    return [w.astype(BF16) for w in (w_a, w_b, w_c, w_vs, wq, wk, wv)]


def _rope_tables(seq):
    inv = 1.0 / (ROPE_THETA ** (jnp.arange(0, MLA_ROPE, 2, dtype=F32) / MLA_ROPE))
    ang = jnp.arange(seq, dtype=F32)[:, None] * inv[None, :]
    cos, sin = jnp.cos(ang), jnp.sin(ang)
    half = MLA_ROPE // 2
    z = lambda n: jnp.zeros((seq, n), F32)
    tail = LANES - MLA_NOPE - MLA_ROPE
    c = jnp.concatenate([jnp.ones((seq, MLA_NOPE), F32), cos, cos, z(tail)], axis=1)
    s1 = jnp.concatenate([z(MLA_NOPE), -sin, z(half), z(tail)], axis=1)
    s2 = jnp.concatenate([z(MLA_NOPE), z(half), sin, z(tail)], axis=1)
    return c, s1, s2


def _compress_weights(pos_k, w1_k, w2_k, pos_v, w1_v, w2_v):
    eye = jnp.eye(2 * NSA_GROUPS, dtype=F32)
    halves = []
    for a in range(CMP_LEN // CMP_STRIDE):
        rows = slice(a * CMP_STRIDE * HEAD_DIM, (a + 1) * CMP_STRIDE * HEAD_DIM)
        wk = w1_k[rows].reshape(CMP_STRIDE, HEAD_DIM, HEAD_DIM)
        wv = w1_v[rows].reshape(CMP_STRIDE, HEAD_DIM, HEAD_DIM)
        per_slot = jnp.stack([wk, wk, wv, wv], axis=0)
        full = jnp.einsum('st,srdj->rsdtj', eye, per_slot)
        halves.append(full.reshape(CMP_STRIDE * 4 * HEAD_DIM, 4 * HEAD_DIM).astype(BF16))
    w2 = jnp.einsum('st,sdj->sdtj', eye, jnp.stack([w2_k, w2_k, w2_v, w2_v])).reshape(4 * HEAD_DIM, 4 * HEAD_DIM)
    pos = jnp.concatenate([pos_k, pos_k, pos_v, pos_v], axis=1)
    pos = pos.reshape(CMP_LEN // CMP_STRIDE, 1, CMP_STRIDE * 4 * HEAD_DIM)
    pos = jnp.broadcast_to(pos, (pos.shape[0], 8, pos.shape[2])).reshape(-1, pos.shape[2])
    return pos, halves[0], halves[1], w2.astype(BF16)


def _overlap_table(n_cmp_pad, n_cmp):
    c0 = np.arange(n_cmp_pad)[None, :] * CMP_STRIDE
    s0 = np.arange(LANES)[:, None] * SEL_LEN
    ov = np.maximum(np.minimum(c0 + CMP_LEN, s0 + SEL_LEN) - np.maximum(c0, s0), 0) / CMP_LEN
    ov = ov * (np.arange(n_cmp_pad)[None, :] < n_cmp)
    return jnp.asarray(ov, BF16)


def _key_feature_table(tk):
    c = np.arange(tk)
    table = np.zeros((tk, LANES), np.float32)
    table[:, 0:3] = (c // POS_SPLIT)[:, None]
    table[:, 3:6] = (c % POS_SPLIT)[:, None]
    table[c, BLOCK_LANE0 + c // SEL_LEN] = 1.0
    return jnp.asarray(table, BF16)


def _gate_expand_table():
    width = NSA_HEADS * HEAD_DIM
    table = np.zeros((LANES, 3 * width), np.float32)
    for h in range(NSA_HEADS):
        for branch in range(3):
            table[h * 3 + branch, branch * width + h * HEAD_DIM:branch * width + (h + 1) * HEAD_DIM] = 1.0
    return jnp.asarray(table, BF16)


def _alibi_slopes(n):
    return jnp.asarray(2.0 ** (-8.0 * np.arange(1, n + 1) / n), dtype=F32)


def _layer0_mixer(x2, b, s, w_in, q_norm, w_uq, kv_norm, w_ukv, pos_k, w1_k, w2_k, pos_v, w1_v, w2_v, w_out):
    d = x2.shape[1]
    rank = d // 4
    w_a, w_b, w_c, w_vs, wq, wk, wv = _layer0_weights(w_in, w_uq, w_ukv, d)
    slab_a, slab_b, slab_c, vt_nsa = _project(x2, [w_a, w_b, w_c, w_vs], [F32, BF16, BF16, BF16],
                                              [False, False, False, True], b, s)
    rope_c, rope_s1, rope_s2 = _rope_tables(s)
    q, k, vt = _mla_prep(slab_a, q_norm.reshape(1, rank), kv_norm.reshape(1, rank), wq, wk, wv,
                         rope_c, rope_s1, rope_s2, b, s)
    o_mla = _mla_attn(q.reshape(b, s, -1), k.reshape(b, s, -1), vt)
    n_chunks = s // CMP_STRIDE
    n_cmp = (s - CMP_LEN) // CMP_STRIDE + 1
    pos, w1a, w1b, w2 = _compress_weights(pos_k, w1_k, w2_k, pos_v, w1_v, w2_v)
    cmp_k, cmp_vt = _compress(slab_c.reshape(b, n_chunks, CMP_STRIDE * slab_c.shape[1]), pos, w1a, w1b, w2, n_cmp)
    slab_b3 = slab_b.reshape(b, s, -1)
    key_features = _key_feature_table(min(KV_TILE, s))
    o_cmp, o_win, sel_bias_t, block_any = _nsa_cmp_win(slab_b3, cmp_k, cmp_vt, vt_nsa, key_features,
                                                       _overlap_table(n_chunks, n_cmp), s)
    o_nsa = _nsa_sel(slab_b3, vt_nsa, key_features, sel_bias_t, block_any, o_cmp, o_win, slab_a,
                     _gate_expand_table(), s, (2 * rank + LANES) // LANES)
    half = o_mla.shape[-1]
    w_out_b = w_out.astype(BF16)
    return [o_mla.reshape(b * s, half), o_nsa.reshape(b * s, -1)], [w_out_b[:half], w_out_b[half:]]


def _layer1_mixer(x2, b, s, w_qkv, lam_q1, lam_k1, lam_q2, lam_k2, subln_g, w_o, layer_idx):
    d = x2.shape[1]
    w = w_qkv.astype(BF16)
    qk, vt = _project(x2, [w[:, :2 * d], w[:, 2 * d:]], [BF16, BF16], [False, True], b, s)
    lam_init = 0.8 - 0.6 * math.exp(-0.3 * layer_idx)
    lam_vecs = jnp.stack([lam_q1, lam_k1, lam_q2, lam_k2]).astype(F32)
    o = _diff_attn(qk.reshape(b, s, -1), vt, _key_feature_table(min(KV_TILE, s)), _alibi_slopes(DIFF_HEADS),
                   lam_vecs, subln_g.reshape(1, -1), lam_init)
    return [o.reshape(b * s, -1)], [w_o.astype(BF16)]


def kernel(x, l0_w_in, l0_mla_q_norm, l0_mla_w_uq, l0_mla_kv_norm, l0_mla_w_ukv, l0_nsa_cmp_pos_k, l0_nsa_cmp_w1_k, l0_nsa_cmp_w2_k, l0_nsa_cmp_pos_v, l0_nsa_cmp_w1_v, l0_nsa_cmp_w2_v, l0_w_out, l0_ln_mix_g, l0_ln_mix_b, l0_w_up, l0_w_down, l0_ln_ffn_g, l0_ln_ffn_b, l1_w_qkv, l1_lam_q1, l1_lam_k1, l1_lam_q2, l1_lam_k2, l1_subln_g, l1_w_o, l1_ln_mix_g, l1_ln_mix_b, l1_w_up, l1_w_down, l1_ln_ffn_g, l1_ln_ffn_b):
    b, s, d = x.shape
    x2 = x.reshape(b * s, d)
    vec = lambda p: p.reshape(1, d)
    acts, weights = _layer0_mixer(x2, b, s, l0_w_in, l0_mla_q_norm, l0_mla_w_uq, l0_mla_kv_norm, l0_mla_w_ukv,
                                  l0_nsa_cmp_pos_k, l0_nsa_cmp_w1_k, l0_nsa_cmp_w2_k,
                                  l0_nsa_cmp_pos_v, l0_nsa_cmp_w1_v, l0_nsa_cmp_w2_v, l0_w_out)
    x2 = _out_ln(acts, weights, x2, vec(l0_ln_mix_g), vec(l0_ln_mix_b))
    x2 = _mlp(x2, l0_w_up.astype(BF16), l0_w_down.astype(BF16), vec(l0_ln_ffn_g), vec(l0_ln_ffn_b))
    acts, weights = _layer1_mixer(x2, b, s, l1_w_qkv, l1_lam_q1, l1_lam_k1, l1_lam_q2, l1_lam_k2,
                                  l1_subln_g, l1_w_o, 1)
    x2 = _out_ln(acts, weights, x2, vec(l1_ln_mix_g), vec(l1_ln_mix_b))
    x2 = _mlp(x2, l1_w_up.astype(BF16), l1_w_down.astype(BF16), vec(l1_ln_ffn_g), vec(l1_ln_ffn_b))
    return x2.reshape(b, s, d)
```

```python
import functools
import math

import jax
import jax.numpy as jnp
import numpy as np
from jax import lax
from jax.experimental import pallas as pl
from jax.experimental.pallas import tpu as pltpu

F32 = jnp.float32
BF16 = jnp.bfloat16

LANES = 128
SUBLANES = 8
BF16_ROWS = 16
MXU_DEPTH = 256
HEAD_DIM = 64
FLASH_Q_TILE = 512
KV_TILE = 512
KEY_CHUNK = 32
ROW_TILE = 512
FF_TILE = 1024
VMEM_LIMIT = 56 * 1024 * 1024

NEG_INF = -1e30
LOG2E = math.log2(math.e)
LN_EPS = 1e-5
RMS_EPS = 1e-6
DEPTH = 2
DN_ALPHA = (2.0 * DEPTH) ** 0.25

MLA_HEADS = 8
MLA_NOPE = 64
MLA_ROPE = 32
ROPE_THETA = 10000.0
NSA_HEADS = 8
NSA_GROUPS = 2
NSA_HG = NSA_HEADS // NSA_GROUPS
CMP_LEN = 32
CMP_STRIDE = 16
SEL_LEN = 64
SEL_TOPK = 16
WINDOW = 512
FORCE_BONUS = 1e3
DIFF_HEADS = 8

POS_SPLIT = 16
FEATURE_ROWS = 16
BLOCK_LANE0 = 8
BLOCKS_PER_TILE = KV_TILE // SEL_LEN


def _params(*sem):
    return pltpu.CompilerParams(dimension_semantics=sem, vmem_limit_bytes=VMEM_LIMIT)


def _dot(a, b):
    return jnp.dot(a, b, preferred_element_type=F32)


def _split_bf16(x):
    hi = x.astype(BF16)
    lo = (x - hi.astype(F32)).astype(BF16)
    return hi, lo


def _layer_norm(z, g, b):
    mu = jnp.mean(z, axis=-1, keepdims=True)
    zc = z - mu
    var = jnp.mean(zc * zc, axis=-1, keepdims=True)
    return zc * lax.rsqrt(var + LN_EPS) * g + b


def _rms_norm(z, g, eps):
    return z * lax.rsqrt(jnp.mean(z * z, axis=-1, keepdims=True) + eps) * g


def _lane_iota(shape):
    return lax.broadcasted_iota(jnp.int32, shape, 1)


def _keep_half(x, half):
    lane = _lane_iota(x.shape)
    keep = (lane < HEAD_DIM) if half == 0 else (lane >= HEAD_DIM)
    return jnp.where(keep, x, jnp.zeros_like(x))


def _move_head(slab, src_half, dst_half):
    if src_half != dst_half:
        slab = pltpu.roll(slab, HEAD_DIM, 1)
    return _keep_half(slab, dst_half)


def _store_transposed(o_ref, res):
    for c in range(res.shape[1] // LANES):
        cols = slice(c * LANES, (c + 1) * LANES)
        o_ref[0, 0, cols, :] = res[:, cols].T.astype(o_ref.dtype)


def _proj_kernel(x_ref, *refs, transposed):
    n_out = len(transposed)
    w_refs, o_refs = refs[:n_out], refs[n_out:]
    xb = x_ref[...].astype(BF16)
    for w_ref, o_ref, t in zip(w_refs, o_refs, transposed):
        res = _dot(xb, w_ref[...])
        if t:
            _store_transposed(o_ref, res)
        else:
            o_ref[...] = res.astype(o_ref.dtype)


def _transposed_out(b, seq, width, tm):
    per_seq = seq // tm
    spec = pl.BlockSpec((1, 1, width, tm), lambda i: (i // per_seq, i % per_seq, 0, 0))
    return spec, jax.ShapeDtypeStruct((b, per_seq, width, tm), BF16)


def _project(x, weights, out_dtypes, transposed, b, seq):
    m, k = x.shape
    tm = min(KV_TILE, seq)
    specs, shapes = [], []
    for w, dt, t in zip(weights, out_dtypes, transposed):
        if t:
            spec, shape = _transposed_out(b, seq, w.shape[1], tm)
        else:
            spec, shape = pl.BlockSpec((tm, w.shape[1]), lambda i: (i, 0)), jax.ShapeDtypeStruct((m, w.shape[1]), dt)
        specs.append(spec)
        shapes.append(shape)
    return pl.pallas_call(
        functools.partial(_proj_kernel, transposed=tuple(transposed)),
        grid=(m // tm,),
        in_specs=[pl.BlockSpec((tm, k), lambda i: (i, 0))]
        + [pl.BlockSpec(w.shape, lambda i: (0, 0)) for w in weights],
        out_specs=specs,
        out_shape=shapes,
        compiler_params=_params("parallel"),
        name="project",
    )(x, *weights)


def _rope_slab(slab, c, s1, s2):
    half = MLA_ROPE // 2
    up = pltpu.roll(slab, half, 1)
    down = pltpu.roll(slab, LANES - half, 1)
    return slab * c + down * s1 + up * s2


def _mla_prep_kernel(ql_ref, kvl_ref, kpe_ref, qg_ref, kvg_ref, wq_ref, wk_ref, wv_ref,
                     c_ref, s1_ref, s2_ref, q_ref, k_ref, vt_ref, *, q_scale):
    c, s1, s2 = c_ref[...], s1_ref[...], s2_ref[...]
    qn = _rms_norm(ql_ref[...], qg_ref[...], RMS_EPS).astype(BF16)
    kvn = _rms_norm(kvl_ref[...], kvg_ref[...], RMS_EPS).astype(BF16)
    q = _dot(qn, wq_ref[...])
    k = _dot(kvn, wk_ref[...])
    _store_transposed(vt_ref, _dot(kvn, wv_ref[...]))
    kpe = _rope_slab(kpe_ref[...], c, s1, s2)
    for h in range(MLA_HEADS):
        sl = slice(h * LANES, (h + 1) * LANES)
        q_ref[:, sl] = (_rope_slab(q[:, sl], c, s1, s2) * q_scale).astype(q_ref.dtype)
        k_ref[:, sl] = (k[:, sl] + kpe).astype(k_ref.dtype)


def _mla_prep(slab_a, q_gain, kv_gain, wq, wk, wv, rope_c, rope_s1, rope_s2, b, seq):
    m = slab_a.shape[0]
    tm = min(KV_TILE, seq)
    per_seq = seq // tm
    rank = q_gain.shape[1]
    row = lambda j: (lambda i: (i, j))
    tab = lambda i: (i % per_seq, 0)
    const = lambda i: (0, 0)
    hw = MLA_HEADS * LANES
    vt_spec, vt_shape = _transposed_out(b, seq, wv.shape[1], tm)
    return pl.pallas_call(
        functools.partial(_mla_prep_kernel, q_scale=float((MLA_NOPE + MLA_ROPE) ** -0.5 * LOG2E)),
        grid=(m // tm,),
        in_specs=[pl.BlockSpec((tm, rank), row(0)), pl.BlockSpec((tm, rank), row(1)),
                  pl.BlockSpec((tm, LANES), row(2 * rank // LANES)),
                  pl.BlockSpec((1, rank), const), pl.BlockSpec((1, rank), const),
                  pl.BlockSpec(wq.shape, const), pl.BlockSpec(wk.shape, const), pl.BlockSpec(wv.shape, const),
                  pl.BlockSpec((tm, LANES), tab), pl.BlockSpec((tm, LANES), tab), pl.BlockSpec((tm, LANES), tab)],
        out_specs=[pl.BlockSpec((tm, hw), row(0)), pl.BlockSpec((tm, hw), row(0)), vt_spec],
        out_shape=[jax.ShapeDtypeStruct((m, hw), BF16), jax.ShapeDtypeStruct((m, hw), BF16), vt_shape],
        compiler_params=_params("parallel"),
        name="mla_prep",
    )(slab_a, slab_a, slab_a, q_gain, kv_gain, wq, wk, wv, rope_c, rope_s1, rope_s2)


def _flash_scratch(n_streams, v_rows, tq, tk):
    scores = pltpu.VMEM((n_streams, tk, tq), F32)
    stat = pltpu.VMEM((n_streams, 1, tq), F32)
    probs = pltpu.VMEM((n_streams, tk, tq), BF16)
    slot = [scores, probs, stat]
    return slot + slot + [stat, pltpu.VMEM((n_streams, v_rows + BF16_ROWS, tq), F32)]


def _chunk_rows(c):
    return slice(c * KEY_CHUNK, (c + 1) * KEY_CHUNK)


def _fold_rows(x):
    return x.reshape(x.shape[0] // SUBLANES, SUBLANES, x.shape[1])


def _flash_transposed(diagonal_tile, n_streams, q_start, tq, tk, key_operand, query_operand, values, offset, scratch,
                      rest_tile, rest_count):
    slot_a, slot_b, (m_ref, acc_ref) = scratch[0:3], scratch[3:6], scratch[6:]
    n_chunks = tk // KEY_CHUNK
    ones_rows = jnp.ones((BF16_ROWS, tk), BF16)
    for i in range(n_streams):
        m_ref[i] = jnp.full((1, tq), NEG_INF, F32)
        acc_ref[i] = jnp.zeros(acc_ref.shape[1:], F32)

    def column_max(s_ref, i):
        part = jnp.full((SUBLANES, tq), NEG_INF, F32)
        for c in range(n_chunks):
            part = jnp.maximum(part, jnp.max(_fold_rows(s_ref[i, _chunk_rows(c), :]), axis=0))
        return jnp.max(part, axis=0, keepdims=True)

    def stage1(j, slot, i):
        slot[0][i] = _dot(key_operand(j, i), query_operand(j, i))

    def stage2(j, slot, i, causal):
        s_ref, p_ref, alpha_ref = slot
        if causal is not None:
            s_ref[i] = jnp.where(causal, s_ref[i], NEG_INF)
        mx = column_max(s_ref, i)
        off = offset(j, i)
        m_prev = m_ref[i]
        if off is None:
            m_next = jnp.maximum(m_prev, mx)
            shift = m_next
        else:
            m_next = jnp.maximum(m_prev, mx + off)
            shift = m_next - off
        alpha = jnp.exp2(m_prev - m_next)
        for c in range(n_chunks):
            p_ref[i, _chunk_rows(c), :] = jnp.exp2(s_ref[i, _chunk_rows(c), :] - shift).astype(BF16)
        m_ref[i] = m_next
        alpha_ref[i] = alpha

    def step(accumulate=None, produce=(), exponentiate=None, causal=None):
        products = []
        for i in range(n_streams + 1):
            if i < n_streams:
                if accumulate is not None:
                    tile, slot = accumulate
                    products.append(_dot(jnp.concatenate([values(tile, i), ones_rows], axis=0), slot[1][i]))
                for tile, slot in produce:
                    stage1(tile, slot, i)
                if exponentiate is not None:
                    stage2(*exponentiate, i, causal)
            if accumulate is not None and i > 0:
                acc_ref[i - 1] = accumulate[1][2][i - 1] * acc_ref[i - 1] + products[i - 1]

    key_minus_query = (lax.broadcasted_iota(jnp.int32, (tk, tq), 0) - lax.broadcasted_iota(jnp.int32, (tk, tq), 1))
    step(produce=((diagonal_tile, slot_a), (rest_tile(0), slot_b)), exponentiate=(diagonal_tile, slot_a),
         causal=key_minus_query <= q_start - diagonal_tile * tk)
    last = rest_count(m_ref)

    def tile_at(position):
        return rest_tile(jnp.clip(position, 1, jnp.maximum(last, 1)) - 1)

    def diag_or_rest(position):
        return jnp.where(position == 0, diagonal_tile, tile_at(position))

    def pair(t, carry):
        p1 = 2 * t + 1
        step((diag_or_rest(p1 - 1), slot_a), ((tile_at(p1 + 1), slot_a),), (tile_at(p1), slot_b))
        step((tile_at(p1), slot_b), ((tile_at(p1 + 2), slot_b),), (tile_at(p1 + 1), slot_a))
        return carry

    lax.fori_loop(0, last // 2, pair, 0)

    @pl.when(last % 2 == 1)
    def _():
        step(accumulate=(diag_or_rest(last - 1), slot_a), exponentiate=(tile_at(last), slot_b))
        step(accumulate=(tile_at(last), slot_b))

    @pl.when(last % 2 == 0)
    def _():
        step(accumulate=(diag_or_rest(last), slot_a))


def _flash_result(scratch, i):
    acc = scratch[-1][i]
    v_rows = acc.shape[0] - BF16_ROWS
    return acc[:v_rows] / acc[v_rows:v_rows + 1]


def _kv_rows(j, tk):
    return pl.ds(pl.multiple_of(j * tk, tk), tk)


def _transposed_bf16(x):
    return x.astype(F32).T.astype(BF16)


def _alibi_rows(coef, tq):
    c = jnp.zeros((1, tq), F32) + coef
    hi = c.astype(BF16).astype(F32)
    rest = c - hi
    mid = rest.astype(BF16).astype(F32)
    lo = rest - mid
    zero = jnp.zeros((1, tq), F32)
    return jnp.concatenate([POS_SPLIT * hi, POS_SPLIT * mid, POS_SPLIT * lo, hi, mid, lo, zero, zero], axis=0)


def _augmented_query(q_t, feature_rows):
    tq = q_t.shape[1]
    pad = jnp.zeros((MXU_DEPTH - LANES - FEATURE_ROWS, tq), BF16)
    return jnp.concatenate([q_t, feature_rows.astype(BF16), pad], axis=0)


MLA_STEP_HEADS = 4


def _mla_attn_kernel(q_ref, k_ref, vt_ref, o_ref, *scratch, tq, tk):
    q_start = pl.program_id(2) * tq
    n = MLA_STEP_HEADS
    queries = [_transposed_bf16(q_ref[0, :, hh * LANES:(hh + 1) * LANES]) for hh in range(n)]
    n_full = q_start // tk
    _flash_transposed(
        n_full, n, q_start, tq, tk,
        lambda j, i: k_ref[0, _kv_rows(j, tk), i * LANES:(i + 1) * LANES],
        lambda j, i: queries[i],
        lambda j, i: vt_ref[0, j, i * HEAD_DIM:(i + 1) * HEAD_DIM, :],
        lambda j, i: None, scratch,
        rest_tile=lambda k: k, rest_count=lambda m_ref: n_full)
    for pair in range(n // 2):
        o_t = jnp.concatenate([_flash_result(scratch, 2 * pair), _flash_result(scratch, 2 * pair + 1)], axis=0)
        o_ref[0, :, pair * LANES:(pair + 1) * LANES] = o_t.T.astype(o_ref.dtype)


def _mla_attn(q, k, vt):
    b, s, _ = q.shape
    tq, tk = min(FLASH_Q_TILE, s), min(KV_TILE, s)
    n = MLA_STEP_HEADS
    groups = MLA_HEADS // n
    return pl.pallas_call(
        functools.partial(_mla_attn_kernel, tq=tq, tk=tk),
        grid=(b, groups, s // tq),
        in_specs=[pl.BlockSpec((1, tq, n * LANES), lambda bi, p, i: (bi, i, p)),
                  pl.BlockSpec((1, s, n * LANES), lambda bi, p, i: (bi, 0, p)),
                  pl.BlockSpec((1, s // tk, n * HEAD_DIM, tk), lambda bi, p, i: (bi, 0, p, 0))],
        out_specs=pl.BlockSpec((1, tq, n * HEAD_DIM), lambda bi, p, i: (bi, i, p)),
        out_shape=jax.ShapeDtypeStruct((b, s, MLA_HEADS * HEAD_DIM), BF16),
        scratch_shapes=_flash_scratch(n, HEAD_DIM, tq, tk),
        compiler_params=_params("parallel", "parallel", "arbitrary"),
        name="mla_attn",
    )(q, k, vt)


def _gelu_tanh(x):
    return 0.5 * x * (1.0 + jnp.tanh(math.sqrt(2.0 / math.pi) * (x + 0.044715 * (x * x * x))))


def _compress_kernel(x_ref, pos_ref, w1a_ref, w1b_ref, w2_ref, k_ref, vt_ref, *, n_real):
    x = x_ref[0]
    n = x.shape[0]
    first = _dot(x, w1a_ref[...])
    second = _dot(x, w1b_ref[...])
    pos_hi, pos_lo = _split_bf16(pos_ref[...])
    bias = (_dot(pos_hi[:8], w1a_ref[...]) + _dot(pos_lo[:8], w1a_ref[...])
            + _dot(pos_hi[8:], w1b_ref[...]) + _dot(pos_lo[8:], w1b_ref[...]))[:1]
    pre = first + pltpu.roll(second, n - 1, 0) + bias
    out = _dot(_gelu_tanh(pre).astype(BF16), w2_ref[...])
    real = lax.broadcasted_iota(jnp.int32, out.shape, 0) < n_real
    out = jnp.where(real, out, 0.0)
    half = out.shape[1] // 2
    k_ref[0] = out[:, :half].astype(k_ref.dtype)
    vt_ref[0] = out[:, half:].T.astype(vt_ref.dtype)


def _compress(x_chunks, pos_exp, w1a, w1b, w2, n_real):
    b, n, width = x_chunks.shape
    half = w2.shape[1] // 2
    const = lambda bi: (0, 0)
    return pl.pallas_call(
        functools.partial(_compress_kernel, n_real=n_real),
        grid=(b,),
        in_specs=[pl.BlockSpec((1, n, width), lambda bi: (bi, 0, 0)),
                  pl.BlockSpec(pos_exp.shape, const), pl.BlockSpec(w1a.shape, const),
                  pl.BlockSpec(w1b.shape, const), pl.BlockSpec(w2.shape, const)],
        out_specs=[pl.BlockSpec((1, n, half), lambda bi: (bi, 0, 0)), pl.BlockSpec((1, half, n), lambda bi: (bi, 0, 0))],
        out_shape=[jax.ShapeDtypeStruct((b, n, half), BF16), jax.ShapeDtypeStruct((b, half, n), BF16)],
        compiler_params=_params("parallel"),
        name="nsa_compress",
    )(x_chunks, pos_exp, w1a, w1b, w2)


def _nsa_head_slope(h):
    return float(2.0 ** (-8.0 * (h + 1) / NSA_HEADS))


def _nsa_queries(q_ref, g, scale):
    out = []
    for hg in range(NSA_HG):
        h = g * NSA_HG + hg
        slab = q_ref[0, :, (h // 2) * LANES:(h // 2 + 1) * LANES].astype(F32) * scale
        out.append(_move_head(slab, h % 2, g))
    return out


def _select_blocks(imp_t, q_pos):
    n_blocks = imp_t.shape[0]
    blk = lax.broadcasted_iota(jnp.int32, imp_t.shape, 0)
    cur = q_pos // SEL_LEN
    forced = jnp.where(blk == 0, 1.0, 0.0) + jnp.where(blk == cur, 1.0, 0.0) + jnp.where(blk == cur - 1, 1.0, 0.0)
    forced = jnp.minimum(forced, 1.0)
    val = jnp.where(blk <= cur, imp_t + FORCE_BONUS * forced, NEG_INF)
    chosen = jnp.zeros(imp_t.shape, F32)
    for _ in range(SEL_TOPK):
        top = jnp.max(val, axis=0, keepdims=True)
        first = jnp.min(jnp.where(val == top, blk, n_blocks), axis=0, keepdims=True)
        hit = blk == first
        chosen = jnp.where(hit, 1.0, chosen)
        val = jnp.where(hit, -jnp.inf, val)
    return jnp.where(chosen > 0.5, 0.0, NEG_INF)


def _masked_softmax_pass(s_ref, p_ref, tiles, tq):
    part = jnp.full((SUBLANES, tq), NEG_INF, F32)
    for t, rows, keep, off in tiles:
        for c in range(rows // KEY_CHUNK):
            s = jnp.where(keep(_chunk_rows(c)), s_ref[t, _chunk_rows(c), :], NEG_INF)
            s_ref[t, _chunk_rows(c), :] = s
            part = jnp.maximum(part, jnp.max(_fold_rows(s), axis=0) + off)
    m = jnp.max(part, axis=0, keepdims=True)
    total = jnp.zeros((SUBLANES, tq), F32)
    for t, rows, keep, off in tiles:
        shift = m - off
        for c in range(rows // KEY_CHUNK):
            p = jnp.exp2(s_ref[t, _chunk_rows(c), :] - shift)
            total = total + jnp.sum(_fold_rows(p), axis=0)
            p_ref[t, _chunk_rows(c), :] = p.astype(BF16)
    return jnp.sum(total, axis=0, keepdims=True), m > 0.5 * NEG_INF


def _nsa_cmp_win_kernel(q_ref, kc_ref, vct_ref, kw_ref, vwt_ref, feat_ref, ovt_ref, oc_ref, ow_ref, sel_ref, any_ref,
                        end_ref, rel_ref, sc_ref, sw_ref, pc_ref, pw_ref, *, tq, tk):
    qi = pl.program_id(1)
    q_start = qi * tq
    n_cmp = kc_ref.shape[1]
    end_ref[...] = (lax.broadcasted_iota(jnp.int32, (n_cmp, tq), 0) * CMP_STRIDE + (CMP_LEN - 1)
                    - lax.broadcasted_iota(jnp.int32, (n_cmp, tq), 1))
    rel_ref[...] = (lax.broadcasted_iota(jnp.int32, (tk, tq), 0) - lax.broadcasted_iota(jnp.int32, (tk, tq), 1))
    prev_tile = jnp.maximum(qi - 1, 0)
    prev_bound = jnp.where(qi >= 1, 0, tk)
    zeros = jnp.zeros((FEATURE_ROWS - SUBLANES, tq), F32)
    k_cmp = jnp.concatenate([kc_ref[0], feat_ref[0:n_cmp, :]], axis=1)
    k_win = [jnp.concatenate([kw_ref[0, _kv_rows(j, tk), :], feat_ref[...]], axis=1) for j in (prev_tile, qi)]
    q_pos = q_start + lax.broadcasted_iota(jnp.int32, (1, tq), 1)
    for g in range(NSA_GROUPS):
        queries = [_transposed_bf16(q) for q in _nsa_queries(q_ref, g, HEAD_DIM ** -0.5 * LOG2E)]
        group_rows = slice(g * HEAD_DIM, (g + 1) * HEAD_DIM)
        imp_t = jnp.zeros((LANES, tq), F32)
        out_c, out_w = [], []
        for hg in range(NSA_HG):
            coef = _nsa_head_slope(g * NSA_HG + hg) * LOG2E
            buf = hg % 2
            sc, sw, pc, pw = sc_ref.at[buf], sw_ref.at[buf], pc_ref.at[buf], pw_ref.at[buf]
            cmp_query = _augmented_query(queries[hg], jnp.concatenate([_alibi_rows(CMP_STRIDE * coef, tq), zeros], 0))
            win_query = _augmented_query(queries[hg], jnp.concatenate([_alibi_rows(coef, tq), zeros], 0))
            sc[0] = _dot(k_cmp, cmp_query)
            sw[0] = _dot(k_win[0], win_query)
            sw[1] = _dot(k_win[1], win_query)
            norm, has_any = _masked_softmax_pass(
                sc, pc, [(0, n_cmp, lambda r: end_ref[r, :] <= q_start, 0.0)], tq)
            inv = jnp.where(has_any, 1.0 / norm, 0.0)
            out_c.append(_dot(vct_ref[0, group_rows, :], pc[0]) * inv)
            imp_t = imp_t + _dot(ovt_ref[...], pc[0]) * inv
            norm, _ = _masked_softmax_pass(
                sw, pw, [(0, tk, lambda r: rel_ref[r, :] > prev_bound, -coef * tk),
                         (1, tk, lambda r: rel_ref[r, :] <= 0, 0.0)], tq)
            acc = (_dot(vwt_ref[0, prev_tile, group_rows, :], pw[0]) + _dot(vwt_ref[0, qi, group_rows, :], pw[1]))
            out_w.append(acc / norm)
        for pair in range(NSA_HG // 2):
            cols = slice((g * 2 + pair) * LANES, (g * 2 + pair + 1) * LANES)
            oc_ref[0, :, cols] = jnp.concatenate(out_c[2 * pair:2 * pair + 2], axis=0).T
            ow_ref[0, :, cols] = jnp.concatenate(out_w[2 * pair:2 * pair + 2], axis=0).T
        bias_t = _select_blocks(imp_t, q_pos)
        sel_ref[0, g * LANES:(g + 1) * LANES, :] = bias_t
        any_ref[0, 0, g * LANES:(g + 1) * LANES, :] = jnp.broadcast_to(
            jnp.max(bias_t, axis=1, keepdims=True), (LANES, LANES))


def _nsa_cmp_win(slab_b, cmp_k, cmp_vt, vt_nsa, key_features, overlap_t, seq):
    b = slab_b.shape[0]
    tq, tk = min(FLASH_Q_TILE, seq), min(KV_TILE, seq)
    assert tq == tk == WINDOW, "the window branch is written for one previous and one diagonal key tile"
    n_cmp = cmp_k.shape[1]
    qw = NSA_HEADS * HEAD_DIM
    base = qw // LANES
    tile = lambda bi, i: (bi, i, 0)
    return pl.pallas_call(
        functools.partial(_nsa_cmp_win_kernel, tq=tq, tk=tk),
        grid=(b, seq // tq),
        in_specs=[pl.BlockSpec((1, tq, qw), tile),
                  pl.BlockSpec((1, n_cmp, LANES), lambda bi, i: (bi, 0, 0)),
                  pl.BlockSpec((1, LANES, n_cmp), lambda bi, i: (bi, 0, 0)),
                  pl.BlockSpec((1, seq, LANES), lambda bi, i: (bi, 0, base + 1)),
                  pl.BlockSpec((1, seq // tk, LANES, tk), lambda bi, i: (bi, 0, 1, 0)),
                  pl.BlockSpec(key_features.shape, lambda bi, i: (0, 0)),
                  pl.BlockSpec(overlap_t.shape, lambda bi, i: (0, 0))],
        out_specs=[pl.BlockSpec((1, tq, qw), tile), pl.BlockSpec((1, tq, qw), tile),
                   pl.BlockSpec((1, NSA_GROUPS * LANES, tq), lambda bi, i: (bi, 0, i)),
                   pl.BlockSpec((1, 1, NSA_GROUPS * LANES, LANES), lambda bi, i: (bi, i, 0, 0))],
        out_shape=[jax.ShapeDtypeStruct((b, seq, qw), F32), jax.ShapeDtypeStruct((b, seq, qw), F32),
                   jax.ShapeDtypeStruct((b, NSA_GROUPS * LANES, seq), F32),
                   jax.ShapeDtypeStruct((b, seq // tq, NSA_GROUPS * LANES, LANES), F32)],
        scratch_shapes=[pltpu.VMEM((n_cmp, tq), jnp.int32), pltpu.VMEM((tk, tq), jnp.int32),
                        pltpu.VMEM((2, 1, n_cmp, tq), F32), pltpu.VMEM((2, 2, tk, tq), F32),
                        pltpu.VMEM((2, 1, n_cmp, tq), BF16), pltpu.VMEM((2, 2, tk, tq), BF16)],
        compiler_params=_params("parallel", "arbitrary"),
        name="nsa_cmp_win",
    )(slab_b, cmp_k, cmp_vt, slab_b, vt_nsa, key_features, overlap_t)


def _nsa_sel_kernel(tiles_ref, counts_ref, q_ref, k_ref, vt_ref, feat_ref, sel_ref, oc_ref, ow_ref, gate_ref, gx_ref,
                    o_ref, *scratch, tq, tk, max_tiles):
    qi = pl.program_id(1)
    q_start = qi * tq
    diagonal_tile = q_start // tk
    out_slabs = []
    for g in range(NSA_GROUPS):
        queries = [_transposed_bf16(q) for q in _nsa_queries(q_ref, g, HEAD_DIM ** -0.5 * LOG2E)]
        coefs = [_nsa_head_slope(g * NSA_HG + hg) * LOG2E for hg in range(NSA_HG)]
        alibi = [_alibi_rows(c, tq) for c in coefs]
        entry = (pl.program_id(0) * pl.num_programs(1) + qi) * NSA_GROUPS + g
        n_active = counts_ref[entry]

        def listed_tile(k, entry=entry):
            return tiles_ref[entry * max_tiles + jnp.minimum(k, max_tiles - 1)]

        def key_operand(j, i):
            return jnp.concatenate([k_ref[0, _kv_rows(j, tk), :], feat_ref[...]], axis=1)

        def query_operand(j, i, g=g, queries=queries, alibi=alibi):
            first_block = pl.multiple_of(g * LANES + j * BLOCKS_PER_TILE, BLOCKS_PER_TILE)
            blocks = sel_ref[0, pl.ds(first_block, BLOCKS_PER_TILE), :]
            return _augmented_query(queries[i], jnp.concatenate([alibi[i], blocks], axis=0))

        _flash_transposed(
            diagonal_tile, NSA_HG, q_start, tq, tk, key_operand, query_operand,
            lambda j, i, g=g: vt_ref[0, j, g * HEAD_DIM:(g + 1) * HEAD_DIM, :],
            lambda j, i, coefs=coefs: coefs[i] * (j * tk - q_start).astype(F32), scratch,
            rest_tile=listed_tile, rest_count=lambda m_ref, n_active=n_active: n_active)
        heads = [_flash_result(scratch, hg) for hg in range(NSA_HG)]
        for pair in range(NSA_HG // 2):
            out_slabs.append(jnp.concatenate(heads[2 * pair:2 * pair + 2], axis=0).T)
    gates = jax.nn.sigmoid(gate_ref[...])
    g_hi, g_lo = _split_bf16(gates)
    width = NSA_HEADS * HEAD_DIM
    for i, o_sel in enumerate(out_slabs):
        mixed = None
        for branch, o_branch in enumerate((oc_ref[0, :, i * LANES:(i + 1) * LANES], o_sel,
                                           ow_ref[0, :, i * LANES:(i + 1) * LANES])):
            gx = gx_ref[:, branch * width + i * LANES:branch * width + (i + 1) * LANES]
            term = (_dot(g_hi, gx) + _dot(g_lo, gx)) * o_branch
            mixed = term if mixed is None else mixed + term
        o_ref[0, :, i * LANES:(i + 1) * LANES] = mixed.astype(o_ref.dtype)


def _active_key_tiles(block_any, tq, tk):
    b, n_q = block_any.shape[:2]
    max_tiles = LANES // BLOCKS_PER_TILE
    hit = block_any[..., 0].reshape(b, n_q, NSA_GROUPS, max_tiles, BLOCKS_PER_TILE).max(axis=-1) > 0.5 * NEG_INF
    before_diagonal = jnp.arange(max_tiles)[None, :] < (jnp.arange(n_q) * tq // tk)[:, None]
    hit = hit & before_diagonal[None, :, None, :]
    order = jnp.argsort(jnp.logical_not(hit), axis=-1, stable=True)
    return order.astype(jnp.int32).reshape(-1), hit.sum(axis=-1).astype(jnp.int32).reshape(-1), max_tiles


def _nsa_sel(slab_b, vt, key_features, sel_bias_t, block_any, o_cmp, o_win, slab_a, gate_expand, seq, gate_col_block):
    b = slab_b.shape[0]
    tq, tk = min(FLASH_Q_TILE, seq), min(KV_TILE, seq)
    qw = NSA_HEADS * HEAD_DIM
    base = qw // LANES
    per_seq = seq // tq
    tiles, counts, max_tiles = _active_key_tiles(block_any, tq, tk)
    tile = lambda bi, i, *_: (bi, i, 0)
    grid_spec = pltpu.PrefetchScalarGridSpec(
        num_scalar_prefetch=2,
        grid=(b, seq // tq),
        in_specs=[pl.BlockSpec((1, tq, qw), tile),
                  pl.BlockSpec((1, seq, LANES), lambda bi, i, *_: (bi, 0, base)),
                  pl.BlockSpec((1, seq // tk, LANES, tk), lambda bi, i, *_: (bi, 0, 0, 0)),
                  pl.BlockSpec(key_features.shape, lambda bi, i, *_: (0, 0)),
                  pl.BlockSpec((1, NSA_GROUPS * LANES, tq), lambda bi, i, *_: (bi, 0, i)),
                  pl.BlockSpec((1, tq, qw), tile), pl.BlockSpec((1, tq, qw), tile),
                  pl.BlockSpec((tq, LANES), lambda bi, i, *_: (bi * per_seq + i, gate_col_block)),
                  pl.BlockSpec(gate_expand.shape, lambda bi, i, *_: (0, 0))],
        out_specs=pl.BlockSpec((1, tq, qw), tile),
        scratch_shapes=_flash_scratch(NSA_HG, HEAD_DIM, tq, tk))
    return pl.pallas_call(
        functools.partial(_nsa_sel_kernel, tq=tq, tk=tk, max_tiles=max_tiles),
        grid_spec=grid_spec,
        out_shape=jax.ShapeDtypeStruct((b, seq, qw), BF16),
        compiler_params=_params("parallel", "arbitrary"),
        name="nsa_sel",
    )(tiles, counts, slab_b, slab_b, vt, key_features, sel_bias_t, o_cmp, o_win, slab_a, gate_expand)


DIFF_STEP_HEADS = 2


SKIP_GAP = 180.0
NORM_SLACK = 1.01


def _diff_attn_kernel(slope_ref, lam_ref, q_ref, k_ref, vt_ref, feat_ref, g_ref, o_ref, knorm_ref, *scratch,
                      tq, tk, lam_init):
    first_head = pl.program_id(1) * DIFF_STEP_HEADS
    q_start = pl.program_id(2) * tq
    n_full = q_start // tk
    n_tiles = k_ref.shape[1] // tk

    @pl.when(pl.program_id(2) == 0)
    def _():
        for hh in range(DIFF_STEP_HEADS):
            for j in range(n_tiles):
                k = k_ref[0, j * tk:(j + 1) * tk, hh * LANES:(hh + 1) * LANES].astype(F32)
                knorm_ref[hh * n_tiles + j] = jnp.sqrt(jnp.max(jnp.sum(k * k, axis=1, keepdims=True)))

    zeros = jnp.zeros((FEATURE_ROWS - SUBLANES, tq), F32)
    coefs, queries, q_norms = [], [], []
    for hh in range(DIFF_STEP_HEADS):
        coef = slope_ref[first_head + hh] * LOG2E
        q = q_ref[0, :, hh * LANES:(hh + 1) * LANES].astype(F32) * (HEAD_DIM ** -0.5 * LOG2E)
        features = jnp.concatenate([_alibi_rows(coef, tq), zeros], axis=0)
        coefs.append(coef)
        for half in range(2):
            q_half = _keep_half(q, half)
            queries.append(_augmented_query(_transposed_bf16(q_half), features))
            q_norms.append(NORM_SLACK * jnp.sqrt(jnp.max(jnp.sum(q_half * q_half, axis=1, keepdims=True))))

    def head_lanes(i):
        return slice((i // 2) * LANES, (i // 2 + 1) * LANES)

    def offset(j, i):
        return coefs[i // 2] * (j * tk - q_start).astype(F32)

    nearest = jnp.maximum(n_full - 1, 0)
    first_needed = []

    def rest_count(m_ref):
        floors = [jnp.min(m_ref[i]) - SKIP_GAP for i in range(2 * DIFF_STEP_HEADS)]

        def body(j, first):
            needed = jnp.bool_(False)
            for i in range(2 * DIFF_STEP_HEADS):
                bound = q_norms[i] * knorm_ref[(i // 2) * n_tiles + j] + coefs[i // 2] * (tk - 1) + offset(j, i)
                needed = jnp.logical_or(needed, bound >= floors[i])
            return jnp.where(needed, jnp.minimum(first, j), first)

        first_needed.append(lax.fori_loop(0, nearest, body, nearest))
        return jnp.minimum(n_full, 1) + nearest - first_needed[0]

    def rest_tile(k):
        if isinstance(k, int):
            return nearest
        return jnp.where(k == 0, nearest, first_needed[0] + k - 1)

    _flash_transposed(
        n_full, 2 * DIFF_STEP_HEADS, q_start, tq, tk,
        lambda j, i: jnp.concatenate([k_ref[0, _kv_rows(j, tk), head_lanes(i)], feat_ref[...]], axis=1),
        lambda j, i: queries[i],
        lambda j, i: vt_ref[0, j, head_lanes(i), :],
        offset, scratch, rest_tile, rest_count)
    lam_vec = lam_ref[...]
    lam = (jnp.exp(jnp.sum(lam_vec[0:1] * lam_vec[1:2], axis=1, keepdims=True))
           - jnp.exp(jnp.sum(lam_vec[2:3] * lam_vec[3:4], axis=1, keepdims=True)) + lam_init)
    for hh in range(DIFF_STEP_HEADS):
        o = (_flash_result(scratch, 2 * hh) - lam * _flash_result(scratch, 2 * hh + 1)).T
        o_ref[0, :, hh * LANES:(hh + 1) * LANES] = (
            _rms_norm(o, g_ref[...], RMS_EPS) * (1.0 - lam_init)).astype(o_ref.dtype)


def _diff_attn(qk, vt, key_features, slopes, lam_vecs, subln_g, lam_init):
    b, s, _ = qk.shape
    tq, tk = min(FLASH_Q_TILE, s), min(KV_TILE, s)
    n = DIFF_STEP_HEADS
    groups = DIFF_HEADS // n
    smem = pl.BlockSpec(memory_space=pltpu.SMEM)
    return pl.pallas_call(
        functools.partial(_diff_attn_kernel, tq=tq, tk=tk, lam_init=lam_init),
        grid=(b, groups, s // tq),
        in_specs=[smem, pl.BlockSpec(lam_vecs.shape, lambda bi, h, i: (0, 0)),
                  pl.BlockSpec((1, tq, n * LANES), lambda bi, h, i: (bi, i, h)),
                  pl.BlockSpec((1, s, n * LANES), lambda bi, h, i: (bi, 0, groups + h)),
                  pl.BlockSpec((1, s // tk, n * LANES, tk), lambda bi, h, i: (bi, 0, h, 0)),
                  pl.BlockSpec(key_features.shape, lambda bi, h, i: (0, 0)),
                  pl.BlockSpec((1, LANES), lambda bi, h, i: (0, 0))],
        out_specs=pl.BlockSpec((1, tq, n * LANES), lambda bi, h, i: (bi, i, h)),
        out_shape=jax.ShapeDtypeStruct((b, s, DIFF_HEADS * LANES), BF16),
        scratch_shapes=[pltpu.SMEM((n * (s // tk),), F32)] + _flash_scratch(2 * n, LANES, tq, tk),
        compiler_params=_params("parallel", "parallel", "arbitrary"),
        name="diff_attn",
    )(slopes, lam_vecs, qk, qk, vt, key_features, subln_g)


def _out_ln_kernel(*refs, n_in):
    a_refs, w_refs = refs[:n_in], refs[n_in:2 * n_in]
    x_ref, g_ref, b_ref, o_ref = refs[2 * n_in:]
    y = None
    for a_ref, w_ref in zip(a_refs, w_refs):
        t = _dot(a_ref[...], w_ref[...])
        y = t if y is None else y + t
    o_ref[...] = _layer_norm(DN_ALPHA * x_ref[...] + y, g_ref[...], b_ref[...])


def _out_ln(acts, weights, x, g, b):
    m, d = x.shape
    tm = min(ROW_TILE, m)
    row = lambda i: (i, 0)
    const = lambda i: (0, 0)
    return pl.pallas_call(
        functools.partial(_out_ln_kernel, n_in=len(acts)),
        grid=(m // tm,),
        in_specs=[pl.BlockSpec((tm, a.shape[1]), row) for a in acts]
        + [pl.BlockSpec(w.shape, const) for w in weights]
        + [pl.BlockSpec((tm, d), row), pl.BlockSpec((1, d), const), pl.BlockSpec((1, d), const)],
        out_specs=pl.BlockSpec((tm, d), row),
        out_shape=jax.ShapeDtypeStruct((m, d), F32),
        compiler_params=_params("parallel"),
        name="out_proj_ln",
    )(*acts, *weights, x, g, b)


def _mlp_kernel(x_ref, wu_ref, wd_ref, g_ref, b_ref, o_ref, xb_ref, acc_ref):
    f = pl.program_id(1)

    @pl.when(f == 0)
    def _():
        xb_ref[...] = x_ref[...].astype(BF16)
        acc_ref[...] = jnp.zeros(acc_ref.shape, F32)

    hidden = jnp.maximum(_dot(xb_ref[...], wu_ref[...]), 0.0)
    acc_ref[...] += _dot((hidden * hidden).astype(BF16), wd_ref[...])

    @pl.when(f == pl.num_programs(1) - 1)
    def _():
        o_ref[...] = _layer_norm(DN_ALPHA * x_ref[...] + acc_ref[...], g_ref[...], b_ref[...])


def _mlp(x, w_up, w_down, g, b):
    m, d = x.shape
    ff = w_up.shape[1]
    tm, tf = min(ROW_TILE, m), min(FF_TILE, ff)
    return pl.pallas_call(
        _mlp_kernel,
        grid=(m // tm, ff // tf),
        in_specs=[pl.BlockSpec((tm, d), lambda i, f: (i, 0)),
                  pl.BlockSpec((d, tf), lambda i, f: (0, f)),
                  pl.BlockSpec((tf, d), lambda i, f: (f, 0)),
                  pl.BlockSpec((1, d), lambda i, f: (0, 0)), pl.BlockSpec((1, d), lambda i, f: (0, 0))],
        out_specs=pl.BlockSpec((tm, d), lambda i, f: (i, 0)),
        out_shape=jax.ShapeDtypeStruct((m, d), F32),
        scratch_shapes=[pltpu.VMEM((tm, d), BF16), pltpu.VMEM((tm, d), F32)],
        compiler_params=_params("parallel", "arbitrary"),
        name="mlp_ln",
    )(x, w_up, w_down, g, b)


def _pad_cols(w, width):
    return jnp.pad(w, ((0, 0), (0, width - w.shape[1])))


def _layer0_weights(w_in, w_uq, w_ukv, d_model):
    rank = d_model // 4
    kvw = NSA_GROUPS * HEAD_DIM
    o = np.cumsum([0, rank, rank, MLA_ROPE, NSA_HEADS * HEAD_DIM] + [kvw] * 6 + [3 * NSA_HEADS])
    seg = lambda i: w_in[:, o[i]:o[i + 1]]
    zeros = lambda n: jnp.zeros((w_in.shape[0], n), w_in.dtype)
    rope_slab = jnp.concatenate([zeros(MLA_NOPE), seg(2), zeros(LANES - MLA_NOPE - MLA_ROPE)], axis=1)
    w_a = jnp.concatenate([seg(0), seg(1), rope_slab, _pad_cols(seg(10), LANES)], axis=1)
    w_b = jnp.concatenate([seg(3), seg(6), seg(8)], axis=1)
    w_c = jnp.concatenate([seg(4), seg(5)], axis=1)
    w_vs = jnp.concatenate([seg(7), seg(9)], axis=1)
    wq = jnp.pad(w_uq.reshape(rank, MLA_HEADS, MLA_NOPE + MLA_ROPE),
                 ((0, 0), (0, 0), (0, LANES - MLA_NOPE - MLA_ROPE))).reshape(rank, MLA_HEADS * LANES)
    ukv = w_ukv.reshape(rank, MLA_HEADS, MLA_NOPE + HEAD_DIM)
    wk = jnp.pad(ukv[:, :, :MLA_NOPE], ((0, 0), (0, 0), (0, LANES - MLA_NOPE))).reshape(rank, MLA_HEADS * LANES)
    wv = ukv[:, :, MLA_NOPE:].reshape(rank, MLA_HEADS * HEAD_DIM)
    return [w.astype(BF16) for w in (w_a, w_b, w_c, w_vs, wq, wk, wv)]


def _rope_tables(seq):
    inv = 1.0 / (ROPE_THETA ** (jnp.arange(0, MLA_ROPE, 2, dtype=F32) / MLA_ROPE))
    ang = jnp.arange(seq, dtype=F32)[:, None] * inv[None, :]
    cos, sin = jnp.cos(ang), jnp.sin(ang)
    half = MLA_ROPE // 2
    z = lambda n: jnp.zeros((seq, n), F32)
    tail = LANES - MLA_NOPE - MLA_ROPE
    c = jnp.concatenate([jnp.ones((seq, MLA_NOPE), F32), cos, cos, z(tail)], axis=1)
    s1 = jnp.concatenate([z(MLA_NOPE), -sin, z(half), z(tail)], axis=1)
    s2 = jnp.concatenate([z(MLA_NOPE), z(half), sin, z(tail)], axis=1)
    return c, s1, s2


def _compress_weights(pos_k, w1_k, w2_k, pos_v, w1_v, w2_v):
    eye = jnp.eye(2 * NSA_GROUPS, dtype=F32)
    halves = []
    for a in range(CMP_LEN // CMP_STRIDE):
        rows = slice(a * CMP_STRIDE * HEAD_DIM, (a + 1) * CMP_STRIDE * HEAD_DIM)
        wk = w1_k[rows].reshape(CMP_STRIDE, HEAD_DIM, HEAD_DIM)
        wv = w1_v[rows].reshape(CMP_STRIDE, HEAD_DIM, HEAD_DIM)
        per_slot = jnp.stack([wk, wk, wv, wv], axis=0)
        full = jnp.einsum('st,srdj->rsdtj', eye, per_slot)
        halves.append(full.reshape(CMP_STRIDE * 4 * HEAD_DIM, 4 * HEAD_DIM).astype(BF16))
    w2 = jnp.einsum('st,sdj->sdtj', eye, jnp.stack([w2_k, w2_k, w2_v, w2_v])).reshape(4 * HEAD_DIM, 4 * HEAD_DIM)
    pos = jnp.concatenate([pos_k, pos_k, pos_v, pos_v], axis=1)
    pos = pos.reshape(CMP_LEN // CMP_STRIDE, 1, CMP_STRIDE * 4 * HEAD_DIM)
    pos = jnp.broadcast_to(pos, (pos.shape[0], 8, pos.shape[2])).reshape(-1, pos.shape[2])
    return pos, halves[0], halves[1], w2.astype(BF16)


def _overlap_table(n_cmp_pad, n_cmp):
    c0 = np.arange(n_cmp_pad)[None, :] * CMP_STRIDE
    s0 = np.arange(LANES)[:, None] * SEL_LEN
    ov = np.maximum(np.minimum(c0 + CMP_LEN, s0 + SEL_LEN) - np.maximum(c0, s0), 0) / CMP_LEN
    ov = ov * (np.arange(n_cmp_pad)[None, :] < n_cmp)
    return jnp.asarray(ov, BF16)


def _key_feature_table(tk):
    c = np.arange(tk)
    table = np.zeros((tk, LANES), np.float32)
    table[:, 0:3] = (c // POS_SPLIT)[:, None]
    table[:, 3:6] = (c % POS_SPLIT)[:, None]
    table[c, BLOCK_LANE0 + c // SEL_LEN] = 1.0
    return jnp.asarray(table, BF16)


def _gate_expand_table():
    width = NSA_HEADS * HEAD_DIM
    table = np.zeros((LANES, 3 * width), np.float32)
    for h in range(NSA_HEADS):
        for branch in range(3):
            table[h * 3 + branch, branch * width + h * HEAD_DIM:branch * width + (h + 1) * HEAD_DIM] = 1.0
    return jnp.asarray(table, BF16)


def _alibi_slopes(n):
    return jnp.asarray(2.0 ** (-8.0 * np.arange(1, n + 1) / n), dtype=F32)


def _layer0_mixer(x2, b, s, w_in, q_norm, w_uq, kv_norm, w_ukv, pos_k, w1_k, w2_k, pos_v, w1_v, w2_v, w_out):
    d = x2.shape[1]
    rank = d // 4
    w_a, w_b, w_c, w_vs, wq, wk, wv = _layer0_weights(w_in, w_uq, w_ukv, d)
    slab_a, slab_b, slab_c, vt_nsa = _project(x2, [w_a, w_b, w_c, w_vs], [F32, BF16, BF16, BF16],
                                              [False, False, False, True], b, s)
    rope_c, rope_s1, rope_s2 = _rope_tables(s)
    q, k, vt = _mla_prep(slab_a, q_norm.reshape(1, rank), kv_norm.reshape(1, rank), wq, wk, wv,
                         rope_c, rope_s1, rope_s2, b, s)
    o_mla = _mla_attn(q.reshape(b, s, -1), k.reshape(b, s, -1), vt)
    n_chunks = s // CMP_STRIDE
    n_cmp = (s - CMP_LEN) // CMP_STRIDE + 1
    pos, w1a, w1b, w2 = _compress_weights(pos_k, w1_k, w2_k, pos_v, w1_v, w2_v)
    cmp_k, cmp_vt = _compress(slab_c.reshape(b, n_chunks, CMP_STRIDE * slab_c.shape[1]), pos, w1a, w1b, w2, n_cmp)
    slab_b3 = slab_b.reshape(b, s, -1)
    key_features = _key_feature_table(min(KV_TILE, s))
    o_cmp, o_win, sel_bias_t, block_any = _nsa_cmp_win(slab_b3, cmp_k, cmp_vt, vt_nsa, key_features,
                                                       _overlap_table(n_chunks, n_cmp), s)
    o_nsa = _nsa_sel(slab_b3, vt_nsa, key_features, sel_bias_t, block_any, o_cmp, o_win, slab_a,
                     _gate_expand_table(), s, (2 * rank + LANES) // LANES)
    half = o_mla.shape[-1]
    w_out_b = w_out.astype(BF16)
    return [o_mla.reshape(b * s, half), o_nsa.reshape(b * s, -1)], [w_out_b[:half], w_out_b[half:]]


def _layer1_mixer(x2, b, s, w_qkv, lam_q1, lam_k1, lam_q2, lam_k2, subln_g, w_o, layer_idx):
    d = x2.shape[1]
    w = w_qkv.astype(BF16)
    qk, vt = _project(x2, [w[:, :2 * d], w[:, 2 * d:]], [BF16, BF16], [False, True], b, s)
    lam_init = 0.8 - 0.6 * math.exp(-0.3 * layer_idx)
    lam_vecs = jnp.stack([lam_q1, lam_k1, lam_q2, lam_k2]).astype(F32)
    o = _diff_attn(qk.reshape(b, s, -1), vt, _key_feature_table(min(KV_TILE, s)), _alibi_slopes(DIFF_HEADS),
                   lam_vecs, subln_g.reshape(1, -1), lam_init)
    return [o.reshape(b * s, -1)], [w_o.astype(BF16)]


def kernel(x, l0_w_in, l0_mla_q_norm, l0_mla_w_uq, l0_mla_kv_norm, l0_mla_w_ukv, l0_nsa_cmp_pos_k, l0_nsa_cmp_w1_k, l0_nsa_cmp_w2_k, l0_nsa_cmp_pos_v, l0_nsa_cmp_w1_v, l0_nsa_cmp_w2_v, l0_w_out, l0_ln_mix_g, l0_ln_mix_b, l0_w_up, l0_w_down, l0_ln_ffn_g, l0_ln_ffn_b, l1_w_qkv, l1_lam_q1, l1_lam_k1, l1_lam_q2, l1_lam_k2, l1_subln_g, l1_w_o, l1_ln_mix_g, l1_ln_mix_b, l1_w_up, l1_w_down, l1_ln_ffn_g, l1_ln_ffn_b):
    b, s, d = x.shape
    x2 = x.reshape(b * s, d)
    vec = lambda p: p.reshape(1, d)
    acts, weights = _layer0_mixer(x2, b, s, l0_w_in, l0_mla_q_norm, l0_mla_w_uq, l0_mla_kv_norm, l0_mla_w_ukv,
                                  l0_nsa_cmp_pos_k, l0_nsa_cmp_w1_k, l0_nsa_cmp_w2_k,
                                  l0_nsa_cmp_pos_v, l0_nsa_cmp_w1_v, l0_nsa_cmp_w2_v, l0_w_out)
    x2 = _out_ln(acts, weights, x2, vec(l0_ln_mix_g), vec(l0_ln_mix_b))
    x2 = _mlp(x2, l0_w_up.astype(BF16), l0_w_down.astype(BF16), vec(l0_ln_ffn_g), vec(l0_ln_ffn_b))
    acts, weights = _layer1_mixer(x2, b, s, l1_w_qkv, l1_lam_q1, l1_lam_k1, l1_lam_q2, l1_lam_k2,
                                  l1_subln_g, l1_w_o, 1)
    x2 = _out_ln(acts, weights, x2, vec(l1_ln_mix_g), vec(l1_ln_mix_b))
    x2 = _mlp(x2, l1_w_up.astype(BF16), l1_w_down.astype(BF16), vec(l1_ln_ffn_g), vec(l1_ln_ffn_b))
    return x2.reshape(b, s, d)
```

```python
import functools
import math

import jax
import jax.numpy as jnp
import numpy as np
from jax import lax
from jax.experimental import pallas as pl
from jax.experimental.pallas import tpu as pltpu

F32 = jnp.float32
BF16 = jnp.bfloat16

LANES = 128
SUBLANES = 8
BF16_ROWS = 16
MXU_DEPTH = 256
HEAD_DIM = 64
FLASH_Q_TILE = 512
KV_TILE = 512
KEY_CHUNK = 32
ROW_TILE = 512
FF_TILE = 1024
VMEM_LIMIT = 56 * 1024 * 1024

NEG_INF = -1e30
LOG2E = math.log2(math.e)
LN_EPS = 1e-5
RMS_EPS = 1e-6
DEPTH = 2
DN_ALPHA = (2.0 * DEPTH) ** 0.25

MLA_HEADS = 8
MLA_NOPE = 64
MLA_ROPE = 32
ROPE_THETA = 10000.0
NSA_HEADS = 8
NSA_GROUPS = 2
NSA_HG = NSA_HEADS // NSA_GROUPS
CMP_LEN = 32
CMP_STRIDE = 16
SEL_LEN = 64
SEL_TOPK = 16
WINDOW = 512
FORCE_BONUS = 1e3
DIFF_HEADS = 8

POS_SPLIT = 16
FEATURE_ROWS = 16
BLOCK_LANE0 = 8
BLOCKS_PER_TILE = KV_TILE // SEL_LEN


def _params(*sem):
    return pltpu.CompilerParams(dimension_semantics=sem, vmem_limit_bytes=VMEM_LIMIT)


def _dot(a, b):
    return jnp.dot(a, b, preferred_element_type=F32)


def _split_bf16(x):
    hi = x.astype(BF16)
    lo = (x - hi.astype(F32)).astype(BF16)
    return hi, lo


def _layer_norm(z, g, b):
    mu = jnp.mean(z, axis=-1, keepdims=True)
    zc = z - mu
    var = jnp.mean(zc * zc, axis=-1, keepdims=True)
    return zc * lax.rsqrt(var + LN_EPS) * g + b


def _rms_norm(z, g, eps):
    return z * lax.rsqrt(jnp.mean(z * z, axis=-1, keepdims=True) + eps) * g


def _lane_iota(shape):
    return lax.broadcasted_iota(jnp.int32, shape, 1)


def _keep_half(x, half):
    lane = _lane_iota(x.shape)
    keep = (lane < HEAD_DIM) if half == 0 else (lane >= HEAD_DIM)
    return jnp.where(keep, x, jnp.zeros_like(x))


def _move_head(slab, src_half, dst_half):
    if src_half != dst_half:
        slab = pltpu.roll(slab, HEAD_DIM, 1)
    return _keep_half(slab, dst_half)


def _store_transposed(o_ref, res):
    for c in range(res.shape[1] // LANES):
        cols = slice(c * LANES, (c + 1) * LANES)
        o_ref[0, 0, cols, :] = res[:, cols].T.astype(o_ref.dtype)


def _proj_kernel(x_ref, *refs, transposed):
    n_out = len(transposed)
    w_refs, o_refs = refs[:n_out], refs[n_out:]
    xb = x_ref[...].astype(BF16)
    for w_ref, o_ref, t in zip(w_refs, o_refs, transposed):
        res = _dot(xb, w_ref[...])
        if t:
            _store_transposed(o_ref, res)
        else:
            o_ref[...] = res.astype(o_ref.dtype)


def _transposed_out(b, seq, width, tm):
    per_seq = seq // tm
    spec = pl.BlockSpec((1, 1, width, tm), lambda i: (i // per_seq, i % per_seq, 0, 0))
    return spec, jax.ShapeDtypeStruct((b, per_seq, width, tm), BF16)


def _project(x, weights, out_dtypes, transposed, b, seq):
    m, k = x.shape
    tm = min(KV_TILE, seq)
    specs, shapes = [], []
    for w, dt, t in zip(weights, out_dtypes, transposed):
        if t:
            spec, shape = _transposed_out(b, seq, w.shape[1], tm)
        else:
            spec, shape = pl.BlockSpec((tm, w.shape[1]), lambda i: (i, 0)), jax.ShapeDtypeStruct((m, w.shape[1]), dt)
        specs.append(spec)
        shapes.append(shape)
    return pl.pallas_call(
        functools.partial(_proj_kernel, transposed=tuple(transposed)),
        grid=(m // tm,),
        in_specs=[pl.BlockSpec((tm, k), lambda i: (i, 0))]
        + [pl.BlockSpec(w.shape, lambda i: (0, 0)) for w in weights],
        out_specs=specs,
        out_shape=shapes,
        compiler_params=_params("parallel"),
        name="project",
    )(x, *weights)


def _rope_slab(slab, c, s1, s2):
    half = MLA_ROPE // 2
    up = pltpu.roll(slab, half, 1)
    down = pltpu.roll(slab, LANES - half, 1)
    return slab * c + down * s1 + up * s2


def _mla_prep_kernel(ql_ref, kvl_ref, kpe_ref, qg_ref, kvg_ref, wq_ref, wk_ref, wv_ref,
                     c_ref, s1_ref, s2_ref, q_ref, k_ref, vt_ref, *, q_scale):
    c, s1, s2 = c_ref[...], s1_ref[...], s2_ref[...]
    qn = _rms_norm(ql_ref[...], qg_ref[...], RMS_EPS).astype(BF16)
    kvn = _rms_norm(kvl_ref[...], kvg_ref[...], RMS_EPS).astype(BF16)
    q = _dot(qn, wq_ref[...])
    k = _dot(kvn, wk_ref[...])
    _store_transposed(vt_ref, _dot(kvn, wv_ref[...]))
    kpe = _rope_slab(kpe_ref[...], c, s1, s2)
    for h in range(MLA_HEADS):
        sl = slice(h * LANES, (h + 1) * LANES)
        q_ref[:, sl] = (_rope_slab(q[:, sl], c, s1, s2) * q_scale).astype(q_ref.dtype)
        k_ref[:, sl] = (k[:, sl] + kpe).astype(k_ref.dtype)


def _mla_prep(slab_a, q_gain, kv_gain, wq, wk, wv, rope_c, rope_s1, rope_s2, b, seq):
    m = slab_a.shape[0]
    tm = min(KV_TILE, seq)
    per_seq = seq // tm
    rank = q_gain.shape[1]
    row = lambda j: (lambda i: (i, j))
    tab = lambda i: (i % per_seq, 0)
    const = lambda i: (0, 0)
    hw = MLA_HEADS * LANES
    vt_spec, vt_shape = _transposed_out(b, seq, wv.shape[1], tm)
    return pl.pallas_call(
        functools.partial(_mla_prep_kernel, q_scale=float((MLA_NOPE + MLA_ROPE) ** -0.5 * LOG2E)),
        grid=(m // tm,),
        in_specs=[pl.BlockSpec((tm, rank), row(0)), pl.BlockSpec((tm, rank), row(1)),
                  pl.BlockSpec((tm, LANES), row(2 * rank // LANES)),
                  pl.BlockSpec((1, rank), const), pl.BlockSpec((1, rank), const),
                  pl.BlockSpec(wq.shape, const), pl.BlockSpec(wk.shape, const), pl.BlockSpec(wv.shape, const),
                  pl.BlockSpec((tm, LANES), tab), pl.BlockSpec((tm, LANES), tab), pl.BlockSpec((tm, LANES), tab)],
        out_specs=[pl.BlockSpec((tm, hw), row(0)), pl.BlockSpec((tm, hw), row(0)), vt_spec],
        out_shape=[jax.ShapeDtypeStruct((m, hw), BF16), jax.ShapeDtypeStruct((m, hw), BF16), vt_shape],
        compiler_params=_params("parallel"),
        name="mla_prep",
    )(slab_a, slab_a, slab_a, q_gain, kv_gain, wq, wk, wv, rope_c, rope_s1, rope_s2)


def _flash_scratch(n_streams, v_rows, tq, tk):
    scores = pltpu.VMEM((n_streams, tk, tq), F32)
    stat = pltpu.VMEM((n_streams, 1, tq), F32)
    probs = pltpu.VMEM((n_streams, tk, tq), BF16)
    slot = [scores, probs, stat]
    return slot + slot + [stat, pltpu.VMEM((n_streams, v_rows + BF16_ROWS, tq), F32)]


def _chunk_rows(c):
    return slice(c * KEY_CHUNK, (c + 1) * KEY_CHUNK)


def _fold_rows(x):
    return x.reshape(x.shape[0] // SUBLANES, SUBLANES, x.shape[1])


def _flash_transposed(diagonal_tile, n_streams, q_start, tq, tk, key_operand, query_operand, values, offset, scratch,
                      rest_tile, rest_count):
    slot_a, slot_b, (m_ref, acc_ref) = scratch[0:3], scratch[3:6], scratch[6:]
    n_chunks = tk // KEY_CHUNK
    for i in range(n_streams):
        m_ref[i] = jnp.full((1, tq), NEG_INF, F32)
        acc_ref[i] = jnp.zeros(acc_ref.shape[1:], F32)

    def column_max(s_ref, i):
        part = jnp.full((SUBLANES, tq), NEG_INF, F32)
        for c in range(n_chunks):
            part = jnp.maximum(part, jnp.max(_fold_rows(s_ref[i, _chunk_rows(c), :]), axis=0))
        return jnp.max(part, axis=0, keepdims=True)

    def stage1(j, slot, i):
        slot[0][i] = _dot(key_operand(j, i), query_operand(j, i))

    def stage2(j, slot, i, causal):
        s_ref, p_ref, alpha_ref = slot
        if causal is not None:
            s_ref[i] = jnp.where(causal, s_ref[i], NEG_INF)
        mx = column_max(s_ref, i)
        off = offset(j, i)
        m_prev = m_ref[i]
        if off is None:
            m_next = jnp.maximum(m_prev, mx)
            shift = m_next
        else:
            m_next = jnp.maximum(m_prev, mx + off)
            shift = m_next - off
        alpha = jnp.exp2(m_prev - m_next)
        for c in range(n_chunks):
            p_ref[i, _chunk_rows(c), :] = jnp.exp2(s_ref[i, _chunk_rows(c), :] - shift).astype(BF16)
        m_ref[i] = m_next
        alpha_ref[i] = alpha

    def step(accumulate=None, produce=(), exponentiate=None, causal=None):
        products = []
        for i in range(n_streams + 1):
            if i < n_streams:
                if accumulate is not None:
                    tile, slot = accumulate
                    products.append(_dot(_values_and_ones(values(tile, i)), slot[1][i]))
                for tile, slot in produce:
                    stage1(tile, slot, i)
                if exponentiate is not None:
                    stage2(*exponentiate, i, causal)
            if accumulate is not None and i > 0:
                acc_ref[i - 1] = accumulate[1][2][i - 1] * acc_ref[i - 1] + products[i - 1]

    key_minus_query = (lax.broadcasted_iota(jnp.int32, (tk, tq), 0) - lax.broadcasted_iota(jnp.int32, (tk, tq), 1))
    step(produce=((diagonal_tile, slot_a), (rest_tile(0), slot_b)), exponentiate=(diagonal_tile, slot_a),
         causal=key_minus_query <= q_start - diagonal_tile * tk)
    last = rest_count(m_ref)

    def tile_at(position):
        return rest_tile(jnp.clip(position, 1, jnp.maximum(last, 1)) - 1)

    def diag_or_rest(position):
        return jnp.where(position == 0, diagonal_tile, tile_at(position))

    def pair(t, carry):
        p1 = 2 * t + 1
        step((diag_or_rest(p1 - 1), slot_a), ((tile_at(p1 + 1), slot_a),), (tile_at(p1), slot_b))
        step((tile_at(p1), slot_b), ((tile_at(p1 + 2), slot_b),), (tile_at(p1 + 1), slot_a))
        return carry

    lax.fori_loop(0, last // 2, pair, 0)

    @pl.when(last % 2 == 1)
    def _():
        step(accumulate=(diag_or_rest(last - 1), slot_a), exponentiate=(tile_at(last), slot_b))
        step(accumulate=(tile_at(last), slot_b))

    @pl.when(last % 2 == 0)
    def _():
        step(accumulate=(diag_or_rest(last), slot_a))


def _flash_result(scratch, i):
    acc = scratch[-1][i]
    v_rows = acc.shape[0] - BF16_ROWS
    return acc[:v_rows] / acc[v_rows:v_rows + 1]


def _kv_rows(j, tk):
    return pl.ds(pl.multiple_of(j * tk, tk), tk)


def _transposed_bf16(x):
    return x.astype(F32).T.astype(BF16)


def _alibi_rows(coef, tq):
    c = jnp.zeros((1, tq), F32) + coef
    hi = c.astype(BF16).astype(F32)
    rest = c - hi
    mid = rest.astype(BF16).astype(F32)
    lo = rest - mid
    zero = jnp.zeros((1, tq), F32)
    return jnp.concatenate([POS_SPLIT * hi, POS_SPLIT * mid, POS_SPLIT * lo, hi, mid, lo, zero, zero], axis=0)


def _augmented_query(q_t, feature_rows):
    tq = q_t.shape[1]
    pad = jnp.zeros((MXU_DEPTH - LANES - FEATURE_ROWS, tq), BF16)
    return jnp.concatenate([q_t, feature_rows.astype(BF16), pad], axis=0)


MLA_STEP_HEADS = 4


def _mla_attn_kernel(q_ref, k_ref, vt_ref, o_ref, *scratch, tq, tk):
    q_start = pl.program_id(2) * tq
    n = MLA_STEP_HEADS
    queries = [_transposed_bf16(q_ref[0, :, hh * LANES:(hh + 1) * LANES]) for hh in range(n)]
    n_full = q_start // tk
    _flash_transposed(
        n_full, n, q_start, tq, tk,
        lambda j, i: k_ref[0, _kv_rows(j, tk), i * LANES:(i + 1) * LANES],
        lambda j, i: queries[i],
        lambda j, i: vt_ref[0, j, i * HEAD_DIM:(i + 1) * HEAD_DIM, :],
        lambda j, i: None, scratch,
        rest_tile=lambda k: k, rest_count=lambda m_ref: n_full)
    for pair in range(n // 2):
        o_t = jnp.concatenate([_flash_result(scratch, 2 * pair), _flash_result(scratch, 2 * pair + 1)], axis=0)
        o_ref[0, :, pair * LANES:(pair + 1) * LANES] = o_t.T.astype(o_ref.dtype)


def _mla_attn(q, k, vt):
    b, s, _ = q.shape
    tq, tk = min(FLASH_Q_TILE, s), min(KV_TILE, s)
    n = MLA_STEP_HEADS
    groups = MLA_HEADS // n
    return pl.pallas_call(
        functools.partial(_mla_attn_kernel, tq=tq, tk=tk),
        grid=(b, groups, s // tq),
        in_specs=[pl.BlockSpec((1, tq, n * LANES), lambda bi, p, i: (bi, i, p)),
                  pl.BlockSpec((1, s, n * LANES), lambda bi, p, i: (bi, 0, p)),
                  pl.BlockSpec((1, s // tk, n * HEAD_DIM, tk), lambda bi, p, i: (bi, 0, p, 0))],
        out_specs=pl.BlockSpec((1, tq, n * HEAD_DIM), lambda bi, p, i: (bi, i, p)),
        out_shape=jax.ShapeDtypeStruct((b, s, MLA_HEADS * HEAD_DIM), BF16),
        scratch_shapes=_flash_scratch(n, HEAD_DIM, tq, tk),
        compiler_params=_params("parallel", "parallel", "arbitrary"),
        name="mla_attn",
    )(q, k, vt)


def _gelu_tanh(x):
    return 0.5 * x * (1.0 + jnp.tanh(math.sqrt(2.0 / math.pi) * (x + 0.044715 * (x * x * x))))


def _compress_kernel(x_ref, pos_ref, w1a_ref, w1b_ref, w2_ref, k_ref, vt_ref, *, n_real):
    x = x_ref[0]
    n = x.shape[0]
    first = _dot(x, w1a_ref[...])
    second = _dot(x, w1b_ref[...])
    pos_hi, pos_lo = _split_bf16(pos_ref[...])
    bias = (_dot(pos_hi[:8], w1a_ref[...]) + _dot(pos_lo[:8], w1a_ref[...])
            + _dot(pos_hi[8:], w1b_ref[...]) + _dot(pos_lo[8:], w1b_ref[...]))[:1]
    pre = first + pltpu.roll(second, n - 1, 0) + bias
    out = _dot(_gelu_tanh(pre).astype(BF16), w2_ref[...])
    real = lax.broadcasted_iota(jnp.int32, out.shape, 0) < n_real
    out = jnp.where(real, out, 0.0)
    half = out.shape[1] // 2
    k_ref[0] = out[:, :half].astype(k_ref.dtype)
    vt_ref[0] = out[:, half:].T.astype(vt_ref.dtype)


def _compress(x_chunks, pos_exp, w1a, w1b, w2, n_real):
    b, n, width = x_chunks.shape
    half = w2.shape[1] // 2
    const = lambda bi: (0, 0)
    return pl.pallas_call(
        functools.partial(_compress_kernel, n_real=n_real),
        grid=(b,),
        in_specs=[pl.BlockSpec((1, n, width), lambda bi: (bi, 0, 0)),
                  pl.BlockSpec(pos_exp.shape, const), pl.BlockSpec(w1a.shape, const),
                  pl.BlockSpec(w1b.shape, const), pl.BlockSpec(w2.shape, const)],
        out_specs=[pl.BlockSpec((1, n, half), lambda bi: (bi, 0, 0)), pl.BlockSpec((1, half, n), lambda bi: (bi, 0, 0))],
        out_shape=[jax.ShapeDtypeStruct((b, n, half), BF16), jax.ShapeDtypeStruct((b, half, n), BF16)],
        compiler_params=_params("parallel"),
        name="nsa_compress",
    )(x_chunks, pos_exp, w1a, w1b, w2)


def _nsa_head_slope(h):
    return float(2.0 ** (-8.0 * (h + 1) / NSA_HEADS))


def _nsa_queries(q_ref, g, scale):
    out = []
    for hg in range(NSA_HG):
        h = g * NSA_HG + hg
        slab = q_ref[0, :, (h // 2) * LANES:(h // 2 + 1) * LANES].astype(F32) * scale
        out.append(_move_head(slab, h % 2, g))
    return out


def _select_blocks(imp_t, q_pos):
    n_blocks = imp_t.shape[0]
    blk = lax.broadcasted_iota(jnp.int32, imp_t.shape, 0)
    cur = q_pos // SEL_LEN
    forced = jnp.where(blk == 0, 1.0, 0.0) + jnp.where(blk == cur, 1.0, 0.0) + jnp.where(blk == cur - 1, 1.0, 0.0)
    forced = jnp.minimum(forced, 1.0)
    val = jnp.where(blk <= cur, imp_t + FORCE_BONUS * forced, NEG_INF)
    chosen = jnp.zeros(imp_t.shape, F32)
    for _ in range(SEL_TOPK):
        top = jnp.max(val, axis=0, keepdims=True)
        first = jnp.min(jnp.where(val == top, blk, n_blocks), axis=0, keepdims=True)
        hit = blk == first
        chosen = jnp.where(hit, 1.0, chosen)
        val = jnp.where(hit, -jnp.inf, val)
    return jnp.where(chosen > 0.5, 0.0, NEG_INF)


def _masked_softmax_pass(s_ref, p_ref, tiles, tq):
    part = jnp.full((SUBLANES, tq), NEG_INF, F32)
    for t, rows, keep, off in tiles:
        for c in range(rows // KEY_CHUNK):
            s = jnp.where(keep(_chunk_rows(c)), s_ref[t, _chunk_rows(c), :], NEG_INF)
            s_ref[t, _chunk_rows(c), :] = s
            part = jnp.maximum(part, jnp.max(_fold_rows(s), axis=0) + off)
    m = jnp.max(part, axis=0, keepdims=True)
    for t, rows, keep, off in tiles:
        shift = m - off
        for c in range(rows // KEY_CHUNK):
            p_ref[t, _chunk_rows(c), :] = jnp.exp2(s_ref[t, _chunk_rows(c), :] - shift).astype(BF16)
    return m > 0.5 * NEG_INF


def _values_and_ones(vt):
    return jnp.concatenate([vt, jnp.ones((BF16_ROWS, vt.shape[1]), BF16)], axis=0)


def _nsa_cmp_win_kernel(q_ref, kc_ref, vct_ref, kw_ref, vwt_ref, feat_ref, ovt_ref, oc_ref, ow_ref, sel_ref, any_ref,
                        end_ref, rel_ref, sc_ref, sw_ref, pc_ref, pw_ref, *, tq, tk):
    qi = pl.program_id(1)
    q_start = qi * tq
    n_cmp = kc_ref.shape[1]
    end_ref[...] = (lax.broadcasted_iota(jnp.int32, (n_cmp, tq), 0) * CMP_STRIDE + (CMP_LEN - 1)
                    - lax.broadcasted_iota(jnp.int32, (n_cmp, tq), 1))
    rel_ref[...] = (lax.broadcasted_iota(jnp.int32, (tk, tq), 0) - lax.broadcasted_iota(jnp.int32, (tk, tq), 1))
    prev_tile = jnp.maximum(qi - 1, 0)
    prev_bound = jnp.where(qi >= 1, 0, tk)
    zeros = jnp.zeros((FEATURE_ROWS - SUBLANES, tq), F32)
    k_cmp = jnp.concatenate([kc_ref[0], feat_ref[0:n_cmp, :]], axis=1)
    k_win = [jnp.concatenate([kw_ref[0, _kv_rows(j, tk), :], feat_ref[...]], axis=1) for j in (prev_tile, qi)]
    q_pos = q_start + lax.broadcasted_iota(jnp.int32, (1, tq), 1)
    for g in range(NSA_GROUPS):
        queries = [_transposed_bf16(q) for q in _nsa_queries(q_ref, g, HEAD_DIM ** -0.5 * LOG2E)]
        group_rows = slice(g * HEAD_DIM, (g + 1) * HEAD_DIM)
        imp_t = jnp.zeros((LANES, tq), F32)
        out_c, out_w = [], []
        for hg in range(NSA_HG):
            coef = _nsa_head_slope(g * NSA_HG + hg) * LOG2E
            buf = hg % 2
            sc, sw, pc, pw = sc_ref.at[buf], sw_ref.at[buf], pc_ref.at[buf], pw_ref.at[buf]
            cmp_query = _augmented_query(queries[hg], jnp.concatenate([_alibi_rows(CMP_STRIDE * coef, tq), zeros], 0))
            win_query = _augmented_query(queries[hg], jnp.concatenate([_alibi_rows(coef, tq), zeros], 0))
            sc[0] = _dot(k_cmp, cmp_query)
            sw[0] = _dot(k_win[0], win_query)
            sw[1] = _dot(k_win[1], win_query)
            has_any = _masked_softmax_pass(sc, pc, [(0, n_cmp, lambda r: end_ref[r, :] <= q_start, 0.0)], tq)
            acc = _dot(_values_and_ones(vct_ref[0, group_rows, :]), pc[0])
            inv = jnp.where(has_any, 1.0 / acc[HEAD_DIM:HEAD_DIM + 1], 0.0)
            out_c.append(acc[:HEAD_DIM] * inv)
            imp_t = imp_t + _dot(ovt_ref[...], pc[0]) * inv
            _masked_softmax_pass(
                sw, pw, [(0, tk, lambda r: rel_ref[r, :] > prev_bound, -coef * tk),
                         (1, tk, lambda r: rel_ref[r, :] <= 0, 0.0)], tq)
            acc = (_dot(_values_and_ones(vwt_ref[0, prev_tile, group_rows, :]), pw[0])
                   + _dot(_values_and_ones(vwt_ref[0, qi, group_rows, :]), pw[1]))
            out_w.append(acc[:HEAD_DIM] / acc[HEAD_DIM:HEAD_DIM + 1])
        for pair in range(NSA_HG // 2):
            cols = slice((g * 2 + pair) * LANES, (g * 2 + pair + 1) * LANES)
            oc_ref[0, :, cols] = jnp.concatenate(out_c[2 * pair:2 * pair + 2], axis=0).T
            ow_ref[0, :, cols] = jnp.concatenate(out_w[2 * pair:2 * pair + 2], axis=0).T
        bias_t = _select_blocks(imp_t, q_pos)
        sel_ref[0, g * LANES:(g + 1) * LANES, :] = bias_t
        any_ref[0, 0, g * LANES:(g + 1) * LANES, :] = jnp.broadcast_to(
            jnp.max(bias_t, axis=1, keepdims=True), (LANES, LANES))


def _nsa_cmp_win(slab_b, cmp_k, cmp_vt, vt_nsa, key_features, overlap_t, seq):
    b = slab_b.shape[0]
    tq, tk = min(FLASH_Q_TILE, seq), min(KV_TILE, seq)
    assert tq == tk == WINDOW, "the window branch is written for one previous and one diagonal key tile"
    n_cmp = cmp_k.shape[1]
    qw = NSA_HEADS * HEAD_DIM
    base = qw // LANES
    tile = lambda bi, i: (bi, i, 0)
    return pl.pallas_call(
        functools.partial(_nsa_cmp_win_kernel, tq=tq, tk=tk),
        grid=(b, seq // tq),
        in_specs=[pl.BlockSpec((1, tq, qw), tile),
                  pl.BlockSpec((1, n_cmp, LANES), lambda bi, i: (bi, 0, 0)),
                  pl.BlockSpec((1, LANES, n_cmp), lambda bi, i: (bi, 0, 0)),
                  pl.BlockSpec((1, seq, LANES), lambda bi, i: (bi, 0, base + 1)),
                  pl.BlockSpec((1, seq // tk, LANES, tk), lambda bi, i: (bi, 0, 1, 0)),
                  pl.BlockSpec(key_features.shape, lambda bi, i: (0, 0)),
                  pl.BlockSpec(overlap_t.shape, lambda bi, i: (0, 0))],
        out_specs=[pl.BlockSpec((1, tq, qw), tile), pl.BlockSpec((1, tq, qw), tile),
                   pl.BlockSpec((1, NSA_GROUPS * LANES, tq), lambda bi, i: (bi, 0, i)),
                   pl.BlockSpec((1, 1, NSA_GROUPS * LANES, LANES), lambda bi, i: (bi, i, 0, 0))],
        out_shape=[jax.ShapeDtypeStruct((b, seq, qw), F32), jax.ShapeDtypeStruct((b, seq, qw), F32),
                   jax.ShapeDtypeStruct((b, NSA_GROUPS * LANES, seq), F32),
                   jax.ShapeDtypeStruct((b, seq // tq, NSA_GROUPS * LANES, LANES), F32)],
        scratch_shapes=[pltpu.VMEM((n_cmp, tq), jnp.int32), pltpu.VMEM((tk, tq), jnp.int32),
                        pltpu.VMEM((2, 1, n_cmp, tq), F32), pltpu.VMEM((2, 2, tk, tq), F32),
                        pltpu.VMEM((2, 1, n_cmp, tq), BF16), pltpu.VMEM((2, 2, tk, tq), BF16)],
        compiler_params=_params("parallel", "arbitrary"),
        name="nsa_cmp_win",
    )(slab_b, cmp_k, cmp_vt, slab_b, vt_nsa, key_features, overlap_t)


def _nsa_sel_kernel(tiles_ref, counts_ref, q_ref, k_ref, vt_ref, feat_ref, sel_ref, oc_ref, ow_ref, gate_ref, gx_ref,
                    o_ref, *scratch, tq, tk, max_tiles):
    qi = pl.program_id(1)
    q_start = qi * tq
    diagonal_tile = q_start // tk
    out_slabs = []
    for g in range(NSA_GROUPS):
        queries = [_transposed_bf16(q) for q in _nsa_queries(q_ref, g, HEAD_DIM ** -0.5 * LOG2E)]
        coefs = [_nsa_head_slope(g * NSA_HG + hg) * LOG2E for hg in range(NSA_HG)]
        alibi = [_alibi_rows(c, tq) for c in coefs]
        entry = (pl.program_id(0) * pl.num_programs(1) + qi) * NSA_GROUPS + g
        n_active = counts_ref[entry]

        def listed_tile(k, entry=entry):
            return tiles_ref[entry * max_tiles + jnp.minimum(k, max_tiles - 1)]

        def key_operand(j, i):
            return jnp.concatenate([k_ref[0, _kv_rows(j, tk), :], feat_ref[...]], axis=1)

        def query_operand(j, i, g=g, queries=queries, alibi=alibi):
            first_block = pl.multiple_of(g * LANES + j * BLOCKS_PER_TILE, BLOCKS_PER_TILE)
            blocks = sel_ref[0, pl.ds(first_block, BLOCKS_PER_TILE), :]
            return _augmented_query(queries[i], jnp.concatenate([alibi[i], blocks], axis=0))

        _flash_transposed(
            diagonal_tile, NSA_HG, q_start, tq, tk, key_operand, query_operand,
            lambda j, i, g=g: vt_ref[0, j, g * HEAD_DIM:(g + 1) * HEAD_DIM, :],
            lambda j, i, coefs=coefs: coefs[i] * (j * tk - q_start).astype(F32), scratch,
            rest_tile=listed_tile, rest_count=lambda m_ref, n_active=n_active: n_active)
        heads = [_flash_result(scratch, hg) for hg in range(NSA_HG)]
        for pair in range(NSA_HG // 2):
            out_slabs.append(jnp.concatenate(heads[2 * pair:2 * pair + 2], axis=0).T)
    gates = jax.nn.sigmoid(gate_ref[...])
    g_hi, g_lo = _split_bf16(gates)
    width = NSA_HEADS * HEAD_DIM
    for i, o_sel in enumerate(out_slabs):
        mixed = None
        for branch, o_branch in enumerate((oc_ref[0, :, i * LANES:(i + 1) * LANES], o_sel,
                                           ow_ref[0, :, i * LANES:(i + 1) * LANES])):
            gx = gx_ref[:, branch * width + i * LANES:branch * width + (i + 1) * LANES]
            term = (_dot(g_hi, gx) + _dot(g_lo, gx)) * o_branch
            mixed = term if mixed is None else mixed + term
        o_ref[0, :, i * LANES:(i + 1) * LANES] = mixed.astype(o_ref.dtype)


def _active_key_tiles(block_any, tq, tk):
    b, n_q = block_any.shape[:2]
    max_tiles = LANES // BLOCKS_PER_TILE
    hit = block_any[..., 0].reshape(b, n_q, NSA_GROUPS, max_tiles, BLOCKS_PER_TILE).max(axis=-1) > 0.5 * NEG_INF
    before_diagonal = jnp.arange(max_tiles)[None, :] < (jnp.arange(n_q) * tq // tk)[:, None]
    hit = hit & before_diagonal[None, :, None, :]
    rank = jnp.cumsum(hit.astype(jnp.int32), axis=-1) - 1
    slots = jnp.arange(max_tiles, dtype=jnp.int32)
    in_slot = hit[..., None, :] & (rank[..., None, :] == slots[:, None])
    tiles = jnp.sum(jnp.where(in_slot, slots, 0), axis=-1)
    return tiles.astype(jnp.int32).reshape(-1), hit.sum(axis=-1).astype(jnp.int32).reshape(-1), max_tiles


def _nsa_sel(slab_b, vt, key_features, sel_bias_t, block_any, o_cmp, o_win, slab_a, gate_expand, seq, gate_col_block):
    b = slab_b.shape[0]
    tq, tk = min(FLASH_Q_TILE, seq), min(KV_TILE, seq)
    qw = NSA_HEADS * HEAD_DIM
    base = qw // LANES
    per_seq = seq // tq
    tiles, counts, max_tiles = _active_key_tiles(block_any, tq, tk)
    tile = lambda bi, i, *_: (bi, i, 0)
    grid_spec = pltpu.PrefetchScalarGridSpec(
        num_scalar_prefetch=2,
        grid=(b, seq // tq),
        in_specs=[pl.BlockSpec((1, tq, qw), tile),
                  pl.BlockSpec((1, seq, LANES), lambda bi, i, *_: (bi, 0, base)),
                  pl.BlockSpec((1, seq // tk, LANES, tk), lambda bi, i, *_: (bi, 0, 0, 0)),
                  pl.BlockSpec(key_features.shape, lambda bi, i, *_: (0, 0)),
                  pl.BlockSpec((1, NSA_GROUPS * LANES, tq), lambda bi, i, *_: (bi, 0, i)),
                  pl.BlockSpec((1, tq, qw), tile), pl.BlockSpec((1, tq, qw), tile),
                  pl.BlockSpec((tq, LANES), lambda bi, i, *_: (bi * per_seq + i, gate_col_block)),
                  pl.BlockSpec(gate_expand.shape, lambda bi, i, *_: (0, 0))],
        out_specs=pl.BlockSpec((1, tq, qw), tile),
        scratch_shapes=_flash_scratch(NSA_HG, HEAD_DIM, tq, tk))
    return pl.pallas_call(
        functools.partial(_nsa_sel_kernel, tq=tq, tk=tk, max_tiles=max_tiles),
        grid_spec=grid_spec,
        out_shape=jax.ShapeDtypeStruct((b, seq, qw), BF16),
        compiler_params=_params("parallel", "arbitrary"),
        name="nsa_sel",
    )(tiles, counts, slab_b, slab_b, vt, key_features, sel_bias_t, o_cmp, o_win, slab_a, gate_expand)


DIFF_STEP_HEADS = 2


SKIP_GAP = 180.0
NORM_SLACK = 1.01


def _diff_attn_kernel(slope_ref, lam_ref, q_ref, k_ref, vt_ref, feat_ref, g_ref, o_ref, knorm_ref, *scratch,
                      tq, tk, lam_init):
    first_head = pl.program_id(1) * DIFF_STEP_HEADS
    q_start = pl.program_id(2) * tq
    n_full = q_start // tk
    n_tiles = k_ref.shape[1] // tk

    @pl.when(pl.program_id(2) == 0)
    def _():
        for hh in range(DIFF_STEP_HEADS):
            for j in range(n_tiles):
                k = k_ref[0, j * tk:(j + 1) * tk, hh * LANES:(hh + 1) * LANES].astype(F32)
                knorm_ref[hh * n_tiles + j] = jnp.sqrt(jnp.max(jnp.sum(k * k, axis=1, keepdims=True)))

    zeros = jnp.zeros((FEATURE_ROWS - SUBLANES, tq), F32)
    coefs, queries, q_norms = [], [], []
    for hh in range(DIFF_STEP_HEADS):
        coef = slope_ref[first_head + hh] * LOG2E
        q = q_ref[0, :, hh * LANES:(hh + 1) * LANES].astype(F32) * (HEAD_DIM ** -0.5 * LOG2E)
        features = jnp.concatenate([_alibi_rows(coef, tq), zeros], axis=0)
        coefs.append(coef)
        for half in range(2):
            q_half = _keep_half(q, half)
            queries.append(_augmented_query(_transposed_bf16(q_half), features))
            q_norms.append(NORM_SLACK * jnp.sqrt(jnp.max(jnp.sum(q_half * q_half, axis=1, keepdims=True))))

    def head_lanes(i):
        return slice((i // 2) * LANES, (i // 2 + 1) * LANES)

    def offset(j, i):
        return coefs[i // 2] * (j * tk - q_start).astype(F32)

    nearest = jnp.maximum(n_full - 1, 0)
    first_needed = []

    def rest_count(m_ref):
        floors = [jnp.min(m_ref[i]) - SKIP_GAP for i in range(2 * DIFF_STEP_HEADS)]

        def body(j, first):
            needed = jnp.bool_(False)
            for i in range(2 * DIFF_STEP_HEADS):
                bound = q_norms[i] * knorm_ref[(i // 2) * n_tiles + j] + coefs[i // 2] * (tk - 1) + offset(j, i)
                needed = jnp.logical_or(needed, bound >= floors[i])
            return jnp.where(needed, jnp.minimum(first, j), first)

        first_needed.append(lax.fori_loop(0, nearest, body, nearest))
        return jnp.minimum(n_full, 1) + nearest - first_needed[0]

    def rest_tile(k):
        if isinstance(k, int):
            return nearest
        return jnp.where(k == 0, nearest, first_needed[0] + k - 1)

    _flash_transposed(
        n_full, 2 * DIFF_STEP_HEADS, q_start, tq, tk,
        lambda j, i: jnp.concatenate([k_ref[0, _kv_rows(j, tk), head_lanes(i)], feat_ref[...]], axis=1),
        lambda j, i: queries[i],
        lambda j, i: vt_ref[0, j, head_lanes(i), :],
        offset, scratch, rest_tile, rest_count)
    lam_vec = lam_ref[...]
    lam = (jnp.exp(jnp.sum(lam_vec[0:1] * lam_vec[1:2], axis=1, keepdims=True))
           - jnp.exp(jnp.sum(lam_vec[2:3] * lam_vec[3:4], axis=1, keepdims=True)) + lam_init)
    for hh in range(DIFF_STEP_HEADS):
        o = (_flash_result(scratch, 2 * hh) - lam * _flash_result(scratch, 2 * hh + 1)).T
        o_ref[0, :, hh * LANES:(hh + 1) * LANES] = (
            _rms_norm(o, g_ref[...], RMS_EPS) * (1.0 - lam_init)).astype(o_ref.dtype)


def _diff_attn(qk, vt, key_features, slopes, lam_vecs, subln_g, lam_init):
    b, s, _ = qk.shape
    tq, tk = min(FLASH_Q_TILE, s), min(KV_TILE, s)
    n = DIFF_STEP_HEADS
    groups = DIFF_HEADS // n
    smem = pl.BlockSpec(memory_space=pltpu.SMEM)
    return pl.pallas_call(
        functools.partial(_diff_attn_kernel, tq=tq, tk=tk, lam_init=lam_init),
        grid=(b, groups, s // tq),
        in_specs=[smem, pl.BlockSpec(lam_vecs.shape, lambda bi, h, i: (0, 0)),
                  pl.BlockSpec((1, tq, n * LANES), lambda bi, h, i: (bi, i, h)),
                  pl.BlockSpec((1, s, n * LANES), lambda bi, h, i: (bi, 0, groups + h)),
                  pl.BlockSpec((1, s // tk, n * LANES, tk), lambda bi, h, i: (bi, 0, h, 0)),
                  pl.BlockSpec(key_features.shape, lambda bi, h, i: (0, 0)),
                  pl.BlockSpec((1, LANES), lambda bi, h, i: (0, 0))],
        out_specs=pl.BlockSpec((1, tq, n * LANES), lambda bi, h, i: (bi, i, h)),
        out_shape=jax.ShapeDtypeStruct((b, s, DIFF_HEADS * LANES), BF16),
        scratch_shapes=[pltpu.SMEM((n * (s // tk),), F32)] + _flash_scratch(2 * n, LANES, tq, tk),
        compiler_params=_params("parallel", "parallel", "arbitrary"),
        name="diff_attn",
    )(slopes, lam_vecs, qk, qk, vt, key_features, subln_g)


def _out_ln_kernel(*refs, n_in):
    a_refs, w_refs = refs[:n_in], refs[n_in:2 * n_in]
    x_ref, g_ref, b_ref, o_ref = refs[2 * n_in:]
    y = None
    for a_ref, w_ref in zip(a_refs, w_refs):
        t = _dot(a_ref[...], w_ref[...])
        y = t if y is None else y + t
    o_ref[...] = _layer_norm(DN_ALPHA * x_ref[...] + y, g_ref[...], b_ref[...])


def _out_ln(acts, weights, x, g, b):
    m, d = x.shape
    tm = min(ROW_TILE, m)
    row = lambda i: (i, 0)
    const = lambda i: (0, 0)
    return pl.pallas_call(
        functools.partial(_out_ln_kernel, n_in=len(acts)),
        grid=(m // tm,),
        in_specs=[pl.BlockSpec((tm, a.shape[1]), row) for a in acts]
        + [pl.BlockSpec(w.shape, const) for w in weights]
        + [pl.BlockSpec((tm, d), row), pl.BlockSpec((1, d), const), pl.BlockSpec((1, d), const)],
        out_specs=pl.BlockSpec((tm, d), row),
        out_shape=jax.ShapeDtypeStruct((m, d), F32),
        compiler_params=_params("parallel"),
        name="out_proj_ln",
    )(*acts, *weights, x, g, b)


def _mlp_kernel(x_ref, wu_ref, wd_ref, g_ref, b_ref, o_ref, *, tf):
    x = x_ref[...]
    xb = x.astype(BF16)
    acc = None
    for f in range(wu_ref.shape[1] // tf):
        cols = slice(f * tf, (f + 1) * tf)
        hidden = jnp.maximum(_dot(xb, wu_ref[:, cols]), 0.0)
        part = _dot((hidden * hidden).astype(BF16), wd_ref[cols, :])
        acc = part if acc is None else acc + part
    o_ref[...] = _layer_norm(DN_ALPHA * x + acc, g_ref[...], b_ref[...])


def _mlp(x, w_up, w_down, g, b):
    m, d = x.shape
    ff = w_up.shape[1]
    tm, tf = min(ROW_TILE, m), min(FF_TILE, ff)
    resident = dict(pipeline_mode=pl.Buffered(1))
    return pl.pallas_call(
        functools.partial(_mlp_kernel, tf=tf),
        grid=(m // tm,),
        in_specs=[pl.BlockSpec((tm, d), lambda i: (i, 0)),
                  pl.BlockSpec((d, ff), lambda i: (0, 0), **resident),
                  pl.BlockSpec((ff, d), lambda i: (0, 0), **resident),
                  pl.BlockSpec((1, d), lambda i: (0, 0)), pl.BlockSpec((1, d), lambda i: (0, 0))],
        out_specs=pl.BlockSpec((tm, d), lambda i: (i, 0)),
        out_shape=jax.ShapeDtypeStruct((m, d), F32),
        compiler_params=_params("parallel"),
        name="mlp_ln",
    )(x, w_up, w_down, g, b)


def _pad_cols(w, width):
    return jnp.pad(w, ((0, 0), (0, width - w.shape[1])))


def _layer0_weights(w_in, w_uq, w_ukv, d_model):
    rank = d_model // 4
    kvw = NSA_GROUPS * HEAD_DIM
    o = np.cumsum([0, rank, rank, MLA_ROPE, NSA_HEADS * HEAD_DIM] + [kvw] * 6 + [3 * NSA_HEADS])
    seg = lambda i: w_in[:, o[i]:o[i + 1]]
    zeros = lambda n: jnp.zeros((w_in.shape[0], n), w_in.dtype)
    rope_slab = jnp.concatenate([zeros(MLA_NOPE), seg(2), zeros(LANES - MLA_NOPE - MLA_ROPE)], axis=1)
    w_a = jnp.concatenate([seg(0), seg(1), rope_slab, _pad_cols(seg(10), LANES)], axis=1)
    w_b = jnp.concatenate([seg(3), seg(6), seg(8)], axis=1)
    w_c = jnp.concatenate([seg(4), seg(5)], axis=1)
    w_vs = jnp.concatenate([seg(7), seg(9)], axis=1)
    wq = jnp.pad(w_uq.reshape(rank, MLA_HEADS, MLA_NOPE + MLA_ROPE),
                 ((0, 0), (0, 0), (0, LANES - MLA_NOPE - MLA_ROPE))).reshape(rank, MLA_HEADS * LANES)
    ukv = w_ukv.reshape(rank, MLA_HEADS, MLA_NOPE + HEAD_DIM)
    wk = jnp.pad(ukv[:, :, :MLA_NOPE], ((0, 0), (0, 0), (0, LANES - MLA_NOPE))).reshape(rank, MLA_HEADS * LANES)
    wv = ukv[:, :, MLA_NOPE:].reshape(rank, MLA_HEADS * HEAD_DIM)
    return [w.astype(BF16) for w in (w_a, w_b, w_c, w_vs, wq, wk, wv)]


def _rope_tables(seq):
    inv = 1.0 / (ROPE_THETA ** (jnp.arange(0, MLA_ROPE, 2, dtype=F32) / MLA_ROPE))
    ang = jnp.arange(seq, dtype=F32)[:, None] * inv[None, :]
    cos, sin = jnp.cos(ang), jnp.sin(ang)
    half = MLA_ROPE // 2
    z = lambda n: jnp.zeros((seq, n), F32)
    tail = LANES - MLA_NOPE - MLA_ROPE
    c = jnp.concatenate([jnp.ones((seq, MLA_NOPE), F32), cos, cos, z(tail)], axis=1)
    s1 = jnp.concatenate([z(MLA_NOPE), -sin, z(half), z(tail)], axis=1)
    s2 = jnp.concatenate([z(MLA_NOPE), z(half), sin, z(tail)], axis=1)
    return c, s1, s2


def _compress_weights(pos_k, w1_k, w2_k, pos_v, w1_v, w2_v):
    eye = jnp.eye(2 * NSA_GROUPS, dtype=F32)
    halves = []
    for a in range(CMP_LEN // CMP_STRIDE):
        rows = slice(a * CMP_STRIDE * HEAD_DIM, (a + 1) * CMP_STRIDE * HEAD_DIM)
        wk = w1_k[rows].reshape(CMP_STRIDE, HEAD_DIM, HEAD_DIM)
        wv = w1_v[rows].reshape(CMP_STRIDE, HEAD_DIM, HEAD_DIM)
        per_slot = jnp.stack([wk, wk, wv, wv], axis=0)
        full = jnp.einsum('st,srdj->rsdtj', eye, per_slot)
        halves.append(full.reshape(CMP_STRIDE * 4 * HEAD_DIM, 4 * HEAD_DIM).astype(BF16))
    w2 = jnp.einsum('st,sdj->sdtj', eye, jnp.stack([w2_k, w2_k, w2_v, w2_v])).reshape(4 * HEAD_DIM, 4 * HEAD_DIM)
    pos = jnp.concatenate([pos_k, pos_k, pos_v, pos_v], axis=1)
    pos = pos.reshape(CMP_LEN // CMP_STRIDE, 1, CMP_STRIDE * 4 * HEAD_DIM)
    pos = jnp.broadcast_to(pos, (pos.shape[0], 8, pos.shape[2])).reshape(-1, pos.shape[2])
    return pos, halves[0], halves[1], w2.astype(BF16)


def _overlap_table(n_cmp_pad, n_cmp):
    c0 = np.arange(n_cmp_pad)[None, :] * CMP_STRIDE
    s0 = np.arange(LANES)[:, None] * SEL_LEN
    ov = np.maximum(np.minimum(c0 + CMP_LEN, s0 + SEL_LEN) - np.maximum(c0, s0), 0) / CMP_LEN
    ov = ov * (np.arange(n_cmp_pad)[None, :] < n_cmp)
    return jnp.asarray(ov, BF16)


def _key_feature_table(tk):
    c = np.arange(tk)
    table = np.zeros((tk, LANES), np.float32)
    table[:, 0:3] = (c // POS_SPLIT)[:, None]
    table[:, 3:6] = (c % POS_SPLIT)[:, None]
    table[c, BLOCK_LANE0 + c // SEL_LEN] = 1.0
    return jnp.asarray(table, BF16)


def _gate_expand_table():
    width = NSA_HEADS * HEAD_DIM
    table = np.zeros((LANES, 3 * width), np.float32)
    for h in range(NSA_HEADS):
        for branch in range(3):
            table[h * 3 + branch, branch * width + h * HEAD_DIM:branch * width + (h + 1) * HEAD_DIM] = 1.0
    return jnp.asarray(table, BF16)


def _alibi_slopes(n):
    return jnp.asarray(2.0 ** (-8.0 * np.arange(1, n + 1) / n), dtype=F32)


def _layer0_mixer(x2, b, s, w_in, q_norm, w_uq, kv_norm, w_ukv, pos_k, w1_k, w2_k, pos_v, w1_v, w2_v, w_out):
    d = x2.shape[1]
    rank = d // 4
    w_a, w_b, w_c, w_vs, wq, wk, wv = _layer0_weights(w_in, w_uq, w_ukv, d)
    slab_a, slab_b, slab_c, vt_nsa = _project(x2, [w_a, w_b, w_c, w_vs], [F32, BF16, BF16, BF16],
                                              [False, False, False, True], b, s)
    rope_c, rope_s1, rope_s2 = _rope_tables(s)
    q, k, vt = _mla_prep(slab_a, q_norm.reshape(1, rank), kv_norm.reshape(1, rank), wq, wk, wv,
                         rope_c, rope_s1, rope_s2, b, s)
    o_mla = _mla_attn(q.reshape(b, s, -1), k.reshape(b, s, -1), vt)
    n_chunks = s // CMP_STRIDE
    n_cmp = (s - CMP_LEN) // CMP_STRIDE + 1
    pos, w1a, w1b, w2 = _compress_weights(pos_k, w1_k, w2_k, pos_v, w1_v, w2_v)
    cmp_k, cmp_vt = _compress(slab_c.reshape(b, n_chunks, CMP_STRIDE * slab_c.shape[1]), pos, w1a, w1b, w2, n_cmp)
    slab_b3 = slab_b.reshape(b, s, -1)
    key_features = _key_feature_table(min(KV_TILE, s))
    o_cmp, o_win, sel_bias_t, block_any = _nsa_cmp_win(slab_b3, cmp_k, cmp_vt, vt_nsa, key_features,
                                                       _overlap_table(n_chunks, n_cmp), s)
    o_nsa = _nsa_sel(slab_b3, vt_nsa, key_features, sel_bias_t, block_any, o_cmp, o_win, slab_a,
                     _gate_expand_table(), s, (2 * rank + LANES) // LANES)
    half = o_mla.shape[-1]
    w_out_b = w_out.astype(BF16)
    return [o_mla.reshape(b * s, half), o_nsa.reshape(b * s, -1)], [w_out_b[:half], w_out_b[half:]]


def _layer1_mixer(x2, b, s, w_qkv, lam_q1, lam_k1, lam_q2, lam_k2, subln_g, w_o, layer_idx):
    d = x2.shape[1]
    w = w_qkv.astype(BF16)
    qk, vt = _project(x2, [w[:, :2 * d], w[:, 2 * d:]], [BF16, BF16], [False, True], b, s)
    lam_init = 0.8 - 0.6 * math.exp(-0.3 * layer_idx)
    lam_vecs = jnp.stack([lam_q1, lam_k1, lam_q2, lam_k2]).astype(F32)
    o = _diff_attn(qk.reshape(b, s, -1), vt, _key_feature_table(min(KV_TILE, s)), _alibi_slopes(DIFF_HEADS),
                   lam_vecs, subln_g.reshape(1, -1), lam_init)
    return [o.reshape(b * s, -1)], [w_o.astype(BF16)]


def kernel(x, l0_w_in, l0_mla_q_norm, l0_mla_w_uq, l0_mla_kv_norm, l0_mla_w_ukv, l0_nsa_cmp_pos_k, l0_nsa_cmp_w1_k, l0_nsa_cmp_w2_k, l0_nsa_cmp_pos_v, l0_nsa_cmp_w1_v, l0_nsa_cmp_w2_v, l0_w_out, l0_ln_mix_g, l0_ln_mix_b, l0_w_up, l0_w_down, l0_ln_ffn_g, l0_ln_ffn_b, l1_w_qkv, l1_lam_q1, l1_lam_k1, l1_lam_q2, l1_lam_k2, l1_subln_g, l1_w_o, l1_ln_mix_g, l1_ln_mix_b, l1_w_up, l1_w_down, l1_ln_ffn_g, l1_ln_ffn_b):
    b, s, d = x.shape
    x2 = x.reshape(b * s, d)
    vec = lambda p: p.reshape(1, d)
    acts, weights = _layer0_mixer(x2, b, s, l0_w_in, l0_mla_q_norm, l0_mla_w_uq, l0_mla_kv_norm, l0_mla_w_ukv,
                                  l0_nsa_cmp_pos_k, l0_nsa_cmp_w1_k, l0_nsa_cmp_w2_k,
                                  l0_nsa_cmp_pos_v, l0_nsa_cmp_w1_v, l0_nsa_cmp_w2_v, l0_w_out)
    x2 = _out_ln(acts, weights, x2, vec(l0_ln_mix_g), vec(l0_ln_mix_b))
    x2 = _mlp(x2, l0_w_up.astype(BF16), l0_w_down.astype(BF16), vec(l0_ln_ffn_g), vec(l0_ln_ffn_b))
    acts, weights = _layer1_mixer(x2, b, s, l1_w_qkv, l1_lam_q1, l1_lam_k1, l1_lam_q2, l1_lam_k2,
                                  l1_subln_g, l1_w_o, 1)
    x2 = _out_ln(acts, weights, x2, vec(l1_ln_mix_g), vec(l1_ln_mix_b))
    x2 = _mlp(x2, l1_w_up.astype(BF16), l1_w_down.astype(BF16), vec(l1_ln_ffn_g), vec(l1_ln_ffn_b))
    return x2.reshape(b, s, d)
```

```python
import functools
import math

import jax
import jax.numpy as jnp
import numpy as np
from jax import lax
from jax.experimental import pallas as pl
from jax.experimental.pallas import tpu as pltpu

F32 = jnp.float32
BF16 = jnp.bfloat16

LANES = 128
SUBLANES = 8
BF16_ROWS = 16
MXU_DEPTH = 256
HEAD_DIM = 64
FLASH_Q_TILE = 512
KV_TILE = 512
KEY_CHUNK = 32
ROW_TILE = 512
FF_TILE = 1024
VMEM_LIMIT = 56 * 1024 * 1024

NEG_INF = -1e30
LOG2E = math.log2(math.e)
LN_EPS = 1e-5
RMS_EPS = 1e-6
DEPTH = 2
DN_ALPHA = (2.0 * DEPTH) ** 0.25

MLA_HEADS = 8
MLA_NOPE = 64
MLA_ROPE = 32
ROPE_THETA = 10000.0
NSA_HEADS = 8
NSA_GROUPS = 2
NSA_HG = NSA_HEADS // NSA_GROUPS
CMP_LEN = 32
CMP_STRIDE = 16
SEL_LEN = 64
SEL_TOPK = 16
WINDOW = 512
FORCE_BONUS = 1e3
DIFF_HEADS = 8

POS_SPLIT = 16
FEATURE_ROWS = 16
BLOCK_LANE0 = 8
BLOCKS_PER_TILE = KV_TILE // SEL_LEN


def _params(*sem):
    return pltpu.CompilerParams(dimension_semantics=sem, vmem_limit_bytes=VMEM_LIMIT)


def _dot(a, b):
    return jnp.dot(a, b, preferred_element_type=F32)


def _split_bf16(x):
    hi = x.astype(BF16)
    lo = (x - hi.astype(F32)).astype(BF16)
    return hi, lo


def _layer_norm(z, g, b):
    mu = jnp.mean(z, axis=-1, keepdims=True)
    zc = z - mu
    var = jnp.mean(zc * zc, axis=-1, keepdims=True)
    return zc * lax.rsqrt(var + LN_EPS) * g + b


def _rms_norm(z, g, eps):
    return z * lax.rsqrt(jnp.mean(z * z, axis=-1, keepdims=True) + eps) * g


def _lane_iota(shape):
    return lax.broadcasted_iota(jnp.int32, shape, 1)


def _keep_half(x, half):
    lane = _lane_iota(x.shape)
    keep = (lane < HEAD_DIM) if half == 0 else (lane >= HEAD_DIM)
    return jnp.where(keep, x, jnp.zeros_like(x))


def _move_head(slab, src_half, dst_half):
    if src_half != dst_half:
        slab = pltpu.roll(slab, HEAD_DIM, 1)
    return _keep_half(slab, dst_half)


def _store_transposed(o_ref, res):
    for c in range(res.shape[1] // LANES):
        cols = slice(c * LANES, (c + 1) * LANES)
        o_ref[0, 0, cols, :] = res[:, cols].T.astype(o_ref.dtype)


def _proj_kernel(x_ref, *refs, transposed):
    n_out = len(transposed)
    w_refs, o_refs = refs[:n_out], refs[n_out:]
    xb = x_ref[...].astype(BF16)
    for w_ref, o_ref, t in zip(w_refs, o_refs, transposed):
        res = _dot(xb, w_ref[...])
        if t:
            _store_transposed(o_ref, res)
        else:
            o_ref[...] = res.astype(o_ref.dtype)


def _transposed_out(b, seq, width, tm):
    per_seq = seq // tm
    spec = pl.BlockSpec((1, 1, width, tm), lambda i: (i // per_seq, i % per_seq, 0, 0))
    return spec, jax.ShapeDtypeStruct((b, per_seq, width, tm), BF16)


def _project(x, weights, out_dtypes, transposed, b, seq):
    m, k = x.shape
    tm = min(KV_TILE, seq)
    specs, shapes = [], []
    for w, dt, t in zip(weights, out_dtypes, transposed):
        if t:
            spec, shape = _transposed_out(b, seq, w.shape[1], tm)
        else:
            spec, shape = pl.BlockSpec((tm, w.shape[1]), lambda i: (i, 0)), jax.ShapeDtypeStruct((m, w.shape[1]), dt)
        specs.append(spec)
        shapes.append(shape)
    return pl.pallas_call(
        functools.partial(_proj_kernel, transposed=tuple(transposed)),
        grid=(m // tm,),
        in_specs=[pl.BlockSpec((tm, k), lambda i: (i, 0))]
        + [pl.BlockSpec(w.shape, lambda i: (0, 0)) for w in weights],
        out_specs=specs,
        out_shape=shapes,
        compiler_params=_params("parallel"),
        name="project",
    )(x, *weights)


def _rope_slab(slab, c, s1, s2):
    half = MLA_ROPE // 2
    up = pltpu.roll(slab, half, 1)
    down = pltpu.roll(slab, LANES - half, 1)
    return slab * c + down * s1 + up * s2


def _mla_prep_kernel(ql_ref, kvl_ref, kpe_ref, qg_ref, kvg_ref, wq_ref, wk_ref, wv_ref,
                     c_ref, s1_ref, s2_ref, q_ref, k_ref, vt_ref, *, q_scale):
    c, s1, s2 = c_ref[...], s1_ref[...], s2_ref[...]
    qn = _rms_norm(ql_ref[...], qg_ref[...], RMS_EPS).astype(BF16)
    kvn = _rms_norm(kvl_ref[...], kvg_ref[...], RMS_EPS).astype(BF16)
    q = _dot(qn, wq_ref[...])
    k = _dot(kvn, wk_ref[...])
    _store_transposed(vt_ref, _dot(kvn, wv_ref[...]))
    kpe = _rope_slab(kpe_ref[...], c, s1, s2)
    for h in range(MLA_HEADS):
        sl = slice(h * LANES, (h + 1) * LANES)
        q_ref[:, sl] = (_rope_slab(q[:, sl], c, s1, s2) * q_scale).astype(q_ref.dtype)
        k_ref[:, sl] = (k[:, sl] + kpe).astype(k_ref.dtype)


def _mla_prep(slab_a, q_gain, kv_gain, wq, wk, wv, rope_c, rope_s1, rope_s2, b, seq):
    m = slab_a.shape[0]
    tm = min(KV_TILE, seq)
    per_seq = seq // tm
    rank = q_gain.shape[1]
    row = lambda j: (lambda i: (i, j))
    tab = lambda i: (i % per_seq, 0)
    const = lambda i: (0, 0)
    hw = MLA_HEADS * LANES
    vt_spec, vt_shape = _transposed_out(b, seq, wv.shape[1], tm)
    return pl.pallas_call(
        functools.partial(_mla_prep_kernel, q_scale=float((MLA_NOPE + MLA_ROPE) ** -0.5 * LOG2E)),
        grid=(m // tm,),
        in_specs=[pl.BlockSpec((tm, rank), row(0)), pl.BlockSpec((tm, rank), row(1)),
                  pl.BlockSpec((tm, LANES), row(2 * rank // LANES)),
                  pl.BlockSpec((1, rank), const), pl.BlockSpec((1, rank), const),
                  pl.BlockSpec(wq.shape, const), pl.BlockSpec(wk.shape, const), pl.BlockSpec(wv.shape, const),
                  pl.BlockSpec((tm, LANES), tab), pl.BlockSpec((tm, LANES), tab), pl.BlockSpec((tm, LANES), tab)],
        out_specs=[pl.BlockSpec((tm, hw), row(0)), pl.BlockSpec((tm, hw), row(0)), vt_spec],
        out_shape=[jax.ShapeDtypeStruct((m, hw), BF16), jax.ShapeDtypeStruct((m, hw), BF16), vt_shape],
        compiler_params=_params("parallel"),
        name="mla_prep",
    )(slab_a, slab_a, slab_a, q_gain, kv_gain, wq, wk, wv, rope_c, rope_s1, rope_s2)


def _flash_scratch(n_streams, v_rows, tq, tk):
    scores = pltpu.VMEM((n_streams, tk, tq), F32)
    stat = pltpu.VMEM((n_streams, 1, tq), F32)
    probs = pltpu.VMEM((n_streams, tk, tq), BF16)
    slot = [scores, probs, stat]
    return slot + slot + [stat, pltpu.VMEM((n_streams, v_rows + BF16_ROWS, tq), F32)]


def _chunk_rows(c):
    return slice(c * KEY_CHUNK, (c + 1) * KEY_CHUNK)


def _fold_rows(x):
    return x.reshape(x.shape[0] // SUBLANES, SUBLANES, x.shape[1])


def _flash_transposed(diagonal_tile, n_streams, q_start, tq, tk, key_operand, query_operand, values, offset, scratch,
                      rest_tile, rest_count):
    assert tq == tk, "the diagonal tile is taken to start at the first query of the tile"
    slot_a, slot_b, (m_ref, acc_ref) = scratch[0:3], scratch[3:6], scratch[6:]
    n_chunks = tk // KEY_CHUNK
    for i in range(n_streams):
        m_ref[i] = jnp.full((1, tq), NEG_INF, F32)
        acc_ref[i] = jnp.zeros(acc_ref.shape[1:], F32)

    def column_max(s_ref, i):
        part = jnp.full((SUBLANES, tq), NEG_INF, F32)
        for c in range(n_chunks):
            part = jnp.maximum(part, jnp.max(_fold_rows(s_ref[i, _chunk_rows(c), :]), axis=0))
        return jnp.max(part, axis=0, keepdims=True)

    def stage1(j, slot, i):
        slot[0][i] = _dot(key_operand(j, i), query_operand(j, i))

    def blocks():
        for c in range(n_chunks):
            for v in range(tq // LANES):
                first_key, last_key = c * KEY_CHUNK, (c + 1) * KEY_CHUNK - 1
                first_query, last_query = v * LANES, (v + 1) * LANES - 1
                kind = "visible" if last_key <= first_query else "hidden" if first_key > last_query else "mixed"
                yield _chunk_rows(c), slice(first_query, last_query + 1), first_query - first_key, kind

    def stage2(j, slot, i, diagonal):
        s_ref, p_ref, alpha_ref = slot
        if diagonal:
            rel = (lax.broadcasted_iota(jnp.int32, (KEY_CHUNK, LANES), 0)
                   - lax.broadcasted_iota(jnp.int32, (KEY_CHUNK, LANES), 1))
            parts = [jnp.full((SUBLANES, LANES), NEG_INF, F32) for _ in range(tq // LANES)]
            for rows, lanes, bound, kind in blocks():
                if kind == "hidden":
                    continue
                s = s_ref[i, rows, lanes]
                if kind == "mixed":
                    s = jnp.where(rel <= bound, s, NEG_INF)
                    s_ref[i, rows, lanes] = s
                v = lanes.start // LANES
                parts[v] = jnp.maximum(parts[v], jnp.max(_fold_rows(s), axis=0))
            mx = jnp.max(jnp.concatenate(parts, axis=1), axis=0, keepdims=True)
        else:
            mx = column_max(s_ref, i)
        off = offset(j, i)
        m_prev = m_ref[i]
        if off is None:
            m_next = jnp.maximum(m_prev, mx)
            shift = m_next
        else:
            m_next = jnp.maximum(m_prev, mx + off)
            shift = m_next - off
        alpha = jnp.exp2(m_prev - m_next)
        if diagonal:
            for rows, lanes, _, kind in blocks():
                if kind == "hidden":
                    p_ref[i, rows, lanes] = jnp.zeros((KEY_CHUNK, LANES), BF16)
                else:
                    p_ref[i, rows, lanes] = jnp.exp2(s_ref[i, rows, lanes] - shift[:, lanes]).astype(BF16)
        else:
            for c in range(n_chunks):
                p_ref[i, _chunk_rows(c), :] = jnp.exp2(s_ref[i, _chunk_rows(c), :] - shift).astype(BF16)
        m_ref[i] = m_next
        alpha_ref[i] = alpha

    def step(accumulate=None, produce=(), exponentiate=None, diagonal=False):
        products = []
        for i in range(n_streams + 1):
            if i < n_streams:
                if accumulate is not None:
                    tile, slot = accumulate
                    products.append(_dot(_values_and_ones(values(tile, i)), slot[1][i]))
                for tile, slot in produce:
                    stage1(tile, slot, i)
                if exponentiate is not None:
                    stage2(*exponentiate, i, diagonal)
            if accumulate is not None and i > 0:
                acc_ref[i - 1] = accumulate[1][2][i - 1] * acc_ref[i - 1] + products[i - 1]

    step(produce=((diagonal_tile, slot_a), (rest_tile(0), slot_b)), exponentiate=(diagonal_tile, slot_a),
         diagonal=True)
    last = rest_count(m_ref)

    def tile_at(position):
        return rest_tile(jnp.clip(position, 1, jnp.maximum(last, 1)) - 1)

    def diag_or_rest(position):
        return jnp.where(position == 0, diagonal_tile, tile_at(position))

    def pair(t, carry):
        p1 = 2 * t + 1
        step((diag_or_rest(p1 - 1), slot_a), ((tile_at(p1 + 1), slot_a),), (tile_at(p1), slot_b))
        step((tile_at(p1), slot_b), ((tile_at(p1 + 2), slot_b),), (tile_at(p1 + 1), slot_a))
        return carry

    lax.fori_loop(0, last // 2, pair, 0)

    @pl.when(last % 2 == 1)
    def _():
        step(accumulate=(diag_or_rest(last - 1), slot_a), exponentiate=(tile_at(last), slot_b))
        step(accumulate=(tile_at(last), slot_b))

    @pl.when(last % 2 == 0)
    def _():
        step(accumulate=(diag_or_rest(last), slot_a))


def _flash_result(scratch, i):
    acc = scratch[-1][i]
    v_rows = acc.shape[0] - BF16_ROWS
    return acc[:v_rows] / acc[v_rows:v_rows + 1]


def _kv_rows(j, tk):
    return pl.ds(pl.multiple_of(j * tk, tk), tk)


def _transposed_bf16(x):
    return x.astype(F32).T.astype(BF16)


def _alibi_rows(coef, tq):
    c = jnp.zeros((1, tq), F32) + coef
    hi = c.astype(BF16).astype(F32)
    rest = c - hi
    mid = rest.astype(BF16).astype(F32)
    lo = rest - mid
    zero = jnp.zeros((1, tq), F32)
    return jnp.concatenate([POS_SPLIT * hi, POS_SPLIT * mid, POS_SPLIT * lo, hi, mid, lo, zero, zero], axis=0)


def _augmented_query(q_t, feature_rows):
    tq = q_t.shape[1]
    pad = jnp.zeros((MXU_DEPTH - LANES - FEATURE_ROWS, tq), BF16)
    return jnp.concatenate([q_t, feature_rows.astype(BF16), pad], axis=0)


MLA_STEP_HEADS = 4


def _mla_attn_kernel(q_ref, k_ref, vt_ref, o_ref, *scratch, tq, tk):
    q_start = pl.program_id(2) * tq
    n = MLA_STEP_HEADS
    queries = [_transposed_bf16(q_ref[0, :, hh * LANES:(hh + 1) * LANES]) for hh in range(n)]
    n_full = q_start // tk
    _flash_transposed(
        n_full, n, q_start, tq, tk,
        lambda j, i: k_ref[0, _kv_rows(j, tk), i * LANES:(i + 1) * LANES],
        lambda j, i: queries[i],
        lambda j, i: vt_ref[0, j, i * HEAD_DIM:(i + 1) * HEAD_DIM, :],
        lambda j, i: None, scratch,
        rest_tile=lambda k: k, rest_count=lambda m_ref: n_full)
    for pair in range(n // 2):
        o_t = jnp.concatenate([_flash_result(scratch, 2 * pair), _flash_result(scratch, 2 * pair + 1)], axis=0)
        o_ref[0, :, pair * LANES:(pair + 1) * LANES] = o_t.T.astype(o_ref.dtype)


def _mla_attn(q, k, vt):
    b, s, _ = q.shape
    tq, tk = min(FLASH_Q_TILE, s), min(KV_TILE, s)
    n = MLA_STEP_HEADS
    groups = MLA_HEADS // n
    return pl.pallas_call(
        functools.partial(_mla_attn_kernel, tq=tq, tk=tk),
        grid=(b, groups, s // tq),
        in_specs=[pl.BlockSpec((1, tq, n * LANES), lambda bi, p, i: (bi, i, p)),
                  pl.BlockSpec((1, s, n * LANES), lambda bi, p, i: (bi, 0, p)),
                  pl.BlockSpec((1, s // tk, n * HEAD_DIM, tk), lambda bi, p, i: (bi, 0, p, 0))],
        out_specs=pl.BlockSpec((1, tq, n * HEAD_DIM), lambda bi, p, i: (bi, i, p)),
        out_shape=jax.ShapeDtypeStruct((b, s, MLA_HEADS * HEAD_DIM), BF16),
        scratch_shapes=_flash_scratch(n, HEAD_DIM, tq, tk),
        compiler_params=_params("parallel", "parallel", "arbitrary"),
        name="mla_attn",
    )(q, k, vt)


def _gelu_tanh(x):
    return 0.5 * x * (1.0 + jnp.tanh(math.sqrt(2.0 / math.pi) * (x + 0.044715 * (x * x * x))))


def _compress_kernel(x_ref, pos_ref, w1a_ref, w1b_ref, w2_ref, k_ref, vt_ref, *, n_real):
    x = x_ref[0]
    n = x.shape[0]
    first = _dot(x, w1a_ref[...])
    second = _dot(x, w1b_ref[...])
    pos_hi, pos_lo = _split_bf16(pos_ref[...])
    bias = (_dot(pos_hi[:8], w1a_ref[...]) + _dot(pos_lo[:8], w1a_ref[...])
            + _dot(pos_hi[8:], w1b_ref[...]) + _dot(pos_lo[8:], w1b_ref[...]))[:1]
    pre = first + pltpu.roll(second, n - 1, 0) + bias
    out = _dot(_gelu_tanh(pre).astype(BF16), w2_ref[...])
    real = lax.broadcasted_iota(jnp.int32, out.shape, 0) < n_real
    out = jnp.where(real, out, 0.0)
    half = out.shape[1] // 2
    k_ref[0] = out[:, :half].astype(k_ref.dtype)
    vt_ref[0] = out[:, half:].T.astype(vt_ref.dtype)


def _compress(x_chunks, pos_exp, w1a, w1b, w2, n_real):
    b, n, width = x_chunks.shape
    half = w2.shape[1] // 2
    const = lambda bi: (0, 0)
    return pl.pallas_call(
        functools.partial(_compress_kernel, n_real=n_real),
        grid=(b,),
        in_specs=[pl.BlockSpec((1, n, width), lambda bi: (bi, 0, 0)),
                  pl.BlockSpec(pos_exp.shape, const), pl.BlockSpec(w1a.shape, const),
                  pl.BlockSpec(w1b.shape, const), pl.BlockSpec(w2.shape, const)],
        out_specs=[pl.BlockSpec((1, n, half), lambda bi: (bi, 0, 0)), pl.BlockSpec((1, half, n), lambda bi: (bi, 0, 0))],
        out_shape=[jax.ShapeDtypeStruct((b, n, half), BF16), jax.ShapeDtypeStruct((b, half, n), BF16)],
        compiler_params=_params("parallel"),
        name="nsa_compress",
    )(x_chunks, pos_exp, w1a, w1b, w2)


def _nsa_head_slope(h):
    return float(2.0 ** (-8.0 * (h + 1) / NSA_HEADS))


def _nsa_queries(q_ref, g, scale):
    out = []
    for hg in range(NSA_HG):
        h = g * NSA_HG + hg
        slab = q_ref[0, :, (h // 2) * LANES:(h // 2 + 1) * LANES].astype(F32) * scale
        out.append(_move_head(slab, h % 2, g))
    return out


def _select_blocks(imp_t, q_pos):
    n_blocks = imp_t.shape[0]
    blk = lax.broadcasted_iota(jnp.int32, imp_t.shape, 0)
    cur = q_pos // SEL_LEN
    forced = jnp.where(blk == 0, 1.0, 0.0) + jnp.where(blk == cur, 1.0, 0.0) + jnp.where(blk == cur - 1, 1.0, 0.0)
    forced = jnp.minimum(forced, 1.0)
    val = jnp.where(blk <= cur, imp_t + FORCE_BONUS * forced, NEG_INF)
    chosen = jnp.zeros(imp_t.shape, F32)
    for _ in range(SEL_TOPK):
        top = jnp.max(val, axis=0, keepdims=True)
        first = jnp.min(jnp.where(val == top, blk, n_blocks), axis=0, keepdims=True)
        hit = blk == first
        chosen = jnp.where(hit, 1.0, chosen)
        val = jnp.where(hit, -jnp.inf, val)
    return jnp.where(chosen > 0.5, 0.0, NEG_INF)


def _masked_softmax_pass(s_ref, p_ref, tiles, tq):
    part = jnp.full((SUBLANES, tq), NEG_INF, F32)
    for t, rows, keep, off in tiles:
        for c in range(rows // KEY_CHUNK):
            s = jnp.where(keep(_chunk_rows(c)), s_ref[t, _chunk_rows(c), :], NEG_INF)
            s_ref[t, _chunk_rows(c), :] = s
            part = jnp.maximum(part, jnp.max(_fold_rows(s), axis=0) + off)
    m = jnp.max(part, axis=0, keepdims=True)
    for t, rows, keep, off in tiles:
        shift = m - off
        for c in range(rows // KEY_CHUNK):
            p_ref[t, _chunk_rows(c), :] = jnp.exp2(s_ref[t, _chunk_rows(c), :] - shift).astype(BF16)
    return m > 0.5 * NEG_INF


def _values_and_ones(vt):
    return jnp.concatenate([vt, jnp.ones((BF16_ROWS, vt.shape[1]), BF16)], axis=0)


def _nsa_cmp_win_kernel(q_ref, kc_ref, vct_ref, kw_ref, vwt_ref, feat_ref, ovt_ref, oc_ref, ow_ref, sel_ref, any_ref,
                        end_ref, rel_ref, sc_ref, sw_ref, pc_ref, pw_ref, *, tq, tk):
    qi = pl.program_id(1)
    q_start = qi * tq
    n_cmp = kc_ref.shape[1]
    end_ref[...] = (lax.broadcasted_iota(jnp.int32, (n_cmp, tq), 0) * CMP_STRIDE + (CMP_LEN - 1)
                    - lax.broadcasted_iota(jnp.int32, (n_cmp, tq), 1))
    rel_ref[...] = (lax.broadcasted_iota(jnp.int32, (tk, tq), 0) - lax.broadcasted_iota(jnp.int32, (tk, tq), 1))
    prev_tile = jnp.maximum(qi - 1, 0)
    prev_bound = jnp.where(qi >= 1, 0, tk)
    zeros = jnp.zeros((FEATURE_ROWS - SUBLANES, tq), F32)
    k_cmp = jnp.concatenate([kc_ref[0], feat_ref[0:n_cmp, :]], axis=1)
    k_win = [jnp.concatenate([kw_ref[0, _kv_rows(j, tk), :], feat_ref[...]], axis=1) for j in (prev_tile, qi)]
    q_pos = q_start + lax.broadcasted_iota(jnp.int32, (1, tq), 1)
    for g in range(NSA_GROUPS):
        queries = [_transposed_bf16(q) for q in _nsa_queries(q_ref, g, HEAD_DIM ** -0.5 * LOG2E)]
        group_rows = slice(g * HEAD_DIM, (g + 1) * HEAD_DIM)
        imp_t = jnp.zeros((LANES, tq), F32)
        out_c, out_w = [], []
        for hg in range(NSA_HG):
            coef = _nsa_head_slope(g * NSA_HG + hg) * LOG2E
            buf = hg % 2
            sc, sw, pc, pw = sc_ref.at[buf], sw_ref.at[buf], pc_ref.at[buf], pw_ref.at[buf]
            cmp_query = _augmented_query(queries[hg], jnp.concatenate([_alibi_rows(CMP_STRIDE * coef, tq), zeros], 0))
            win_query = _augmented_query(queries[hg], jnp.concatenate([_alibi_rows(coef, tq), zeros], 0))
            sc[0] = _dot(k_cmp, cmp_query)
            sw[0] = _dot(k_win[0], win_query)
            sw[1] = _dot(k_win[1], win_query)
            has_any = _masked_softmax_pass(sc, pc, [(0, n_cmp, lambda r: end_ref[r, :] <= q_start, 0.0)], tq)
            acc = _dot(_values_and_ones(vct_ref[0, group_rows, :]), pc[0])
            inv = jnp.where(has_any, 1.0 / acc[HEAD_DIM:HEAD_DIM + 1], 0.0)
            out_c.append(acc[:HEAD_DIM] * inv)
            imp_t = imp_t + _dot(ovt_ref[...], pc[0]) * inv
            _masked_softmax_pass(
                sw, pw, [(0, tk, lambda r: rel_ref[r, :] > prev_bound, -coef * tk),
                         (1, tk, lambda r: rel_ref[r, :] <= 0, 0.0)], tq)
            acc = (_dot(_values_and_ones(vwt_ref[0, prev_tile, group_rows, :]), pw[0])
                   + _dot(_values_and_ones(vwt_ref[0, qi, group_rows, :]), pw[1]))
            out_w.append(acc[:HEAD_DIM] / acc[HEAD_DIM:HEAD_DIM + 1])
        for pair in range(NSA_HG // 2):
            cols = slice((g * 2 + pair) * LANES, (g * 2 + pair + 1) * LANES)
            oc_ref[0, :, cols] = jnp.concatenate(out_c[2 * pair:2 * pair + 2], axis=0).T
            ow_ref[0, :, cols] = jnp.concatenate(out_w[2 * pair:2 * pair + 2], axis=0).T
        bias_t = _select_blocks(imp_t, q_pos)
        sel_ref[0, g * LANES:(g + 1) * LANES, :] = bias_t
        any_ref[0, 0, g * LANES:(g + 1) * LANES, :] = jnp.broadcast_to(
            jnp.max(bias_t, axis=1, keepdims=True), (LANES, LANES))


def _nsa_cmp_win(slab_b, cmp_k, cmp_vt, vt_nsa, key_features, overlap_t, seq):
    b = slab_b.shape[0]
    tq, tk = min(FLASH_Q_TILE, seq), min(KV_TILE, seq)
    assert tq == tk == WINDOW, "the window branch is written for one previous and one diagonal key tile"
    n_cmp = cmp_k.shape[1]
    qw = NSA_HEADS * HEAD_DIM
    base = qw // LANES
    tile = lambda bi, i: (bi, i, 0)
    return pl.pallas_call(
        functools.partial(_nsa_cmp_win_kernel, tq=tq, tk=tk),
        grid=(b, seq // tq),
        in_specs=[pl.BlockSpec((1, tq, qw), tile),
                  pl.BlockSpec((1, n_cmp, LANES), lambda bi, i: (bi, 0, 0)),
                  pl.BlockSpec((1, LANES, n_cmp), lambda bi, i: (bi, 0, 0)),
                  pl.BlockSpec((1, seq, LANES), lambda bi, i: (bi, 0, base + 1)),
                  pl.BlockSpec((1, seq // tk, LANES, tk), lambda bi, i: (bi, 0, 1, 0)),
                  pl.BlockSpec(key_features.shape, lambda bi, i: (0, 0)),
                  pl.BlockSpec(overlap_t.shape, lambda bi, i: (0, 0))],
        out_specs=[pl.BlockSpec((1, tq, qw), tile), pl.BlockSpec((1, tq, qw), tile),
                   pl.BlockSpec((1, NSA_GROUPS * LANES, tq), lambda bi, i: (bi, 0, i)),
                   pl.BlockSpec((1, 1, NSA_GROUPS * LANES, LANES), lambda bi, i: (bi, i, 0, 0))],
        out_shape=[jax.ShapeDtypeStruct((b, seq, qw), F32), jax.ShapeDtypeStruct((b, seq, qw), F32),
                   jax.ShapeDtypeStruct((b, NSA_GROUPS * LANES, seq), F32),
                   jax.ShapeDtypeStruct((b, seq // tq, NSA_GROUPS * LANES, LANES), F32)],
        scratch_shapes=[pltpu.VMEM((n_cmp, tq), jnp.int32), pltpu.VMEM((tk, tq), jnp.int32),
                        pltpu.VMEM((2, 1, n_cmp, tq), F32), pltpu.VMEM((2, 2, tk, tq), F32),
                        pltpu.VMEM((2, 1, n_cmp, tq), BF16), pltpu.VMEM((2, 2, tk, tq), BF16)],
        compiler_params=_params("parallel", "arbitrary"),
        name="nsa_cmp_win",
    )(slab_b, cmp_k, cmp_vt, slab_b, vt_nsa, key_features, overlap_t)


def _nsa_sel_kernel(tiles_ref, counts_ref, q_ref, k_ref, vt_ref, feat_ref, sel_ref, oc_ref, ow_ref, gate_ref, gx_ref,
                    o_ref, *scratch, tq, tk, max_tiles):
    qi = pl.program_id(1)
    q_start = qi * tq
    diagonal_tile = q_start // tk
    out_slabs = []
    for g in range(NSA_GROUPS):
        queries = [_transposed_bf16(q) for q in _nsa_queries(q_ref, g, HEAD_DIM ** -0.5 * LOG2E)]
        coefs = [_nsa_head_slope(g * NSA_HG + hg) * LOG2E for hg in range(NSA_HG)]
        alibi = [_alibi_rows(c, tq) for c in coefs]
        entry = (pl.program_id(0) * pl.num_programs(1) + qi) * NSA_GROUPS + g
        n_active = counts_ref[entry]

        def listed_tile(k, entry=entry):
            return tiles_ref[entry * max_tiles + jnp.minimum(k, max_tiles - 1)]

        def key_operand(j, i):
            return jnp.concatenate([k_ref[0, _kv_rows(j, tk), :], feat_ref[...]], axis=1)

        def query_operand(j, i, g=g, queries=queries, alibi=alibi):
            first_block = pl.multiple_of(g * LANES + j * BLOCKS_PER_TILE, BLOCKS_PER_TILE)
            blocks = sel_ref[0, pl.ds(first_block, BLOCKS_PER_TILE), :]
            return _augmented_query(queries[i], jnp.concatenate([alibi[i], blocks], axis=0))

        _flash_transposed(
            diagonal_tile, NSA_HG, q_start, tq, tk, key_operand, query_operand,
            lambda j, i, g=g: vt_ref[0, j, g * HEAD_DIM:(g + 1) * HEAD_DIM, :],
            lambda j, i, coefs=coefs: coefs[i] * (j * tk - q_start).astype(F32), scratch,
            rest_tile=listed_tile, rest_count=lambda m_ref, n_active=n_active: n_active)
        heads = [_flash_result(scratch, hg) for hg in range(NSA_HG)]
        for pair in range(NSA_HG // 2):
            out_slabs.append(jnp.concatenate(heads[2 * pair:2 * pair + 2], axis=0).T)
    gates = jax.nn.sigmoid(gate_ref[...])
    g_hi, g_lo = _split_bf16(gates)
    width = NSA_HEADS * HEAD_DIM
    for i, o_sel in enumerate(out_slabs):
        mixed = None
        for branch, o_branch in enumerate((oc_ref[0, :, i * LANES:(i + 1) * LANES], o_sel,
                                           ow_ref[0, :, i * LANES:(i + 1) * LANES])):
            gx = gx_ref[:, branch * width + i * LANES:branch * width + (i + 1) * LANES]
            term = (_dot(g_hi, gx) + _dot(g_lo, gx)) * o_branch
            mixed = term if mixed is None else mixed + term
        o_ref[0, :, i * LANES:(i + 1) * LANES] = mixed.astype(o_ref.dtype)


def _active_key_tiles(block_any, tq, tk):
    b, n_q = block_any.shape[:2]
    max_tiles = LANES // BLOCKS_PER_TILE
    hit = block_any[..., 0].reshape(b, n_q, NSA_GROUPS, max_tiles, BLOCKS_PER_TILE).max(axis=-1) > 0.5 * NEG_INF
    before_diagonal = jnp.arange(max_tiles)[None, :] < (jnp.arange(n_q) * tq // tk)[:, None]
    hit = hit & before_diagonal[None, :, None, :]
    rank = jnp.cumsum(hit.astype(jnp.int32), axis=-1) - 1
    slots = jnp.arange(max_tiles, dtype=jnp.int32)
    in_slot = hit[..., None, :] & (rank[..., None, :] == slots[:, None])
    tiles = jnp.sum(jnp.where(in_slot, slots, 0), axis=-1)
    return tiles.astype(jnp.int32).reshape(-1), hit.sum(axis=-1).astype(jnp.int32).reshape(-1), max_tiles


def _nsa_sel(slab_b, vt, key_features, sel_bias_t, block_any, o_cmp, o_win, slab_a, gate_expand, seq, gate_col_block):
    b = slab_b.shape[0]
    tq, tk = min(FLASH_Q_TILE, seq), min(KV_TILE, seq)
    qw = NSA_HEADS * HEAD_DIM
    base = qw // LANES
    per_seq = seq // tq
    tiles, counts, max_tiles = _active_key_tiles(block_any, tq, tk)
    tile = lambda bi, i, *_: (bi, i, 0)
    grid_spec = pltpu.PrefetchScalarGridSpec(
        num_scalar_prefetch=2,
        grid=(b, seq // tq),
        in_specs=[pl.BlockSpec((1, tq, qw), tile),
                  pl.BlockSpec((1, seq, LANES), lambda bi, i, *_: (bi, 0, base)),
                  pl.BlockSpec((1, seq // tk, LANES, tk), lambda bi, i, *_: (bi, 0, 0, 0)),
                  pl.BlockSpec(key_features.shape, lambda bi, i, *_: (0, 0)),
                  pl.BlockSpec((1, NSA_GROUPS * LANES, tq), lambda bi, i, *_: (bi, 0, i)),
                  pl.BlockSpec((1, tq, qw), tile), pl.BlockSpec((1, tq, qw), tile),
                  pl.BlockSpec((tq, LANES), lambda bi, i, *_: (bi * per_seq + i, gate_col_block)),
                  pl.BlockSpec(gate_expand.shape, lambda bi, i, *_: (0, 0))],
        out_specs=pl.BlockSpec((1, tq, qw), tile),
        scratch_shapes=_flash_scratch(NSA_HG, HEAD_DIM, tq, tk))
    return pl.pallas_call(
        functools.partial(_nsa_sel_kernel, tq=tq, tk=tk, max_tiles=max_tiles),
        grid_spec=grid_spec,
        out_shape=jax.ShapeDtypeStruct((b, seq, qw), BF16),
        compiler_params=_params("parallel", "arbitrary"),
        name="nsa_sel",
    )(tiles, counts, slab_b, slab_b, vt, key_features, sel_bias_t, o_cmp, o_win, slab_a, gate_expand)


DIFF_STEP_HEADS = 2


SKIP_GAP = 180.0
NORM_SLACK = 1.01


def _diff_attn_kernel(slope_ref, lam_ref, q_ref, k_ref, vt_ref, feat_ref, g_ref, o_ref, knorm_ref, *scratch,
                      tq, tk, lam_init):
    first_head = pl.program_id(1) * DIFF_STEP_HEADS
    q_start = pl.program_id(2) * tq
    n_full = q_start // tk
    n_tiles = k_ref.shape[1] // tk

    @pl.when(pl.program_id(2) == 0)
    def _():
        for hh in range(DIFF_STEP_HEADS):
            for j in range(n_tiles):
                k = k_ref[0, j * tk:(j + 1) * tk, hh * LANES:(hh + 1) * LANES].astype(F32)
                knorm_ref[hh * n_tiles + j] = jnp.sqrt(jnp.max(jnp.sum(k * k, axis=1, keepdims=True)))

    zeros = jnp.zeros((FEATURE_ROWS - SUBLANES, tq), F32)
    coefs, queries, q_norms = [], [], []
    for hh in range(DIFF_STEP_HEADS):
        coef = slope_ref[first_head + hh] * LOG2E
        q = q_ref[0, :, hh * LANES:(hh + 1) * LANES].astype(F32) * (HEAD_DIM ** -0.5 * LOG2E)
        features = jnp.concatenate([_alibi_rows(coef, tq), zeros], axis=0)
        coefs.append(coef)
        for half in range(2):
            q_half = _keep_half(q, half)
            queries.append(_augmented_query(_transposed_bf16(q_half), features))
            q_norms.append(NORM_SLACK * jnp.sqrt(jnp.max(jnp.sum(q_half * q_half, axis=1, keepdims=True))))

    def head_lanes(i):
        return slice((i // 2) * LANES, (i // 2 + 1) * LANES)

    def offset(j, i):
        return coefs[i // 2] * (j * tk - q_start).astype(F32)

    nearest = jnp.maximum(n_full - 1, 0)
    first_needed = []

    def rest_count(m_ref):
        floors = [jnp.min(m_ref[i]) - SKIP_GAP for i in range(2 * DIFF_STEP_HEADS)]

        def body(j, first):
            needed = jnp.bool_(False)
            for i in range(2 * DIFF_STEP_HEADS):
                bound = q_norms[i] * knorm_ref[(i // 2) * n_tiles + j] + coefs[i // 2] * (tk - 1) + offset(j, i)
                needed = jnp.logical_or(needed, bound >= floors[i])
            return jnp.where(needed, jnp.minimum(first, j), first)

        first_needed.append(lax.fori_loop(0, nearest, body, nearest))
        return jnp.minimum(n_full, 1) + nearest - first_needed[0]

    def rest_tile(k):
        if isinstance(k, int):
            return nearest
        return jnp.where(k == 0, nearest, first_needed[0] + k - 1)

    _flash_transposed(
        n_full, 2 * DIFF_STEP_HEADS, q_start, tq, tk,
        lambda j, i: jnp.concatenate([k_ref[0, _kv_rows(j, tk), head_lanes(i)], feat_ref[...]], axis=1),
        lambda j, i: queries[i],
        lambda j, i: vt_ref[0, j, head_lanes(i), :],
        offset, scratch, rest_tile, rest_count)
    lam_vec = lam_ref[...]
    lam = (jnp.exp(jnp.sum(lam_vec[0:1] * lam_vec[1:2], axis=1, keepdims=True))
           - jnp.exp(jnp.sum(lam_vec[2:3] * lam_vec[3:4], axis=1, keepdims=True)) + lam_init)
    for hh in range(DIFF_STEP_HEADS):
        o = (_flash_result(scratch, 2 * hh) - lam * _flash_result(scratch, 2 * hh + 1)).T
        o_ref[0, :, hh * LANES:(hh + 1) * LANES] = (
            _rms_norm(o, g_ref[...], RMS_EPS) * (1.0 - lam_init)).astype(o_ref.dtype)


def _diff_attn(qk, vt, key_features, slopes, lam_vecs, subln_g, lam_init):
    b, s, _ = qk.shape
    tq, tk = min(FLASH_Q_TILE, s), min(KV_TILE, s)
    n = DIFF_STEP_HEADS
    groups = DIFF_HEADS // n
    smem = pl.BlockSpec(memory_space=pltpu.SMEM)
    return pl.pallas_call(
        functools.partial(_diff_attn_kernel, tq=tq, tk=tk, lam_init=lam_init),
        grid=(b, groups, s // tq),
        in_specs=[smem, pl.BlockSpec(lam_vecs.shape, lambda bi, h, i: (0, 0)),
                  pl.BlockSpec((1, tq, n * LANES), lambda bi, h, i: (bi, i, h)),
                  pl.BlockSpec((1, s, n * LANES), lambda bi, h, i: (bi, 0, groups + h)),
                  pl.BlockSpec((1, s // tk, n * LANES, tk), lambda bi, h, i: (bi, 0, h, 0)),
                  pl.BlockSpec(key_features.shape, lambda bi, h, i: (0, 0)),
                  pl.BlockSpec((1, LANES), lambda bi, h, i: (0, 0))],
        out_specs=pl.BlockSpec((1, tq, n * LANES), lambda bi, h, i: (bi, i, h)),
        out_shape=jax.ShapeDtypeStruct((b, s, DIFF_HEADS * LANES), BF16),
        scratch_shapes=[pltpu.SMEM((n * (s // tk),), F32)] + _flash_scratch(2 * n, LANES, tq, tk),
        compiler_params=_params("parallel", "parallel", "arbitrary"),
        name="diff_attn",
    )(slopes, lam_vecs, qk, qk, vt, key_features, subln_g)


def _out_ln_kernel(*refs, n_in):
    a_refs, w_refs = refs[:n_in], refs[n_in:2 * n_in]
    x_ref, g_ref, b_ref, o_ref = refs[2 * n_in:]
    y = None
    for a_ref, w_ref in zip(a_refs, w_refs):
        t = _dot(a_ref[...], w_ref[...])
        y = t if y is None else y + t
    o_ref[...] = _layer_norm(DN_ALPHA * x_ref[...] + y, g_ref[...], b_ref[...])


def _out_ln(acts, weights, x, g, b):
    m, d = x.shape
    tm = min(ROW_TILE, m)
    row = lambda i: (i, 0)
    const = lambda i: (0, 0)
    return pl.pallas_call(
        functools.partial(_out_ln_kernel, n_in=len(acts)),
        grid=(m // tm,),
        in_specs=[pl.BlockSpec((tm, a.shape[1]), row) for a in acts]
        + [pl.BlockSpec(w.shape, const) for w in weights]
        + [pl.BlockSpec((tm, d), row), pl.BlockSpec((1, d), const), pl.BlockSpec((1, d), const)],
        out_specs=pl.BlockSpec((tm, d), row),
        out_shape=jax.ShapeDtypeStruct((m, d), F32),
        compiler_params=_params("parallel"),
        name="out_proj_ln",
    )(*acts, *weights, x, g, b)


def _mlp_kernel(x_ref, wu_ref, wd_ref, g_ref, b_ref, o_ref, *, tf):
    x = x_ref[...]
    xb = x.astype(BF16)
    acc = None
    for f in range(wu_ref.shape[1] // tf):
        cols = slice(f * tf, (f + 1) * tf)
        hidden = jnp.maximum(_dot(xb, wu_ref[:, cols]), 0.0)
        part = _dot((hidden * hidden).astype(BF16), wd_ref[cols, :])
        acc = part if acc is None else acc + part
    o_ref[...] = _layer_norm(DN_ALPHA * x + acc, g_ref[...], b_ref[...])


def _mlp(x, w_up, w_down, g, b):
    m, d = x.shape
    ff = w_up.shape[1]
    tm, tf = min(ROW_TILE, m), min(FF_TILE, ff)
    resident = dict(pipeline_mode=pl.Buffered(1))
    return pl.pallas_call(
        functools.partial(_mlp_kernel, tf=tf),
        grid=(m // tm,),
        in_specs=[pl.BlockSpec((tm, d), lambda i: (i, 0)),
                  pl.BlockSpec((d, ff), lambda i: (0, 0), **resident),
                  pl.BlockSpec((ff, d), lambda i: (0, 0), **resident),
                  pl.BlockSpec((1, d), lambda i: (0, 0)), pl.BlockSpec((1, d), lambda i: (0, 0))],
        out_specs=pl.BlockSpec((tm, d), lambda i: (i, 0)),
        out_shape=jax.ShapeDtypeStruct((m, d), F32),
        compiler_params=_params("parallel"),
        name="mlp_ln",
    )(x, w_up, w_down, g, b)


def _pad_cols(w, width):
    return jnp.pad(w, ((0, 0), (0, width - w.shape[1])))


def _layer0_weights(w_in, w_uq, w_ukv, d_model):
    rank = d_model // 4
    kvw = NSA_GROUPS * HEAD_DIM
    o = np.cumsum([0, rank, rank, MLA_ROPE, NSA_HEADS * HEAD_DIM] + [kvw] * 6 + [3 * NSA_HEADS])
    seg = lambda i: w_in[:, o[i]:o[i + 1]]
    zeros = lambda n: jnp.zeros((w_in.shape[0], n), w_in.dtype)
    rope_slab = jnp.concatenate([zeros(MLA_NOPE), seg(2), zeros(LANES - MLA_NOPE - MLA_ROPE)], axis=1)
    w_a = jnp.concatenate([seg(0), seg(1), rope_slab, _pad_cols(seg(10), LANES)], axis=1)
    w_b = jnp.concatenate([seg(3), seg(6), seg(8)], axis=1)
    w_c = jnp.concatenate([seg(4), seg(5)], axis=1)
    w_vs = jnp.concatenate([seg(7), seg(9)], axis=1)
    wq = jnp.pad(w_uq.reshape(rank, MLA_HEADS, MLA_NOPE + MLA_ROPE),
                 ((0, 0), (0, 0), (0, LANES - MLA_NOPE - MLA_ROPE))).reshape(rank, MLA_HEADS * LANES)
    ukv = w_ukv.reshape(rank, MLA_HEADS, MLA_NOPE + HEAD_DIM)
    wk = jnp.pad(ukv[:, :, :MLA_NOPE], ((0, 0), (0, 0), (0, LANES - MLA_NOPE))).reshape(rank, MLA_HEADS * LANES)
    wv = ukv[:, :, MLA_NOPE:].reshape(rank, MLA_HEADS * HEAD_DIM)
    return [w.astype(BF16) for w in (w_a, w_b, w_c, w_vs, wq, wk, wv)]


def _rope_tables(seq):
    inv = 1.0 / (ROPE_THETA ** (jnp.arange(0, MLA_ROPE, 2, dtype=F32) / MLA_ROPE))
    ang = jnp.arange(seq, dtype=F32)[:, None] * inv[None, :]
    cos, sin = jnp.cos(ang), jnp.sin(ang)
    half = MLA_ROPE // 2
    z = lambda n: jnp.zeros((seq, n), F32)
    tail = LANES - MLA_NOPE - MLA_ROPE
    c = jnp.concatenate([jnp.ones((seq, MLA_NOPE), F32), cos, cos, z(tail)], axis=1)
    s1 = jnp.concatenate([z(MLA_NOPE), -sin, z(half), z(tail)], axis=1)
    s2 = jnp.concatenate([z(MLA_NOPE), z(half), sin, z(tail)], axis=1)
    return c, s1, s2


def _compress_weights(pos_k, w1_k, w2_k, pos_v, w1_v, w2_v):
    eye = jnp.eye(2 * NSA_GROUPS, dtype=F32)
    halves = []
    for a in range(CMP_LEN // CMP_STRIDE):
        rows = slice(a * CMP_STRIDE * HEAD_DIM, (a + 1) * CMP_STRIDE * HEAD_DIM)
        wk = w1_k[rows].reshape(CMP_STRIDE, HEAD_DIM, HEAD_DIM)
        wv = w1_v[rows].reshape(CMP_STRIDE, HEAD_DIM, HEAD_DIM)
        per_slot = jnp.stack([wk, wk, wv, wv], axis=0)
        full = jnp.einsum('st,srdj->rsdtj', eye, per_slot)
        halves.append(full.reshape(CMP_STRIDE * 4 * HEAD_DIM, 4 * HEAD_DIM).astype(BF16))
    w2 = jnp.einsum('st,sdj->sdtj', eye, jnp.stack([w2_k, w2_k, w2_v, w2_v])).reshape(4 * HEAD_DIM, 4 * HEAD_DIM)
    pos = jnp.concatenate([pos_k, pos_k, pos_v, pos_v], axis=1)
    pos = pos.reshape(CMP_LEN // CMP_STRIDE, 1, CMP_STRIDE * 4 * HEAD_DIM)
    pos = jnp.broadcast_to(pos, (pos.shape[0], 8, pos.shape[2])).reshape(-1, pos.shape[2])
    return pos, halves[0], halves[1], w2.astype(BF16)


def _overlap_table(n_cmp_pad, n_cmp):
    c0 = np.arange(n_cmp_pad)[None, :] * CMP_STRIDE
    s0 = np.arange(LANES)[:, None] * SEL_LEN
    ov = np.maximum(np.minimum(c0 + CMP_LEN, s0 + SEL_LEN) - np.maximum(c0, s0), 0) / CMP_LEN
    ov = ov * (np.arange(n_cmp_pad)[None, :] < n_cmp)
    return jnp.asarray(ov, BF16)


def _key_feature_table(tk):
    c = np.arange(tk)
    table = np.zeros((tk, LANES), np.float32)
    table[:, 0:3] = (c // POS_SPLIT)[:, None]
    table[:, 3:6] = (c % POS_SPLIT)[:, None]
    table[c, BLOCK_LANE0 + c // SEL_LEN] = 1.0
    return jnp.asarray(table, BF16)


def _gate_expand_table():
    width = NSA_HEADS * HEAD_DIM
    table = np.zeros((LANES, 3 * width), np.float32)
    for h in range(NSA_HEADS):
        for branch in range(3):
            table[h * 3 + branch, branch * width + h * HEAD_DIM:branch * width + (h + 1) * HEAD_DIM] = 1.0
    return jnp.asarray(table, BF16)


def _alibi_slopes(n):
    return jnp.asarray(2.0 ** (-8.0 * np.arange(1, n + 1) / n), dtype=F32)


def _layer0_mixer(x2, b, s, w_in, q_norm, w_uq, kv_norm, w_ukv, pos_k, w1_k, w2_k, pos_v, w1_v, w2_v, w_out):
    d = x2.shape[1]
    rank = d // 4
    w_a, w_b, w_c, w_vs, wq, wk, wv = _layer0_weights(w_in, w_uq, w_ukv, d)
    slab_a, slab_b, slab_c, vt_nsa = _project(x2, [w_a, w_b, w_c, w_vs], [F32, BF16, BF16, BF16],
                                              [False, False, False, True], b, s)
    rope_c, rope_s1, rope_s2 = _rope_tables(s)
    q, k, vt = _mla_prep(slab_a, q_norm.reshape(1, rank), kv_norm.reshape(1, rank), wq, wk, wv,
                         rope_c, rope_s1, rope_s2, b, s)
    o_mla = _mla_attn(q.reshape(b, s, -1), k.reshape(b, s, -1), vt)
    n_chunks = s // CMP_STRIDE
    n_cmp = (s - CMP_LEN) // CMP_STRIDE + 1
    pos, w1a, w1b, w2 = _compress_weights(pos_k, w1_k, w2_k, pos_v, w1_v, w2_v)
    cmp_k, cmp_vt = _compress(slab_c.reshape(b, n_chunks, CMP_STRIDE * slab_c.shape[1]), pos, w1a, w1b, w2, n_cmp)
    slab_b3 = slab_b.reshape(b, s, -1)
    key_features = _key_feature_table(min(KV_TILE, s))
    o_cmp, o_win, sel_bias_t, block_any = _nsa_cmp_win(slab_b3, cmp_k, cmp_vt, vt_nsa, key_features,
                                                       _overlap_table(n_chunks, n_cmp), s)
    o_nsa = _nsa_sel(slab_b3, vt_nsa, key_features, sel_bias_t, block_any, o_cmp, o_win, slab_a,
                     _gate_expand_table(), s, (2 * rank + LANES) // LANES)
    half = o_mla.shape[-1]
    w_out_b = w_out.astype(BF16)
    return [o_mla.reshape(b * s, half), o_nsa.reshape(b * s, -1)], [w_out_b[:half], w_out_b[half:]]


def _layer1_mixer(x2, b, s, w_qkv, lam_q1, lam_k1, lam_q2, lam_k2, subln_g, w_o, layer_idx):
    d = x2.shape[1]
    w = w_qkv.astype(BF16)
    qk, vt = _project(x2, [w[:, :2 * d], w[:, 2 * d:]], [BF16, BF16], [False, True], b, s)
    lam_init = 0.8 - 0.6 * math.exp(-0.3 * layer_idx)
    lam_vecs = jnp.stack([lam_q1, lam_k1, lam_q2, lam_k2]).astype(F32)
    o = _diff_attn(qk.reshape(b, s, -1), vt, _key_feature_table(min(KV_TILE, s)), _alibi_slopes(DIFF_HEADS),
                   lam_vecs, subln_g.reshape(1, -1), lam_init)
    return [o.reshape(b * s, -1)], [w_o.astype(BF16)]


def kernel(x, l0_w_in, l0_mla_q_norm, l0_mla_w_uq, l0_mla_kv_norm, l0_mla_w_ukv, l0_nsa_cmp_pos_k, l0_nsa_cmp_w1_k, l0_nsa_cmp_w2_k, l0_nsa_cmp_pos_v, l0_nsa_cmp_w1_v, l0_nsa_cmp_w2_v, l0_w_out, l0_ln_mix_g, l0_ln_mix_b, l0_w_up, l0_w_down, l0_ln_ffn_g, l0_ln_ffn_b, l1_w_qkv, l1_lam_q1, l1_lam_k1, l1_lam_q2, l1_lam_k2, l1_subln_g, l1_w_o, l1_ln_mix_g, l1_ln_mix_b, l1_w_up, l1_w_down, l1_ln_ffn_g, l1_ln_ffn_b):
    b, s, d = x.shape
    x2 = x.reshape(b * s, d)
    vec = lambda p: p.reshape(1, d)
    acts, weights = _layer0_mixer(x2, b, s, l0_w_in, l0_mla_q_norm, l0_mla_w_uq, l0_mla_kv_norm, l0_mla_w_ukv,
                                  l0_nsa_cmp_pos_k, l0_nsa_cmp_w1_k, l0_nsa_cmp_w2_k,
                                  l0_nsa_cmp_pos_v, l0_nsa_cmp_w1_v, l0_nsa_cmp_w2_v, l0_w_out)
    x2 = _out_ln(acts, weights, x2, vec(l0_ln_mix_g), vec(l0_ln_mix_b))
    x2 = _mlp(x2, l0_w_up.astype(BF16), l0_w_down.astype(BF16), vec(l0_ln_ffn_g), vec(l0_ln_ffn_b))
    acts, weights = _layer1_mixer(x2, b, s, l1_w_qkv, l1_lam_q1, l1_lam_k1, l1_lam_q2, l1_lam_k2,
                                  l1_subln_g, l1_w_o, 1)
    x2 = _out_ln(acts, weights, x2, vec(l1_ln_mix_g), vec(l1_ln_mix_b))
    x2 = _mlp(x2, l1_w_up.astype(BF16), l1_w_down.astype(BF16), vec(l1_ln_ffn_g), vec(l1_ln_ffn_b))
    return x2.reshape(b, s, d)
```

```python
import functools
import math

import jax
import jax.numpy as jnp
import numpy as np
from jax import lax
from jax.experimental import pallas as pl
from jax.experimental.pallas import tpu as pltpu

F32 = jnp.float32
BF16 = jnp.bfloat16

LANES = 128
SUBLANES = 8
BF16_ROWS = 16
MXU_DEPTH = 256
HEAD_DIM = 64
FLASH_Q_TILE = 512
KV_TILE = 512
KEY_CHUNK = 32
ROW_TILE = 512
FF_TILE = 1024
VMEM_LIMIT = 56 * 1024 * 1024

NEG_INF = -1e30
LOG2E = math.log2(math.e)
LN_EPS = 1e-5
RMS_EPS = 1e-6
DEPTH = 2
DN_ALPHA = (2.0 * DEPTH) ** 0.25

MLA_HEADS = 8
MLA_NOPE = 64
MLA_ROPE = 32
ROPE_THETA = 10000.0
NSA_HEADS = 8
NSA_GROUPS = 2
NSA_HG = NSA_HEADS // NSA_GROUPS
CMP_LEN = 32
CMP_STRIDE = 16
SEL_LEN = 64
SEL_TOPK = 16
WINDOW = 512
FORCE_BONUS = 1e3
DIFF_HEADS = 8

POS_SPLIT = 16
FEATURE_ROWS = 16
BLOCK_LANE0 = 8
BLOCKS_PER_TILE = KV_TILE // SEL_LEN


def _params(*sem):
    return pltpu.CompilerParams(dimension_semantics=sem, vmem_limit_bytes=VMEM_LIMIT)


def _dot(a, b):
    return jnp.dot(a, b, preferred_element_type=F32)


def _split_bf16(x):
    hi = x.astype(BF16)
    lo = (x - hi.astype(F32)).astype(BF16)
    return hi, lo


def _layer_norm(z, g, b):
    mu = jnp.mean(z, axis=-1, keepdims=True)
    zc = z - mu
    var = jnp.mean(zc * zc, axis=-1, keepdims=True)
    return zc * lax.rsqrt(var + LN_EPS) * g + b


def _rms_norm(z, g, eps):
    return z * lax.rsqrt(jnp.mean(z * z, axis=-1, keepdims=True) + eps) * g


def _lane_iota(shape):
    return lax.broadcasted_iota(jnp.int32, shape, 1)


def _keep_half(x, half):
    lane = _lane_iota(x.shape)
    keep = (lane < HEAD_DIM) if half == 0 else (lane >= HEAD_DIM)
    return jnp.where(keep, x, jnp.zeros_like(x))


def _move_head(slab, src_half, dst_half):
    if src_half != dst_half:
        slab = pltpu.roll(slab, HEAD_DIM, 1)
    return _keep_half(slab, dst_half)


def _store_transposed(o_ref, res):
    for c in range(res.shape[1] // LANES):
        cols = slice(c * LANES, (c + 1) * LANES)
        o_ref[0, 0, cols, :] = res[:, cols].T.astype(o_ref.dtype)


def _proj_kernel(x_ref, *refs, transposed):
    n_out = len(transposed)
    w_refs, o_refs = refs[:n_out], refs[n_out:]
    xb = x_ref[...].astype(BF16)
    for w_ref, o_ref, t in zip(w_refs, o_refs, transposed):
        res = _dot(xb, w_ref[...])
        if t:
            _store_transposed(o_ref, res)
        else:
            o_ref[...] = res.astype(o_ref.dtype)


def _transposed_out(b, seq, width, tm):
    per_seq = seq // tm
    spec = pl.BlockSpec((1, 1, width, tm), lambda i: (i // per_seq, i % per_seq, 0, 0))
    return spec, jax.ShapeDtypeStruct((b, per_seq, width, tm), BF16)


def _project(x, weights, out_dtypes, transposed, b, seq):
    m, k = x.shape
    tm = min(KV_TILE, seq)
    specs, shapes = [], []
    for w, dt, t in zip(weights, out_dtypes, transposed):
        if t:
            spec, shape = _transposed_out(b, seq, w.shape[1], tm)
        else:
            spec, shape = pl.BlockSpec((tm, w.shape[1]), lambda i: (i, 0)), jax.ShapeDtypeStruct((m, w.shape[1]), dt)
        specs.append(spec)
        shapes.append(shape)
    return pl.pallas_call(
        functools.partial(_proj_kernel, transposed=tuple(transposed)),
        grid=(m // tm,),
        in_specs=[pl.BlockSpec((tm, k), lambda i: (i, 0))]
        + [pl.BlockSpec(w.shape, lambda i: (0, 0)) for w in weights],
        out_specs=specs,
        out_shape=shapes,
        compiler_params=_params("parallel"),
        name="project",
    )(x, *weights)


def _rope_slab(slab, c, s1, s2):
    half = MLA_ROPE // 2
    up = pltpu.roll(slab, half, 1)
    down = pltpu.roll(slab, LANES - half, 1)
    return slab * c + down * s1 + up * s2


def _mla_prep_kernel(ql_ref, kvl_ref, kpe_ref, qg_ref, kvg_ref, wq_ref, wk_ref, wv_ref,
                     c_ref, s1_ref, s2_ref, q_ref, k_ref, vt_ref, *, q_scale):
    c, s1, s2 = c_ref[...], s1_ref[...], s2_ref[...]
    qn = _rms_norm(ql_ref[...], qg_ref[...], RMS_EPS).astype(BF16)
    kvn = _rms_norm(kvl_ref[...], kvg_ref[...], RMS_EPS).astype(BF16)
    q = _dot(qn, wq_ref[...])
    k = _dot(kvn, wk_ref[...])
    _store_transposed(vt_ref, _dot(kvn, wv_ref[...]))
    kpe = _rope_slab(kpe_ref[...], c, s1, s2)
    for h in range(MLA_HEADS):
        sl = slice(h * LANES, (h + 1) * LANES)
        q_ref[:, sl] = (_rope_slab(q[:, sl], c, s1, s2) * q_scale).astype(q_ref.dtype)
        k_ref[:, sl] = (k[:, sl] + kpe).astype(k_ref.dtype)


def _mla_prep(slab_a, q_gain, kv_gain, wq, wk, wv, rope_c, rope_s1, rope_s2, b, seq):
    m = slab_a.shape[0]
    tm = min(KV_TILE, seq)
    per_seq = seq // tm
    rank = q_gain.shape[1]
    row = lambda j: (lambda i: (i, j))
    tab = lambda i: (i % per_seq, 0)
    const = lambda i: (0, 0)
    hw = MLA_HEADS * LANES
    vt_spec, vt_shape = _transposed_out(b, seq, wv.shape[1], tm)
    return pl.pallas_call(
        functools.partial(_mla_prep_kernel, q_scale=float((MLA_NOPE + MLA_ROPE) ** -0.5 * LOG2E)),
        grid=(m // tm,),
        in_specs=[pl.BlockSpec((tm, rank), row(0)), pl.BlockSpec((tm, rank), row(1)),
                  pl.BlockSpec((tm, LANES), row(2 * rank // LANES)),
                  pl.BlockSpec((1, rank), const), pl.BlockSpec((1, rank), const),
                  pl.BlockSpec(wq.shape, const), pl.BlockSpec(wk.shape, const), pl.BlockSpec(wv.shape, const),
                  pl.BlockSpec((tm, LANES), tab), pl.BlockSpec((tm, LANES), tab), pl.BlockSpec((tm, LANES), tab)],
        out_specs=[pl.BlockSpec((tm, hw), row(0)), pl.BlockSpec((tm, hw), row(0)), vt_spec],
        out_shape=[jax.ShapeDtypeStruct((m, hw), BF16), jax.ShapeDtypeStruct((m, hw), BF16), vt_shape],
        compiler_params=_params("parallel"),
        name="mla_prep",
    )(slab_a, slab_a, slab_a, q_gain, kv_gain, wq, wk, wv, rope_c, rope_s1, rope_s2)


def _flash_scratch(n_streams, v_rows, tq, tk):
    scores = pltpu.VMEM((n_streams, tk, tq), F32)
    stat = pltpu.VMEM((n_streams, 1, tq), F32)
    probs = pltpu.VMEM((n_streams, tk, tq), BF16)
    slot = [scores, probs, stat]
    return slot + slot + [stat, pltpu.VMEM((n_streams, v_rows + BF16_ROWS, tq), F32)]


def _chunk_rows(c):
    return slice(c * KEY_CHUNK, (c + 1) * KEY_CHUNK)


def _diagonal_blocks(tk, tq, keys_up_to_query):
    for c in range(tk // KEY_CHUNK):
        for v in range(tq // LANES):
            first_key, last_key = c * KEY_CHUNK, (c + 1) * KEY_CHUNK - 1
            first_query, last_query = v * LANES, (v + 1) * LANES - 1
            all_up_to = last_key <= first_query
            all_beyond = first_key > last_query
            if all_up_to or all_beyond:
                kind = "visible" if all_up_to == keys_up_to_query else "hidden"
            else:
                kind = "mixed"
            yield _chunk_rows(c), slice(first_query, last_query + 1), first_query - first_key, kind


def _fold_rows(x):
    return x.reshape(x.shape[0] // SUBLANES, SUBLANES, x.shape[1])


def _flash_transposed(diagonal_tile, n_streams, q_start, tq, tk, key_operand, query_operand, values, offset, scratch,
                      rest_tile, rest_count):
    assert tq == tk, "the diagonal tile is taken to start at the first query of the tile"
    slot_a, slot_b, (m_ref, acc_ref) = scratch[0:3], scratch[3:6], scratch[6:]
    n_chunks = tk // KEY_CHUNK
    for i in range(n_streams):
        m_ref[i] = jnp.full((1, tq), NEG_INF, F32)
        acc_ref[i] = jnp.zeros(acc_ref.shape[1:], F32)

    def column_max(s_ref, i):
        part = jnp.full((SUBLANES, tq), NEG_INF, F32)
        for c in range(n_chunks):
            part = jnp.maximum(part, jnp.max(_fold_rows(s_ref[i, _chunk_rows(c), :]), axis=0))
        return jnp.max(part, axis=0, keepdims=True)

    def stage1(j, slot, i):
        slot[0][i] = _dot(key_operand(j, i), query_operand(j, i))

    def blocks():
        return _diagonal_blocks(tk, tq, True)

    def stage2(j, slot, i, diagonal):
        s_ref, p_ref, alpha_ref = slot
        if diagonal:
            rel = (lax.broadcasted_iota(jnp.int32, (KEY_CHUNK, LANES), 0)
                   - lax.broadcasted_iota(jnp.int32, (KEY_CHUNK, LANES), 1))
            parts = [jnp.full((SUBLANES, LANES), NEG_INF, F32) for _ in range(tq // LANES)]
            for rows, lanes, bound, kind in blocks():
                if kind == "hidden":
                    continue
                s = s_ref[i, rows, lanes]
                if kind == "mixed":
                    s = jnp.where(rel <= bound, s, NEG_INF)
                    s_ref[i, rows, lanes] = s
                v = lanes.start // LANES
                parts[v] = jnp.maximum(parts[v], jnp.max(_fold_rows(s), axis=0))
            mx = jnp.max(jnp.concatenate(parts, axis=1), axis=0, keepdims=True)
        else:
            mx = column_max(s_ref, i)
        off = offset(j, i)
        m_prev = m_ref[i]
        if off is None:
            m_next = jnp.maximum(m_prev, mx)
            shift = m_next
        else:
            m_next = jnp.maximum(m_prev, mx + off)
            shift = m_next - off
        alpha = jnp.exp2(m_prev - m_next)
        if diagonal:
            for rows, lanes, _, kind in blocks():
                if kind == "hidden":
                    p_ref[i, rows, lanes] = jnp.zeros((KEY_CHUNK, LANES), BF16)
                else:
                    p_ref[i, rows, lanes] = jnp.exp2(s_ref[i, rows, lanes] - shift[:, lanes]).astype(BF16)
        else:
            for c in range(n_chunks):
                p_ref[i, _chunk_rows(c), :] = jnp.exp2(s_ref[i, _chunk_rows(c), :] - shift).astype(BF16)
        m_ref[i] = m_next
        alpha_ref[i] = alpha

    def step(accumulate=None, produce=(), exponentiate=None, diagonal=False):
        products = []
        for i in range(n_streams + 1):
            if i < n_streams:
                if accumulate is not None:
                    tile, slot = accumulate
                    products.append(_dot(_values_and_ones(values(tile, i)), slot[1][i]))
                for tile, slot in produce:
                    stage1(tile, slot, i)
                if exponentiate is not None:
                    stage2(*exponentiate, i, diagonal)
            if accumulate is not None and i > 0:
                acc_ref[i - 1] = accumulate[1][2][i - 1] * acc_ref[i - 1] + products[i - 1]

    step(produce=((diagonal_tile, slot_a), (rest_tile(0), slot_b)), exponentiate=(diagonal_tile, slot_a),
         diagonal=True)
    last = rest_count(m_ref)

    def tile_at(position):
        return rest_tile(jnp.clip(position, 1, jnp.maximum(last, 1)) - 1)

    def diag_or_rest(position):
        return jnp.where(position == 0, diagonal_tile, tile_at(position))

    def pair(t, carry):
        p1 = 2 * t + 1
        step((diag_or_rest(p1 - 1), slot_a), ((tile_at(p1 + 1), slot_a),), (tile_at(p1), slot_b))
        step((tile_at(p1), slot_b), ((tile_at(p1 + 2), slot_b),), (tile_at(p1 + 1), slot_a))
        return carry

    lax.fori_loop(0, last // 2, pair, 0)

    @pl.when(last % 2 == 1)
    def _():
        step(accumulate=(diag_or_rest(last - 1), slot_a), exponentiate=(tile_at(last), slot_b))
        step(accumulate=(tile_at(last), slot_b))

    @pl.when(last % 2 == 0)
    def _():
        step(accumulate=(diag_or_rest(last), slot_a))


def _flash_result(scratch, i):
    acc = scratch[-1][i]
    v_rows = acc.shape[0] - BF16_ROWS
    return acc[:v_rows] / acc[v_rows:v_rows + 1]


def _kv_rows(j, tk):
    return pl.ds(pl.multiple_of(j * tk, tk), tk)


def _transposed_bf16(x):
    return x.astype(F32).T.astype(BF16)


def _alibi_rows(coef, tq):
    c = jnp.zeros((1, tq), F32) + coef
    hi = c.astype(BF16).astype(F32)
    rest = c - hi
    mid = rest.astype(BF16).astype(F32)
    lo = rest - mid
    zero = jnp.zeros((1, tq), F32)
    return jnp.concatenate([POS_SPLIT * hi, POS_SPLIT * mid, POS_SPLIT * lo, hi, mid, lo, zero, zero], axis=0)


def _augmented_query(q_t, feature_rows):
    tq = q_t.shape[1]
    pad = jnp.zeros((MXU_DEPTH - LANES - FEATURE_ROWS, tq), BF16)
    return jnp.concatenate([q_t, feature_rows.astype(BF16), pad], axis=0)


MLA_STEP_HEADS = 4


def _mla_attn_kernel(q_ref, k_ref, vt_ref, o_ref, *scratch, tq, tk):
    q_start = pl.program_id(2) * tq
    n = MLA_STEP_HEADS
    queries = [_transposed_bf16(q_ref[0, :, hh * LANES:(hh + 1) * LANES]) for hh in range(n)]
    n_full = q_start // tk
    _flash_transposed(
        n_full, n, q_start, tq, tk,
        lambda j, i: k_ref[0, _kv_rows(j, tk), i * LANES:(i + 1) * LANES],
        lambda j, i: queries[i],
        lambda j, i: vt_ref[0, j, i * HEAD_DIM:(i + 1) * HEAD_DIM, :],
        lambda j, i: None, scratch,
        rest_tile=lambda k: k, rest_count=lambda m_ref: n_full)
    for pair in range(n // 2):
        o_t = jnp.concatenate([_flash_result(scratch, 2 * pair), _flash_result(scratch, 2 * pair + 1)], axis=0)
        o_ref[0, :, pair * LANES:(pair + 1) * LANES] = o_t.T.astype(o_ref.dtype)


def _mla_attn(q, k, vt):
    b, s, _ = q.shape
    tq, tk = min(FLASH_Q_TILE, s), min(KV_TILE, s)
    n = MLA_STEP_HEADS
    groups = MLA_HEADS // n
    return pl.pallas_call(
        functools.partial(_mla_attn_kernel, tq=tq, tk=tk),
        grid=(b, groups, s // tq),
        in_specs=[pl.BlockSpec((1, tq, n * LANES), lambda bi, p, i: (bi, i, p)),
                  pl.BlockSpec((1, s, n * LANES), lambda bi, p, i: (bi, 0, p)),
                  pl.BlockSpec((1, s // tk, n * HEAD_DIM, tk), lambda bi, p, i: (bi, 0, p, 0))],
        out_specs=pl.BlockSpec((1, tq, n * HEAD_DIM), lambda bi, p, i: (bi, i, p)),
        out_shape=jax.ShapeDtypeStruct((b, s, MLA_HEADS * HEAD_DIM), BF16),
        scratch_shapes=_flash_scratch(n, HEAD_DIM, tq, tk),
        compiler_params=_params("parallel", "parallel", "arbitrary"),
        name="mla_attn",
    )(q, k, vt)


def _gelu_tanh(x):
    return 0.5 * x * (1.0 + jnp.tanh(math.sqrt(2.0 / math.pi) * (x + 0.044715 * (x * x * x))))


def _compress_kernel(x_ref, pos_ref, w1a_ref, w1b_ref, w2_ref, k_ref, vt_ref, *, n_real):
    x = x_ref[0]
    n = x.shape[0]
    first = _dot(x, w1a_ref[...])
    second = _dot(x, w1b_ref[...])
    pos_hi, pos_lo = _split_bf16(pos_ref[...])
    bias = (_dot(pos_hi[:8], w1a_ref[...]) + _dot(pos_lo[:8], w1a_ref[...])
            + _dot(pos_hi[8:], w1b_ref[...]) + _dot(pos_lo[8:], w1b_ref[...]))[:1]
    pre = first + pltpu.roll(second, n - 1, 0) + bias
    out = _dot(_gelu_tanh(pre).astype(BF16), w2_ref[...])
    real = lax.broadcasted_iota(jnp.int32, out.shape, 0) < n_real
    out = jnp.where(real, out, 0.0)
    half = out.shape[1] // 2
    k_ref[0] = out[:, :half].astype(k_ref.dtype)
    vt_ref[0] = out[:, half:].T.astype(vt_ref.dtype)


def _compress(x_chunks, pos_exp, w1a, w1b, w2, n_real):
    b, n, width = x_chunks.shape
    half = w2.shape[1] // 2
    const = lambda bi: (0, 0)
    return pl.pallas_call(
        functools.partial(_compress_kernel, n_real=n_real),
        grid=(b,),
        in_specs=[pl.BlockSpec((1, n, width), lambda bi: (bi, 0, 0)),
                  pl.BlockSpec(pos_exp.shape, const), pl.BlockSpec(w1a.shape, const),
                  pl.BlockSpec(w1b.shape, const), pl.BlockSpec(w2.shape, const)],
        out_specs=[pl.BlockSpec((1, n, half), lambda bi: (bi, 0, 0)), pl.BlockSpec((1, half, n), lambda bi: (bi, 0, 0))],
        out_shape=[jax.ShapeDtypeStruct((b, n, half), BF16), jax.ShapeDtypeStruct((b, half, n), BF16)],
        compiler_params=_params("parallel"),
        name="nsa_compress",
    )(x_chunks, pos_exp, w1a, w1b, w2)


def _nsa_head_slope(h):
    return float(2.0 ** (-8.0 * (h + 1) / NSA_HEADS))


def _nsa_queries(q_ref, g, scale):
    out = []
    for hg in range(NSA_HG):
        h = g * NSA_HG + hg
        slab = q_ref[0, :, (h // 2) * LANES:(h // 2 + 1) * LANES].astype(F32) * scale
        out.append(_move_head(slab, h % 2, g))
    return out


def _selection_frame(q_pos, n_blocks):
    blk = lax.broadcasted_iota(jnp.int32, (n_blocks, q_pos.shape[1]), 0)
    behind = q_pos // SEL_LEN - blk
    near = jnp.abs(2 * behind - 1) <= 1
    bonus = jnp.where(near, FORCE_BONUS, jnp.where(blk == 0, FORCE_BONUS, 0.0))
    return blk, bonus, behind >= 0


def _select_blocks(imp_t, blk, bonus, allowed):
    val = jnp.where(allowed, imp_t + bonus, NEG_INF)
    for _ in range(SEL_TOPK):
        top = jnp.max(val, axis=0, keepdims=True)
        first = jnp.min(jnp.where(val == top, blk, imp_t.shape[0]), axis=0, keepdims=True)
        val = jnp.where(blk == first, -jnp.inf, val)
    return jnp.where(val == -jnp.inf, 0.0, NEG_INF)


def _masked_softmax_pass(s_ref, p_ref, tiles, tq):
    part = jnp.full((SUBLANES, tq), NEG_INF, F32)
    for t, rows, keep, off in tiles:
        for c in range(rows // KEY_CHUNK):
            s = jnp.where(keep(_chunk_rows(c)), s_ref[t, _chunk_rows(c), :], NEG_INF)
            s_ref[t, _chunk_rows(c), :] = s
            part = jnp.maximum(part, jnp.max(_fold_rows(s), axis=0) + off)
    m = jnp.max(part, axis=0, keepdims=True)
    for t, rows, keep, off in tiles:
        shift = m - off
        for c in range(rows // KEY_CHUNK):
            p_ref[t, _chunk_rows(c), :] = jnp.exp2(s_ref[t, _chunk_rows(c), :] - shift).astype(BF16)
    return m > 0.5 * NEG_INF


def _window_softmax(s_ref, p_ref, prev_offset, tq, tk):
    rel = (lax.broadcasted_iota(jnp.int32, (KEY_CHUNK, LANES), 0)
           - lax.broadcasted_iota(jnp.int32, (KEY_CHUNK, LANES), 1))
    tiles = [(0, False, prev_offset), (1, True, 0.0)]
    parts = [jnp.full((SUBLANES, LANES), NEG_INF, F32) for _ in range(tq // LANES)]
    for t, below, off in tiles:
        for rows, lanes, bound, kind in _diagonal_blocks(tk, tq, below):
            if kind == "hidden":
                continue
            s = s_ref[t, rows, lanes]
            if kind == "mixed":
                s = jnp.where((rel <= bound) if below else (rel > bound), s, NEG_INF)
                s_ref[t, rows, lanes] = s
            v = lanes.start // LANES
            parts[v] = jnp.maximum(parts[v], jnp.max(_fold_rows(s), axis=0) + off)
    m = jnp.max(jnp.concatenate(parts, axis=1), axis=0, keepdims=True)
    for t, below, off in tiles:
        shift = m - off
        for rows, lanes, _, kind in _diagonal_blocks(tk, tq, below):
            if kind == "hidden":
                p_ref[t, rows, lanes] = jnp.zeros((KEY_CHUNK, LANES), BF16)
            else:
                p_ref[t, rows, lanes] = jnp.exp2(s_ref[t, rows, lanes] - shift[:, lanes]).astype(BF16)


def _values_and_ones(vt):
    return jnp.concatenate([vt, jnp.ones((BF16_ROWS, vt.shape[1]), BF16)], axis=0)


def _nsa_cmp_win_kernel(q_ref, kc_ref, vct_ref, kw_ref, vwt_ref, feat_ref, ovt_ref, oc_ref, ow_ref, sel_ref, any_ref,
                        end_ref, sc_ref, sw_ref, pc_ref, pw_ref, *, tq, tk):
    qi = pl.program_id(1)
    q_start = qi * tq
    n_cmp = kc_ref.shape[1]
    end_ref[...] = (lax.broadcasted_iota(jnp.int32, (n_cmp, tq), 0) * CMP_STRIDE + (CMP_LEN - 1)
                    - lax.broadcasted_iota(jnp.int32, (n_cmp, tq), 1))
    prev_tile = jnp.maximum(qi - 1, 0)
    zeros = jnp.zeros((FEATURE_ROWS - SUBLANES, tq), F32)
    k_cmp = jnp.concatenate([kc_ref[0], feat_ref[0:n_cmp, :]], axis=1)
    k_win = [jnp.concatenate([kw_ref[0, _kv_rows(j, tk), :], feat_ref[...]], axis=1) for j in (prev_tile, qi)]
    frame = _selection_frame(q_start + lax.broadcasted_iota(jnp.int32, (1, tq), 1), LANES)
    for g in range(NSA_GROUPS):
        queries = [_transposed_bf16(q) for q in _nsa_queries(q_ref, g, HEAD_DIM ** -0.5 * LOG2E)]
        group_rows = slice(g * HEAD_DIM, (g + 1) * HEAD_DIM)
        imp_t = jnp.zeros((LANES, tq), F32)
        out_c, out_w = [], []
        for hg in range(NSA_HG):
            coef = _nsa_head_slope(g * NSA_HG + hg) * LOG2E
            buf = hg % 2
            sc, sw, pc, pw = sc_ref.at[buf], sw_ref.at[buf], pc_ref.at[buf], pw_ref.at[buf]
            cmp_query = _augmented_query(queries[hg], jnp.concatenate([_alibi_rows(CMP_STRIDE * coef, tq), zeros], 0))
            win_query = _augmented_query(queries[hg], jnp.concatenate([_alibi_rows(coef, tq), zeros], 0))
            sc[0] = _dot(k_cmp, cmp_query)
            sw[0] = _dot(k_win[0], win_query)
            sw[1] = _dot(k_win[1], win_query)
            has_any = _masked_softmax_pass(sc, pc, [(0, n_cmp, lambda r: end_ref[r, :] <= q_start, 0.0)], tq)
            acc = _dot(_values_and_ones(vct_ref[0, group_rows, :]), pc[0])
            inv = jnp.where(has_any, 1.0 / acc[HEAD_DIM:HEAD_DIM + 1], 0.0)
            out_c.append(acc[:HEAD_DIM] * inv)
            imp_t = imp_t + _dot(ovt_ref[...], pc[0]) * inv
            _window_softmax(sw, pw, jnp.where(qi >= 1, -coef * tk, NEG_INF), tq, tk)
            acc = (_dot(_values_and_ones(vwt_ref[0, prev_tile, group_rows, :]), pw[0])
                   + _dot(_values_and_ones(vwt_ref[0, qi, group_rows, :]), pw[1]))
            out_w.append(acc[:HEAD_DIM] / acc[HEAD_DIM:HEAD_DIM + 1])
        for pair in range(NSA_HG // 2):
            cols = slice((g * 2 + pair) * LANES, (g * 2 + pair + 1) * LANES)
            oc_ref[0, :, cols] = jnp.concatenate(out_c[2 * pair:2 * pair + 2], axis=0).T
            ow_ref[0, :, cols] = jnp.concatenate(out_w[2 * pair:2 * pair + 2], axis=0).T
        bias_t = _select_blocks(imp_t, *frame)
        sel_ref[0, g * LANES:(g + 1) * LANES, :] = bias_t
        any_ref[0, 0, g * LANES:(g + 1) * LANES, :] = jnp.broadcast_to(
            jnp.max(bias_t, axis=1, keepdims=True), (LANES, LANES))


def _nsa_cmp_win(slab_b, cmp_k, cmp_vt, vt_nsa, key_features, overlap_t, seq):
    b = slab_b.shape[0]
    tq, tk = min(FLASH_Q_TILE, seq), min(KV_TILE, seq)
    assert tq == tk == WINDOW, "the window branch is written for one previous and one diagonal key tile"
    n_cmp = cmp_k.shape[1]
    qw = NSA_HEADS * HEAD_DIM
    base = qw // LANES
    tile = lambda bi, i: (bi, i, 0)
    return pl.pallas_call(
        functools.partial(_nsa_cmp_win_kernel, tq=tq, tk=tk),
        grid=(b, seq // tq),
        in_specs=[pl.BlockSpec((1, tq, qw), tile),
                  pl.BlockSpec((1, n_cmp, LANES), lambda bi, i: (bi, 0, 0)),
                  pl.BlockSpec((1, LANES, n_cmp), lambda bi, i: (bi, 0, 0)),
                  pl.BlockSpec((1, seq, LANES), lambda bi, i: (bi, 0, base + 1)),
                  pl.BlockSpec((1, seq // tk, LANES, tk), lambda bi, i: (bi, 0, 1, 0)),
                  pl.BlockSpec(key_features.shape, lambda bi, i: (0, 0)),
                  pl.BlockSpec(overlap_t.shape, lambda bi, i: (0, 0))],
        out_specs=[pl.BlockSpec((1, tq, qw), tile), pl.BlockSpec((1, tq, qw), tile),
                   pl.BlockSpec((1, NSA_GROUPS * LANES, tq), lambda bi, i: (bi, 0, i)),
                   pl.BlockSpec((1, 1, NSA_GROUPS * LANES, LANES), lambda bi, i: (bi, i, 0, 0))],
        out_shape=[jax.ShapeDtypeStruct((b, seq, qw), F32), jax.ShapeDtypeStruct((b, seq, qw), F32),
                   jax.ShapeDtypeStruct((b, NSA_GROUPS * LANES, seq), F32),
                   jax.ShapeDtypeStruct((b, seq // tq, NSA_GROUPS * LANES, LANES), F32)],
        scratch_shapes=[pltpu.VMEM((n_cmp, tq), jnp.int32),
                        pltpu.VMEM((2, 1, n_cmp, tq), F32), pltpu.VMEM((2, 2, tk, tq), F32),
                        pltpu.VMEM((2, 1, n_cmp, tq), BF16), pltpu.VMEM((2, 2, tk, tq), BF16)],
        compiler_params=_params("parallel", "arbitrary"),
        name="nsa_cmp_win",
    )(slab_b, cmp_k, cmp_vt, slab_b, vt_nsa, key_features, overlap_t)


def _nsa_sel_kernel(tiles_ref, counts_ref, q_ref, k_ref, vt_ref, feat_ref, sel_ref, oc_ref, ow_ref, gate_ref, gx_ref,
                    o_ref, *scratch, tq, tk, max_tiles):
    qi = pl.program_id(1)
    q_start = qi * tq
    diagonal_tile = q_start // tk
    out_slabs = []
    for g in range(NSA_GROUPS):
        queries = [_transposed_bf16(q) for q in _nsa_queries(q_ref, g, HEAD_DIM ** -0.5 * LOG2E)]
        coefs = [_nsa_head_slope(g * NSA_HG + hg) * LOG2E for hg in range(NSA_HG)]
        alibi = [_alibi_rows(c, tq) for c in coefs]
        entry = (pl.program_id(0) * pl.num_programs(1) + qi) * NSA_GROUPS + g
        n_active = counts_ref[entry]

        def listed_tile(k, entry=entry):
            return tiles_ref[entry * max_tiles + jnp.minimum(k, max_tiles - 1)]

        def key_operand(j, i):
            return jnp.concatenate([k_ref[0, _kv_rows(j, tk), :], feat_ref[...]], axis=1)

        def query_operand(j, i, g=g, queries=queries, alibi=alibi):
            first_block = pl.multiple_of(g * LANES + j * BLOCKS_PER_TILE, BLOCKS_PER_TILE)
            blocks = sel_ref[0, pl.ds(first_block, BLOCKS_PER_TILE), :]
            return _augmented_query(queries[i], jnp.concatenate([alibi[i], blocks], axis=0))

        _flash_transposed(
            diagonal_tile, NSA_HG, q_start, tq, tk, key_operand, query_operand,
            lambda j, i, g=g: vt_ref[0, j, g * HEAD_DIM:(g + 1) * HEAD_DIM, :],
            lambda j, i, coefs=coefs: coefs[i] * (j * tk - q_start).astype(F32), scratch,
            rest_tile=listed_tile, rest_count=lambda m_ref, n_active=n_active: n_active)
        heads = [_flash_result(scratch, hg) for hg in range(NSA_HG)]
        for pair in range(NSA_HG // 2):
            out_slabs.append(jnp.concatenate(heads[2 * pair:2 * pair + 2], axis=0).T)
    gates = jax.nn.sigmoid(gate_ref[...])
    g_hi, g_lo = _split_bf16(gates)
    width = NSA_HEADS * HEAD_DIM
    for i, o_sel in enumerate(out_slabs):
        mixed = None
        for branch, o_branch in enumerate((oc_ref[0, :, i * LANES:(i + 1) * LANES], o_sel,
                                           ow_ref[0, :, i * LANES:(i + 1) * LANES])):
            gx = gx_ref[:, branch * width + i * LANES:branch * width + (i + 1) * LANES]
            term = (_dot(g_hi, gx) + _dot(g_lo, gx)) * o_branch
            mixed = term if mixed is None else mixed + term
        o_ref[0, :, i * LANES:(i + 1) * LANES] = mixed.astype(o_ref.dtype)


def _active_key_tiles(block_any, tq, tk):
    b, n_q = block_any.shape[:2]
    max_tiles = LANES // BLOCKS_PER_TILE
    hit = block_any[..., 0].reshape(b, n_q, NSA_GROUPS, max_tiles, BLOCKS_PER_TILE).max(axis=-1) > 0.5 * NEG_INF
    before_diagonal = jnp.arange(max_tiles)[None, :] < (jnp.arange(n_q) * tq // tk)[:, None]
    hit = hit & before_diagonal[None, :, None, :]
    rank = jnp.cumsum(hit.astype(jnp.int32), axis=-1) - 1
    slots = jnp.arange(max_tiles, dtype=jnp.int32)
    in_slot = hit[..., None, :] & (rank[..., None, :] == slots[:, None])
    tiles = jnp.sum(jnp.where(in_slot, slots, 0), axis=-1)
    return tiles.astype(jnp.int32).reshape(-1), hit.sum(axis=-1).astype(jnp.int32).reshape(-1), max_tiles


def _nsa_sel(slab_b, vt, key_features, sel_bias_t, block_any, o_cmp, o_win, slab_a, gate_expand, seq, gate_col_block):
    b = slab_b.shape[0]
    tq, tk = min(FLASH_Q_TILE, seq), min(KV_TILE, seq)
    qw = NSA_HEADS * HEAD_DIM
    base = qw // LANES
    per_seq = seq // tq
    tiles, counts, max_tiles = _active_key_tiles(block_any, tq, tk)
    tile = lambda bi, i, *_: (bi, i, 0)
    grid_spec = pltpu.PrefetchScalarGridSpec(
        num_scalar_prefetch=2,
        grid=(b, seq // tq),
        in_specs=[pl.BlockSpec((1, tq, qw), tile),
                  pl.BlockSpec((1, seq, LANES), lambda bi, i, *_: (bi, 0, base)),
                  pl.BlockSpec((1, seq // tk, LANES, tk), lambda bi, i, *_: (bi, 0, 0, 0)),
                  pl.BlockSpec(key_features.shape, lambda bi, i, *_: (0, 0)),
                  pl.BlockSpec((1, NSA_GROUPS * LANES, tq), lambda bi, i, *_: (bi, 0, i)),
                  pl.BlockSpec((1, tq, qw), tile), pl.BlockSpec((1, tq, qw), tile),
                  pl.BlockSpec((tq, LANES), lambda bi, i, *_: (bi * per_seq + i, gate_col_block)),
                  pl.BlockSpec(gate_expand.shape, lambda bi, i, *_: (0, 0))],
        out_specs=pl.BlockSpec((1, tq, qw), tile),
        scratch_shapes=_flash_scratch(NSA_HG, HEAD_DIM, tq, tk))
    return pl.pallas_call(
        functools.partial(_nsa_sel_kernel, tq=tq, tk=tk, max_tiles=max_tiles),
        grid_spec=grid_spec,
        out_shape=jax.ShapeDtypeStruct((b, seq, qw), BF16),
        compiler_params=_params("parallel", "arbitrary"),
        name="nsa_sel",
    )(tiles, counts, slab_b, slab_b, vt, key_features, sel_bias_t, o_cmp, o_win, slab_a, gate_expand)


DIFF_STEP_HEADS = 2


SKIP_GAP = 180.0
NORM_SLACK = 1.01


def _diff_attn_kernel(slope_ref, lam_ref, q_ref, k_ref, vt_ref, feat_ref, g_ref, o_ref, knorm_ref, *scratch,
                      tq, tk, lam_init):
    first_head = pl.program_id(1) * DIFF_STEP_HEADS
    q_start = pl.program_id(2) * tq
    n_full = q_start // tk
    n_tiles = k_ref.shape[1] // tk

    @pl.when(pl.program_id(2) == 0)
    def _():
        for hh in range(DIFF_STEP_HEADS):
            for j in range(n_tiles):
                k = k_ref[0, j * tk:(j + 1) * tk, hh * LANES:(hh + 1) * LANES].astype(F32)
                knorm_ref[hh * n_tiles + j] = jnp.sqrt(jnp.max(jnp.sum(k * k, axis=1, keepdims=True)))

    zeros = jnp.zeros((FEATURE_ROWS - SUBLANES, tq), F32)
    coefs, queries, q_norms = [], [], []
    for hh in range(DIFF_STEP_HEADS):
        coef = slope_ref[first_head + hh] * LOG2E
        q = q_ref[0, :, hh * LANES:(hh + 1) * LANES].astype(F32) * (HEAD_DIM ** -0.5 * LOG2E)
        features = jnp.concatenate([_alibi_rows(coef, tq), zeros], axis=0)
        coefs.append(coef)
        for half in range(2):
            q_half = _keep_half(q, half)
            queries.append(_augmented_query(_transposed_bf16(q_half), features))
            q_norms.append(NORM_SLACK * jnp.sqrt(jnp.max(jnp.sum(q_half * q_half, axis=1, keepdims=True))))

    def head_lanes(i):
        return slice((i // 2) * LANES, (i // 2 + 1) * LANES)

    def offset(j, i):
        return coefs[i // 2] * (j * tk - q_start).astype(F32)

    nearest = jnp.maximum(n_full - 1, 0)
    first_needed = []

    def rest_count(m_ref):
        floors = [jnp.min(m_ref[i]) - SKIP_GAP for i in range(2 * DIFF_STEP_HEADS)]

        def body(j, first):
            needed = jnp.bool_(False)
            for i in range(2 * DIFF_STEP_HEADS):
                bound = q_norms[i] * knorm_ref[(i // 2) * n_tiles + j] + coefs[i // 2] * (tk - 1) + offset(j, i)
                needed = jnp.logical_or(needed, bound >= floors[i])
            return jnp.where(needed, jnp.minimum(first, j), first)

        first_needed.append(lax.fori_loop(0, nearest, body, nearest))
        return jnp.minimum(n_full, 1) + nearest - first_needed[0]

    def rest_tile(k):
        if isinstance(k, int):
            return nearest
        return jnp.where(k == 0, nearest, first_needed[0] + k - 1)

    _flash_transposed(
        n_full, 2 * DIFF_STEP_HEADS, q_start, tq, tk,
        lambda j, i: jnp.concatenate([k_ref[0, _kv_rows(j, tk), head_lanes(i)], feat_ref[...]], axis=1),
        lambda j, i: queries[i],
        lambda j, i: vt_ref[0, j, head_lanes(i), :],
        offset, scratch, rest_tile, rest_count)
    lam_vec = lam_ref[...]
    lam = (jnp.exp(jnp.sum(lam_vec[0:1] * lam_vec[1:2], axis=1, keepdims=True))
           - jnp.exp(jnp.sum(lam_vec[2:3] * lam_vec[3:4], axis=1, keepdims=True)) + lam_init)
    for hh in range(DIFF_STEP_HEADS):
        o = (_flash_result(scratch, 2 * hh) - lam * _flash_result(scratch, 2 * hh + 1)).T
        o_ref[0, :, hh * LANES:(hh + 1) * LANES] = (
            _rms_norm(o, g_ref[...], RMS_EPS) * (1.0 - lam_init)).astype(o_ref.dtype)


def _diff_attn(qk, vt, key_features, slopes, lam_vecs, subln_g, lam_init):
    b, s, _ = qk.shape
    tq, tk = min(FLASH_Q_TILE, s), min(KV_TILE, s)
    n = DIFF_STEP_HEADS
    groups = DIFF_HEADS // n
    smem = pl.BlockSpec(memory_space=pltpu.SMEM)
    return pl.pallas_call(
        functools.partial(_diff_attn_kernel, tq=tq, tk=tk, lam_init=lam_init),
        grid=(b, groups, s // tq),
        in_specs=[smem, pl.BlockSpec(lam_vecs.shape, lambda bi, h, i: (0, 0)),
                  pl.BlockSpec((1, tq, n * LANES), lambda bi, h, i: (bi, i, h)),
                  pl.BlockSpec((1, s, n * LANES), lambda bi, h, i: (bi, 0, groups + h)),
                  pl.BlockSpec((1, s // tk, n * LANES, tk), lambda bi, h, i: (bi, 0, h, 0)),
                  pl.BlockSpec(key_features.shape, lambda bi, h, i: (0, 0)),
                  pl.BlockSpec((1, LANES), lambda bi, h, i: (0, 0))],
        out_specs=pl.BlockSpec((1, tq, n * LANES), lambda bi, h, i: (bi, i, h)),
        out_shape=jax.ShapeDtypeStruct((b, s, DIFF_HEADS * LANES), BF16),
        scratch_shapes=[pltpu.SMEM((n * (s // tk),), F32)] + _flash_scratch(2 * n, LANES, tq, tk),
        compiler_params=_params("parallel", "parallel", "arbitrary"),
        name="diff_attn",
    )(slopes, lam_vecs, qk, qk, vt, key_features, subln_g)


def _out_ln_kernel(*refs, n_in):
    a_refs, w_refs = refs[:n_in], refs[n_in:2 * n_in]
    x_ref, g_ref, b_ref, o_ref = refs[2 * n_in:]
    y = None
    for a_ref, w_ref in zip(a_refs, w_refs):
        t = _dot(a_ref[...], w_ref[...])
        y = t if y is None else y + t
    o_ref[...] = _layer_norm(DN_ALPHA * x_ref[...] + y, g_ref[...], b_ref[...])


def _out_ln(acts, weights, x, g, b):
    m, d = x.shape
    tm = min(ROW_TILE, m)
    row = lambda i: (i, 0)
    const = lambda i: (0, 0)
    return pl.pallas_call(
        functools.partial(_out_ln_kernel, n_in=len(acts)),
        grid=(m // tm,),
        in_specs=[pl.BlockSpec((tm, a.shape[1]), row) for a in acts]
        + [pl.BlockSpec(w.shape, const) for w in weights]
        + [pl.BlockSpec((tm, d), row), pl.BlockSpec((1, d), const), pl.BlockSpec((1, d), const)],
        out_specs=pl.BlockSpec((tm, d), row),
        out_shape=jax.ShapeDtypeStruct((m, d), F32),
        compiler_params=_params("parallel"),
        name="out_proj_ln",
    )(*acts, *weights, x, g, b)


def _mlp_kernel(x_ref, wu_ref, wd_ref, g_ref, b_ref, o_ref, *, tf):
    x = x_ref[...]
    xb = x.astype(BF16)
    acc = None
    for f in range(wu_ref.shape[1] // tf):
        cols = slice(f * tf, (f + 1) * tf)
        hidden = jnp.maximum(_dot(xb, wu_ref[:, cols]), 0.0)
        part = _dot((hidden * hidden).astype(BF16), wd_ref[cols, :])
        acc = part if acc is None else acc + part
    o_ref[...] = _layer_norm(DN_ALPHA * x + acc, g_ref[...], b_ref[...])


def _mlp(x, w_up, w_down, g, b):
    m, d = x.shape
    ff = w_up.shape[1]
    tm, tf = min(ROW_TILE, m), min(FF_TILE, ff)
    resident = dict(pipeline_mode=pl.Buffered(1))
    return pl.pallas_call(
        functools.partial(_mlp_kernel, tf=tf),
        grid=(m // tm,),
        in_specs=[pl.BlockSpec((tm, d), lambda i: (i, 0)),
                  pl.BlockSpec((d, ff), lambda i: (0, 0), **resident),
                  pl.BlockSpec((ff, d), lambda i: (0, 0), **resident),
                  pl.BlockSpec((1, d), lambda i: (0, 0)), pl.BlockSpec((1, d), lambda i: (0, 0))],
        out_specs=pl.BlockSpec((tm, d), lambda i: (i, 0)),
        out_shape=jax.ShapeDtypeStruct((m, d), F32),
        compiler_params=_params("parallel"),
        name="mlp_ln",
    )(x, w_up, w_down, g, b)


def _pad_cols(w, width):
    return jnp.pad(w, ((0, 0), (0, width - w.shape[1])))


def _layer0_weights(w_in, w_uq, w_ukv, d_model):
    rank = d_model // 4
    kvw = NSA_GROUPS * HEAD_DIM
    o = np.cumsum([0, rank, rank, MLA_ROPE, NSA_HEADS * HEAD_DIM] + [kvw] * 6 + [3 * NSA_HEADS])
    seg = lambda i: w_in[:, o[i]:o[i + 1]]
    zeros = lambda n: jnp.zeros((w_in.shape[0], n), w_in.dtype)
    rope_slab = jnp.concatenate([zeros(MLA_NOPE), seg(2), zeros(LANES - MLA_NOPE - MLA_ROPE)], axis=1)
    w_a = jnp.concatenate([seg(0), seg(1), rope_slab, _pad_cols(seg(10), LANES)], axis=1)
    w_b = jnp.concatenate([seg(3), seg(6), seg(8)], axis=1)
    w_c = jnp.concatenate([seg(4), seg(5)], axis=1)
    w_vs = jnp.concatenate([seg(7), seg(9)], axis=1)
    wq = jnp.pad(w_uq.reshape(rank, MLA_HEADS, MLA_NOPE + MLA_ROPE),
                 ((0, 0), (0, 0), (0, LANES - MLA_NOPE - MLA_ROPE))).reshape(rank, MLA_HEADS * LANES)
    ukv = w_ukv.reshape(rank, MLA_HEADS, MLA_NOPE + HEAD_DIM)
    wk = jnp.pad(ukv[:, :, :MLA_NOPE], ((0, 0), (0, 0), (0, LANES - MLA_NOPE))).reshape(rank, MLA_HEADS * LANES)
    wv = ukv[:, :, MLA_NOPE:].reshape(rank, MLA_HEADS * HEAD_DIM)
    return [w.astype(BF16) for w in (w_a, w_b, w_c, w_vs, wq, wk, wv)]


def _rope_tables(seq):
    inv = 1.0 / (ROPE_THETA ** (jnp.arange(0, MLA_ROPE, 2, dtype=F32) / MLA_ROPE))
    ang = jnp.arange(seq, dtype=F32)[:, None] * inv[None, :]
    cos, sin = jnp.cos(ang), jnp.sin(ang)
    half = MLA_ROPE // 2
    z = lambda n: jnp.zeros((seq, n), F32)
    tail = LANES - MLA_NOPE - MLA_ROPE
    c = jnp.concatenate([jnp.ones((seq, MLA_NOPE), F32), cos, cos, z(tail)], axis=1)
    s1 = jnp.concatenate([z(MLA_NOPE), -sin, z(half), z(tail)], axis=1)
    s2 = jnp.concatenate([z(MLA_NOPE), z(half), sin, z(tail)], axis=1)
    return c, s1, s2


def _compress_weights(pos_k, w1_k, w2_k, pos_v, w1_v, w2_v):
    eye = jnp.eye(2 * NSA_GROUPS, dtype=F32)
    halves = []
    for a in range(CMP_LEN // CMP_STRIDE):
        rows = slice(a * CMP_STRIDE * HEAD_DIM, (a + 1) * CMP_STRIDE * HEAD_DIM)
        wk = w1_k[rows].reshape(CMP_STRIDE, HEAD_DIM, HEAD_DIM)
        wv = w1_v[rows].reshape(CMP_STRIDE, HEAD_DIM, HEAD_DIM)
        per_slot = jnp.stack([wk, wk, wv, wv], axis=0)
        full = jnp.einsum('st,srdj->rsdtj', eye, per_slot)
        halves.append(full.reshape(CMP_STRIDE * 4 * HEAD_DIM, 4 * HEAD_DIM).astype(BF16))
    w2 = jnp.einsum('st,sdj->sdtj', eye, jnp.stack([w2_k, w2_k, w2_v, w2_v])).reshape(4 * HEAD_DIM, 4 * HEAD_DIM)
    pos = jnp.concatenate([pos_k, pos_k, pos_v, pos_v], axis=1)
    pos = pos.reshape(CMP_LEN // CMP_STRIDE, 1, CMP_STRIDE * 4 * HEAD_DIM)
    pos = jnp.broadcast_to(pos, (pos.shape[0], 8, pos.shape[2])).reshape(-1, pos.shape[2])
    return pos, halves[0], halves[1], w2.astype(BF16)


def _overlap_table(n_cmp_pad, n_cmp):
    c0 = np.arange(n_cmp_pad)[None, :] * CMP_STRIDE
    s0 = np.arange(LANES)[:, None] * SEL_LEN
    ov = np.maximum(np.minimum(c0 + CMP_LEN, s0 + SEL_LEN) - np.maximum(c0, s0), 0) / CMP_LEN
    ov = ov * (np.arange(n_cmp_pad)[None, :] < n_cmp)
    return jnp.asarray(ov, BF16)


def _key_feature_table(tk):
    c = np.arange(tk)
    table = np.zeros((tk, LANES), np.float32)
    table[:, 0:3] = (c // POS_SPLIT)[:, None]
    table[:, 3:6] = (c % POS_SPLIT)[:, None]
    table[c, BLOCK_LANE0 + c // SEL_LEN] = 1.0
    return jnp.asarray(table, BF16)


def _gate_expand_table():
    width = NSA_HEADS * HEAD_DIM
    table = np.zeros((LANES, 3 * width), np.float32)
    for h in range(NSA_HEADS):
        for branch in range(3):
            table[h * 3 + branch, branch * width + h * HEAD_DIM:branch * width + (h + 1) * HEAD_DIM] = 1.0
    return jnp.asarray(table, BF16)


def _alibi_slopes(n):
    return jnp.asarray(2.0 ** (-8.0 * np.arange(1, n + 1) / n), dtype=F32)


def _layer0_mixer(x2, b, s, w_in, q_norm, w_uq, kv_norm, w_ukv, pos_k, w1_k, w2_k, pos_v, w1_v, w2_v, w_out):
    d = x2.shape[1]
    rank = d // 4
    w_a, w_b, w_c, w_vs, wq, wk, wv = _layer0_weights(w_in, w_uq, w_ukv, d)
    slab_a, slab_b, slab_c, vt_nsa = _project(x2, [w_a, w_b, w_c, w_vs], [F32, BF16, BF16, BF16],
                                              [False, False, False, True], b, s)
    rope_c, rope_s1, rope_s2 = _rope_tables(s)
    q, k, vt = _mla_prep(slab_a, q_norm.reshape(1, rank), kv_norm.reshape(1, rank), wq, wk, wv,
                         rope_c, rope_s1, rope_s2, b, s)
    o_mla = _mla_attn(q.reshape(b, s, -1), k.reshape(b, s, -1), vt)
    n_chunks = s // CMP_STRIDE
    n_cmp = (s - CMP_LEN) // CMP_STRIDE + 1
    pos, w1a, w1b, w2 = _compress_weights(pos_k, w1_k, w2_k, pos_v, w1_v, w2_v)
    cmp_k, cmp_vt = _compress(slab_c.reshape(b, n_chunks, CMP_STRIDE * slab_c.shape[1]), pos, w1a, w1b, w2, n_cmp)
    slab_b3 = slab_b.reshape(b, s, -1)
    key_features = _key_feature_table(min(KV_TILE, s))
    o_cmp, o_win, sel_bias_t, block_any = _nsa_cmp_win(slab_b3, cmp_k, cmp_vt, vt_nsa, key_features,
                                                       _overlap_table(n_chunks, n_cmp), s)
    o_nsa = _nsa_sel(slab_b3, vt_nsa, key_features, sel_bias_t, block_any, o_cmp, o_win, slab_a,
                     _gate_expand_table(), s, (2 * rank + LANES) // LANES)
    half = o_mla.shape[-1]
    w_out_b = w_out.astype(BF16)
    return [o_mla.reshape(b * s, half), o_nsa.reshape(b * s, -1)], [w_out_b[:half], w_out_b[half:]]


def _layer1_mixer(x2, b, s, w_qkv, lam_q1, lam_k1, lam_q2, lam_k2, subln_g, w_o, layer_idx):
    d = x2.shape[1]
    w = w_qkv.astype(BF16)
    qk, vt = _project(x2, [w[:, :2 * d], w[:, 2 * d:]], [BF16, BF16], [False, True], b, s)
    lam_init = 0.8 - 0.6 * math.exp(-0.3 * layer_idx)
    lam_vecs = jnp.stack([lam_q1, lam_k1, lam_q2, lam_k2]).astype(F32)
    o = _diff_attn(qk.reshape(b, s, -1), vt, _key_feature_table(min(KV_TILE, s)), _alibi_slopes(DIFF_HEADS),
                   lam_vecs, subln_g.reshape(1, -1), lam_init)
    return [o.reshape(b * s, -1)], [w_o.astype(BF16)]


def kernel(x, l0_w_in, l0_mla_q_norm, l0_mla_w_uq, l0_mla_kv_norm, l0_mla_w_ukv, l0_nsa_cmp_pos_k, l0_nsa_cmp_w1_k, l0_nsa_cmp_w2_k, l0_nsa_cmp_pos_v, l0_nsa_cmp_w1_v, l0_nsa_cmp_w2_v, l0_w_out, l0_ln_mix_g, l0_ln_mix_b, l0_w_up, l0_w_down, l0_ln_ffn_g, l0_ln_ffn_b, l1_w_qkv, l1_lam_q1, l1_lam_k1, l1_lam_q2, l1_lam_k2, l1_subln_g, l1_w_o, l1_ln_mix_g, l1_ln_mix_b, l1_w_up, l1_w_down, l1_ln_ffn_g, l1_ln_ffn_b):
    b, s, d = x.shape
    x2 = x.reshape(b * s, d)
    vec = lambda p: p.reshape(1, d)
    acts, weights = _layer0_mixer(x2, b, s, l0_w_in, l0_mla_q_norm, l0_mla_w_uq, l0_mla_kv_norm, l0_mla_w_ukv,
                                  l0_nsa_cmp_pos_k, l0_nsa_cmp_w1_k, l0_nsa_cmp_w2_k,
                                  l0_nsa_cmp_pos_v, l0_nsa_cmp_w1_v, l0_nsa_cmp_w2_v, l0_w_out)
    x2 = _out_ln(acts, weights, x2, vec(l0_ln_mix_g), vec(l0_ln_mix_b))
    x2 = _mlp(x2, l0_w_up.astype(BF16), l0_w_down.astype(BF16), vec(l0_ln_ffn_g), vec(l0_ln_ffn_b))
    acts, weights = _layer1_mixer(x2, b, s, l1_w_qkv, l1_lam_q1, l1_lam_k1, l1_lam_q2, l1_lam_k2,
                                  l1_subln_g, l1_w_o, 1)
    x2 = _out_ln(acts, weights, x2, vec(l1_ln_mix_g), vec(l1_ln_mix_b))
    x2 = _mlp(x2, l1_w_up.astype(BF16), l1_w_down.astype(BF16), vec(l1_ln_ffn_g), vec(l1_ln_ffn_b))
    return x2.reshape(b, s, d)
```

```python
import functools
import math

import jax
import jax.numpy as jnp
import numpy as np
from jax import lax
from jax.experimental import pallas as pl
from jax.experimental.pallas import tpu as pltpu

F32 = jnp.float32
BF16 = jnp.bfloat16

LANES = 128
SUBLANES = 8
BF16_ROWS = 16
MXU_DEPTH = 256
HEAD_DIM = 64
FLASH_Q_TILE = 512
KV_TILE = 512
KEY_CHUNK = 32
ROW_TILE = 512
FF_TILE = 1024
VMEM_LIMIT = 56 * 1024 * 1024

NEG_INF = -1e30
LOG2E = math.log2(math.e)
LN_EPS = 1e-5
RMS_EPS = 1e-6
DEPTH = 2
DN_ALPHA = (2.0 * DEPTH) ** 0.25

MLA_HEADS = 8
MLA_NOPE = 64
MLA_ROPE = 32
ROPE_THETA = 10000.0
NSA_HEADS = 8
NSA_GROUPS = 2
NSA_HG = NSA_HEADS // NSA_GROUPS
CMP_LEN = 32
CMP_STRIDE = 16
SEL_LEN = 64
SEL_TOPK = 16
WINDOW = 512
FORCE_BONUS = 1e3
DIFF_HEADS = 8

POS_SPLIT = 16
FEATURE_ROWS = 16
BLOCK_LANE0 = 8
BLOCKS_PER_TILE = KV_TILE // SEL_LEN


def _params(*sem):
    return pltpu.CompilerParams(dimension_semantics=sem, vmem_limit_bytes=VMEM_LIMIT)


def _dot(a, b):
    return jnp.dot(a, b, preferred_element_type=F32)


def _split_bf16(x):
    hi = x.astype(BF16)
    lo = (x - hi.astype(F32)).astype(BF16)
    return hi, lo


def _layer_norm(z, g, b):
    mu = jnp.mean(z, axis=-1, keepdims=True)
    zc = z - mu
    var = jnp.mean(zc * zc, axis=-1, keepdims=True)
    return zc * lax.rsqrt(var + LN_EPS) * g + b


def _rms_norm(z, g, eps):
    return z * lax.rsqrt(jnp.mean(z * z, axis=-1, keepdims=True) + eps) * g


def _lane_iota(shape):
    return lax.broadcasted_iota(jnp.int32, shape, 1)


def _keep_half(x, half):
    lane = _lane_iota(x.shape)
    keep = (lane < HEAD_DIM) if half == 0 else (lane >= HEAD_DIM)
    return jnp.where(keep, x, jnp.zeros_like(x))


def _move_head(slab, src_half, dst_half):
    if src_half != dst_half:
        slab = pltpu.roll(slab, HEAD_DIM, 1)
    return _keep_half(slab, dst_half)


def _store_transposed(o_ref, res):
    for c in range(res.shape[1] // LANES):
        cols = slice(c * LANES, (c + 1) * LANES)
        o_ref[0, 0, cols, :] = res[:, cols].T.astype(o_ref.dtype)


def _proj_kernel(x_ref, *refs, transposed):
    n_out = len(transposed)
    w_refs, o_refs = refs[:n_out], refs[n_out:]
    xb = x_ref[...].astype(BF16)
    for w_ref, o_ref, t in zip(w_refs, o_refs, transposed):
        res = _dot(xb, w_ref[...])
        if t:
            _store_transposed(o_ref, res)
        else:
            o_ref[...] = res.astype(o_ref.dtype)


def _transposed_out(b, seq, width, tm):
    per_seq = seq // tm
    spec = pl.BlockSpec((1, 1, width, tm), lambda i: (i // per_seq, i % per_seq, 0, 0))
    return spec, jax.ShapeDtypeStruct((b, per_seq, width, tm), BF16)


def _project(x, weights, out_dtypes, transposed, b, seq):
    m, k = x.shape
    tm = min(KV_TILE, seq)
    specs, shapes = [], []
    for w, dt, t in zip(weights, out_dtypes, transposed):
        if t:
            spec, shape = _transposed_out(b, seq, w.shape[1], tm)
        else:
            spec, shape = pl.BlockSpec((tm, w.shape[1]), lambda i: (i, 0)), jax.ShapeDtypeStruct((m, w.shape[1]), dt)
        specs.append(spec)
        shapes.append(shape)
    return pl.pallas_call(
        functools.partial(_proj_kernel, transposed=tuple(transposed)),
        grid=(m // tm,),
        in_specs=[pl.BlockSpec((tm, k), lambda i: (i, 0))]
        + [pl.BlockSpec(w.shape, lambda i: (0, 0)) for w in weights],
        out_specs=specs,
        out_shape=shapes,
        compiler_params=_params("parallel"),
        name="project",
    )(x, *weights)


def _rope_slab(slab, c, s1, s2):
    half = MLA_ROPE // 2
    up = pltpu.roll(slab, half, 1)
    down = pltpu.roll(slab, LANES - half, 1)
    return slab * c + down * s1 + up * s2


def _mla_prep_kernel(ql_ref, kvl_ref, kpe_ref, qg_ref, kvg_ref, wq_ref, wk_ref, wv_ref,
                     c_ref, s1_ref, s2_ref, q_ref, k_ref, vt_ref, *, q_scale):
    c, s1, s2 = c_ref[...], s1_ref[...], s2_ref[...]
    qn = _rms_norm(ql_ref[...], qg_ref[...], RMS_EPS).astype(BF16)
    kvn = _rms_norm(kvl_ref[...], kvg_ref[...], RMS_EPS).astype(BF16)
    q = _dot(qn, wq_ref[...])
    k = _dot(kvn, wk_ref[...])
    _store_transposed(vt_ref, _dot(kvn, wv_ref[...]))
    kpe = _rope_slab(kpe_ref[...], c, s1, s2)
    for h in range(MLA_HEADS):
        sl = slice(h * LANES, (h + 1) * LANES)
        q_ref[:, sl] = (_rope_slab(q[:, sl], c, s1, s2) * q_scale).astype(q_ref.dtype)
        k_ref[:, sl] = (k[:, sl] + kpe).astype(k_ref.dtype)


def _mla_prep(slab_a, q_gain, kv_gain, wq, wk, wv, rope_c, rope_s1, rope_s2, b, seq):
    m = slab_a.shape[0]
    tm = min(KV_TILE, seq)
    per_seq = seq // tm
    rank = q_gain.shape[1]
    row = lambda j: (lambda i: (i, j))
    tab = lambda i: (i % per_seq, 0)
    const = lambda i: (0, 0)
    hw = MLA_HEADS * LANES
    vt_spec, vt_shape = _transposed_out(b, seq, wv.shape[1], tm)
    return pl.pallas_call(
        functools.partial(_mla_prep_kernel, q_scale=float((MLA_NOPE + MLA_ROPE) ** -0.5 * LOG2E)),
        grid=(m // tm,),
        in_specs=[pl.BlockSpec((tm, rank), row(0)), pl.BlockSpec((tm, rank), row(1)),
                  pl.BlockSpec((tm, LANES), row(2 * rank // LANES)),
                  pl.BlockSpec((1, rank), const), pl.BlockSpec((1, rank), const),
                  pl.BlockSpec(wq.shape, const), pl.BlockSpec(wk.shape, const), pl.BlockSpec(wv.shape, const),
                  pl.BlockSpec((tm, LANES), tab), pl.BlockSpec((tm, LANES), tab), pl.BlockSpec((tm, LANES), tab)],
        out_specs=[pl.BlockSpec((tm, hw), row(0)), pl.BlockSpec((tm, hw), row(0)), vt_spec],
        out_shape=[jax.ShapeDtypeStruct((m, hw), BF16), jax.ShapeDtypeStruct((m, hw), BF16), vt_shape],
        compiler_params=_params("parallel"),
        name="mla_prep",
    )(slab_a, slab_a, slab_a, q_gain, kv_gain, wq, wk, wv, rope_c, rope_s1, rope_s2)


def _flash_scratch(n_streams, v_rows, tq, tk):
    scores = pltpu.VMEM((n_streams, tk, tq), F32)
    stat = pltpu.VMEM((n_streams, 1, tq), F32)
    probs = pltpu.VMEM((n_streams, tk, tq), BF16)
    slot = [scores, probs, stat]
    return slot + slot + [stat, pltpu.VMEM((n_streams, v_rows + BF16_ROWS, tq), F32)]


def _chunk_rows(c):
    return slice(c * KEY_CHUNK, (c + 1) * KEY_CHUNK)


def _diagonal_blocks(tk, tq, keys_up_to_query):
    for c in range(tk // KEY_CHUNK):
        for v in range(tq // LANES):
            first_key, last_key = c * KEY_CHUNK, (c + 1) * KEY_CHUNK - 1
            first_query, last_query = v * LANES, (v + 1) * LANES - 1
            all_up_to = last_key <= first_query
            all_beyond = first_key > last_query
            if all_up_to or all_beyond:
                kind = "visible" if all_up_to == keys_up_to_query else "hidden"
            else:
                kind = "mixed"
            yield _chunk_rows(c), slice(first_query, last_query + 1), first_query - first_key, kind


def _fold_rows(x):
    return x.reshape(x.shape[0] // SUBLANES, SUBLANES, x.shape[1])


def _flash_transposed(diagonal_tile, n_streams, q_start, tq, tk, key_operand, query_operand, values, offset, scratch,
                      rest_tile, rest_count):
    assert tq == tk, "the diagonal tile is taken to start at the first query of the tile"
    slot_a, slot_b, (m_ref, acc_ref) = scratch[0:3], scratch[3:6], scratch[6:]
    n_chunks = tk // KEY_CHUNK
    for i in range(n_streams):
        m_ref[i] = jnp.full((1, tq), NEG_INF, F32)
        acc_ref[i] = jnp.zeros(acc_ref.shape[1:], F32)

    def column_max(s_ref, i):
        part = jnp.full((SUBLANES, tq), NEG_INF, F32)
        for c in range(n_chunks):
            part = jnp.maximum(part, jnp.max(_fold_rows(s_ref[i, _chunk_rows(c), :]), axis=0))
        return jnp.max(part, axis=0, keepdims=True)

    def stage1(j, slot, i):
        slot[0][i] = _dot(key_operand(j, i), query_operand(j, i))

    def blocks():
        return _diagonal_blocks(tk, tq, True)

    def stage2(j, slot, i, diagonal):
        s_ref, p_ref, alpha_ref = slot
        if diagonal:
            rel = (lax.broadcasted_iota(jnp.int32, (KEY_CHUNK, LANES), 0)
                   - lax.broadcasted_iota(jnp.int32, (KEY_CHUNK, LANES), 1))
            parts = [jnp.full((SUBLANES, LANES), NEG_INF, F32) for _ in range(tq // LANES)]
            for rows, lanes, bound, kind in blocks():
                if kind == "hidden":
                    continue
                s = s_ref[i, rows, lanes]
                if kind == "mixed":
                    s = jnp.where(rel <= bound, s, NEG_INF)
                    s_ref[i, rows, lanes] = s
                v = lanes.start // LANES
                parts[v] = jnp.maximum(parts[v], jnp.max(_fold_rows(s), axis=0))
            mx = jnp.max(jnp.concatenate(parts, axis=1), axis=0, keepdims=True)
        else:
            mx = column_max(s_ref, i)
        off = offset(j, i)
        m_prev = m_ref[i]
        if off is None:
            m_next = jnp.maximum(m_prev, mx)
            shift = m_next
        else:
            m_next = jnp.maximum(m_prev, mx + off)
            shift = m_next - off
        alpha = jnp.exp2(m_prev - m_next)
        if diagonal:
            for rows, lanes, _, kind in blocks():
                if kind == "hidden":
                    p_ref[i, rows, lanes] = jnp.zeros((KEY_CHUNK, LANES), BF16)
                else:
                    p_ref[i, rows, lanes] = jnp.exp2(s_ref[i, rows, lanes] - shift[:, lanes]).astype(BF16)
        else:
            for c in range(n_chunks):
                p_ref[i, _chunk_rows(c), :] = jnp.exp2(s_ref[i, _chunk_rows(c), :] - shift).astype(BF16)
        m_ref[i] = m_next
        alpha_ref[i] = alpha

    def step(accumulate=None, produce=(), exponentiate=None, diagonal=False):
        products = []
        for i in range(n_streams + 1):
            if i < n_streams:
                if accumulate is not None:
                    tile, slot = accumulate
                    products.append(_dot(_values_and_ones(values(tile, i)), slot[1][i]))
                for tile, slot in produce:
                    stage1(tile, slot, i)
                if exponentiate is not None:
                    stage2(*exponentiate, i, diagonal)
            if accumulate is not None and i > 0:
                acc_ref[i - 1] = accumulate[1][2][i - 1] * acc_ref[i - 1] + products[i - 1]

    step(produce=((diagonal_tile, slot_a), (rest_tile(0), slot_b)), exponentiate=(diagonal_tile, slot_a),
         diagonal=True)
    last = rest_count(m_ref)

    def tile_at(position):
        return rest_tile(jnp.clip(position, 1, jnp.maximum(last, 1)) - 1)

    def diag_or_rest(position):
        return jnp.where(position == 0, diagonal_tile, tile_at(position))

    def pair(t, carry):
        p1 = 2 * t + 1
        step((diag_or_rest(p1 - 1), slot_a), ((tile_at(p1 + 1), slot_a),), (tile_at(p1), slot_b))
        step((tile_at(p1), slot_b), ((tile_at(p1 + 2), slot_b),), (tile_at(p1 + 1), slot_a))
        return carry

    lax.fori_loop(0, last // 2, pair, 0)

    @pl.when(last % 2 == 1)
    def _():
        step(accumulate=(diag_or_rest(last - 1), slot_a), exponentiate=(tile_at(last), slot_b))
        step(accumulate=(tile_at(last), slot_b))

    @pl.when(last % 2 == 0)
    def _():
        step(accumulate=(diag_or_rest(last), slot_a))


def _flash_result(scratch, i):
    acc = scratch[-1][i]
    v_rows = acc.shape[0] - BF16_ROWS
    return acc[:v_rows] / acc[v_rows:v_rows + 1]


def _kv_rows(j, tk):
    return pl.ds(pl.multiple_of(j * tk, tk), tk)


def _transposed_bf16(x):
    return x.astype(F32).T.astype(BF16)


def _alibi_rows(coef, tq):
    c = jnp.zeros((1, tq), F32) + coef
    hi = c.astype(BF16).astype(F32)
    rest = c - hi
    mid = rest.astype(BF16).astype(F32)
    lo = rest - mid
    zero = jnp.zeros((1, tq), F32)
    return jnp.concatenate([POS_SPLIT * hi, POS_SPLIT * mid, POS_SPLIT * lo, hi, mid, lo, zero, zero], axis=0)


def _augmented_query(q_t, feature_rows):
    tq = q_t.shape[1]
    pad = jnp.zeros((MXU_DEPTH - LANES - FEATURE_ROWS, tq), BF16)
    return jnp.concatenate([q_t, feature_rows.astype(BF16), pad], axis=0)


MLA_STEP_HEADS = 4


def _mla_attn_kernel(q_ref, k_ref, vt_ref, o_ref, *scratch, tq, tk):
    q_start = pl.program_id(2) * tq
    n = MLA_STEP_HEADS
    queries = [_transposed_bf16(q_ref[0, :, hh * LANES:(hh + 1) * LANES]) for hh in range(n)]
    n_full = q_start // tk
    _flash_transposed(
        n_full, n, q_start, tq, tk,
        lambda j, i: k_ref[0, _kv_rows(j, tk), i * LANES:(i + 1) * LANES],
        lambda j, i: queries[i],
        lambda j, i: vt_ref[0, j, i * HEAD_DIM:(i + 1) * HEAD_DIM, :],
        lambda j, i: None, scratch,
        rest_tile=lambda k: k, rest_count=lambda m_ref: n_full)
    for pair in range(n // 2):
        o_t = jnp.concatenate([_flash_result(scratch, 2 * pair), _flash_result(scratch, 2 * pair + 1)], axis=0)
        o_ref[0, :, pair * LANES:(pair + 1) * LANES] = o_t.T.astype(o_ref.dtype)


def _mla_attn(q, k, vt):
    b, s, _ = q.shape
    tq, tk = min(FLASH_Q_TILE, s), min(KV_TILE, s)
    n = MLA_STEP_HEADS
    groups = MLA_HEADS // n
    return pl.pallas_call(
        functools.partial(_mla_attn_kernel, tq=tq, tk=tk),
        grid=(b, groups, s // tq),
        in_specs=[pl.BlockSpec((1, tq, n * LANES), lambda bi, p, i: (bi, i, p)),
                  pl.BlockSpec((1, s, n * LANES), lambda bi, p, i: (bi, 0, p)),
                  pl.BlockSpec((1, s // tk, n * HEAD_DIM, tk), lambda bi, p, i: (bi, 0, p, 0))],
        out_specs=pl.BlockSpec((1, tq, n * HEAD_DIM), lambda bi, p, i: (bi, i, p)),
        out_shape=jax.ShapeDtypeStruct((b, s, MLA_HEADS * HEAD_DIM), BF16),
        scratch_shapes=_flash_scratch(n, HEAD_DIM, tq, tk),
        compiler_params=_params("parallel", "parallel", "arbitrary"),
        name="mla_attn",
    )(q, k, vt)


def _gelu_tanh(x):
    return 0.5 * x * (1.0 + jnp.tanh(math.sqrt(2.0 / math.pi) * (x + 0.044715 * (x * x * x))))


def _compress_kernel(x_ref, pos_ref, w1a_ref, w1b_ref, w2_ref, k_ref, vt_ref, *, n_real):
    x = x_ref[0]
    n = x.shape[0]
    first = _dot(x, w1a_ref[...])
    second = _dot(x, w1b_ref[...])
    pos_hi, pos_lo = _split_bf16(pos_ref[...])
    bias = (_dot(pos_hi[:8], w1a_ref[...]) + _dot(pos_lo[:8], w1a_ref[...])
            + _dot(pos_hi[8:], w1b_ref[...]) + _dot(pos_lo[8:], w1b_ref[...]))[:1]
    pre = first + pltpu.roll(second, n - 1, 0) + bias
    out = _dot(_gelu_tanh(pre).astype(BF16), w2_ref[...])
    real = lax.broadcasted_iota(jnp.int32, out.shape, 0) < n_real
    out = jnp.where(real, out, 0.0)
    half = out.shape[1] // 2
    k_ref[0] = out[:, :half].astype(k_ref.dtype)
    vt_ref[0] = out[:, half:].T.astype(vt_ref.dtype)


def _compress(x_chunks, pos_exp, w1a, w1b, w2, n_real):
    b, n, width = x_chunks.shape
    half = w2.shape[1] // 2
    const = lambda bi: (0, 0)
    return pl.pallas_call(
        functools.partial(_compress_kernel, n_real=n_real),
        grid=(b,),
        in_specs=[pl.BlockSpec((1, n, width), lambda bi: (bi, 0, 0)),
                  pl.BlockSpec(pos_exp.shape, const), pl.BlockSpec(w1a.shape, const),
                  pl.BlockSpec(w1b.shape, const), pl.BlockSpec(w2.shape, const)],
        out_specs=[pl.BlockSpec((1, n, half), lambda bi: (bi, 0, 0)), pl.BlockSpec((1, half, n), lambda bi: (bi, 0, 0))],
        out_shape=[jax.ShapeDtypeStruct((b, n, half), BF16), jax.ShapeDtypeStruct((b, half, n), BF16)],
        compiler_params=_params("parallel"),
        name="nsa_compress",
    )(x_chunks, pos_exp, w1a, w1b, w2)


def _nsa_head_slope(h):
    return float(2.0 ** (-8.0 * (h + 1) / NSA_HEADS))


def _nsa_queries(q_ref, g, scale):
    out = []
    for hg in range(NSA_HG):
        h = g * NSA_HG + hg
        slab = q_ref[0, :, (h // 2) * LANES:(h // 2 + 1) * LANES].astype(F32) * scale
        out.append(_move_head(slab, h % 2, g))
    return out


def _selection_frame(q_pos, n_blocks):
    blk = lax.broadcasted_iota(jnp.int32, (n_blocks, q_pos.shape[1]), 0)
    behind = q_pos // SEL_LEN - blk
    near = jnp.abs(2 * behind - 1) <= 1
    bonus = jnp.where(near, FORCE_BONUS, jnp.where(blk == 0, FORCE_BONUS, 0.0))
    return blk, bonus, behind >= 0


def _select_blocks(imp_t, blk, bonus, allowed):
    val = jnp.where(allowed, imp_t + bonus, NEG_INF)
    for _ in range(SEL_TOPK):
        top = jnp.max(val, axis=0, keepdims=True)
        first = jnp.min(jnp.where(val == top, blk, imp_t.shape[0]), axis=0, keepdims=True)
        val = jnp.where(blk == first, -jnp.inf, val)
    return jnp.where(val == -jnp.inf, 0.0, NEG_INF)


def _masked_softmax_pass(s_ref, p_ref, tiles, tq):
    part = jnp.full((SUBLANES, tq), NEG_INF, F32)
    for t, rows, keep, off in tiles:
        for c in range(rows // KEY_CHUNK):
            s = jnp.where(keep(_chunk_rows(c)), s_ref[t, _chunk_rows(c), :], NEG_INF)
            s_ref[t, _chunk_rows(c), :] = s
            part = jnp.maximum(part, jnp.max(_fold_rows(s), axis=0) + off)
    m = jnp.max(part, axis=0, keepdims=True)
    for t, rows, keep, off in tiles:
        shift = m - off
        for c in range(rows // KEY_CHUNK):
            p_ref[t, _chunk_rows(c), :] = jnp.exp2(s_ref[t, _chunk_rows(c), :] - shift).astype(BF16)
    return m > 0.5 * NEG_INF


def _window_softmax(s_ref, p_ref, prev_offset, tq, tk):
    rel = (lax.broadcasted_iota(jnp.int32, (KEY_CHUNK, LANES), 0)
           - lax.broadcasted_iota(jnp.int32, (KEY_CHUNK, LANES), 1))
    tiles = [(0, False, prev_offset), (1, True, 0.0)]
    parts = [jnp.full((SUBLANES, LANES), NEG_INF, F32) for _ in range(tq // LANES)]
    for t, below, off in tiles:
        for rows, lanes, bound, kind in _diagonal_blocks(tk, tq, below):
            if kind == "hidden":
                continue
            s = s_ref[t, rows, lanes]
            if kind == "mixed":
                s = jnp.where((rel <= bound) if below else (rel > bound), s, NEG_INF)
                s_ref[t, rows, lanes] = s
            v = lanes.start // LANES
            parts[v] = jnp.maximum(parts[v], jnp.max(_fold_rows(s), axis=0) + off)
    m = jnp.max(jnp.concatenate(parts, axis=1), axis=0, keepdims=True)
    for t, below, off in tiles:
        shift = m - off
        for rows, lanes, _, kind in _diagonal_blocks(tk, tq, below):
            if kind == "hidden":
                p_ref[t, rows, lanes] = jnp.zeros((KEY_CHUNK, LANES), BF16)
            else:
                p_ref[t, rows, lanes] = jnp.exp2(s_ref[t, rows, lanes] - shift[:, lanes]).astype(BF16)


def _values_and_ones(vt):
    return jnp.concatenate([vt, jnp.ones((BF16_ROWS, vt.shape[1]), BF16)], axis=0)


def _nsa_cmp_win_kernel(q_ref, kc_ref, vct_ref, kw_ref, vwt_ref, feat_ref, ovt_ref, oc_ref, ow_ref, sel_ref, any_ref,
                        end_ref, sc_ref, sw_ref, pc_ref, pw_ref, *, tq, tk):
    qi = pl.program_id(1)
    q_start = qi * tq
    n_cmp = kc_ref.shape[1]
    end_ref[...] = (lax.broadcasted_iota(jnp.int32, (n_cmp, tq), 0) * CMP_STRIDE + (CMP_LEN - 1)
                    - lax.broadcasted_iota(jnp.int32, (n_cmp, tq), 1))
    prev_tile = jnp.maximum(qi - 1, 0)
    zeros = jnp.zeros((FEATURE_ROWS - SUBLANES, tq), F32)
    k_cmp = jnp.concatenate([kc_ref[0], feat_ref[0:n_cmp, :]], axis=1)
    k_win = [jnp.concatenate([kw_ref[0, _kv_rows(j, tk), :], feat_ref[...]], axis=1) for j in (prev_tile, qi)]
    frame = _selection_frame(q_start + lax.broadcasted_iota(jnp.int32, (1, tq), 1), LANES)
    for g in range(NSA_GROUPS):
        queries = [_transposed_bf16(q) for q in _nsa_queries(q_ref, g, HEAD_DIM ** -0.5 * LOG2E)]
        group_rows = slice(g * HEAD_DIM, (g + 1) * HEAD_DIM)
        imp_t = jnp.zeros((LANES, tq), F32)
        out_c, out_w = [], []
        for hg in range(NSA_HG):
            coef = _nsa_head_slope(g * NSA_HG + hg) * LOG2E
            buf = hg % 2
            sc, sw, pc, pw = sc_ref.at[buf], sw_ref.at[buf], pc_ref.at[buf], pw_ref.at[buf]
            cmp_query = _augmented_query(queries[hg], jnp.concatenate([_alibi_rows(CMP_STRIDE * coef, tq), zeros], 0))
            win_query = _augmented_query(queries[hg], jnp.concatenate([_alibi_rows(coef, tq), zeros], 0))
            sc[0] = _dot(k_cmp, cmp_query)
            sw[0] = _dot(k_win[0], win_query)
            sw[1] = _dot(k_win[1], win_query)
            has_any = _masked_softmax_pass(sc, pc, [(0, n_cmp, lambda r: end_ref[r, :] <= q_start, 0.0)], tq)
            acc = _dot(_values_and_ones(vct_ref[0, group_rows, :]), pc[0])
            inv = jnp.where(has_any, 1.0 / acc[HEAD_DIM:HEAD_DIM + 1], 0.0)
            out_c.append(acc[:HEAD_DIM] * inv)
            imp_t = imp_t + _dot(ovt_ref[...], pc[0]) * inv
            _window_softmax(sw, pw, jnp.where(qi >= 1, -coef * tk, NEG_INF), tq, tk)
            acc = (_dot(_values_and_ones(vwt_ref[0, prev_tile, group_rows, :]), pw[0])
                   + _dot(_values_and_ones(vwt_ref[0, qi, group_rows, :]), pw[1]))
            out_w.append(acc[:HEAD_DIM] / acc[HEAD_DIM:HEAD_DIM + 1])
        for pair in range(NSA_HG // 2):
            cols = slice((g * 2 + pair) * LANES, (g * 2 + pair + 1) * LANES)
            oc_ref[0, :, cols] = jnp.concatenate(out_c[2 * pair:2 * pair + 2], axis=0).T
            ow_ref[0, :, cols] = jnp.concatenate(out_w[2 * pair:2 * pair + 2], axis=0).T
        bias_t = _select_blocks(imp_t, *frame)
        sel_ref[0, g * LANES:(g + 1) * LANES, :] = bias_t
        any_ref[0, 0, g * LANES:(g + 1) * LANES, :] = jnp.broadcast_to(
            jnp.max(bias_t, axis=1, keepdims=True), (LANES, LANES))


def _nsa_cmp_win(slab_b, cmp_k, cmp_vt, vt_nsa, key_features, overlap_t, seq):
    b = slab_b.shape[0]
    tq, tk = min(FLASH_Q_TILE, seq), min(KV_TILE, seq)
    assert tq == tk == WINDOW, "the window branch is written for one previous and one diagonal key tile"
    n_cmp = cmp_k.shape[1]
    qw = NSA_HEADS * HEAD_DIM
    base = qw // LANES
    tile = lambda bi, i: (bi, i, 0)
    return pl.pallas_call(
        functools.partial(_nsa_cmp_win_kernel, tq=tq, tk=tk),
        grid=(b, seq // tq),
        in_specs=[pl.BlockSpec((1, tq, qw), tile),
                  pl.BlockSpec((1, n_cmp, LANES), lambda bi, i: (bi, 0, 0)),
                  pl.BlockSpec((1, LANES, n_cmp), lambda bi, i: (bi, 0, 0)),
                  pl.BlockSpec((1, seq, LANES), lambda bi, i: (bi, 0, base + 1)),
                  pl.BlockSpec((1, seq // tk, LANES, tk), lambda bi, i: (bi, 0, 1, 0)),
                  pl.BlockSpec(key_features.shape, lambda bi, i: (0, 0)),
                  pl.BlockSpec(overlap_t.shape, lambda bi, i: (0, 0))],
        out_specs=[pl.BlockSpec((1, tq, qw), tile), pl.BlockSpec((1, tq, qw), tile),
                   pl.BlockSpec((1, NSA_GROUPS * LANES, tq), lambda bi, i: (bi, 0, i)),
                   pl.BlockSpec((1, 1, NSA_GROUPS * LANES, LANES), lambda bi, i: (bi, i, 0, 0))],
        out_shape=[jax.ShapeDtypeStruct((b, seq, qw), F32), jax.ShapeDtypeStruct((b, seq, qw), F32),
                   jax.ShapeDtypeStruct((b, NSA_GROUPS * LANES, seq), F32),
                   jax.ShapeDtypeStruct((b, seq // tq, NSA_GROUPS * LANES, LANES), F32)],
        scratch_shapes=[pltpu.VMEM((n_cmp, tq), jnp.int32),
                        pltpu.VMEM((2, 1, n_cmp, tq), F32), pltpu.VMEM((2, 2, tk, tq), F32),
                        pltpu.VMEM((2, 1, n_cmp, tq), BF16), pltpu.VMEM((2, 2, tk, tq), BF16)],
        compiler_params=_params("parallel", "arbitrary"),
        name="nsa_cmp_win",
    )(slab_b, cmp_k, cmp_vt, slab_b, vt_nsa, key_features, overlap_t)


def _nsa_sel_kernel(tiles_ref, counts_ref, q_ref, k_ref, vt_ref, feat_ref, sel_ref, oc_ref, ow_ref, gate_ref, gx_ref,
                    o_ref, *scratch, tq, tk, max_tiles):
    qi = pl.program_id(1)
    q_start = qi * tq
    diagonal_tile = q_start // tk
    out_slabs = []
    for g in range(NSA_GROUPS):
        queries = [_transposed_bf16(q) for q in _nsa_queries(q_ref, g, HEAD_DIM ** -0.5 * LOG2E)]
        coefs = [_nsa_head_slope(g * NSA_HG + hg) * LOG2E for hg in range(NSA_HG)]
        alibi = [_alibi_rows(c, tq) for c in coefs]
        entry = (pl.program_id(0) * pl.num_programs(1) + qi) * NSA_GROUPS + g
        n_active = counts_ref[entry]

        def listed_tile(k, entry=entry):
            return tiles_ref[entry * max_tiles + jnp.minimum(k, max_tiles - 1)]

        def key_operand(j, i):
            return jnp.concatenate([k_ref[0, _kv_rows(j, tk), :], feat_ref[...]], axis=1)

        def query_operand(j, i, g=g, queries=queries, alibi=alibi):
            first_block = pl.multiple_of(g * LANES + j * BLOCKS_PER_TILE, BLOCKS_PER_TILE)
            blocks = sel_ref[0, pl.ds(first_block, BLOCKS_PER_TILE), :]
            return _augmented_query(queries[i], jnp.concatenate([alibi[i], blocks], axis=0))

        _flash_transposed(
            diagonal_tile, NSA_HG, q_start, tq, tk, key_operand, query_operand,
            lambda j, i, g=g: vt_ref[0, j, g * HEAD_DIM:(g + 1) * HEAD_DIM, :],
            lambda j, i, coefs=coefs: coefs[i] * (j * tk - q_start).astype(F32), scratch,
            rest_tile=listed_tile, rest_count=lambda m_ref, n_active=n_active: n_active)
        heads = [_flash_result(scratch, hg) for hg in range(NSA_HG)]
        for pair in range(NSA_HG // 2):
            out_slabs.append(jnp.concatenate(heads[2 * pair:2 * pair + 2], axis=0).T)
    gates = jax.nn.sigmoid(gate_ref[...])
    g_hi, g_lo = _split_bf16(gates)
    width = NSA_HEADS * HEAD_DIM
    for i, o_sel in enumerate(out_slabs):
        mixed = None
        for branch, o_branch in enumerate((oc_ref[0, :, i * LANES:(i + 1) * LANES], o_sel,
                                           ow_ref[0, :, i * LANES:(i + 1) * LANES])):
            gx = gx_ref[:, branch * width + i * LANES:branch * width + (i + 1) * LANES]
            term = (_dot(g_hi, gx) + _dot(g_lo, gx)) * o_branch
            mixed = term if mixed is None else mixed + term
        o_ref[0, :, i * LANES:(i + 1) * LANES] = mixed.astype(o_ref.dtype)


def _active_key_tiles(block_any, tq, tk):
    b, n_q = block_any.shape[:2]
    max_tiles = LANES // BLOCKS_PER_TILE
    hit = block_any[..., 0].reshape(b, n_q, NSA_GROUPS, max_tiles, BLOCKS_PER_TILE).max(axis=-1) > 0.5 * NEG_INF
    before_diagonal = jnp.arange(max_tiles)[None, :] < (jnp.arange(n_q) * tq // tk)[:, None]
    hit = hit & before_diagonal[None, :, None, :]
    rank = jnp.cumsum(hit.astype(jnp.int32), axis=-1) - 1
    slots = jnp.arange(max_tiles, dtype=jnp.int32)
    in_slot = hit[..., None, :] & (rank[..., None, :] == slots[:, None])
    tiles = jnp.sum(jnp.where(in_slot, slots, 0), axis=-1)
    return tiles.astype(jnp.int32).reshape(-1), hit.sum(axis=-1).astype(jnp.int32).reshape(-1), max_tiles


def _nsa_sel(slab_b, vt, key_features, sel_bias_t, block_any, o_cmp, o_win, slab_a, gate_expand, seq, gate_col_block):
    b = slab_b.shape[0]
    tq, tk = min(FLASH_Q_TILE, seq), min(KV_TILE, seq)
    qw = NSA_HEADS * HEAD_DIM
    base = qw // LANES
    per_seq = seq // tq
    tiles, counts, max_tiles = _active_key_tiles(block_any, tq, tk)
    tile = lambda bi, i, *_: (bi, i, 0)
    grid_spec = pltpu.PrefetchScalarGridSpec(
        num_scalar_prefetch=2,
        grid=(b, seq // tq),
        in_specs=[pl.BlockSpec((1, tq, qw), tile),
                  pl.BlockSpec((1, seq, LANES), lambda bi, i, *_: (bi, 0, base)),
                  pl.BlockSpec((1, seq // tk, LANES, tk), lambda bi, i, *_: (bi, 0, 0, 0)),
                  pl.BlockSpec(key_features.shape, lambda bi, i, *_: (0, 0)),
                  pl.BlockSpec((1, NSA_GROUPS * LANES, tq), lambda bi, i, *_: (bi, 0, i)),
                  pl.BlockSpec((1, tq, qw), tile), pl.BlockSpec((1, tq, qw), tile),
                  pl.BlockSpec((tq, LANES), lambda bi, i, *_: (bi * per_seq + i, gate_col_block)),
                  pl.BlockSpec(gate_expand.shape, lambda bi, i, *_: (0, 0))],
        out_specs=pl.BlockSpec((1, tq, qw), tile),
        scratch_shapes=_flash_scratch(NSA_HG, HEAD_DIM, tq, tk))
    return pl.pallas_call(
        functools.partial(_nsa_sel_kernel, tq=tq, tk=tk, max_tiles=max_tiles),
        grid_spec=grid_spec,
        out_shape=jax.ShapeDtypeStruct((b, seq, qw), BF16),
        compiler_params=_params("parallel", "arbitrary"),
        name="nsa_sel",
    )(tiles, counts, slab_b, slab_b, vt, key_features, sel_bias_t, o_cmp, o_win, slab_a, gate_expand)


DIFF_STEP_HEADS = 2


SKIP_GAP = 180.0
NORM_SLACK = 1.01


def _diff_attn_kernel(slope_ref, lam_ref, q_ref, k_ref, vt_ref, feat_ref, g_ref, o_ref, knorm_ref, *scratch,
                      tq, tk, lam_init):
    first_head = pl.program_id(1) * DIFF_STEP_HEADS
    q_start = pl.program_id(2) * tq
    n_full = q_start // tk
    n_tiles = k_ref.shape[1] // tk

    @pl.when(pl.program_id(2) == 0)
    def _():
        for hh in range(DIFF_STEP_HEADS):
            for j in range(n_tiles):
                k = k_ref[0, j * tk:(j + 1) * tk, hh * LANES:(hh + 1) * LANES].astype(F32)
                knorm_ref[hh * n_tiles + j] = jnp.sqrt(jnp.max(jnp.sum(k * k, axis=1, keepdims=True)))

    zeros = jnp.zeros((FEATURE_ROWS - SUBLANES, tq), F32)
    coefs, queries, q_norms = [], [], []
    for hh in range(DIFF_STEP_HEADS):
        coef = slope_ref[first_head + hh] * LOG2E
        q = q_ref[0, :, hh * LANES:(hh + 1) * LANES].astype(F32) * (HEAD_DIM ** -0.5 * LOG2E)
        features = jnp.concatenate([_alibi_rows(coef, tq), zeros], axis=0)
        coefs.append(coef)
        for half in range(2):
            q_half = _keep_half(q, half)
            queries.append(_augmented_query(_transposed_bf16(q_half), features))
            q_norms.append(NORM_SLACK * jnp.sqrt(jnp.max(jnp.sum(q_half * q_half, axis=1, keepdims=True))))

    def head_lanes(i):
        return slice((i // 2) * LANES, (i // 2 + 1) * LANES)

    def offset(j, i):
        return coefs[i // 2] * (j * tk - q_start).astype(F32)

    nearest = jnp.maximum(n_full - 1, 0)
    first_needed = []

    def rest_count(m_ref):
        floors = [jnp.min(m_ref[i]) - SKIP_GAP for i in range(2 * DIFF_STEP_HEADS)]

        def body(j, first):
            needed = jnp.bool_(False)
            for i in range(2 * DIFF_STEP_HEADS):
                bound = q_norms[i] * knorm_ref[(i // 2) * n_tiles + j] + coefs[i // 2] * (tk - 1) + offset(j, i)
                needed = jnp.logical_or(needed, bound >= floors[i])
            return jnp.where(needed, jnp.minimum(first, j), first)

        first_needed.append(lax.fori_loop(0, nearest, body, nearest))
        return jnp.minimum(n_full, 1) + nearest - first_needed[0]

    def rest_tile(k):
        if isinstance(k, int):
            return nearest
        return jnp.where(k == 0, nearest, first_needed[0] + k - 1)

    _flash_transposed(
        n_full, 2 * DIFF_STEP_HEADS, q_start, tq, tk,
        lambda j, i: jnp.concatenate([k_ref[0, _kv_rows(j, tk), head_lanes(i)], feat_ref[...]], axis=1),
        lambda j, i: queries[i],
        lambda j, i: vt_ref[0, j, head_lanes(i), :],
        offset, scratch, rest_tile, rest_count)
    lam_vec = lam_ref[...]
    lam = (jnp.exp(jnp.sum(lam_vec[0:1] * lam_vec[1:2], axis=1, keepdims=True))
           - jnp.exp(jnp.sum(lam_vec[2:3] * lam_vec[3:4], axis=1, keepdims=True)) + lam_init)
    for hh in range(DIFF_STEP_HEADS):
        o = (_flash_result(scratch, 2 * hh) - lam * _flash_result(scratch, 2 * hh + 1)).T
        o_ref[0, :, hh * LANES:(hh + 1) * LANES] = (
            _rms_norm(o, g_ref[...], RMS_EPS) * (1.0 - lam_init)).astype(o_ref.dtype)


def _diff_attn(qk, vt, key_features, slopes, lam_vecs, subln_g, lam_init):
    b, s, _ = qk.shape
    tq, tk = min(FLASH_Q_TILE, s), min(KV_TILE, s)
    n = DIFF_STEP_HEADS
    groups = DIFF_HEADS // n
    smem = pl.BlockSpec(memory_space=pltpu.SMEM)
    return pl.pallas_call(
        functools.partial(_diff_attn_kernel, tq=tq, tk=tk, lam_init=lam_init),
        grid=(b, groups, s // tq),
        in_specs=[smem, pl.BlockSpec(lam_vecs.shape, lambda bi, h, i: (0, 0)),
                  pl.BlockSpec((1, tq, n * LANES), lambda bi, h, i: (bi, i, h)),
                  pl.BlockSpec((1, s, n * LANES), lambda bi, h, i: (bi, 0, groups + h)),
                  pl.BlockSpec((1, s // tk, n * LANES, tk), lambda bi, h, i: (bi, 0, h, 0)),
                  pl.BlockSpec(key_features.shape, lambda bi, h, i: (0, 0)),
                  pl.BlockSpec((1, LANES), lambda bi, h, i: (0, 0))],
        out_specs=pl.BlockSpec((1, tq, n * LANES), lambda bi, h, i: (bi, i, h)),
        out_shape=jax.ShapeDtypeStruct((b, s, DIFF_HEADS * LANES), BF16),
        scratch_shapes=[pltpu.SMEM((n * (s // tk),), F32)] + _flash_scratch(2 * n, LANES, tq, tk),
        compiler_params=_params("parallel", "parallel", "arbitrary"),
        name="diff_attn",
    )(slopes, lam_vecs, qk, qk, vt, key_features, subln_g)


def _out_ln_kernel(*refs, n_in):
    a_refs, w_refs = refs[:n_in], refs[n_in:2 * n_in]
    x_ref, g_ref, b_ref, o_ref = refs[2 * n_in:]
    half = x_ref.shape[0] // 2
    for rows in (slice(0, half), slice(half, 2 * half)):
        y = None
        for a_ref, w_ref in zip(a_refs, w_refs):
            t = _dot(a_ref[rows, :], w_ref[...])
            y = t if y is None else y + t
        o_ref[rows, :] = _layer_norm(DN_ALPHA * x_ref[rows, :] + y, g_ref[...], b_ref[...])


def _out_ln(acts, weights, x, g, b):
    m, d = x.shape
    tm = min(ROW_TILE, m)
    row = lambda i: (i, 0)
    const = lambda i: (0, 0)
    return pl.pallas_call(
        functools.partial(_out_ln_kernel, n_in=len(acts)),
        grid=(m // tm,),
        in_specs=[pl.BlockSpec((tm, a.shape[1]), row) for a in acts]
        + [pl.BlockSpec(w.shape, const) for w in weights]
        + [pl.BlockSpec((tm, d), row), pl.BlockSpec((1, d), const), pl.BlockSpec((1, d), const)],
        out_specs=pl.BlockSpec((tm, d), row),
        out_shape=jax.ShapeDtypeStruct((m, d), F32),
        compiler_params=_params("parallel"),
        name="out_proj_ln",
    )(*acts, *weights, x, g, b)


def _mlp_kernel(x_ref, wu_ref, wd_ref, g_ref, b_ref, o_ref, *, tf):
    x = x_ref[...]
    xb = x.astype(BF16)
    acc = None
    for f in range(wu_ref.shape[1] // tf):
        cols = slice(f * tf, (f + 1) * tf)
        hidden = jnp.maximum(_dot(xb, wu_ref[:, cols]), 0.0)
        part = _dot((hidden * hidden).astype(BF16), wd_ref[cols, :])
        acc = part if acc is None else acc + part
    o_ref[...] = _layer_norm(DN_ALPHA * x + acc, g_ref[...], b_ref[...])


def _mlp(x, w_up, w_down, g, b):
    m, d = x.shape
    ff = w_up.shape[1]
    tm, tf = min(ROW_TILE, m), min(FF_TILE, ff)
    resident = dict(pipeline_mode=pl.Buffered(1))
    return pl.pallas_call(
        functools.partial(_mlp_kernel, tf=tf),
        grid=(m // tm,),
        in_specs=[pl.BlockSpec((tm, d), lambda i: (i, 0)),
                  pl.BlockSpec((d, ff), lambda i: (0, 0), **resident),
                  pl.BlockSpec((ff, d), lambda i: (0, 0), **resident),
                  pl.BlockSpec((1, d), lambda i: (0, 0)), pl.BlockSpec((1, d), lambda i: (0, 0))],
        out_specs=pl.BlockSpec((tm, d), lambda i: (i, 0)),
        out_shape=jax.ShapeDtypeStruct((m, d), F32),
        compiler_params=_params("parallel"),
        name="mlp_ln",
    )(x, w_up, w_down, g, b)


def _pad_cols(w, width):
    return jnp.pad(w, ((0, 0), (0, width - w.shape[1])))


def _layer0_weights(w_in, w_uq, w_ukv, d_model):
    rank = d_model // 4
    kvw = NSA_GROUPS * HEAD_DIM
    o = np.cumsum([0, rank, rank, MLA_ROPE, NSA_HEADS * HEAD_DIM] + [kvw] * 6 + [3 * NSA_HEADS])
    seg = lambda i: w_in[:, o[i]:o[i + 1]]
    zeros = lambda n: jnp.zeros((w_in.shape[0], n), w_in.dtype)
    rope_slab = jnp.concatenate([zeros(MLA_NOPE), seg(2), zeros(LANES - MLA_NOPE - MLA_ROPE)], axis=1)
    w_a = jnp.concatenate([seg(0), seg(1), rope_slab, _pad_cols(seg(10), LANES)], axis=1)
    w_b = jnp.concatenate([seg(3), seg(6), seg(8)], axis=1)
    w_c = jnp.concatenate([seg(4), seg(5)], axis=1)
    w_vs = jnp.concatenate([seg(7), seg(9)], axis=1)
    wq = jnp.pad(w_uq.reshape(rank, MLA_HEADS, MLA_NOPE + MLA_ROPE),
                 ((0, 0), (0, 0), (0, LANES - MLA_NOPE - MLA_ROPE))).reshape(rank, MLA_HEADS * LANES)
    ukv = w_ukv.reshape(rank, MLA_HEADS, MLA_NOPE + HEAD_DIM)
    wk = jnp.pad(ukv[:, :, :MLA_NOPE], ((0, 0), (0, 0), (0, LANES - MLA_NOPE))).reshape(rank, MLA_HEADS * LANES)
    wv = ukv[:, :, MLA_NOPE:].reshape(rank, MLA_HEADS * HEAD_DIM)
    return [w.astype(BF16) for w in (w_a, w_b, w_c, w_vs, wq, wk, wv)]


def _rope_tables(seq):
    inv = 1.0 / (ROPE_THETA ** (jnp.arange(0, MLA_ROPE, 2, dtype=F32) / MLA_ROPE))
    ang = jnp.arange(seq, dtype=F32)[:, None] * inv[None, :]
    cos, sin = jnp.cos(ang), jnp.sin(ang)
    half = MLA_ROPE // 2
    z = lambda n: jnp.zeros((seq, n), F32)
    tail = LANES - MLA_NOPE - MLA_ROPE
    c = jnp.concatenate([jnp.ones((seq, MLA_NOPE), F32), cos, cos, z(tail)], axis=1)
    s1 = jnp.concatenate([z(MLA_NOPE), -sin, z(half), z(tail)], axis=1)
    s2 = jnp.concatenate([z(MLA_NOPE), z(half), sin, z(tail)], axis=1)
    return c, s1, s2


def _compress_weights(pos_k, w1_k, w2_k, pos_v, w1_v, w2_v):
    eye = jnp.eye(2 * NSA_GROUPS, dtype=F32)
    halves = []
    for a in range(CMP_LEN // CMP_STRIDE):
        rows = slice(a * CMP_STRIDE * HEAD_DIM, (a + 1) * CMP_STRIDE * HEAD_DIM)
        wk = w1_k[rows].reshape(CMP_STRIDE, HEAD_DIM, HEAD_DIM)
        wv = w1_v[rows].reshape(CMP_STRIDE, HEAD_DIM, HEAD_DIM)
        per_slot = jnp.stack([wk, wk, wv, wv], axis=0)
        full = jnp.einsum('st,srdj->rsdtj', eye, per_slot)
        halves.append(full.reshape(CMP_STRIDE * 4 * HEAD_DIM, 4 * HEAD_DIM).astype(BF16))
    w2 = jnp.einsum('st,sdj->sdtj', eye, jnp.stack([w2_k, w2_k, w2_v, w2_v])).reshape(4 * HEAD_DIM, 4 * HEAD_DIM)
    pos = jnp.concatenate([pos_k, pos_k, pos_v, pos_v], axis=1)
    pos = pos.reshape(CMP_LEN // CMP_STRIDE, 1, CMP_STRIDE * 4 * HEAD_DIM)
    pos = jnp.broadcast_to(pos, (pos.shape[0], 8, pos.shape[2])).reshape(-1, pos.shape[2])
    return pos, halves[0], halves[1], w2.astype(BF16)


def _overlap_table(n_cmp_pad, n_cmp):
    c0 = np.arange(n_cmp_pad)[None, :] * CMP_STRIDE
    s0 = np.arange(LANES)[:, None] * SEL_LEN
    ov = np.maximum(np.minimum(c0 + CMP_LEN, s0 + SEL_LEN) - np.maximum(c0, s0), 0) / CMP_LEN
    ov = ov * (np.arange(n_cmp_pad)[None, :] < n_cmp)
    return jnp.asarray(ov, BF16)


def _key_feature_table(tk):
    c = np.arange(tk)
    table = np.zeros((tk, LANES), np.float32)
    table[:, 0:3] = (c // POS_SPLIT)[:, None]
    table[:, 3:6] = (c % POS_SPLIT)[:, None]
    table[c, BLOCK_LANE0 + c // SEL_LEN] = 1.0
    return jnp.asarray(table, BF16)


def _gate_expand_table():
    width = NSA_HEADS * HEAD_DIM
    table = np.zeros((LANES, 3 * width), np.float32)
    for h in range(NSA_HEADS):
        for branch in range(3):
            table[h * 3 + branch, branch * width + h * HEAD_DIM:branch * width + (h + 1) * HEAD_DIM] = 1.0
    return jnp.asarray(table, BF16)


def _alibi_slopes(n):
    return jnp.asarray(2.0 ** (-8.0 * np.arange(1, n + 1) / n), dtype=F32)


def _layer0_mixer(x2, b, s, w_in, q_norm, w_uq, kv_norm, w_ukv, pos_k, w1_k, w2_k, pos_v, w1_v, w2_v, w_out):
    d = x2.shape[1]
    rank = d // 4
    w_a, w_b, w_c, w_vs, wq, wk, wv = _layer0_weights(w_in, w_uq, w_ukv, d)
    slab_a, slab_b, slab_c, vt_nsa = _project(x2, [w_a, w_b, w_c, w_vs], [F32, BF16, BF16, BF16],
                                              [False, False, False, True], b, s)
    rope_c, rope_s1, rope_s2 = _rope_tables(s)
    q, k, vt = _mla_prep(slab_a, q_norm.reshape(1, rank), kv_norm.reshape(1, rank), wq, wk, wv,
                         rope_c, rope_s1, rope_s2, b, s)
    o_mla = _mla_attn(q.reshape(b, s, -1), k.reshape(b, s, -1), vt)
    n_chunks = s // CMP_STRIDE
    n_cmp = (s - CMP_LEN) // CMP_STRIDE + 1
    pos, w1a, w1b, w2 = _compress_weights(pos_k, w1_k, w2_k, pos_v, w1_v, w2_v)
    cmp_k, cmp_vt = _compress(slab_c.reshape(b, n_chunks, CMP_STRIDE * slab_c.shape[1]), pos, w1a, w1b, w2, n_cmp)
    slab_b3 = slab_b.reshape(b, s, -1)
    key_features = _key_feature_table(min(KV_TILE, s))
    o_cmp, o_win, sel_bias_t, block_any = _nsa_cmp_win(slab_b3, cmp_k, cmp_vt, vt_nsa, key_features,
                                                       _overlap_table(n_chunks, n_cmp), s)
    o_nsa = _nsa_sel(slab_b3, vt_nsa, key_features, sel_bias_t, block_any, o_cmp, o_win, slab_a,
                     _gate_expand_table(), s, (2 * rank + LANES) // LANES)
    half = o_mla.shape[-1]
    w_out_b = w_out.astype(BF16)
    return [o_mla.reshape(b * s, half), o_nsa.reshape(b * s, -1)], [w_out_b[:half], w_out_b[half:]]


def _layer1_mixer(x2, b, s, w_qkv, lam_q1, lam_k1, lam_q2, lam_k2, subln_g, w_o, layer_idx):
    d = x2.shape[1]
    w = w_qkv.astype(BF16)
    qk, vt = _project(x2, [w[:, :2 * d], w[:, 2 * d:]], [BF16, BF16], [False, True], b, s)
    lam_init = 0.8 - 0.6 * math.exp(-0.3 * layer_idx)
    lam_vecs = jnp.stack([lam_q1, lam_k1, lam_q2, lam_k2]).astype(F32)
    o = _diff_attn(qk.reshape(b, s, -1), vt, _key_feature_table(min(KV_TILE, s)), _alibi_slopes(DIFF_HEADS),
                   lam_vecs, subln_g.reshape(1, -1), lam_init)
    return [o.reshape(b * s, -1)], [w_o.astype(BF16)]


def kernel(x, l0_w_in, l0_mla_q_norm, l0_mla_w_uq, l0_mla_kv_norm, l0_mla_w_ukv, l0_nsa_cmp_pos_k, l0_nsa_cmp_w1_k, l0_nsa_cmp_w2_k, l0_nsa_cmp_pos_v, l0_nsa_cmp_w1_v, l0_nsa_cmp_w2_v, l0_w_out, l0_ln_mix_g, l0_ln_mix_b, l0_w_up, l0_w_down, l0_ln_ffn_g, l0_ln_ffn_b, l1_w_qkv, l1_lam_q1, l1_lam_k1, l1_lam_q2, l1_lam_k2, l1_subln_g, l1_w_o, l1_ln_mix_g, l1_ln_mix_b, l1_w_up, l1_w_down, l1_ln_ffn_g, l1_ln_ffn_b):
    b, s, d = x.shape
    x2 = x.reshape(b * s, d)
    vec = lambda p: p.reshape(1, d)
    acts, weights = _layer0_mixer(x2, b, s, l0_w_in, l0_mla_q_norm, l0_mla_w_uq, l0_mla_kv_norm, l0_mla_w_ukv,
                                  l0_nsa_cmp_pos_k, l0_nsa_cmp_w1_k, l0_nsa_cmp_w2_k,
                                  l0_nsa_cmp_pos_v, l0_nsa_cmp_w1_v, l0_nsa_cmp_w2_v, l0_w_out)
    x2 = _out_ln(acts, weights, x2, vec(l0_ln_mix_g), vec(l0_ln_mix_b))
    x2 = _mlp(x2, l0_w_up.astype(BF16), l0_w_down.astype(BF16), vec(l0_ln_ffn_g), vec(l0_ln_ffn_b))
    acts, weights = _layer1_mixer(x2, b, s, l1_w_qkv, l1_lam_q1, l1_lam_k1, l1_lam_q2, l1_lam_k2,
                                  l1_subln_g, l1_w_o, 1)
    x2 = _out_ln(acts, weights, x2, vec(l1_ln_mix_g), vec(l1_ln_mix_b))
    x2 = _mlp(x2, l1_w_up.astype(BF16), l1_w_down.astype(BF16), vec(l1_ln_ffn_g), vec(l1_ln_ffn_b))
    return x2.reshape(b, s, d)
```

```python
import functools
import math

import jax
import jax.numpy as jnp
import numpy as np
from jax import lax
from jax.experimental import pallas as pl
from jax.experimental.pallas import tpu as pltpu

F32 = jnp.float32
BF16 = jnp.bfloat16

LANES = 128
SUBLANES = 8
BF16_ROWS = 16
MXU_DEPTH = 256
HEAD_DIM = 64
FLASH_Q_TILE = 512
KV_TILE = 512
KEY_CHUNK = 32
ROW_TILE = 512
FF_TILE = 1024
VMEM_LIMIT = 56 * 1024 * 1024

NEG_INF = -1e30
LOG2E = math.log2(math.e)
LN_EPS = 1e-5
RMS_EPS = 1e-6
DEPTH = 2
DN_ALPHA = (2.0 * DEPTH) ** 0.25

MLA_HEADS = 8
MLA_NOPE = 64
MLA_ROPE = 32
ROPE_THETA = 10000.0
NSA_HEADS = 8
NSA_GROUPS = 2
NSA_HG = NSA_HEADS // NSA_GROUPS
CMP_LEN = 32
CMP_STRIDE = 16
SEL_LEN = 64
SEL_TOPK = 16
WINDOW = 512
FORCE_BONUS = 1e3
DIFF_HEADS = 8

POS_SPLIT = 16
FEATURE_ROWS = 16
BLOCK_LANE0 = 8
BLOCKS_PER_TILE = KV_TILE // SEL_LEN


def _params(*sem):
    return pltpu.CompilerParams(dimension_semantics=sem, vmem_limit_bytes=VMEM_LIMIT)


def _dot(a, b):
    return jnp.dot(a, b, preferred_element_type=F32)


def _split_bf16(x):
    hi = x.astype(BF16)
    lo = (x - hi.astype(F32)).astype(BF16)
    return hi, lo


def _layer_norm(z, g, b):
    mu = jnp.mean(z, axis=-1, keepdims=True)
    zc = z - mu
    var = jnp.mean(zc * zc, axis=-1, keepdims=True)
    return zc * lax.rsqrt(var + LN_EPS) * g + b


def _rms_norm(z, g, eps):
    return z * lax.rsqrt(jnp.mean(z * z, axis=-1, keepdims=True) + eps) * g


def _lane_iota(shape):
    return lax.broadcasted_iota(jnp.int32, shape, 1)


def _keep_half(x, half):
    lane = _lane_iota(x.shape)
    keep = (lane < HEAD_DIM) if half == 0 else (lane >= HEAD_DIM)
    return jnp.where(keep, x, jnp.zeros_like(x))


def _move_head(slab, src_half, dst_half):
    if src_half != dst_half:
        slab = pltpu.roll(slab, HEAD_DIM, 1)
    return _keep_half(slab, dst_half)


def _store_transposed(o_ref, res):
    for c in range(res.shape[1] // LANES):
        cols = slice(c * LANES, (c + 1) * LANES)
        o_ref[0, 0, cols, :] = res[:, cols].T.astype(o_ref.dtype)


def _proj_kernel(x_ref, *refs, transposed):
    n_out = len(transposed)
    w_refs, o_refs = refs[:n_out], refs[n_out:]
    xb = x_ref[...].astype(BF16)
    for w_ref, o_ref, t in zip(w_refs, o_refs, transposed):
        res = _dot(xb, w_ref[...])
        if t:
            _store_transposed(o_ref, res)
        else:
            o_ref[...] = res.astype(o_ref.dtype)


def _transposed_out(b, seq, width, tm):
    per_seq = seq // tm
    spec = pl.BlockSpec((1, 1, width, tm), lambda i: (i // per_seq, i % per_seq, 0, 0))
    return spec, jax.ShapeDtypeStruct((b, per_seq, width, tm), BF16)


def _project(x, weights, out_dtypes, transposed, b, seq):
    m, k = x.shape
    tm = min(KV_TILE, seq)
    specs, shapes = [], []
    for w, dt, t in zip(weights, out_dtypes, transposed):
        if t:
            spec, shape = _transposed_out(b, seq, w.shape[1], tm)
        else:
            spec, shape = pl.BlockSpec((tm, w.shape[1]), lambda i: (i, 0)), jax.ShapeDtypeStruct((m, w.shape[1]), dt)
        specs.append(spec)
        shapes.append(shape)
    return pl.pallas_call(
        functools.partial(_proj_kernel, transposed=tuple(transposed)),
        grid=(m // tm,),
        in_specs=[pl.BlockSpec((tm, k), lambda i: (i, 0))]
        + [pl.BlockSpec(w.shape, lambda i: (0, 0)) for w in weights],
        out_specs=specs,
        out_shape=shapes,
        compiler_params=_params("parallel"),
        name="project",
    )(x, *weights)


def _rope_slab(slab, c, s):
    return slab * c + pltpu.roll(slab, LANES - MLA_ROPE // 2, 1) * s


def _mla_prep_kernel(ql_ref, kvl_ref, kpe_ref, qg_ref, kvg_ref, wq_ref, wk_ref, wv_ref,
                     c_ref, s_ref, q_ref, k_ref, vt_ref, *, q_scale):
    c, s = c_ref[...], s_ref[...]
    qn = _rms_norm(ql_ref[...], qg_ref[...], RMS_EPS).astype(BF16)
    kvn = _rms_norm(kvl_ref[...], kvg_ref[...], RMS_EPS).astype(BF16)
    q = _dot(qn, wq_ref[...])
    k = _dot(kvn, wk_ref[...])
    _store_transposed(vt_ref, _dot(kvn, wv_ref[...]))
    kpe = _rope_slab(kpe_ref[...], c, s)
    for h in range(MLA_HEADS):
        sl = slice(h * LANES, (h + 1) * LANES)
        q_ref[:, sl] = (_rope_slab(q[:, sl], c, s) * q_scale).astype(q_ref.dtype)
        k_ref[:, sl] = (k[:, sl] + kpe).astype(k_ref.dtype)


def _mla_prep(slab_a, q_gain, kv_gain, wq, wk, wv, rope_c, rope_s, b, seq):
    m = slab_a.shape[0]
    tm = min(KV_TILE, seq)
    per_seq = seq // tm
    rank = q_gain.shape[1]
    row = lambda j: (lambda i: (i, j))
    tab = lambda i: (i % per_seq, 0)
    const = lambda i: (0, 0)
    hw = MLA_HEADS * LANES
    vt_spec, vt_shape = _transposed_out(b, seq, wv.shape[1], tm)
    return pl.pallas_call(
        functools.partial(_mla_prep_kernel, q_scale=float((MLA_NOPE + MLA_ROPE) ** -0.5 * LOG2E)),
        grid=(m // tm,),
        in_specs=[pl.BlockSpec((tm, rank), row(0)), pl.BlockSpec((tm, rank), row(1)),
                  pl.BlockSpec((tm, LANES), row(2 * rank // LANES)),
                  pl.BlockSpec((1, rank), const), pl.BlockSpec((1, rank), const),
                  pl.BlockSpec(wq.shape, const), pl.BlockSpec(wk.shape, const), pl.BlockSpec(wv.shape, const),
                  pl.BlockSpec((tm, LANES), tab), pl.BlockSpec((tm, LANES), tab)],
        out_specs=[pl.BlockSpec((tm, hw), row(0)), pl.BlockSpec((tm, hw), row(0)), vt_spec],
        out_shape=[jax.ShapeDtypeStruct((m, hw), BF16), jax.ShapeDtypeStruct((m, hw), BF16), vt_shape],
        compiler_params=_params("parallel"),
        name="mla_prep",
    )(slab_a, slab_a, slab_a, q_gain, kv_gain, wq, wk, wv, rope_c, rope_s)


def _flash_scratch(n_streams, v_rows, tq, tk):
    scores = pltpu.VMEM((n_streams, tk, tq), F32)
    stat = pltpu.VMEM((n_streams, 1, tq), F32)
    probs = pltpu.VMEM((n_streams, tk, tq), BF16)
    slot = [scores, probs, stat]
    return slot + slot + [stat, pltpu.VMEM((n_streams, v_rows + BF16_ROWS, tq), F32)]


def _chunk_rows(c):
    return slice(c * KEY_CHUNK, (c + 1) * KEY_CHUNK)


def _diagonal_blocks(tk, tq, keys_up_to_query):
    for c in range(tk // KEY_CHUNK):
        for v in range(tq // LANES):
            first_key, last_key = c * KEY_CHUNK, (c + 1) * KEY_CHUNK - 1
            first_query, last_query = v * LANES, (v + 1) * LANES - 1
            all_up_to = last_key <= first_query
            all_beyond = first_key > last_query
            if all_up_to or all_beyond:
                kind = "visible" if all_up_to == keys_up_to_query else "hidden"
            else:
                kind = "mixed"
            yield _chunk_rows(c), slice(first_query, last_query + 1), first_query - first_key, kind


def _fold_rows(x):
    return x.reshape(x.shape[0] // SUBLANES, SUBLANES, x.shape[1])


def _flash_transposed(diagonal_tile, n_streams, q_start, tq, tk, key_operand, query_operand, values, offset, scratch,
                      rest_tile, rest_count):
    assert tq == tk, "the diagonal tile is taken to start at the first query of the tile"
    slot_a, slot_b, (m_ref, acc_ref) = scratch[0:3], scratch[3:6], scratch[6:]
    n_chunks = tk // KEY_CHUNK
    for i in range(n_streams):
        m_ref[i] = jnp.full((1, tq), NEG_INF, F32)
        acc_ref[i] = jnp.zeros(acc_ref.shape[1:], F32)

    def column_max(s_ref, i):
        part = jnp.full((SUBLANES, tq), NEG_INF, F32)
        for c in range(n_chunks):
            part = jnp.maximum(part, jnp.max(_fold_rows(s_ref[i, _chunk_rows(c), :]), axis=0))
        return jnp.max(part, axis=0, keepdims=True)

    def stage1(j, slot, i):
        slot[0][i] = _dot(key_operand(j, i), query_operand(j, i))

    def blocks():
        return _diagonal_blocks(tk, tq, True)

    def stage2(j, slot, i, diagonal):
        s_ref, p_ref, alpha_ref = slot
        if diagonal:
            rel = (lax.broadcasted_iota(jnp.int32, (KEY_CHUNK, LANES), 0)
                   - lax.broadcasted_iota(jnp.int32, (KEY_CHUNK, LANES), 1))
            parts = [jnp.full((SUBLANES, LANES), NEG_INF, F32) for _ in range(tq // LANES)]
            for rows, lanes, bound, kind in blocks():
                if kind == "hidden":
                    continue
                s = s_ref[i, rows, lanes]
                if kind == "mixed":
                    s = jnp.where(rel <= bound, s, NEG_INF)
                    s_ref[i, rows, lanes] = s
                v = lanes.start // LANES
                parts[v] = jnp.maximum(parts[v], jnp.max(_fold_rows(s), axis=0))
            mx = jnp.max(jnp.concatenate(parts, axis=1), axis=0, keepdims=True)
        else:
            mx = column_max(s_ref, i)
        off = offset(j, i)
        m_prev = m_ref[i]
        if off is None:
            m_next = jnp.maximum(m_prev, mx)
            shift = m_next
        else:
            m_next = jnp.maximum(m_prev, mx + off)
            shift = m_next - off
        alpha = jnp.exp2(m_prev - m_next)
        if diagonal:
            for rows, lanes, _, kind in blocks():
                if kind == "hidden":
                    p_ref[i, rows, lanes] = jnp.zeros((KEY_CHUNK, LANES), BF16)
                else:
                    p_ref[i, rows, lanes] = jnp.exp2(s_ref[i, rows, lanes] - shift[:, lanes]).astype(BF16)
        else:
            for c in range(n_chunks):
                p_ref[i, _chunk_rows(c), :] = jnp.exp2(s_ref[i, _chunk_rows(c), :] - shift).astype(BF16)
        m_ref[i] = m_next
        alpha_ref[i] = alpha

    def step(accumulate=None, produce=(), exponentiate=None, diagonal=False):
        products = []
        for i in range(n_streams + 1):
            if i < n_streams:
                if accumulate is not None:
                    tile, slot = accumulate
                    products.append(_dot(_values_and_ones(values(tile, i)), slot[1][i]))
                for tile, slot in produce:
                    stage1(tile, slot, i)
                if exponentiate is not None:
                    stage2(*exponentiate, i, diagonal)
            if accumulate is not None and i > 0:
                acc_ref[i - 1] = accumulate[1][2][i - 1] * acc_ref[i - 1] + products[i - 1]

    step(produce=((diagonal_tile, slot_a), (rest_tile(0), slot_b)), exponentiate=(diagonal_tile, slot_a),
         diagonal=True)
    last = rest_count(m_ref)

    def tile_at(position):
        return rest_tile(jnp.clip(position, 1, jnp.maximum(last, 1)) - 1)

    def diag_or_rest(position):
        return jnp.where(position == 0, diagonal_tile, tile_at(position))

    def pair(t, carry):
        p1 = 2 * t + 1
        step((diag_or_rest(p1 - 1), slot_a), ((tile_at(p1 + 1), slot_a),), (tile_at(p1), slot_b))
        step((tile_at(p1), slot_b), ((tile_at(p1 + 2), slot_b),), (tile_at(p1 + 1), slot_a))
        return carry

    lax.fori_loop(0, last // 2, pair, 0)

    @pl.when(last % 2 == 1)
    def _():
        step(accumulate=(diag_or_rest(last - 1), slot_a), exponentiate=(tile_at(last), slot_b))
        step(accumulate=(tile_at(last), slot_b))

    @pl.when(last % 2 == 0)
    def _():
        step(accumulate=(diag_or_rest(last), slot_a))


def _flash_result(scratch, i):
    acc = scratch[-1][i]
    v_rows = acc.shape[0] - BF16_ROWS
    return acc[:v_rows] / acc[v_rows:v_rows + 1]


def _kv_rows(j, tk):
    return pl.ds(pl.multiple_of(j * tk, tk), tk)


def _transposed_bf16(x):
    return x.astype(F32).T.astype(BF16)


def _alibi_rows(coef, tq):
    c = jnp.zeros((1, tq), F32) + coef
    hi = c.astype(BF16).astype(F32)
    rest = c - hi
    mid = rest.astype(BF16).astype(F32)
    lo = rest - mid
    zero = jnp.zeros((1, tq), F32)
    return jnp.concatenate([POS_SPLIT * hi, POS_SPLIT * mid, POS_SPLIT * lo, hi, mid, lo, zero, zero], axis=0)


def _augmented_query(q_t, feature_rows):
    tq = q_t.shape[1]
    pad = jnp.zeros((MXU_DEPTH - LANES - FEATURE_ROWS, tq), BF16)
    return jnp.concatenate([q_t, feature_rows.astype(BF16), pad], axis=0)


MLA_STEP_HEADS = 4


def _mla_attn_kernel(q_ref, k_ref, vt_ref, o_ref, *scratch, tq, tk):
    q_start = pl.program_id(2) * tq
    n = MLA_STEP_HEADS
    queries = [_transposed_bf16(q_ref[0, :, hh * LANES:(hh + 1) * LANES]) for hh in range(n)]
    n_full = q_start // tk
    _flash_transposed(
        n_full, n, q_start, tq, tk,
        lambda j, i: k_ref[0, _kv_rows(j, tk), i * LANES:(i + 1) * LANES],
        lambda j, i: queries[i],
        lambda j, i: vt_ref[0, j, i * HEAD_DIM:(i + 1) * HEAD_DIM, :],
        lambda j, i: None, scratch,
        rest_tile=lambda k: k, rest_count=lambda m_ref: n_full)
    for pair in range(n // 2):
        o_t = jnp.concatenate([_flash_result(scratch, 2 * pair), _flash_result(scratch, 2 * pair + 1)], axis=0)
        o_ref[0, :, pair * LANES:(pair + 1) * LANES] = o_t.T.astype(o_ref.dtype)


def _mla_attn(q, k, vt):
    b, s, _ = q.shape
    tq, tk = min(FLASH_Q_TILE, s), min(KV_TILE, s)
    n = MLA_STEP_HEADS
    groups = MLA_HEADS // n
    return pl.pallas_call(
        functools.partial(_mla_attn_kernel, tq=tq, tk=tk),
        grid=(b, groups, s // tq),
        in_specs=[pl.BlockSpec((1, tq, n * LANES), lambda bi, p, i: (bi, i, p)),
                  pl.BlockSpec((1, s, n * LANES), lambda bi, p, i: (bi, 0, p)),
                  pl.BlockSpec((1, s // tk, n * HEAD_DIM, tk), lambda bi, p, i: (bi, 0, p, 0))],
        out_specs=pl.BlockSpec((1, tq, n * HEAD_DIM), lambda bi, p, i: (bi, i, p)),
        out_shape=jax.ShapeDtypeStruct((b, s, MLA_HEADS * HEAD_DIM), BF16),
        scratch_shapes=_flash_scratch(n, HEAD_DIM, tq, tk),
        compiler_params=_params("parallel", "parallel", "arbitrary"),
        name="mla_attn",
    )(q, k, vt)


def _gelu_tanh(x):
    return 0.5 * x * (1.0 + jnp.tanh(math.sqrt(2.0 / math.pi) * (x + 0.044715 * (x * x * x))))


def _compress_kernel(x_ref, pos_ref, w1a_ref, w1b_ref, w2_ref, k_ref, vt_ref, *, n_real):
    x = x_ref[0]
    n = x.shape[0]
    first = _dot(x, w1a_ref[...])
    second = _dot(x, w1b_ref[...])
    pos_hi, pos_lo = _split_bf16(pos_ref[...])
    bias = (_dot(pos_hi[:8], w1a_ref[...]) + _dot(pos_lo[:8], w1a_ref[...])
            + _dot(pos_hi[8:], w1b_ref[...]) + _dot(pos_lo[8:], w1b_ref[...]))[:1]
    pre = first + pltpu.roll(second, n - 1, 0) + bias
    out = _dot(_gelu_tanh(pre).astype(BF16), w2_ref[...])
    real = lax.broadcasted_iota(jnp.int32, out.shape, 0) < n_real
    out = jnp.where(real, out, 0.0)
    half = out.shape[1] // 2
    k_ref[0] = out[:, :half].astype(k_ref.dtype)
    vt_ref[0] = out[:, half:].T.astype(vt_ref.dtype)


def _compress(x_chunks, pos_exp, w1a, w1b, w2, n_real):
    b, n, width = x_chunks.shape
    half = w2.shape[1] // 2
    const = lambda bi: (0, 0)
    return pl.pallas_call(
        functools.partial(_compress_kernel, n_real=n_real),
        grid=(b,),
        in_specs=[pl.BlockSpec((1, n, width), lambda bi: (bi, 0, 0)),
                  pl.BlockSpec(pos_exp.shape, const), pl.BlockSpec(w1a.shape, const),
                  pl.BlockSpec(w1b.shape, const), pl.BlockSpec(w2.shape, const)],
        out_specs=[pl.BlockSpec((1, n, half), lambda bi: (bi, 0, 0)), pl.BlockSpec((1, half, n), lambda bi: (bi, 0, 0))],
        out_shape=[jax.ShapeDtypeStruct((b, n, half), BF16), jax.ShapeDtypeStruct((b, half, n), BF16)],
        compiler_params=_params("parallel"),
        name="nsa_compress",
    )(x_chunks, pos_exp, w1a, w1b, w2)


def _nsa_head_slope(h):
    return float(2.0 ** (-8.0 * (h + 1) / NSA_HEADS))


def _nsa_queries(q_ref, g, scale):
    out = []
    for hg in range(NSA_HG):
        h = g * NSA_HG + hg
        slab = q_ref[0, :, (h // 2) * LANES:(h // 2 + 1) * LANES].astype(F32) * scale
        out.append(_move_head(slab, h % 2, g))
    return out


def _selection_frame(q_pos, n_blocks):
    blk = lax.broadcasted_iota(jnp.int32, (n_blocks, q_pos.shape[1]), 0)
    behind = q_pos // SEL_LEN - blk
    near = jnp.abs(2 * behind - 1) <= 1
    bonus = jnp.where(near, FORCE_BONUS, jnp.where(blk == 0, FORCE_BONUS, 0.0))
    return blk, bonus, behind >= 0


def _select_blocks(imp_t, blk, bonus, allowed):
    val = jnp.where(allowed, imp_t + bonus, NEG_INF)
    for _ in range(SEL_TOPK):
        top = jnp.max(val, axis=0, keepdims=True)
        first = jnp.min(jnp.where(val == top, blk, imp_t.shape[0]), axis=0, keepdims=True)
        val = jnp.where(blk == first, -jnp.inf, val)
    return jnp.where(val == -jnp.inf, 0.0, NEG_INF)


def _masked_softmax_pass(s_ref, p_ref, tiles, tq):
    part = jnp.full((SUBLANES, tq), NEG_INF, F32)
    for t, rows, keep, off in tiles:
        for c in range(rows // KEY_CHUNK):
            s = jnp.where(keep(_chunk_rows(c)), s_ref[t, _chunk_rows(c), :], NEG_INF)
            s_ref[t, _chunk_rows(c), :] = s
            part = jnp.maximum(part, jnp.max(_fold_rows(s), axis=0) + off)
    m = jnp.max(part, axis=0, keepdims=True)
    for t, rows, keep, off in tiles:
        shift = m - off
        for c in range(rows // KEY_CHUNK):
            p_ref[t, _chunk_rows(c), :] = jnp.exp2(s_ref[t, _chunk_rows(c), :] - shift).astype(BF16)
    return m > 0.5 * NEG_INF


def _window_softmax(s_ref, p_ref, prev_offset, tq, tk):
    rel = (lax.broadcasted_iota(jnp.int32, (KEY_CHUNK, LANES), 0)
           - lax.broadcasted_iota(jnp.int32, (KEY_CHUNK, LANES), 1))
    tiles = [(0, False, prev_offset), (1, True, 0.0)]
    parts = [jnp.full((SUBLANES, LANES), NEG_INF, F32) for _ in range(tq // LANES)]
    for t, below, off in tiles:
        for rows, lanes, bound, kind in _diagonal_blocks(tk, tq, below):
            if kind == "hidden":
                continue
            s = s_ref[t, rows, lanes]
            if kind == "mixed":
                s = jnp.where((rel <= bound) if below else (rel > bound), s, NEG_INF)
                s_ref[t, rows, lanes] = s
            v = lanes.start // LANES
            parts[v] = jnp.maximum(parts[v], jnp.max(_fold_rows(s), axis=0) + off)
    m = jnp.max(jnp.concatenate(parts, axis=1), axis=0, keepdims=True)
    for t, below, off in tiles:
        shift = m - off
        for rows, lanes, _, kind in _diagonal_blocks(tk, tq, below):
            if kind == "hidden":
                p_ref[t, rows, lanes] = jnp.zeros((KEY_CHUNK, LANES), BF16)
            else:
                p_ref[t, rows, lanes] = jnp.exp2(s_ref[t, rows, lanes] - shift[:, lanes]).astype(BF16)


def _values_and_ones(vt):
    return jnp.concatenate([vt, jnp.ones((BF16_ROWS, vt.shape[1]), BF16)], axis=0)


def _nsa_cmp_win_kernel(q_ref, kc_ref, vct_ref, kw_ref, vwt_ref, feat_ref, ovt_ref, oc_ref, ow_ref, sel_ref, any_ref,
                        end_ref, sc_ref, sw_ref, pc_ref, pw_ref, *, tq, tk):
    qi = pl.program_id(1)
    q_start = qi * tq
    n_cmp = kc_ref.shape[1]
    end_ref[...] = (lax.broadcasted_iota(jnp.int32, (n_cmp, tq), 0) * CMP_STRIDE + (CMP_LEN - 1)
                    - lax.broadcasted_iota(jnp.int32, (n_cmp, tq), 1))
    prev_tile = jnp.maximum(qi - 1, 0)
    zeros = jnp.zeros((FEATURE_ROWS - SUBLANES, tq), F32)
    k_cmp = jnp.concatenate([kc_ref[0], feat_ref[0:n_cmp, :]], axis=1)
    k_win = [jnp.concatenate([kw_ref[0, _kv_rows(j, tk), :], feat_ref[...]], axis=1) for j in (prev_tile, qi)]
    frame = _selection_frame(q_start + lax.broadcasted_iota(jnp.int32, (1, tq), 1), LANES)
    for g in range(NSA_GROUPS):
        queries = [_transposed_bf16(q) for q in _nsa_queries(q_ref, g, HEAD_DIM ** -0.5 * LOG2E)]
        group_rows = slice(g * HEAD_DIM, (g + 1) * HEAD_DIM)
        imp_t = jnp.zeros((LANES, tq), F32)
        out_c, out_w = [], []
        for hg in range(NSA_HG):
            coef = _nsa_head_slope(g * NSA_HG + hg) * LOG2E
            buf = hg % 2
            sc, sw, pc, pw = sc_ref.at[buf], sw_ref.at[buf], pc_ref.at[buf], pw_ref.at[buf]
            cmp_query = _augmented_query(queries[hg], jnp.concatenate([_alibi_rows(CMP_STRIDE * coef, tq), zeros], 0))
            win_query = _augmented_query(queries[hg], jnp.concatenate([_alibi_rows(coef, tq), zeros], 0))
            sc[0] = _dot(k_cmp, cmp_query)
            sw[0] = _dot(k_win[0], win_query)
            sw[1] = _dot(k_win[1], win_query)
            has_any = _masked_softmax_pass(sc, pc, [(0, n_cmp, lambda r: end_ref[r, :] <= q_start, 0.0)], tq)
            acc = _dot(_values_and_ones(vct_ref[0, group_rows, :]), pc[0])
            inv = jnp.where(has_any, 1.0 / acc[HEAD_DIM:HEAD_DIM + 1], 0.0)
            out_c.append(acc[:HEAD_DIM] * inv)
            imp_t = imp_t + _dot(ovt_ref[...], pc[0]) * inv
            _window_softmax(sw, pw, jnp.where(qi >= 1, -coef * tk, NEG_INF), tq, tk)
            acc = (_dot(_values_and_ones(vwt_ref[0, prev_tile, group_rows, :]), pw[0])
                   + _dot(_values_and_ones(vwt_ref[0, qi, group_rows, :]), pw[1]))
            out_w.append(acc[:HEAD_DIM] / acc[HEAD_DIM:HEAD_DIM + 1])
        for pair in range(NSA_HG // 2):
            cols = slice((g * 2 + pair) * LANES, (g * 2 + pair + 1) * LANES)
            oc_ref[0, :, cols] = jnp.concatenate(out_c[2 * pair:2 * pair + 2], axis=0).T
            ow_ref[0, :, cols] = jnp.concatenate(out_w[2 * pair:2 * pair + 2], axis=0).T
        bias_t = _select_blocks(imp_t, *frame)
        sel_ref[0, g * LANES:(g + 1) * LANES, :] = bias_t
        any_ref[0, 0, g * LANES:(g + 1) * LANES, :] = jnp.broadcast_to(
            jnp.max(bias_t, axis=1, keepdims=True), (LANES, LANES))


def _nsa_cmp_win(slab_b, cmp_k, cmp_vt, vt_nsa, key_features, overlap_t, seq):
    b = slab_b.shape[0]
    tq, tk = min(FLASH_Q_TILE, seq), min(KV_TILE, seq)
    assert tq == tk == WINDOW, "the window branch is written for one previous and one diagonal key tile"
    n_cmp = cmp_k.shape[1]
    qw = NSA_HEADS * HEAD_DIM
    base = qw // LANES
    tile = lambda bi, i: (bi, i, 0)
    return pl.pallas_call(
        functools.partial(_nsa_cmp_win_kernel, tq=tq, tk=tk),
        grid=(b, seq // tq),
        in_specs=[pl.BlockSpec((1, tq, qw), tile),
                  pl.BlockSpec((1, n_cmp, LANES), lambda bi, i: (bi, 0, 0)),
                  pl.BlockSpec((1, LANES, n_cmp), lambda bi, i: (bi, 0, 0)),
                  pl.BlockSpec((1, seq, LANES), lambda bi, i: (bi, 0, base + 1)),
                  pl.BlockSpec((1, seq // tk, LANES, tk), lambda bi, i: (bi, 0, 1, 0)),
                  pl.BlockSpec(key_features.shape, lambda bi, i: (0, 0)),
                  pl.BlockSpec(overlap_t.shape, lambda bi, i: (0, 0))],
        out_specs=[pl.BlockSpec((1, tq, qw), tile), pl.BlockSpec((1, tq, qw), tile),
                   pl.BlockSpec((1, NSA_GROUPS * LANES, tq), lambda bi, i: (bi, 0, i)),
                   pl.BlockSpec((1, 1, NSA_GROUPS * LANES, LANES), lambda bi, i: (bi, i, 0, 0))],
        out_shape=[jax.ShapeDtypeStruct((b, seq, qw), F32), jax.ShapeDtypeStruct((b, seq, qw), F32),
                   jax.ShapeDtypeStruct((b, NSA_GROUPS * LANES, seq), F32),
                   jax.ShapeDtypeStruct((b, seq // tq, NSA_GROUPS * LANES, LANES), F32)],
        scratch_shapes=[pltpu.VMEM((n_cmp, tq), jnp.int32),
                        pltpu.VMEM((2, 1, n_cmp, tq), F32), pltpu.VMEM((2, 2, tk, tq), F32),
                        pltpu.VMEM((2, 1, n_cmp, tq), BF16), pltpu.VMEM((2, 2, tk, tq), BF16)],
        compiler_params=_params("parallel", "arbitrary"),
        name="nsa_cmp_win",
    )(slab_b, cmp_k, cmp_vt, slab_b, vt_nsa, key_features, overlap_t)


def _nsa_sel_kernel(tiles_ref, counts_ref, q_ref, k_ref, vt_ref, feat_ref, sel_ref, oc_ref, ow_ref, gate_ref, gx_ref,
                    o_ref, *scratch, tq, tk, max_tiles):
    qi = pl.program_id(1)
    q_start = qi * tq
    diagonal_tile = q_start // tk
    out_slabs = []
    for g in range(NSA_GROUPS):
        queries = [_transposed_bf16(q) for q in _nsa_queries(q_ref, g, HEAD_DIM ** -0.5 * LOG2E)]
        coefs = [_nsa_head_slope(g * NSA_HG + hg) * LOG2E for hg in range(NSA_HG)]
        alibi = [_alibi_rows(c, tq) for c in coefs]
        entry = (pl.program_id(0) * pl.num_programs(1) + qi) * NSA_GROUPS + g
        n_active = counts_ref[entry]

        def listed_tile(k, entry=entry):
            return tiles_ref[entry * max_tiles + jnp.minimum(k, max_tiles - 1)]

        def key_operand(j, i):
            return jnp.concatenate([k_ref[0, _kv_rows(j, tk), :], feat_ref[...]], axis=1)

        def query_operand(j, i, g=g, queries=queries, alibi=alibi):
            first_block = pl.multiple_of(g * LANES + j * BLOCKS_PER_TILE, BLOCKS_PER_TILE)
            blocks = sel_ref[0, pl.ds(first_block, BLOCKS_PER_TILE), :]
            return _augmented_query(queries[i], jnp.concatenate([alibi[i], blocks], axis=0))

        _flash_transposed(
            diagonal_tile, NSA_HG, q_start, tq, tk, key_operand, query_operand,
            lambda j, i, g=g: vt_ref[0, j, g * HEAD_DIM:(g + 1) * HEAD_DIM, :],
            lambda j, i, coefs=coefs: coefs[i] * (j * tk - q_start).astype(F32), scratch,
            rest_tile=listed_tile, rest_count=lambda m_ref, n_active=n_active: n_active)
        heads = [_flash_result(scratch, hg) for hg in range(NSA_HG)]
        for pair in range(NSA_HG // 2):
            out_slabs.append(jnp.concatenate(heads[2 * pair:2 * pair + 2], axis=0).T)
    gates = jax.nn.sigmoid(gate_ref[...])
    g_hi, g_lo = _split_bf16(gates)
    width = NSA_HEADS * HEAD_DIM
    for i, o_sel in enumerate(out_slabs):
        mixed = None
        for branch, o_branch in enumerate((oc_ref[0, :, i * LANES:(i + 1) * LANES], o_sel,
                                           ow_ref[0, :, i * LANES:(i + 1) * LANES])):
            gx = gx_ref[:, branch * width + i * LANES:branch * width + (i + 1) * LANES]
            term = (_dot(g_hi, gx) + _dot(g_lo, gx)) * o_branch
            mixed = term if mixed is None else mixed + term
        o_ref[0, :, i * LANES:(i + 1) * LANES] = mixed.astype(o_ref.dtype)


def _active_key_tiles(block_any, tq, tk):
    b, n_q = block_any.shape[:2]
    max_tiles = LANES // BLOCKS_PER_TILE
    hit = block_any[..., 0].reshape(b, n_q, NSA_GROUPS, max_tiles, BLOCKS_PER_TILE).max(axis=-1) > 0.5 * NEG_INF
    before_diagonal = jnp.arange(max_tiles)[None, :] < (jnp.arange(n_q) * tq // tk)[:, None]
    hit = hit & before_diagonal[None, :, None, :]
    rank = jnp.cumsum(hit.astype(jnp.int32), axis=-1) - 1
    slots = jnp.arange(max_tiles, dtype=jnp.int32)
    in_slot = hit[..., None, :] & (rank[..., None, :] == slots[:, None])
    tiles = jnp.sum(jnp.where(in_slot, slots, 0), axis=-1)
    return tiles.astype(jnp.int32).reshape(-1), hit.sum(axis=-1).astype(jnp.int32).reshape(-1), max_tiles


def _nsa_sel(slab_b, vt, key_features, sel_bias_t, block_any, o_cmp, o_win, slab_a, gate_expand, seq, gate_col_block):
    b = slab_b.shape[0]
    tq, tk = min(FLASH_Q_TILE, seq), min(KV_TILE, seq)
    qw = NSA_HEADS * HEAD_DIM
    base = qw // LANES
    per_seq = seq // tq
    tiles, counts, max_tiles = _active_key_tiles(block_any, tq, tk)
    tile = lambda bi, i, *_: (bi, i, 0)
    grid_spec = pltpu.PrefetchScalarGridSpec(
        num_scalar_prefetch=2,
        grid=(b, seq // tq),
        in_specs=[pl.BlockSpec((1, tq, qw), tile),
                  pl.BlockSpec((1, seq, LANES), lambda bi, i, *_: (bi, 0, base)),
                  pl.BlockSpec((1, seq // tk, LANES, tk), lambda bi, i, *_: (bi, 0, 0, 0)),
                  pl.BlockSpec(key_features.shape, lambda bi, i, *_: (0, 0)),
                  pl.BlockSpec((1, NSA_GROUPS * LANES, tq), lambda bi, i, *_: (bi, 0, i)),
                  pl.BlockSpec((1, tq, qw), tile), pl.BlockSpec((1, tq, qw), tile),
                  pl.BlockSpec((tq, LANES), lambda bi, i, *_: (bi * per_seq + i, gate_col_block)),
                  pl.BlockSpec(gate_expand.shape, lambda bi, i, *_: (0, 0))],
        out_specs=pl.BlockSpec((1, tq, qw), tile),
        scratch_shapes=_flash_scratch(NSA_HG, HEAD_DIM, tq, tk))
    return pl.pallas_call(
        functools.partial(_nsa_sel_kernel, tq=tq, tk=tk, max_tiles=max_tiles),
        grid_spec=grid_spec,
        out_shape=jax.ShapeDtypeStruct((b, seq, qw), BF16),
        compiler_params=_params("parallel", "arbitrary"),
        name="nsa_sel",
    )(tiles, counts, slab_b, slab_b, vt, key_features, sel_bias_t, o_cmp, o_win, slab_a, gate_expand)


DIFF_STEP_HEADS = 2


SKIP_GAP = 180.0
NORM_SLACK = 1.01


def _diff_attn_kernel(slope_ref, lam_ref, q_ref, k_ref, vt_ref, feat_ref, g_ref, o_ref, knorm_ref, *scratch,
                      tq, tk, lam_init):
    first_head = pl.program_id(1) * DIFF_STEP_HEADS
    q_start = pl.program_id(2) * tq
    n_full = q_start // tk
    n_tiles = k_ref.shape[1] // tk

    @pl.when(pl.program_id(2) == 0)
    def _():
        for hh in range(DIFF_STEP_HEADS):
            for j in range(n_tiles):
                k = k_ref[0, j * tk:(j + 1) * tk, hh * LANES:(hh + 1) * LANES].astype(F32)
                knorm_ref[hh * n_tiles + j] = jnp.sqrt(jnp.max(jnp.sum(k * k, axis=1, keepdims=True)))

    zeros = jnp.zeros((FEATURE_ROWS - SUBLANES, tq), F32)
    coefs, queries, q_norms = [], [], []
    for hh in range(DIFF_STEP_HEADS):
        coef = slope_ref[first_head + hh] * LOG2E
        q = q_ref[0, :, hh * LANES:(hh + 1) * LANES].astype(F32) * (HEAD_DIM ** -0.5 * LOG2E)
        features = jnp.concatenate([_alibi_rows(coef, tq), zeros], axis=0)
        coefs.append(coef)
        for half in range(2):
            q_half = _keep_half(q, half)
            queries.append(_augmented_query(_transposed_bf16(q_half), features))
            q_norms.append(NORM_SLACK * jnp.sqrt(jnp.max(jnp.sum(q_half * q_half, axis=1, keepdims=True))))

    def head_lanes(i):
        return slice((i // 2) * LANES, (i // 2 + 1) * LANES)

    def offset(j, i):
        return coefs[i // 2] * (j * tk - q_start).astype(F32)

    nearest = jnp.maximum(n_full - 1, 0)
    first_needed = []

    def rest_count(m_ref):
        floors = [jnp.min(m_ref[i]) - SKIP_GAP for i in range(2 * DIFF_STEP_HEADS)]

        def body(j, first):
            needed = jnp.bool_(False)
            for i in range(2 * DIFF_STEP_HEADS):
                bound = q_norms[i] * knorm_ref[(i // 2) * n_tiles + j] + coefs[i // 2] * (tk - 1) + offset(j, i)
                needed = jnp.logical_or(needed, bound >= floors[i])
            return jnp.where(needed, jnp.minimum(first, j), first)

        first_needed.append(lax.fori_loop(0, nearest, body, nearest))
        return jnp.minimum(n_full, 1) + nearest - first_needed[0]

    def rest_tile(k):
        if isinstance(k, int):
            return nearest
        return jnp.where(k == 0, nearest, first_needed[0] + k - 1)

    _flash_transposed(
        n_full, 2 * DIFF_STEP_HEADS, q_start, tq, tk,
        lambda j, i: jnp.concatenate([k_ref[0, _kv_rows(j, tk), head_lanes(i)], feat_ref[...]], axis=1),
        lambda j, i: queries[i],
        lambda j, i: vt_ref[0, j, head_lanes(i), :],
        offset, scratch, rest_tile, rest_count)
    lam_vec = lam_ref[...]
    lam = (jnp.exp(jnp.sum(lam_vec[0:1] * lam_vec[1:2], axis=1, keepdims=True))
           - jnp.exp(jnp.sum(lam_vec[2:3] * lam_vec[3:4], axis=1, keepdims=True)) + lam_init)
    for hh in range(DIFF_STEP_HEADS):
        o = (_flash_result(scratch, 2 * hh) - lam * _flash_result(scratch, 2 * hh + 1)).T
        o_ref[0, :, hh * LANES:(hh + 1) * LANES] = (
            _rms_norm(o, g_ref[...], RMS_EPS) * (1.0 - lam_init)).astype(o_ref.dtype)


def _diff_attn(qk, vt, key_features, slopes, lam_vecs, subln_g, lam_init):
    b, s, _ = qk.shape
    tq, tk = min(FLASH_Q_TILE, s), min(KV_TILE, s)
    n = DIFF_STEP_HEADS
    groups = DIFF_HEADS // n
    smem = pl.BlockSpec(memory_space=pltpu.SMEM)
    return pl.pallas_call(
        functools.partial(_diff_attn_kernel, tq=tq, tk=tk, lam_init=lam_init),
        grid=(b, groups, s // tq),
        in_specs=[smem, pl.BlockSpec(lam_vecs.shape, lambda bi, h, i: (0, 0)),
                  pl.BlockSpec((1, tq, n * LANES), lambda bi, h, i: (bi, i, h)),
                  pl.BlockSpec((1, s, n * LANES), lambda bi, h, i: (bi, 0, groups + h)),
                  pl.BlockSpec((1, s // tk, n * LANES, tk), lambda bi, h, i: (bi, 0, h, 0)),
                  pl.BlockSpec(key_features.shape, lambda bi, h, i: (0, 0)),
                  pl.BlockSpec((1, LANES), lambda bi, h, i: (0, 0))],
        out_specs=pl.BlockSpec((1, tq, n * LANES), lambda bi, h, i: (bi, i, h)),
        out_shape=jax.ShapeDtypeStruct((b, s, DIFF_HEADS * LANES), BF16),
        scratch_shapes=[pltpu.SMEM((n * (s // tk),), F32)] + _flash_scratch(2 * n, LANES, tq, tk),
        compiler_params=_params("parallel", "parallel", "arbitrary"),
        name="diff_attn",
    )(slopes, lam_vecs, qk, qk, vt, key_features, subln_g)


def _out_ln_kernel(*refs, n_in):
    a_refs, w_refs = refs[:n_in], refs[n_in:2 * n_in]
    x_ref, g_ref, b_ref, o_ref = refs[2 * n_in:]
    y = None
    for a_ref, w_ref in zip(a_refs, w_refs):
        t = _dot(a_ref[...], w_ref[...])
        y = t if y is None else y + t
    o_ref[...] = _layer_norm(DN_ALPHA * x_ref[...] + y, g_ref[...], b_ref[...])


def _out_ln(acts, weights, x, g, b):
    m, d = x.shape
    tm = min(ROW_TILE, m)
    row = lambda i: (i, 0)
    const = lambda i: (0, 0)
    return pl.pallas_call(
        functools.partial(_out_ln_kernel, n_in=len(acts)),
        grid=(m // tm,),
        in_specs=[pl.BlockSpec((tm, a.shape[1]), row) for a in acts]
        + [pl.BlockSpec(w.shape, const) for w in weights]
        + [pl.BlockSpec((tm, d), row), pl.BlockSpec((1, d), const), pl.BlockSpec((1, d), const)],
        out_specs=pl.BlockSpec((tm, d), row),
        out_shape=jax.ShapeDtypeStruct((m, d), F32),
        compiler_params=_params("parallel"),
        name="out_proj_ln",
    )(*acts, *weights, x, g, b)


def _mlp_kernel(x_ref, wu_ref, wd_ref, g_ref, b_ref, o_ref, *, tf):
    x = x_ref[...]
    xb = x.astype(BF16)
    acc = None
    for f in range(wu_ref.shape[1] // tf):
        cols = slice(f * tf, (f + 1) * tf)
        hidden = jnp.maximum(_dot(xb, wu_ref[:, cols]), 0.0)
        part = _dot((hidden * hidden).astype(BF16), wd_ref[cols, :])
        acc = part if acc is None else acc + part
    o_ref[...] = _layer_norm(DN_ALPHA * x + acc, g_ref[...], b_ref[...])


def _mlp(x, w_up, w_down, g, b):
    m, d = x.shape
    ff = w_up.shape[1]
    tm, tf = min(ROW_TILE, m), min(FF_TILE, ff)
    resident = dict(pipeline_mode=pl.Buffered(1))
    return pl.pallas_call(
        functools.partial(_mlp_kernel, tf=tf),
        grid=(m // tm,),
        in_specs=[pl.BlockSpec((tm, d), lambda i: (i, 0)),
                  pl.BlockSpec((d, ff), lambda i: (0, 0), **resident),
                  pl.BlockSpec((ff, d), lambda i: (0, 0), **resident),
                  pl.BlockSpec((1, d), lambda i: (0, 0)), pl.BlockSpec((1, d), lambda i: (0, 0))],
        out_specs=pl.BlockSpec((tm, d), lambda i: (i, 0)),
        out_shape=jax.ShapeDtypeStruct((m, d), F32),
        compiler_params=_params("parallel"),
        name="mlp_ln",
    )(x, w_up, w_down, g, b)


def _pad_cols(w, width):
    return jnp.pad(w, ((0, 0), (0, width - w.shape[1])))


def _layer0_weights(w_in, w_uq, w_ukv, d_model):
    rank = d_model // 4
    kvw = NSA_GROUPS * HEAD_DIM
    o = np.cumsum([0, rank, rank, MLA_ROPE, NSA_HEADS * HEAD_DIM] + [kvw] * 6 + [3 * NSA_HEADS])
    seg = lambda i: w_in[:, o[i]:o[i + 1]]
    zeros = lambda n: jnp.zeros((w_in.shape[0], n), w_in.dtype)
    half = MLA_ROPE // 2
    tail = LANES - MLA_NOPE - MLA_ROPE - half
    rope_slab = jnp.concatenate([zeros(MLA_NOPE), seg(2), seg(2)[:, :half], zeros(tail)], axis=1)
    w_a = jnp.concatenate([seg(0), seg(1), rope_slab, _pad_cols(seg(10), LANES)], axis=1)
    w_b = jnp.concatenate([seg(3), seg(6), seg(8)], axis=1)
    w_c = jnp.concatenate([seg(4), seg(5)], axis=1)
    w_vs = jnp.concatenate([seg(7), seg(9)], axis=1)
    uq = w_uq.reshape(rank, MLA_HEADS, MLA_NOPE + MLA_ROPE)
    wq = jnp.pad(jnp.concatenate([uq, uq[:, :, MLA_NOPE:MLA_NOPE + half]], axis=2),
                 ((0, 0), (0, 0), (0, tail))).reshape(rank, MLA_HEADS * LANES)
    ukv = w_ukv.reshape(rank, MLA_HEADS, MLA_NOPE + HEAD_DIM)
    wk = jnp.pad(ukv[:, :, :MLA_NOPE], ((0, 0), (0, 0), (0, LANES - MLA_NOPE))).reshape(rank, MLA_HEADS * LANES)
    wv = ukv[:, :, MLA_NOPE:].reshape(rank, MLA_HEADS * HEAD_DIM)
    return [w.astype(BF16) for w in (w_a, w_b, w_c, w_vs, wq, wk, wv)]


def _rope_tables(seq):
    inv = 1.0 / (ROPE_THETA ** (jnp.arange(0, MLA_ROPE, 2, dtype=F32) / MLA_ROPE))
    ang = jnp.arange(seq, dtype=F32)[:, None] * inv[None, :]
    cos, sin = jnp.cos(ang), jnp.sin(ang)
    z = lambda n: jnp.zeros((seq, n), F32)
    tail = LANES - MLA_NOPE - MLA_ROPE
    c = jnp.concatenate([jnp.ones((seq, MLA_NOPE), F32), cos, cos, z(tail)], axis=1)
    s = jnp.concatenate([z(MLA_NOPE), -sin, sin, z(tail)], axis=1)
    return c, s


def _compress_weights(pos_k, w1_k, w2_k, pos_v, w1_v, w2_v):
    eye = jnp.eye(2 * NSA_GROUPS, dtype=F32)
    halves = []
    for a in range(CMP_LEN // CMP_STRIDE):
        rows = slice(a * CMP_STRIDE * HEAD_DIM, (a + 1) * CMP_STRIDE * HEAD_DIM)
        wk = w1_k[rows].reshape(CMP_STRIDE, HEAD_DIM, HEAD_DIM)
        wv = w1_v[rows].reshape(CMP_STRIDE, HEAD_DIM, HEAD_DIM)
        per_slot = jnp.stack([wk, wk, wv, wv], axis=0)
        full = jnp.einsum('st,srdj->rsdtj', eye, per_slot)
        halves.append(full.reshape(CMP_STRIDE * 4 * HEAD_DIM, 4 * HEAD_DIM).astype(BF16))
    w2 = jnp.einsum('st,sdj->sdtj', eye, jnp.stack([w2_k, w2_k, w2_v, w2_v])).reshape(4 * HEAD_DIM, 4 * HEAD_DIM)
    pos = jnp.concatenate([pos_k, pos_k, pos_v, pos_v], axis=1)
    pos = pos.reshape(CMP_LEN // CMP_STRIDE, 1, CMP_STRIDE * 4 * HEAD_DIM)
    pos = jnp.broadcast_to(pos, (pos.shape[0], 8, pos.shape[2])).reshape(-1, pos.shape[2])
    return pos, halves[0], halves[1], w2.astype(BF16)


def _overlap_table(n_cmp_pad, n_cmp):
    c0 = np.arange(n_cmp_pad)[None, :] * CMP_STRIDE
    s0 = np.arange(LANES)[:, None] * SEL_LEN
    ov = np.maximum(np.minimum(c0 + CMP_LEN, s0 + SEL_LEN) - np.maximum(c0, s0), 0) / CMP_LEN
    ov = ov * (np.arange(n_cmp_pad)[None, :] < n_cmp)
    return jnp.asarray(ov, BF16)


def _key_feature_table(tk):
    c = np.arange(tk)
    table = np.zeros((tk, LANES), np.float32)
    table[:, 0:3] = (c // POS_SPLIT)[:, None]
    table[:, 3:6] = (c % POS_SPLIT)[:, None]
    table[c, BLOCK_LANE0 + c // SEL_LEN] = 1.0
    return jnp.asarray(table, BF16)


def _gate_expand_table():
    width = NSA_HEADS * HEAD_DIM
    table = np.zeros((LANES, 3 * width), np.float32)
    for h in range(NSA_HEADS):
        for branch in range(3):
            table[h * 3 + branch, branch * width + h * HEAD_DIM:branch * width + (h + 1) * HEAD_DIM] = 1.0
    return jnp.asarray(table, BF16)


def _alibi_slopes(n):
    return jnp.asarray(2.0 ** (-8.0 * np.arange(1, n + 1) / n), dtype=F32)


def _layer0_mixer(x2, b, s, w_in, q_norm, w_uq, kv_norm, w_ukv, pos_k, w1_k, w2_k, pos_v, w1_v, w2_v, w_out):
    d = x2.shape[1]
    rank = d // 4
    w_a, w_b, w_c, w_vs, wq, wk, wv = _layer0_weights(w_in, w_uq, w_ukv, d)
    slab_a, slab_b, slab_c, vt_nsa = _project(x2, [w_a, w_b, w_c, w_vs], [F32, BF16, BF16, BF16],
                                              [False, False, False, True], b, s)
    rope_c, rope_s = _rope_tables(s)
    q, k, vt = _mla_prep(slab_a, q_norm.reshape(1, rank), kv_norm.reshape(1, rank), wq, wk, wv,
                         rope_c, rope_s, b, s)
    o_mla = _mla_attn(q.reshape(b, s, -1), k.reshape(b, s, -1), vt)
    n_chunks = s // CMP_STRIDE
    n_cmp = (s - CMP_LEN) // CMP_STRIDE + 1
    pos, w1a, w1b, w2 = _compress_weights(pos_k, w1_k, w2_k, pos_v, w1_v, w2_v)
    cmp_k, cmp_vt = _compress(slab_c.reshape(b, n_chunks, CMP_STRIDE * slab_c.shape[1]), pos, w1a, w1b, w2, n_cmp)
    slab_b3 = slab_b.reshape(b, s, -1)
    key_features = _key_feature_table(min(KV_TILE, s))
    o_cmp, o_win, sel_bias_t, block_any = _nsa_cmp_win(slab_b3, cmp_k, cmp_vt, vt_nsa, key_features,
                                                       _overlap_table(n_chunks, n_cmp), s)
    o_nsa = _nsa_sel(slab_b3, vt_nsa, key_features, sel_bias_t, block_any, o_cmp, o_win, slab_a,
                     _gate_expand_table(), s, (2 * rank + LANES) // LANES)
    half = o_mla.shape[-1]
    w_out_b = w_out.astype(BF16)
    return [o_mla.reshape(b * s, half), o_nsa.reshape(b * s, -1)], [w_out_b[:half], w_out_b[half:]]


def _layer1_mixer(x2, b, s, w_qkv, lam_q1, lam_k1, lam_q2, lam_k2, subln_g, w_o, layer_idx):
    d = x2.shape[1]
    w = w_qkv.astype(BF16)
    qk, vt = _project(x2, [w[:, :2 * d], w[:, 2 * d:]], [BF16, BF16], [False, True], b, s)
    lam_init = 0.8 - 0.6 * math.exp(-0.3 * layer_idx)
    lam_vecs = jnp.stack([lam_q1, lam_k1, lam_q2, lam_k2]).astype(F32)
    o = _diff_attn(qk.reshape(b, s, -1), vt, _key_feature_table(min(KV_TILE, s)), _alibi_slopes(DIFF_HEADS),
                   lam_vecs, subln_g.reshape(1, -1), lam_init)
    return [o.reshape(b * s, -1)], [w_o.astype(BF16)]


def kernel(x, l0_w_in, l0_mla_q_norm, l0_mla_w_uq, l0_mla_kv_norm, l0_mla_w_ukv, l0_nsa_cmp_pos_k, l0_nsa_cmp_w1_k, l0_nsa_cmp_w2_k, l0_nsa_cmp_pos_v, l0_nsa_cmp_w1_v, l0_nsa_cmp_w2_v, l0_w_out, l0_ln_mix_g, l0_ln_mix_b, l0_w_up, l0_w_down, l0_ln_ffn_g, l0_ln_ffn_b, l1_w_qkv, l1_lam_q1, l1_lam_k1, l1_lam_q2, l1_lam_k2, l1_subln_g, l1_w_o, l1_ln_mix_g, l1_ln_mix_b, l1_w_up, l1_w_down, l1_ln_ffn_g, l1_ln_ffn_b):
    b, s, d = x.shape
    x2 = x.reshape(b * s, d)
    vec = lambda p: p.reshape(1, d)
    acts, weights = _layer0_mixer(x2, b, s, l0_w_in, l0_mla_q_norm, l0_mla_w_uq, l0_mla_kv_norm, l0_mla_w_ukv,
                                  l0_nsa_cmp_pos_k, l0_nsa_cmp_w1_k, l0_nsa_cmp_w2_k,
                                  l0_nsa_cmp_pos_v, l0_nsa_cmp_w1_v, l0_nsa_cmp_w2_v, l0_w_out)
    x2 = _out_ln(acts, weights, x2, vec(l0_ln_mix_g), vec(l0_ln_mix_b))
    x2 = _mlp(x2, l0_w_up.astype(BF16), l0_w_down.astype(BF16), vec(l0_ln_ffn_g), vec(l0_ln_ffn_b))
    acts, weights = _layer1_mixer(x2, b, s, l1_w_qkv, l1_lam_q1, l1_lam_k1, l1_lam_q2, l1_lam_k2,
                                  l1_subln_g, l1_w_o, 1)
    x2 = _out_ln(acts, weights, x2, vec(l1_ln_mix_g), vec(l1_ln_mix_b))
    x2 = _mlp(x2, l1_w_up.astype(BF16), l1_w_down.astype(BF16), vec(l1_ln_ffn_g), vec(l1_ln_ffn_b))
    return x2.reshape(b, s, d)
```

```python
import functools
import math

import jax
import jax.numpy as jnp
import numpy as np
from jax import lax
from jax.experimental import pallas as pl
from jax.experimental.pallas import tpu as pltpu

F32 = jnp.float32
BF16 = jnp.bfloat16

LANES = 128
SUBLANES = 8
BF16_ROWS = 16
MXU_DEPTH = 256
HEAD_DIM = 64
FLASH_Q_TILE = 512
KV_TILE = 512
KEY_CHUNK = 32
ROW_TILE = 512
FF_TILE = 1024
VMEM_LIMIT = 56 * 1024 * 1024

NEG_INF = -1e30
LOG2E = math.log2(math.e)
LN_EPS = 1e-5
RMS_EPS = 1e-6
DEPTH = 2
DN_ALPHA = (2.0 * DEPTH) ** 0.25

MLA_HEADS = 8
MLA_NOPE = 64
MLA_ROPE = 32
ROPE_THETA = 10000.0
NSA_HEADS = 8
NSA_GROUPS = 2
NSA_HG = NSA_HEADS // NSA_GROUPS
CMP_LEN = 32
CMP_STRIDE = 16
SEL_LEN = 64
SEL_TOPK = 16
WINDOW = 512
FORCE_BONUS = 1e3
DIFF_HEADS = 8

POS_SPLIT = 16
FEATURE_ROWS = 16
BLOCK_LANE0 = 8
BLOCKS_PER_TILE = KV_TILE // SEL_LEN


def _params(*sem):
    return pltpu.CompilerParams(dimension_semantics=sem, vmem_limit_bytes=VMEM_LIMIT)


def _dot(a, b):
    return jnp.dot(a, b, preferred_element_type=F32)


def _split_bf16(x):
    hi = x.astype(BF16)
    lo = (x - hi.astype(F32)).astype(BF16)
    return hi, lo


def _layer_norm(z, g, b):
    mu = jnp.mean(z, axis=-1, keepdims=True)
    zc = z - mu
    var = jnp.mean(zc * zc, axis=-1, keepdims=True)
    return zc * lax.rsqrt(var + LN_EPS) * g + b


def _rms_norm(z, g, eps):
    return z * lax.rsqrt(jnp.mean(z * z, axis=-1, keepdims=True) + eps) * g


def _lane_iota(shape):
    return lax.broadcasted_iota(jnp.int32, shape, 1)


def _keep_half(x, half):
    lane = _lane_iota(x.shape)
    keep = (lane < HEAD_DIM) if half == 0 else (lane >= HEAD_DIM)
    return jnp.where(keep, x, jnp.zeros_like(x))


def _move_head(slab, src_half, dst_half):
    if src_half != dst_half:
        slab = pltpu.roll(slab, HEAD_DIM, 1)
    return _keep_half(slab, dst_half)


def _store_transposed(o_ref, res):
    for c in range(res.shape[1] // LANES):
        cols = slice(c * LANES, (c + 1) * LANES)
        o_ref[0, 0, cols, :] = res[:, cols].T.astype(o_ref.dtype)


def _store_chunked(o_ref, res, stage_ref):
    rows, width = res.shape
    for c in range(width // LANES):
        stage_ref[c] = res[:, c * LANES:(c + 1) * LANES]
    for r in range(CMP_STRIDE):
        for c in range(width // LANES):
            part = stage_ref[c, pl.ds(r, rows // CMP_STRIDE, stride=CMP_STRIDE), :]
            o_ref[:, r * width + c * LANES:r * width + (c + 1) * LANES] = part.astype(o_ref.dtype)


def _proj_kernel(x_ref, *refs, transposed):
    n_out = len(transposed)
    w_refs, o_refs, scratch = refs[:n_out], refs[n_out:2 * n_out], refs[2 * n_out:]
    xb = x_ref[...].astype(BF16)
    for w_ref, o_ref, t in zip(w_refs, o_refs, transposed):
        res = _dot(xb, w_ref[...])
        if t == "chunks":
            _store_chunked(o_ref, res, scratch[0])
        elif t:
            _store_transposed(o_ref, res)
        else:
            o_ref[...] = res.astype(o_ref.dtype)


def _transposed_out(b, seq, width, tm):
    per_seq = seq // tm
    spec = pl.BlockSpec((1, 1, width, tm), lambda i: (i // per_seq, i % per_seq, 0, 0))
    return spec, jax.ShapeDtypeStruct((b, per_seq, width, tm), BF16)


def _project(x, weights, out_dtypes, transposed, b, seq):
    m, k = x.shape
    tm = min(KV_TILE, seq)
    specs, shapes, scratch = [], [], []
    for w, dt, t in zip(weights, out_dtypes, transposed):
        if t == "chunks":
            wide = CMP_STRIDE * w.shape[1]
            spec = pl.BlockSpec((tm // CMP_STRIDE, wide), lambda i: (i, 0))
            shape = jax.ShapeDtypeStruct((m // CMP_STRIDE, wide), dt)
            scratch = [pltpu.VMEM((w.shape[1] // LANES, tm, LANES), F32)]
        elif t:
            spec, shape = _transposed_out(b, seq, w.shape[1], tm)
        else:
            spec, shape = pl.BlockSpec((tm, w.shape[1]), lambda i: (i, 0)), jax.ShapeDtypeStruct((m, w.shape[1]), dt)
        specs.append(spec)
        shapes.append(shape)
    return pl.pallas_call(
        functools.partial(_proj_kernel, transposed=tuple(transposed)),
        grid=(m // tm,),
        in_specs=[pl.BlockSpec((tm, k), lambda i: (i, 0))]
        + [pl.BlockSpec(w.shape, lambda i: (0, 0)) for w in weights],
        out_specs=specs,
        out_shape=shapes,
        scratch_shapes=scratch,
        compiler_params=_params("parallel"),
        name="project",
    )(x, *weights)


def _rope_slab(slab, c, s):
    return slab * c + pltpu.roll(slab, LANES - MLA_ROPE // 2, 1) * s


def _mla_prep_kernel(ql_ref, kvl_ref, kpe_ref, qg_ref, kvg_ref, wq_ref, wk_ref, wv_ref,
                     c_ref, s_ref, q_ref, k_ref, vt_ref, *, q_scale):
    c, s = c_ref[...], s_ref[...]
    qn = _rms_norm(ql_ref[...], qg_ref[...], RMS_EPS).astype(BF16)
    kvn = _rms_norm(kvl_ref[...], kvg_ref[...], RMS_EPS).astype(BF16)
    q = _dot(qn, wq_ref[...])
    k = _dot(kvn, wk_ref[...])
    _store_transposed(vt_ref, _dot(kvn, wv_ref[...]))
    kpe = _rope_slab(kpe_ref[...], c, s)
    for h in range(MLA_HEADS):
        sl = slice(h * LANES, (h + 1) * LANES)
        q_ref[:, sl] = (_rope_slab(q[:, sl], c, s) * q_scale).astype(q_ref.dtype)
        k_ref[:, sl] = (k[:, sl] + kpe).astype(k_ref.dtype)


def _mla_prep(slab_a, q_gain, kv_gain, wq, wk, wv, rope_c, rope_s, b, seq):
    m = slab_a.shape[0]
    tm = min(KV_TILE, seq)
    per_seq = seq // tm
    rank = q_gain.shape[1]
    row = lambda j: (lambda i: (i, j))
    tab = lambda i: (i % per_seq, 0)
    const = lambda i: (0, 0)
    hw = MLA_HEADS * LANES
    vt_spec, vt_shape = _transposed_out(b, seq, wv.shape[1], tm)
    return pl.pallas_call(
        functools.partial(_mla_prep_kernel, q_scale=float((MLA_NOPE + MLA_ROPE) ** -0.5 * LOG2E)),
        grid=(m // tm,),
        in_specs=[pl.BlockSpec((tm, rank), row(0)), pl.BlockSpec((tm, rank), row(1)),
                  pl.BlockSpec((tm, LANES), row(2 * rank // LANES)),
                  pl.BlockSpec((1, rank), const), pl.BlockSpec((1, rank), const),
                  pl.BlockSpec(wq.shape, const), pl.BlockSpec(wk.shape, const), pl.BlockSpec(wv.shape, const),
                  pl.BlockSpec((tm, LANES), tab), pl.BlockSpec((tm, LANES), tab)],
        out_specs=[pl.BlockSpec((tm, hw), row(0)), pl.BlockSpec((tm, hw), row(0)), vt_spec],
        out_shape=[jax.ShapeDtypeStruct((m, hw), BF16), jax.ShapeDtypeStruct((m, hw), BF16), vt_shape],
        compiler_params=_params("parallel"),
        name="mla_prep",
    )(slab_a, slab_a, slab_a, q_gain, kv_gain, wq, wk, wv, rope_c, rope_s)


def _flash_scratch(n_streams, v_rows, tq, tk):
    scores = pltpu.VMEM((n_streams, tk, tq), F32)
    stat = pltpu.VMEM((n_streams, 1, tq), F32)
    probs = pltpu.VMEM((n_streams, tk, tq), BF16)
    slot = [scores, probs, stat]
    return slot + slot + [stat, pltpu.VMEM((n_streams, v_rows + BF16_ROWS, tq), F32)]


def _chunk_rows(c):
    return slice(c * KEY_CHUNK, (c + 1) * KEY_CHUNK)


def _diagonal_blocks(tk, tq, keys_up_to_query):
    for c in range(tk // KEY_CHUNK):
        for v in range(tq // LANES):
            first_key, last_key = c * KEY_CHUNK, (c + 1) * KEY_CHUNK - 1
            first_query, last_query = v * LANES, (v + 1) * LANES - 1
            all_up_to = last_key <= first_query
            all_beyond = first_key > last_query
            if all_up_to or all_beyond:
                kind = "visible" if all_up_to == keys_up_to_query else "hidden"
            else:
                kind = "mixed"
            yield _chunk_rows(c), slice(first_query, last_query + 1), first_query - first_key, kind


def _fold_rows(x):
    return x.reshape(x.shape[0] // SUBLANES, SUBLANES, x.shape[1])


def _flash_transposed(diagonal_tile, n_streams, q_start, tq, tk, key_operand, query_operand, values, offset, scratch,
                      rest_tile, rest_count):
    assert tq == tk, "the diagonal tile is taken to start at the first query of the tile"
    slot_a, slot_b, (m_ref, acc_ref) = scratch[0:3], scratch[3:6], scratch[6:]
    n_chunks = tk // KEY_CHUNK
    for i in range(n_streams):
        m_ref[i] = jnp.full((1, tq), NEG_INF, F32)
        acc_ref[i] = jnp.zeros(acc_ref.shape[1:], F32)

    def column_max(s_ref, i):
        part = jnp.full((SUBLANES, tq), NEG_INF, F32)
        for c in range(n_chunks):
            part = jnp.maximum(part, jnp.max(_fold_rows(s_ref[i, _chunk_rows(c), :]), axis=0))
        return jnp.max(part, axis=0, keepdims=True)

    def stage1(j, slot, i):
        slot[0][i] = _dot(key_operand(j, i), query_operand(j, i))

    def blocks():
        return _diagonal_blocks(tk, tq, True)

    def stage2(j, slot, i, diagonal):
        s_ref, p_ref, alpha_ref = slot
        if diagonal:
            rel = (lax.broadcasted_iota(jnp.int32, (KEY_CHUNK, LANES), 0)
                   - lax.broadcasted_iota(jnp.int32, (KEY_CHUNK, LANES), 1))
            parts = [jnp.full((SUBLANES, LANES), NEG_INF, F32) for _ in range(tq // LANES)]
            for rows, lanes, bound, kind in blocks():
                if kind == "hidden":
                    continue
                s = s_ref[i, rows, lanes]
                if kind == "mixed":
                    s = jnp.where(rel <= bound, s, NEG_INF)
                    s_ref[i, rows, lanes] = s
                v = lanes.start // LANES
                parts[v] = jnp.maximum(parts[v], jnp.max(_fold_rows(s), axis=0))
            mx = jnp.max(jnp.concatenate(parts, axis=1), axis=0, keepdims=True)
        else:
            mx = column_max(s_ref, i)
        off = offset(j, i)
        m_prev = m_ref[i]
        if off is None:
            m_next = jnp.maximum(m_prev, mx)
            shift = m_next
        else:
            m_next = jnp.maximum(m_prev, mx + off)
            shift = m_next - off
        alpha = jnp.exp2(m_prev - m_next)
        if diagonal:
            for rows, lanes, _, kind in blocks():
                if kind == "hidden":
                    p_ref[i, rows, lanes] = jnp.zeros((KEY_CHUNK, LANES), BF16)
                else:
                    p_ref[i, rows, lanes] = jnp.exp2(s_ref[i, rows, lanes] - shift[:, lanes]).astype(BF16)
        else:
            for c in range(n_chunks):
                p_ref[i, _chunk_rows(c), :] = jnp.exp2(s_ref[i, _chunk_rows(c), :] - shift).astype(BF16)
        m_ref[i] = m_next
        alpha_ref[i] = alpha

    def step(accumulate=None, produce=(), exponentiate=None, diagonal=False):
        products = []
        for i in range(n_streams + 1):
            if i < n_streams:
                if accumulate is not None:
                    tile, slot = accumulate
                    products.append(_dot(_values_and_ones(values(tile, i)), slot[1][i]))
                for tile, slot in produce:
                    stage1(tile, slot, i)
                if exponentiate is not None:
                    stage2(*exponentiate, i, diagonal)
            if accumulate is not None and i > 0:
                acc_ref[i - 1] = accumulate[1][2][i - 1] * acc_ref[i - 1] + products[i - 1]

    step(produce=((diagonal_tile, slot_a), (rest_tile(0), slot_b)), exponentiate=(diagonal_tile, slot_a),
         diagonal=True)
    last = rest_count(m_ref)

    def tile_at(position):
        return rest_tile(jnp.clip(position, 1, jnp.maximum(last, 1)) - 1)

    def diag_or_rest(position):
        return jnp.where(position == 0, diagonal_tile, tile_at(position))

    def pair(t, carry):
        p1 = 2 * t + 1
        step((diag_or_rest(p1 - 1), slot_a), ((tile_at(p1 + 1), slot_a),), (tile_at(p1), slot_b))
        step((tile_at(p1), slot_b), ((tile_at(p1 + 2), slot_b),), (tile_at(p1 + 1), slot_a))
        return carry

    lax.fori_loop(0, last // 2, pair, 0)

    @pl.when(last % 2 == 1)
    def _():
        step(accumulate=(diag_or_rest(last - 1), slot_a), exponentiate=(tile_at(last), slot_b))
        step(accumulate=(tile_at(last), slot_b))

    @pl.when(last % 2 == 0)
    def _():
        step(accumulate=(diag_or_rest(last), slot_a))


def _flash_result(scratch, i):
    acc = scratch[-1][i]
    v_rows = acc.shape[0] - BF16_ROWS
    return acc[:v_rows] / acc[v_rows:v_rows + 1]


def _kv_rows(j, tk):
    return pl.ds(pl.multiple_of(j * tk, tk), tk)


def _transposed_bf16(x):
    return x.astype(F32).T.astype(BF16)


def _alibi_rows(coef, tq):
    c = jnp.zeros((1, tq), F32) + coef
    hi = c.astype(BF16).astype(F32)
    rest = c - hi
    mid = rest.astype(BF16).astype(F32)
    lo = rest - mid
    zero = jnp.zeros((1, tq), F32)
    return jnp.concatenate([POS_SPLIT * hi, POS_SPLIT * mid, POS_SPLIT * lo, hi, mid, lo, zero, zero], axis=0)


def _augmented_query(q_t, feature_rows):
    tq = q_t.shape[1]
    pad = jnp.zeros((MXU_DEPTH - LANES - FEATURE_ROWS, tq), BF16)
    return jnp.concatenate([q_t, feature_rows.astype(BF16), pad], axis=0)


MLA_STEP_HEADS = 4


def _mla_attn_kernel(q_ref, k_ref, vt_ref, o_ref, *scratch, tq, tk):
    q_start = pl.program_id(2) * tq
    n = MLA_STEP_HEADS
    queries = [_transposed_bf16(q_ref[0, :, hh * LANES:(hh + 1) * LANES]) for hh in range(n)]
    n_full = q_start // tk
    _flash_transposed(
        n_full, n, q_start, tq, tk,
        lambda j, i: k_ref[0, _kv_rows(j, tk), i * LANES:(i + 1) * LANES],
        lambda j, i: queries[i],
        lambda j, i: vt_ref[0, j, i * HEAD_DIM:(i + 1) * HEAD_DIM, :],
        lambda j, i: None, scratch,
        rest_tile=lambda k: k, rest_count=lambda m_ref: n_full)
    for pair in range(n // 2):
        o_t = jnp.concatenate([_flash_result(scratch, 2 * pair), _flash_result(scratch, 2 * pair + 1)], axis=0)
        o_ref[0, :, pair * LANES:(pair + 1) * LANES] = o_t.T.astype(o_ref.dtype)


def _mla_attn(q, k, vt):
    b, s, _ = q.shape
    tq, tk = min(FLASH_Q_TILE, s), min(KV_TILE, s)
    n = MLA_STEP_HEADS
    groups = MLA_HEADS // n
    return pl.pallas_call(
        functools.partial(_mla_attn_kernel, tq=tq, tk=tk),
        grid=(b, groups, s // tq),
        in_specs=[pl.BlockSpec((1, tq, n * LANES), lambda bi, p, i: (bi, i, p)),
                  pl.BlockSpec((1, s, n * LANES), lambda bi, p, i: (bi, 0, p)),
                  pl.BlockSpec((1, s // tk, n * HEAD_DIM, tk), lambda bi, p, i: (bi, 0, p, 0))],
        out_specs=pl.BlockSpec((1, tq, n * HEAD_DIM), lambda bi, p, i: (bi, i, p)),
        out_shape=jax.ShapeDtypeStruct((b, s, MLA_HEADS * HEAD_DIM), BF16),
        scratch_shapes=_flash_scratch(n, HEAD_DIM, tq, tk),
        compiler_params=_params("parallel", "parallel", "arbitrary"),
        name="mla_attn",
    )(q, k, vt)


def _gelu_tanh(x):
    return 0.5 * x * (1.0 + jnp.tanh(math.sqrt(2.0 / math.pi) * (x + 0.044715 * (x * x * x))))


def _compress_kernel(x_ref, pos_ref, w1a_ref, w1b_ref, w2_ref, k_ref, vt_ref, *, n_real):
    x = x_ref[0]
    n = x.shape[0]
    first = _dot(x, w1a_ref[...])
    second = _dot(x, w1b_ref[...])
    pos_hi, pos_lo = _split_bf16(pos_ref[...])
    bias = (_dot(pos_hi[:8], w1a_ref[...]) + _dot(pos_lo[:8], w1a_ref[...])
            + _dot(pos_hi[8:], w1b_ref[...]) + _dot(pos_lo[8:], w1b_ref[...]))[:1]
    pre = first + pltpu.roll(second, n - 1, 0) + bias
    out = _dot(_gelu_tanh(pre).astype(BF16), w2_ref[...])
    real = lax.broadcasted_iota(jnp.int32, out.shape, 0) < n_real
    out = jnp.where(real, out, 0.0)
    half = out.shape[1] // 2
    k_ref[0] = out[:, :half].astype(k_ref.dtype)
    vt_ref[0] = out[:, half:].T.astype(vt_ref.dtype)


def _compress(x_chunks, pos_exp, w1a, w1b, w2, n_real):
    b, n, width = x_chunks.shape
    half = w2.shape[1] // 2
    const = lambda bi: (0, 0)
    return pl.pallas_call(
        functools.partial(_compress_kernel, n_real=n_real),
        grid=(b,),
        in_specs=[pl.BlockSpec((1, n, width), lambda bi: (bi, 0, 0)),
                  pl.BlockSpec(pos_exp.shape, const), pl.BlockSpec(w1a.shape, const),
                  pl.BlockSpec(w1b.shape, const), pl.BlockSpec(w2.shape, const)],
        out_specs=[pl.BlockSpec((1, n, half), lambda bi: (bi, 0, 0)), pl.BlockSpec((1, half, n), lambda bi: (bi, 0, 0))],
        out_shape=[jax.ShapeDtypeStruct((b, n, half), BF16), jax.ShapeDtypeStruct((b, half, n), BF16)],
        compiler_params=_params("parallel"),
        name="nsa_compress",
    )(x_chunks, pos_exp, w1a, w1b, w2)


def _nsa_head_slope(h):
    return float(2.0 ** (-8.0 * (h + 1) / NSA_HEADS))


def _nsa_queries(q_ref, g, scale):
    out = []
    for hg in range(NSA_HG):
        h = g * NSA_HG + hg
        slab = q_ref[0, :, (h // 2) * LANES:(h // 2 + 1) * LANES].astype(F32) * scale
        out.append(_move_head(slab, h % 2, g))
    return out


def _selection_frame(q_pos, n_blocks):
    blk = lax.broadcasted_iota(jnp.int32, (n_blocks, q_pos.shape[1]), 0)
    behind = q_pos // SEL_LEN - blk
    near = jnp.abs(2 * behind - 1) <= 1
    bonus = jnp.where(near, FORCE_BONUS, jnp.where(blk == 0, FORCE_BONUS, 0.0))
    return blk, bonus, behind >= 0


def _select_blocks(imp_t, blk, bonus, allowed):
    val = jnp.where(allowed, imp_t + bonus, NEG_INF)
    for _ in range(SEL_TOPK):
        top = jnp.max(val, axis=0, keepdims=True)
        first = jnp.min(jnp.where(val == top, blk, imp_t.shape[0]), axis=0, keepdims=True)
        val = jnp.where(blk == first, -jnp.inf, val)
    return jnp.where(val == -jnp.inf, 0.0, NEG_INF)


def _masked_softmax_pass(s_ref, p_ref, tiles, tq):
    part = jnp.full((SUBLANES, tq), NEG_INF, F32)
    for t, rows, keep, off in tiles:
        for c in range(rows // KEY_CHUNK):
            s = jnp.where(keep(_chunk_rows(c)), s_ref[t, _chunk_rows(c), :], NEG_INF)
            s_ref[t, _chunk_rows(c), :] = s
            part = jnp.maximum(part, jnp.max(_fold_rows(s), axis=0) + off)
    m = jnp.max(part, axis=0, keepdims=True)
    for t, rows, keep, off in tiles:
        shift = m - off
        for c in range(rows // KEY_CHUNK):
            p_ref[t, _chunk_rows(c), :] = jnp.exp2(s_ref[t, _chunk_rows(c), :] - shift).astype(BF16)
    return m > 0.5 * NEG_INF


def _window_softmax(s_ref, p_ref, prev_offset, tq, tk):
    rel = (lax.broadcasted_iota(jnp.int32, (KEY_CHUNK, LANES), 0)
           - lax.broadcasted_iota(jnp.int32, (KEY_CHUNK, LANES), 1))
    tiles = [(0, False, prev_offset), (1, True, 0.0)]
    parts = [jnp.full((SUBLANES, LANES), NEG_INF, F32) for _ in range(tq // LANES)]
    for t, below, off in tiles:
        for rows, lanes, bound, kind in _diagonal_blocks(tk, tq, below):
            if kind == "hidden":
                continue
            s = s_ref[t, rows, lanes]
            if kind == "mixed":
                s = jnp.where((rel <= bound) if below else (rel > bound), s, NEG_INF)
                s_ref[t, rows, lanes] = s
            v = lanes.start // LANES
            parts[v] = jnp.maximum(parts[v], jnp.max(_fold_rows(s), axis=0) + off)
    m = jnp.max(jnp.concatenate(parts, axis=1), axis=0, keepdims=True)
    for t, below, off in tiles:
        shift = m - off
        for rows, lanes, _, kind in _diagonal_blocks(tk, tq, below):
            if kind == "hidden":
                p_ref[t, rows, lanes] = jnp.zeros((KEY_CHUNK, LANES), BF16)
            else:
                p_ref[t, rows, lanes] = jnp.exp2(s_ref[t, rows, lanes] - shift[:, lanes]).astype(BF16)


def _values_and_ones(vt):
    return jnp.concatenate([vt, jnp.ones((BF16_ROWS, vt.shape[1]), BF16)], axis=0)


def _nsa_cmp_win_kernel(q_ref, kc_ref, vct_ref, kw_ref, vwt_ref, feat_ref, ovt_ref, oc_ref, ow_ref, sel_ref, any_ref,
                        end_ref, sc_ref, sw_ref, pc_ref, pw_ref, *, tq, tk):
    qi = pl.program_id(1)
    q_start = qi * tq
    n_cmp = kc_ref.shape[1]
    end_ref[...] = (lax.broadcasted_iota(jnp.int32, (n_cmp, tq), 0) * CMP_STRIDE + (CMP_LEN - 1)
                    - lax.broadcasted_iota(jnp.int32, (n_cmp, tq), 1))
    prev_tile = jnp.maximum(qi - 1, 0)
    zeros = jnp.zeros((FEATURE_ROWS - SUBLANES, tq), F32)
    k_cmp = jnp.concatenate([kc_ref[0], feat_ref[0:n_cmp, :]], axis=1)
    k_win = [jnp.concatenate([kw_ref[0, _kv_rows(j, tk), :], feat_ref[...]], axis=1) for j in (prev_tile, qi)]
    frame = _selection_frame(q_start + lax.broadcasted_iota(jnp.int32, (1, tq), 1), LANES)
    for g in range(NSA_GROUPS):
        queries = [_transposed_bf16(q) for q in _nsa_queries(q_ref, g, HEAD_DIM ** -0.5 * LOG2E)]
        group_rows = slice(g * HEAD_DIM, (g + 1) * HEAD_DIM)
        imp_t = jnp.zeros((LANES, tq), F32)
        out_c, out_w = [], []
        for hg in range(NSA_HG):
            coef = _nsa_head_slope(g * NSA_HG + hg) * LOG2E
            buf = hg % 2
            sc, sw, pc, pw = sc_ref.at[buf], sw_ref.at[buf], pc_ref.at[buf], pw_ref.at[buf]
            cmp_query = _augmented_query(queries[hg], jnp.concatenate([_alibi_rows(CMP_STRIDE * coef, tq), zeros], 0))
            win_query = _augmented_query(queries[hg], jnp.concatenate([_alibi_rows(coef, tq), zeros], 0))
            sc[0] = _dot(k_cmp, cmp_query)
            sw[0] = _dot(k_win[0], win_query)
            sw[1] = _dot(k_win[1], win_query)
            has_any = _masked_softmax_pass(sc, pc, [(0, n_cmp, lambda r: end_ref[r, :] <= q_start, 0.0)], tq)
            acc = _dot(_values_and_ones(vct_ref[0, group_rows, :]), pc[0])
            inv = jnp.where(has_any, 1.0 / acc[HEAD_DIM:HEAD_DIM + 1], 0.0)
            out_c.append(acc[:HEAD_DIM] * inv)
            imp_t = imp_t + _dot(ovt_ref[...], pc[0]) * inv
            _window_softmax(sw, pw, jnp.where(qi >= 1, -coef * tk, NEG_INF), tq, tk)
            acc = (_dot(_values_and_ones(vwt_ref[0, prev_tile, group_rows, :]), pw[0])
                   + _dot(_values_and_ones(vwt_ref[0, qi, group_rows, :]), pw[1]))
            out_w.append(acc[:HEAD_DIM] / acc[HEAD_DIM:HEAD_DIM + 1])
        for pair in range(NSA_HG // 2):
            cols = slice((g * 2 + pair) * LANES, (g * 2 + pair + 1) * LANES)
            oc_ref[0, :, cols] = jnp.concatenate(out_c[2 * pair:2 * pair + 2], axis=0).T
            ow_ref[0, :, cols] = jnp.concatenate(out_w[2 * pair:2 * pair + 2], axis=0).T
        bias_t = _select_blocks(imp_t, *frame)
        sel_ref[0, g * LANES:(g + 1) * LANES, :] = bias_t
        any_ref[0, 0, g * LANES:(g + 1) * LANES, :] = jnp.broadcast_to(
            jnp.max(bias_t, axis=1, keepdims=True), (LANES, LANES))


def _nsa_cmp_win(slab_b, cmp_k, cmp_vt, vt_nsa, key_features, overlap_t, seq):
    b = slab_b.shape[0]
    tq, tk = min(FLASH_Q_TILE, seq), min(KV_TILE, seq)
    assert tq == tk == WINDOW, "the window branch is written for one previous and one diagonal key tile"
    n_cmp = cmp_k.shape[1]
    qw = NSA_HEADS * HEAD_DIM
    base = qw // LANES
    tile = lambda bi, i: (bi, i, 0)
    return pl.pallas_call(
        functools.partial(_nsa_cmp_win_kernel, tq=tq, tk=tk),
        grid=(b, seq // tq),
        in_specs=[pl.BlockSpec((1, tq, qw), tile),
                  pl.BlockSpec((1, n_cmp, LANES), lambda bi, i: (bi, 0, 0)),
                  pl.BlockSpec((1, LANES, n_cmp), lambda bi, i: (bi, 0, 0)),
                  pl.BlockSpec((1, seq, LANES), lambda bi, i: (bi, 0, base + 1)),
                  pl.BlockSpec((1, seq // tk, LANES, tk), lambda bi, i: (bi, 0, 1, 0)),
                  pl.BlockSpec(key_features.shape, lambda bi, i: (0, 0)),
                  pl.BlockSpec(overlap_t.shape, lambda bi, i: (0, 0))],
        out_specs=[pl.BlockSpec((1, tq, qw), tile), pl.BlockSpec((1, tq, qw), tile),
                   pl.BlockSpec((1, NSA_GROUPS * LANES, tq), lambda bi, i: (bi, 0, i)),
                   pl.BlockSpec((1, 1, NSA_GROUPS * LANES, LANES), lambda bi, i: (bi, i, 0, 0))],
        out_shape=[jax.ShapeDtypeStruct((b, seq, qw), F32), jax.ShapeDtypeStruct((b, seq, qw), F32),
                   jax.ShapeDtypeStruct((b, NSA_GROUPS * LANES, seq), F32),
                   jax.ShapeDtypeStruct((b, seq // tq, NSA_GROUPS * LANES, LANES), F32)],
        scratch_shapes=[pltpu.VMEM((n_cmp, tq), jnp.int32),
                        pltpu.VMEM((2, 1, n_cmp, tq), F32), pltpu.VMEM((2, 2, tk, tq), F32),
                        pltpu.VMEM((2, 1, n_cmp, tq), BF16), pltpu.VMEM((2, 2, tk, tq), BF16)],
        compiler_params=_params("parallel", "arbitrary"),
        name="nsa_cmp_win",
    )(slab_b, cmp_k, cmp_vt, slab_b, vt_nsa, key_features, overlap_t)


def _nsa_sel_kernel(tiles_ref, counts_ref, q_ref, k_ref, vt_ref, feat_ref, sel_ref, oc_ref, ow_ref, gate_ref, gx_ref,
                    o_ref, *scratch, tq, tk, max_tiles):
    qi = pl.program_id(1)
    q_start = qi * tq
    diagonal_tile = q_start // tk
    out_slabs = []
    for g in range(NSA_GROUPS):
        queries = [_transposed_bf16(q) for q in _nsa_queries(q_ref, g, HEAD_DIM ** -0.5 * LOG2E)]
        coefs = [_nsa_head_slope(g * NSA_HG + hg) * LOG2E for hg in range(NSA_HG)]
        alibi = [_alibi_rows(c, tq) for c in coefs]
        entry = (pl.program_id(0) * pl.num_programs(1) + qi) * NSA_GROUPS + g
        n_active = counts_ref[entry]

        def listed_tile(k, entry=entry):
            return tiles_ref[entry * max_tiles + jnp.minimum(k, max_tiles - 1)]

        def key_operand(j, i):
            return jnp.concatenate([k_ref[0, _kv_rows(j, tk), :], feat_ref[...]], axis=1)

        def query_operand(j, i, g=g, queries=queries, alibi=alibi):
            first_block = pl.multiple_of(g * LANES + j * BLOCKS_PER_TILE, BLOCKS_PER_TILE)
            blocks = sel_ref[0, pl.ds(first_block, BLOCKS_PER_TILE), :]
            return _augmented_query(queries[i], jnp.concatenate([alibi[i], blocks], axis=0))

        _flash_transposed(
            diagonal_tile, NSA_HG, q_start, tq, tk, key_operand, query_operand,
            lambda j, i, g=g: vt_ref[0, j, g * HEAD_DIM:(g + 1) * HEAD_DIM, :],
            lambda j, i, coefs=coefs: coefs[i] * (j * tk - q_start).astype(F32), scratch,
            rest_tile=listed_tile, rest_count=lambda m_ref, n_active=n_active: n_active)
        heads = [_flash_result(scratch, hg) for hg in range(NSA_HG)]
        for pair in range(NSA_HG // 2):
            out_slabs.append(jnp.concatenate(heads[2 * pair:2 * pair + 2], axis=0).T)
    gates = jax.nn.sigmoid(gate_ref[...])
    g_hi, g_lo = _split_bf16(gates)
    width = NSA_HEADS * HEAD_DIM
    for i, o_sel in enumerate(out_slabs):
        mixed = None
        for branch, o_branch in enumerate((oc_ref[0, :, i * LANES:(i + 1) * LANES], o_sel,
                                           ow_ref[0, :, i * LANES:(i + 1) * LANES])):
            gx = gx_ref[:, branch * width + i * LANES:branch * width + (i + 1) * LANES]
            term = (_dot(g_hi, gx) + _dot(g_lo, gx)) * o_branch
            mixed = term if mixed is None else mixed + term
        o_ref[0, :, i * LANES:(i + 1) * LANES] = mixed.astype(o_ref.dtype)


def _active_key_tiles(block_any, tq, tk):
    b, n_q = block_any.shape[:2]
    max_tiles = LANES // BLOCKS_PER_TILE
    hit = block_any[..., 0].reshape(b, n_q, NSA_GROUPS, max_tiles, BLOCKS_PER_TILE).max(axis=-1) > 0.5 * NEG_INF
    before_diagonal = jnp.arange(max_tiles)[None, :] < (jnp.arange(n_q) * tq // tk)[:, None]
    hit = hit & before_diagonal[None, :, None, :]
    rank = jnp.cumsum(hit.astype(jnp.int32), axis=-1) - 1
    slots = jnp.arange(max_tiles, dtype=jnp.int32)
    in_slot = hit[..., None, :] & (rank[..., None, :] == slots[:, None])
    tiles = jnp.sum(jnp.where(in_slot, slots, 0), axis=-1)
    return tiles.astype(jnp.int32).reshape(-1), hit.sum(axis=-1).astype(jnp.int32).reshape(-1), max_tiles


def _nsa_sel(slab_b, vt, key_features, sel_bias_t, block_any, o_cmp, o_win, slab_a, gate_expand, seq, gate_col_block):
    b = slab_b.shape[0]
    tq, tk = min(FLASH_Q_TILE, seq), min(KV_TILE, seq)
    qw = NSA_HEADS * HEAD_DIM
    base = qw // LANES
    per_seq = seq // tq
    tiles, counts, max_tiles = _active_key_tiles(block_any, tq, tk)
    tile = lambda bi, i, *_: (bi, i, 0)
    grid_spec = pltpu.PrefetchScalarGridSpec(
        num_scalar_prefetch=2,
        grid=(b, seq // tq),
        in_specs=[pl.BlockSpec((1, tq, qw), tile),
                  pl.BlockSpec((1, seq, LANES), lambda bi, i, *_: (bi, 0, base)),
                  pl.BlockSpec((1, seq // tk, LANES, tk), lambda bi, i, *_: (bi, 0, 0, 0)),
                  pl.BlockSpec(key_features.shape, lambda bi, i, *_: (0, 0)),
                  pl.BlockSpec((1, NSA_GROUPS * LANES, tq), lambda bi, i, *_: (bi, 0, i)),
                  pl.BlockSpec((1, tq, qw), tile), pl.BlockSpec((1, tq, qw), tile),
                  pl.BlockSpec((tq, LANES), lambda bi, i, *_: (bi * per_seq + i, gate_col_block)),
                  pl.BlockSpec(gate_expand.shape, lambda bi, i, *_: (0, 0))],
        out_specs=pl.BlockSpec((1, tq, qw), tile),
        scratch_shapes=_flash_scratch(NSA_HG, HEAD_DIM, tq, tk))
    return pl.pallas_call(
        functools.partial(_nsa_sel_kernel, tq=tq, tk=tk, max_tiles=max_tiles),
        grid_spec=grid_spec,
        out_shape=jax.ShapeDtypeStruct((b, seq, qw), BF16),
        compiler_params=_params("parallel", "arbitrary"),
        name="nsa_sel",
    )(tiles, counts, slab_b, slab_b, vt, key_features, sel_bias_t, o_cmp, o_win, slab_a, gate_expand)


DIFF_STEP_HEADS = 2


SKIP_GAP = 180.0
NORM_SLACK = 1.01


def _diff_attn_kernel(slope_ref, lam_ref, q_ref, k_ref, vt_ref, feat_ref, g_ref, o_ref, knorm_ref, *scratch,
                      tq, tk, lam_init):
    first_head = pl.program_id(1) * DIFF_STEP_HEADS
    q_start = pl.program_id(2) * tq
    n_full = q_start // tk
    n_tiles = k_ref.shape[1] // tk

    @pl.when(pl.program_id(2) == 0)
    def _():
        for hh in range(DIFF_STEP_HEADS):
            for j in range(n_tiles):
                k = k_ref[0, j * tk:(j + 1) * tk, hh * LANES:(hh + 1) * LANES].astype(F32)
                knorm_ref[hh * n_tiles + j] = jnp.sqrt(jnp.max(jnp.sum(k * k, axis=1, keepdims=True)))

    zeros = jnp.zeros((FEATURE_ROWS - SUBLANES, tq), F32)
    coefs, queries, q_norms = [], [], []
    for hh in range(DIFF_STEP_HEADS):
        coef = slope_ref[first_head + hh] * LOG2E
        q = q_ref[0, :, hh * LANES:(hh + 1) * LANES].astype(F32) * (HEAD_DIM ** -0.5 * LOG2E)
        features = jnp.concatenate([_alibi_rows(coef, tq), zeros], axis=0)
        coefs.append(coef)
        for half in range(2):
            q_half = _keep_half(q, half)
            queries.append(_augmented_query(_transposed_bf16(q_half), features))
            q_norms.append(NORM_SLACK * jnp.sqrt(jnp.max(jnp.sum(q_half * q_half, axis=1, keepdims=True))))

    def head_lanes(i):
        return slice((i // 2) * LANES, (i // 2 + 1) * LANES)

    def offset(j, i):
        return coefs[i // 2] * (j * tk - q_start).astype(F32)

    nearest = jnp.maximum(n_full - 1, 0)
    first_needed = []

    def rest_count(m_ref):
        floors = [jnp.min(m_ref[i]) - SKIP_GAP for i in range(2 * DIFF_STEP_HEADS)]

        def body(j, first):
            needed = jnp.bool_(False)
            for i in range(2 * DIFF_STEP_HEADS):
                bound = q_norms[i] * knorm_ref[(i // 2) * n_tiles + j] + coefs[i // 2] * (tk - 1) + offset(j, i)
                needed = jnp.logical_or(needed, bound >= floors[i])
            return jnp.where(needed, jnp.minimum(first, j), first)

        first_needed.append(lax.fori_loop(0, nearest, body, nearest))
        return jnp.minimum(n_full, 1) + nearest - first_needed[0]

    def rest_tile(k):
        if isinstance(k, int):
            return nearest
        return jnp.where(k == 0, nearest, first_needed[0] + k - 1)

    _flash_transposed(
        n_full, 2 * DIFF_STEP_HEADS, q_start, tq, tk,
        lambda j, i: jnp.concatenate([k_ref[0, _kv_rows(j, tk), head_lanes(i)], feat_ref[...]], axis=1),
        lambda j, i: queries[i],
        lambda j, i: vt_ref[0, j, head_lanes(i), :],
        offset, scratch, rest_tile, rest_count)
    lam_vec = lam_ref[...]
    lam = (jnp.exp(jnp.sum(lam_vec[0:1] * lam_vec[1:2], axis=1, keepdims=True))
           - jnp.exp(jnp.sum(lam_vec[2:3] * lam_vec[3:4], axis=1, keepdims=True)) + lam_init)
    for hh in range(DIFF_STEP_HEADS):
        o = (_flash_result(scratch, 2 * hh) - lam * _flash_result(scratch, 2 * hh + 1)).T
        o_ref[0, :, hh * LANES:(hh + 1) * LANES] = (
            _rms_norm(o, g_ref[...], RMS_EPS) * (1.0 - lam_init)).astype(o_ref.dtype)


def _diff_attn(qk, vt, key_features, slopes, lam_vecs, subln_g, lam_init):
    b, s, _ = qk.shape
    tq, tk = min(FLASH_Q_TILE, s), min(KV_TILE, s)
    n = DIFF_STEP_HEADS
    groups = DIFF_HEADS // n
    smem = pl.BlockSpec(memory_space=pltpu.SMEM)
    return pl.pallas_call(
        functools.partial(_diff_attn_kernel, tq=tq, tk=tk, lam_init=lam_init),
        grid=(b, groups, s // tq),
        in_specs=[smem, pl.BlockSpec(lam_vecs.shape, lambda bi, h, i: (0, 0)),
                  pl.BlockSpec((1, tq, n * LANES), lambda bi, h, i: (bi, i, h)),
                  pl.BlockSpec((1, s, n * LANES), lambda bi, h, i: (bi, 0, groups + h)),
                  pl.BlockSpec((1, s // tk, n * LANES, tk), lambda bi, h, i: (bi, 0, h, 0)),
                  pl.BlockSpec(key_features.shape, lambda bi, h, i: (0, 0)),
                  pl.BlockSpec((1, LANES), lambda bi, h, i: (0, 0))],
        out_specs=pl.BlockSpec((1, tq, n * LANES), lambda bi, h, i: (bi, i, h)),
        out_shape=jax.ShapeDtypeStruct((b, s, DIFF_HEADS * LANES), BF16),
        scratch_shapes=[pltpu.SMEM((n * (s // tk),), F32)] + _flash_scratch(2 * n, LANES, tq, tk),
        compiler_params=_params("parallel", "parallel", "arbitrary"),
        name="diff_attn",
    )(slopes, lam_vecs, qk, qk, vt, key_features, subln_g)


def _out_ln_kernel(*refs, n_in):
    a_refs, w_refs = refs[:n_in], refs[n_in:2 * n_in]
    x_ref, g_ref, b_ref, o_ref = refs[2 * n_in:]
    y = None
    for a_ref, w_ref in zip(a_refs, w_refs):
        t = _dot(a_ref[...], w_ref[...])
        y = t if y is None else y + t
    o_ref[...] = _layer_norm(DN_ALPHA * x_ref[...] + y, g_ref[...], b_ref[...])


def _out_ln(acts, weights, x, g, b):
    m, d = x.shape
    tm = min(ROW_TILE, m)
    row = lambda i: (i, 0)
    const = lambda i: (0, 0)
    return pl.pallas_call(
        functools.partial(_out_ln_kernel, n_in=len(acts)),
        grid=(m // tm,),
        in_specs=[pl.BlockSpec((tm, a.shape[1]), row) for a in acts]
        + [pl.BlockSpec(w.shape, const) for w in weights]
        + [pl.BlockSpec((tm, d), row), pl.BlockSpec((1, d), const), pl.BlockSpec((1, d), const)],
        out_specs=pl.BlockSpec((tm, d), row),
        out_shape=jax.ShapeDtypeStruct((m, d), F32),
        compiler_params=_params("parallel"),
        name="out_proj_ln",
    )(*acts, *weights, x, g, b)


def _mlp_kernel(x_ref, wu_ref, wd_ref, g_ref, b_ref, o_ref, *, tf):
    x = x_ref[...]
    xb = x.astype(BF16)
    acc = None
    for f in range(wu_ref.shape[1] // tf):
        cols = slice(f * tf, (f + 1) * tf)
        hidden = jnp.maximum(_dot(xb, wu_ref[:, cols]), 0.0)
        part = _dot((hidden * hidden).astype(BF16), wd_ref[cols, :])
        acc = part if acc is None else acc + part
    o_ref[...] = _layer_norm(DN_ALPHA * x + acc, g_ref[...], b_ref[...])


def _mlp(x, w_up, w_down, g, b):
    m, d = x.shape
    ff = w_up.shape[1]
    tm, tf = min(ROW_TILE, m), min(FF_TILE, ff)
    resident = dict(pipeline_mode=pl.Buffered(1))
    return pl.pallas_call(
        functools.partial(_mlp_kernel, tf=tf),
        grid=(m // tm,),
        in_specs=[pl.BlockSpec((tm, d), lambda i: (i, 0)),
                  pl.BlockSpec((d, ff), lambda i: (0, 0), **resident),
                  pl.BlockSpec((ff, d), lambda i: (0, 0), **resident),
                  pl.BlockSpec((1, d), lambda i: (0, 0)), pl.BlockSpec((1, d), lambda i: (0, 0))],
        out_specs=pl.BlockSpec((tm, d), lambda i: (i, 0)),
        out_shape=jax.ShapeDtypeStruct((m, d), F32),
        compiler_params=_params("parallel"),
        name="mlp_ln",
    )(x, w_up, w_down, g, b)


def _pad_cols(w, width):
    return jnp.pad(w, ((0, 0), (0, width - w.shape[1])))


def _layer0_weights(w_in, w_uq, w_ukv, d_model):
    rank = d_model // 4
    kvw = NSA_GROUPS * HEAD_DIM
    o = np.cumsum([0, rank, rank, MLA_ROPE, NSA_HEADS * HEAD_DIM] + [kvw] * 6 + [3 * NSA_HEADS])
    seg = lambda i: w_in[:, o[i]:o[i + 1]]
    zeros = lambda n: jnp.zeros((w_in.shape[0], n), w_in.dtype)
    half = MLA_ROPE // 2
    tail = LANES - MLA_NOPE - MLA_ROPE - half
    rope_slab = jnp.concatenate([zeros(MLA_NOPE), seg(2), seg(2)[:, :half], zeros(tail)], axis=1)
    w_a = jnp.concatenate([seg(0), seg(1), rope_slab, _pad_cols(seg(10), LANES)], axis=1)
    w_b = jnp.concatenate([seg(3), seg(6), seg(8)], axis=1)
    w_c = jnp.concatenate([seg(4), seg(5)], axis=1)
    w_vs = jnp.concatenate([seg(7), seg(9)], axis=1)
    uq = w_uq.reshape(rank, MLA_HEADS, MLA_NOPE + MLA_ROPE)
    wq = jnp.pad(jnp.concatenate([uq, uq[:, :, MLA_NOPE:MLA_NOPE + half]], axis=2),
                 ((0, 0), (0, 0), (0, tail))).reshape(rank, MLA_HEADS * LANES)
    ukv = w_ukv.reshape(rank, MLA_HEADS, MLA_NOPE + HEAD_DIM)
    wk = jnp.pad(ukv[:, :, :MLA_NOPE], ((0, 0), (0, 0), (0, LANES - MLA_NOPE))).reshape(rank, MLA_HEADS * LANES)
    wv = ukv[:, :, MLA_NOPE:].reshape(rank, MLA_HEADS * HEAD_DIM)
    return [w.astype(BF16) for w in (w_a, w_b, w_c, w_vs, wq, wk, wv)]


def _rope_tables(seq):
    inv = 1.0 / (ROPE_THETA ** (jnp.arange(0, MLA_ROPE, 2, dtype=F32) / MLA_ROPE))
    ang = jnp.arange(seq, dtype=F32)[:, None] * inv[None, :]
    cos, sin = jnp.cos(ang), jnp.sin(ang)
    z = lambda n: jnp.zeros((seq, n), F32)
    tail = LANES - MLA_NOPE - MLA_ROPE
    c = jnp.concatenate([jnp.ones((seq, MLA_NOPE), F32), cos, cos, z(tail)], axis=1)
    s = jnp.concatenate([z(MLA_NOPE), -sin, sin, z(tail)], axis=1)
    return c, s


def _compress_weights(pos_k, w1_k, w2_k, pos_v, w1_v, w2_v):
    eye = jnp.eye(2 * NSA_GROUPS, dtype=F32)
    halves = []
    for a in range(CMP_LEN // CMP_STRIDE):
        rows = slice(a * CMP_STRIDE * HEAD_DIM, (a + 1) * CMP_STRIDE * HEAD_DIM)
        wk = w1_k[rows].reshape(CMP_STRIDE, HEAD_DIM, HEAD_DIM)
        wv = w1_v[rows].reshape(CMP_STRIDE, HEAD_DIM, HEAD_DIM)
        per_slot = jnp.stack([wk, wk, wv, wv], axis=0)
        full = jnp.einsum('st,srdj->rsdtj', eye, per_slot)
        halves.append(full.reshape(CMP_STRIDE * 4 * HEAD_DIM, 4 * HEAD_DIM).astype(BF16))
    w2 = jnp.einsum('st,sdj->sdtj', eye, jnp.stack([w2_k, w2_k, w2_v, w2_v])).reshape(4 * HEAD_DIM, 4 * HEAD_DIM)
    pos = jnp.concatenate([pos_k, pos_k, pos_v, pos_v], axis=1)
    pos = pos.reshape(CMP_LEN // CMP_STRIDE, 1, CMP_STRIDE * 4 * HEAD_DIM)
    pos = jnp.broadcast_to(pos, (pos.shape[0], 8, pos.shape[2])).reshape(-1, pos.shape[2])
    return pos, halves[0], halves[1], w2.astype(BF16)


def _overlap_table(n_cmp_pad, n_cmp):
    c0 = np.arange(n_cmp_pad)[None, :] * CMP_STRIDE
    s0 = np.arange(LANES)[:, None] * SEL_LEN
    ov = np.maximum(np.minimum(c0 + CMP_LEN, s0 + SEL_LEN) - np.maximum(c0, s0), 0) / CMP_LEN
    ov = ov * (np.arange(n_cmp_pad)[None, :] < n_cmp)
    return jnp.asarray(ov, BF16)


def _key_feature_table(tk):
    c = np.arange(tk)
    table = np.zeros((tk, LANES), np.float32)
    table[:, 0:3] = (c // POS_SPLIT)[:, None]
    table[:, 3:6] = (c % POS_SPLIT)[:, None]
    table[c, BLOCK_LANE0 + c // SEL_LEN] = 1.0
    return jnp.asarray(table, BF16)


def _gate_expand_table():
    width = NSA_HEADS * HEAD_DIM
    table = np.zeros((LANES, 3 * width), np.float32)
    for h in range(NSA_HEADS):
        for branch in range(3):
            table[h * 3 + branch, branch * width + h * HEAD_DIM:branch * width + (h + 1) * HEAD_DIM] = 1.0
    return jnp.asarray(table, BF16)


def _alibi_slopes(n):
    return jnp.asarray(2.0 ** (-8.0 * np.arange(1, n + 1) / n), dtype=F32)


def _layer0_mixer(x2, b, s, w_in, q_norm, w_uq, kv_norm, w_ukv, pos_k, w1_k, w2_k, pos_v, w1_v, w2_v, w_out):
    d = x2.shape[1]
    rank = d // 4
    w_a, w_b, w_c, w_vs, wq, wk, wv = _layer0_weights(w_in, w_uq, w_ukv, d)
    slab_a, slab_b, chunks_c, vt_nsa = _project(x2, [w_a, w_b, w_c, w_vs], [F32, BF16, BF16, BF16],
                                                [False, False, "chunks", True], b, s)
    rope_c, rope_s = _rope_tables(s)
    q, k, vt = _mla_prep(slab_a, q_norm.reshape(1, rank), kv_norm.reshape(1, rank), wq, wk, wv,
                         rope_c, rope_s, b, s)
    o_mla = _mla_attn(q.reshape(b, s, -1), k.reshape(b, s, -1), vt)
    n_chunks = s // CMP_STRIDE
    n_cmp = (s - CMP_LEN) // CMP_STRIDE + 1
    pos, w1a, w1b, w2 = _compress_weights(pos_k, w1_k, w2_k, pos_v, w1_v, w2_v)
    cmp_k, cmp_vt = _compress(chunks_c.reshape(b, n_chunks, chunks_c.shape[1]), pos, w1a, w1b, w2, n_cmp)
    slab_b3 = slab_b.reshape(b, s, -1)
    key_features = _key_feature_table(min(KV_TILE, s))
    o_cmp, o_win, sel_bias_t, block_any = _nsa_cmp_win(slab_b3, cmp_k, cmp_vt, vt_nsa, key_features,
                                                       _overlap_table(n_chunks, n_cmp), s)
    o_nsa = _nsa_sel(slab_b3, vt_nsa, key_features, sel_bias_t, block_any, o_cmp, o_win, slab_a,
                     _gate_expand_table(), s, (2 * rank + LANES) // LANES)
    half = o_mla.shape[-1]
    w_out_b = w_out.astype(BF16)
    return [o_mla.reshape(b * s, half), o_nsa.reshape(b * s, -1)], [w_out_b[:half], w_out_b[half:]]


def _layer1_mixer(x2, b, s, w_qkv, lam_q1, lam_k1, lam_q2, lam_k2, subln_g, w_o, layer_idx):
    d = x2.shape[1]
    w = w_qkv.astype(BF16)
    qk, vt = _project(x2, [w[:, :2 * d], w[:, 2 * d:]], [BF16, BF16], [False, True], b, s)
    lam_init = 0.8 - 0.6 * math.exp(-0.3 * layer_idx)
    lam_vecs = jnp.stack([lam_q1, lam_k1, lam_q2, lam_k2]).astype(F32)
    o = _diff_attn(qk.reshape(b, s, -1), vt, _key_feature_table(min(KV_TILE, s)), _alibi_slopes(DIFF_HEADS),
                   lam_vecs, subln_g.reshape(1, -1), lam_init)
    return [o.reshape(b * s, -1)], [w_o.astype(BF16)]


def kernel(x, l0_w_in, l0_mla_q_norm, l0_mla_w_uq, l0_mla_kv_norm, l0_mla_w_ukv, l0_nsa_cmp_pos_k, l0_nsa_cmp_w1_k, l0_nsa_cmp_w2_k, l0_nsa_cmp_pos_v, l0_nsa_cmp_w1_v, l0_nsa_cmp_w2_v, l0_w_out, l0_ln_mix_g, l0_ln_mix_b, l0_w_up, l0_w_down, l0_ln_ffn_g, l0_ln_ffn_b, l1_w_qkv, l1_lam_q1, l1_lam_k1, l1_lam_q2, l1_lam_k2, l1_subln_g, l1_w_o, l1_ln_mix_g, l1_ln_mix_b, l1_w_up, l1_w_down, l1_ln_ffn_g, l1_ln_ffn_b):
    b, s, d = x.shape
    x2 = x.reshape(b * s, d)
    vec = lambda p: p.reshape(1, d)
    acts, weights = _layer0_mixer(x2, b, s, l0_w_in, l0_mla_q_norm, l0_mla_w_uq, l0_mla_kv_norm, l0_mla_w_ukv,
                                  l0_nsa_cmp_pos_k, l0_nsa_cmp_w1_k, l0_nsa_cmp_w2_k,
                                  l0_nsa_cmp_pos_v, l0_nsa_cmp_w1_v, l0_nsa_cmp_w2_v, l0_w_out)
    x2 = _out_ln(acts, weights, x2, vec(l0_ln_mix_g), vec(l0_ln_mix_b))
    x2 = _mlp(x2, l0_w_up.astype(BF16), l0_w_down.astype(BF16), vec(l0_ln_ffn_g), vec(l0_ln_ffn_b))
    acts, weights = _layer1_mixer(x2, b, s, l1_w_qkv, l1_lam_q1, l1_lam_k1, l1_lam_q2, l1_lam_k2,
                                  l1_subln_g, l1_w_o, 1)
    x2 = _out_ln(acts, weights, x2, vec(l1_ln_mix_g), vec(l1_ln_mix_b))
    x2 = _mlp(x2, l1_w_up.astype(BF16), l1_w_down.astype(BF16), vec(l1_ln_ffn_g), vec(l1_ln_ffn_b))
    return x2.reshape(b, s, d)
```
